```python
import math
import jax
import jax.numpy as jnp
from jax import lax
import numpy as np

D_MODEL = 1024
BATCH = 32
SEQ = 2048
DEPTH = 4

RMS_EPS = 1e-6
D_FF = 2816

DN_HEADS = 4
DN_HEAD_DIM = 128
DN_WIDTH = DN_HEADS * DN_HEAD_DIM
DN_CONV = 5
DN_CHUNK = 64
L2_EPS = 1e-6

POOL_WINDOWS = (2, 4, 8, 16)
POOL_GROUPS = len(POOL_WINDOWS)
POOL_GROUP_DIM = 128
POOL_WIDTH = POOL_GROUPS * POOL_GROUP_DIM

DA_CONFIGS = ((128, 1), (512, 4), (2048, 16))
DA_NGROUPS = len(DA_CONFIGS)
DA_HEADS_PER_GROUP = 4
DA_HEAD_DIM = 64
DA_WIDTH = DA_NGROUPS * DA_HEADS_PER_GROUP * DA_HEAD_DIM
DA_OUT = DA_HEADS_PER_GROUP * DA_HEAD_DIM
DA_BLOCK = 64
ROPE_THETA = 10000.0
MASK_VALUE = -1e30

N_BRANCHES = 3

OFF_DN_QKV = 0
OFF_DN_Z = OFF_DN_QKV + 3 * DN_WIDTH
OFF_DN_BETA = OFF_DN_Z + DN_WIDTH
OFF_DN_A = OFF_DN_BETA + 2 * DN_HEADS
OFF_POOL = OFF_DN_A + 2 * DN_HEADS
OFF_DA = OFF_POOL + POOL_WIDTH
N_IN = OFF_DA + 3 * DA_WIDTH

kernel_name = 'hybrid_bidir_deltanet_pool_dilated_encoder'


def _rmsnorm(x, gain):
    xf = x.astype(jnp.float32)
    y = xf * lax.rsqrt(jnp.mean(xf * xf, axis=-1, keepdims=True) + RMS_EPS)
    return (y * gain.astype(jnp.float32)).astype(x.dtype)


def _l2norm(x):
    return x * lax.rsqrt(jnp.sum(x * x, axis=-1, keepdims=True) + L2_EPS)


def _swiglu(h, w_gate, w_up, w_down):
    return (jax.nn.silu(h @ w_gate) * (h @ w_up)) @ w_down


def _depthwise_conv_centred(x, w):
    K, C = w.shape
    return lax.conv_general_dilated(
        x, w[:, None, :], window_strides=(1,), padding=[(K // 2, K // 2)],
        dimension_numbers=('NWC', 'WIO', 'NWC'), feature_group_count=C)


def _gated_delta_chunked(q, k, v, g, beta):
    f32 = jnp.float32
    B_, H, S, Dk = q.shape
    Dv = v.shape[-1]
    C = DN_CHUNK
    N = S // C
    q = q.reshape(B_, H, N, C, Dk)
    k = k.reshape(B_, H, N, C, Dk)
    v = v.reshape(B_, H, N, C, Dv)
    beta = beta.reshape(B_, H, N, C)
    G = jnp.cumsum(g.reshape(B_, H, N, C), axis=-1)
    idx = jnp.arange(C)
    lower_incl = idx[:, None] >= idx[None, :]
    strict = idx[:, None] > idx[None, :]
    decay = jnp.exp(jnp.where(lower_incl, G[..., :, None] - G[..., None, :], -jnp.inf))
    kb = k * beta[..., None]
    kk = jnp.einsum('bhnid,bhnjd->bhnij', kb, k) * decay
    tri = jnp.where(strict, kk, 0.0) + jnp.eye(C, dtype=f32)
    rhs = jnp.concatenate([v * beta[..., None], kb * jnp.exp(G)[..., None]], axis=-1)
    sol = lax.linalg.triangular_solve(tri, rhs, left_side=True, lower=True, unit_diagonal=True)
    u, w = sol[..., :Dv], sol[..., Dv:]
    qk = jnp.where(lower_incl, jnp.einsum('bhnid,bhnjd->bhnij', q, k) * decay, 0.0)
    q_dec = q * jnp.exp(G)[..., None]
    k_dec = k * jnp.exp(G[..., -1:] - G)[..., None]
    g_last = jnp.exp(G[..., -1])

    def step(state, xs):
        qk_c, qd_c, kd_c, u_c, w_c, gl_c = xs
        v_new = u_c - jnp.einsum('bhck,bhkv->bhcv', w_c, state)
        o_c = (jnp.einsum('bhck,bhkv->bhcv', qd_c, state)
               + jnp.einsum('bhij,bhjv->bhiv', qk_c, v_new))
        state = state * gl_c[..., None, None] + jnp.einsum('bhck,bhcv->bhkv', kd_c, v_new)
        return state, o_c

    xs = tuple(jnp.moveaxis(t, 2, 0) for t in (qk, q_dec, k_dec, u, w, g_last))
    state0 = jnp.zeros((B_, H, Dk, Dv), f32)
    _, o = lax.scan(step, state0, xs)
    return jnp.moveaxis(o, 0, 2).reshape(B_, H, S, Dv)


def _deltanet_branch(qkv, z, beta_raw, a_raw, conv_w, a_log, dt_bias, out_norm):
    f32 = jnp.float32
    B_, S, _ = qkv.shape
    qkv = jax.nn.silu(_depthwise_conv_centred(qkv, conv_w)).astype(f32)

    def heads(t):
        return t.reshape(B_, S, DN_HEADS, DN_HEAD_DIM).transpose(0, 2, 1, 3)

    q = _l2norm(heads(qkv[..., :DN_WIDTH])) * (DN_HEAD_DIM ** -0.5)
    k = _l2norm(heads(qkv[..., DN_WIDTH:2 * DN_WIDTH]))
    v = heads(qkv[..., 2 * DN_WIDTH:])
    beta = jax.nn.sigmoid(beta_raw.astype(f32)).reshape(B_, S, 2, DN_HEADS).transpose(2, 0, 3, 1)
    g = (-jnp.exp(a_log.astype(f32))
         * jax.nn.softplus(a_raw.astype(f32).reshape(B_, S, 2, DN_HEADS) + dt_bias.astype(f32)))
    g = g.transpose(2, 0, 3, 1)
    o_fwd = _gated_delta_chunked(q, k, v, g[0], beta[0])
    flip = lambda t: jnp.flip(t, axis=2)
    o_bwd = flip(_gated_delta_chunked(flip(q), flip(k), flip(v), flip(g[1]), flip(beta[1])))
    o = (o_fwd + o_bwd).transpose(0, 2, 1, 3)
    o = _rmsnorm(o, out_norm) * jax.nn.silu(z.astype(f32).reshape(B_, S, DN_HEADS, DN_HEAD_DIM))
    return o.reshape(B_, S, DN_WIDTH).astype(z.dtype)


def _pooling_branch(u, pool_w, pool_scale):
    f32 = jnp.float32
    B_, S, _ = u.shape
    ug = u.astype(f32).reshape(B_, S, POOL_GROUPS, POOL_GROUP_DIM)
    csum = jnp.concatenate([jnp.zeros_like(ug[:, :1]), jnp.cumsum(ug, axis=1)], axis=1)
    pos = jnp.arange(S)
    outs = []
    for gi, win in enumerate(POOL_WINDOWS):
        lo = jnp.clip(pos - win // 2, 0, S)
        hi = jnp.clip(pos + (win - win // 2), 0, S)
        cnt = (hi - lo).astype(f32)
        cg = csum[:, :, gi]
        mean = (jnp.take(cg, hi, axis=1) - jnp.take(cg, lo, axis=1)) / cnt[None, :, None]
        outs.append(mean - ug[:, :, gi])
    pooled = jnp.stack(outs, axis=2)
    mixed = jnp.einsum('bsgc,gcd->bsgd', pooled, pool_w.astype(f32))
    return (mixed.reshape(B_, S, POOL_WIDTH) * pool_scale.astype(f32)).astype(u.dtype)


def _rope(x, pos):
    half = x.shape[-1] // 2
    inv_freq = ROPE_THETA ** (-jnp.arange(half, dtype=jnp.float32) / half)
    ang = pos.astype(jnp.float32)[:, None] * inv_freq[None, :]
    cos = jnp.cos(ang)[:, None, None, :]
    sin = jnp.sin(ang)[:, None, None, :]
    x1, x2 = x[..., :half], x[..., half:]
    return jnp.concatenate([x1 * cos - x2 * sin, x2 * cos + x1 * sin], axis=-1)


def _dilated_window_attention(q, k, v, dilation, radius):
    B_, S, H, Dh = q.shape
    L = S // dilation
    Q = DA_BLOCK
    nn = -(-radius // Q)
    nb = -(-L // Q)
    Lp = nb * Q

    def strided(t):
        return t.reshape(B_, L, dilation, H, Dh).transpose(0, 2, 1, 3, 4)

    qs, ks, vs = strided(q), strided(k), strided(v)
    qb = jnp.pad(qs, ((0, 0), (0, 0), (0, Lp - L), (0, 0), (0, 0))).reshape(B_, dilation, nb, Q, H, Dh)
    padk = ((0, 0), (0, 0), (nn * Q, Lp - L + nn * Q), (0, 0), (0, 0))
    kp = jnp.pad(ks, padk).reshape(B_, dilation, nb + 2 * nn, Q, H, Dh)
    vp = jnp.pad(vs, padk).reshape(B_, dilation, nb + 2 * nn, Q, H, Dh)
    kb = jnp.concatenate([kp[:, :, j:j + nb] for j in range(2 * nn + 1)], axis=3)
    vb = jnp.concatenate([vp[:, :, j:j + nb] for j in range(2 * nn + 1)], axis=3)
    s = jnp.einsum('brnqhd,brnkhd->brnhqk', qb, kb)
    blk = jnp.arange(nb)
    qpos = blk[:, None] * Q + jnp.arange(Q)[None, :]
    kpos = blk[:, None] * Q + jnp.arange((2 * nn + 1) * Q)[None, :] - nn * Q
    delta = kpos[:, None, :] - qpos[:, :, None]
    valid = (jnp.abs(delta) <= radius) & (kpos[:, None, :] >= 0) & (kpos[:, None, :] < L)
    s = jnp.where(valid[:, None], s, MASK_VALUE)
    lse = jax.nn.logsumexp(s, axis=-1)
    p = jnp.exp(s - lse[..., None])
    o = jnp.einsum('brnhqk,brnkhd->brnqhd', p, vb)
    o = o.reshape(B_, dilation, Lp, H, Dh)[:, :, :L].transpose(0, 2, 1, 3, 4).reshape(B_, S, H, Dh)
    lse = lse.transpose(0, 1, 2, 4, 3).reshape(B_, dilation, Lp, H)[:, :, :L]
    lse = lse.transpose(0, 2, 1, 3).reshape(B_, S, H)
    return o, lse


def _dilated_branch(qkv):
    f32 = jnp.float32
    B_, S, _ = qkv.shape
    t = qkv.astype(f32).reshape(B_, S, 3, DA_NGROUPS, DA_HEADS_PER_GROUP, DA_HEAD_DIM)
    pos = jnp.arange(S)
    q = _rope(t[:, :, 0], pos) * (DA_HEAD_DIM ** -0.5)
    k = _rope(t[:, :, 1], pos)
    v = t[:, :, 2]
    outs, lses = [], []
    for gi, (window, dil) in enumerate(DA_CONFIGS):
        o_g, lse_g = _dilated_window_attention(q[:, :, gi], k[:, :, gi], v[:, :, gi], dil, window // (2 * dil))
        outs.append(o_g)
        lses.append(lse_g)
    wts = jax.nn.softmax(jnp.stack(lses, axis=0), axis=0)
    merged = jnp.einsum('gbsh,gbshd->bshd', wts, jnp.stack(outs, axis=0))
    return merged.reshape(B_, S, DA_OUT).astype(qkv.dtype)


def _fwd_setup_inputs(seed: int = 0) -> dict:
    key = jax.random.key(seed)
    ks = jax.random.split(key, 26)
    f32 = jnp.float32
    Lr, D, F = DEPTH, D_MODEL, D_FF

    def nrm(k_, shape, scale):
        return jax.random.normal(k_, shape, f32) * scale

    def gain(k_, shape):
        return 1.0 + 0.02 * jax.random.normal(k_, shape, f32)

    a_init = jax.random.uniform(ks[8], (Lr, 2, DN_HEADS), f32, 1.0, 16.0)
    dt = jnp.exp(jax.random.uniform(ks[9], (Lr, 2, DN_HEADS), f32, math.log(1e-3), math.log(1e-1)))
    return {
        'x': jax.random.normal(ks[0], (BATCH, SEQ, D), f32),
        'ffn1_norm': gain(ks[1], (Lr, D)),
        'ffn1_w_gate': nrm(ks[2], (Lr, D, F), D ** -0.5),
        'ffn1_w_up': nrm(ks[3], (Lr, D, F), D ** -0.5),
        'ffn1_w_down': nrm(ks[4], (Lr, F, D), F ** -0.5),
        'mix_norm': gain(ks[5], (Lr, D)),
        'w_in': nrm(ks[6], (Lr, D, N_IN), D ** -0.5),
        'dn_conv': nrm(ks[7], (Lr, DN_CONV, 3 * DN_WIDTH), DN_CONV ** -0.5),
        'dn_a_log': jnp.log(a_init),
        'dn_dt_bias': dt + jnp.log(-jnp.expm1(-dt)),
        'dn_out_norm': gain(ks[10], (Lr, DN_HEAD_DIM)),
        'pool_w': nrm(ks[11], (Lr, POOL_GROUPS, POOL_GROUP_DIM, POOL_GROUP_DIM), POOL_GROUP_DIM ** -0.5),
        'pool_scale': gain(ks[12], (Lr, POOL_WIDTH)),
        'w_proj_a': nrm(ks[13], (Lr, DN_WIDTH, D), DN_WIDTH ** -0.5),
        'w_proj_b': nrm(ks[14], (Lr, POOL_WIDTH, D), POOL_WIDTH ** -0.5),
        'w_proj_c': nrm(ks[15], (Lr, DA_OUT, D), DA_OUT ** -0.5),
        'w_gate': nrm(ks[16], (Lr, D, N_BRANCHES * D), D ** -0.5),
        'b_gate': nrm(ks[17], (Lr, N_BRANCHES * D), 0.01),
        'w_out': nrm(ks[18], (Lr, D, D), D ** -0.5),
        'ffn2_norm': gain(ks[19], (Lr, D)),
        'ffn2_w_gate': nrm(ks[20], (Lr, D, F), D ** -0.5),
        'ffn2_w_up': nrm(ks[21], (Lr, D, F), D ** -0.5),
        'ffn2_w_down': nrm(ks[22], (Lr, F, D), F ** -0.5),
        'final_norm': gain(ks[23], (D,)),
    }


def _fwd_reference(x, ffn1_norm, ffn1_w_gate, ffn1_w_up, ffn1_w_down, mix_norm, w_in, dn_conv, dn_a_log,
              dn_dt_bias, dn_out_norm, pool_w, pool_scale, w_proj_a, w_proj_b, w_proj_c, w_gate, b_gate,
              w_out, ffn2_norm, ffn2_w_gate, ffn2_w_up, ffn2_w_down, final_norm):
    B_, S, D = x.shape
    for l in range(DEPTH):
        x = x + 0.5 * _swiglu(_rmsnorm(x, ffn1_norm[l]), ffn1_w_gate[l], ffn1_w_up[l], ffn1_w_down[l])
        h = _rmsnorm(x, mix_norm[l])
        proj = h @ w_in[l]
        y_a = _deltanet_branch(proj[..., OFF_DN_QKV:OFF_DN_Z], proj[..., OFF_DN_Z:OFF_DN_BETA],
                               proj[..., OFF_DN_BETA:OFF_DN_A], proj[..., OFF_DN_A:OFF_POOL],
                               dn_conv[l], dn_a_log[l], dn_dt_bias[l], dn_out_norm[l]) @ w_proj_a[l]
        y_b = _pooling_branch(proj[..., OFF_POOL:OFF_DA], pool_w[l], pool_scale[l]) @ w_proj_b[l]
        y_c = _dilated_branch(proj[..., OFF_DA:N_IN]) @ w_proj_c[l]
        gates = jax.nn.sigmoid(h @ w_gate[l] + b_gate[l]).reshape(B_, S, N_BRANCHES, D)
        merged = gates[:, :, 0] * y_a + gates[:, :, 1] * y_b + gates[:, :, 2] * y_c
        x = x + merged @ w_out[l]
        x = x + 0.5 * _swiglu(_rmsnorm(x, ffn2_norm[l]), ffn2_w_gate[l], ffn2_w_up[l], ffn2_w_down[l])
    return _rmsnorm(x, final_norm)


import jax as _jax
import jax.numpy as _jnp

TWIN_FORMAT = 'train_step'
FWD_PARAMS = ['x', 'ffn1_norm', 'ffn1_w_gate', 'ffn1_w_up', 'ffn1_w_down', 'mix_norm', 'w_in', 'dn_conv', 'dn_a_log', 'dn_dt_bias', 'dn_out_norm', 'pool_w', 'pool_scale', 'w_proj_a', 'w_proj_b', 'w_proj_c', 'w_gate', 'b_gate', 'w_out', 'ffn2_norm', 'ffn2_w_gate', 'ffn2_w_up', 'ffn2_w_down', 'final_norm']
TWIN_WEIGHTS = ['ffn1_norm', 'ffn1_w_gate', 'ffn1_w_up', 'ffn1_w_down', 'mix_norm', 'w_in', 'dn_conv', 'dn_a_log', 'dn_dt_bias', 'dn_out_norm', 'pool_w', 'pool_scale', 'w_proj_a', 'w_proj_b', 'w_proj_c', 'w_gate', 'b_gate', 'w_out', 'ffn2_norm', 'ffn2_w_gate', 'ffn2_w_up', 'ffn2_w_down', 'final_norm']
TWIN_DIFF_INPUT = 'x'
TWIN_INPUTS = ['x', 'ffn1_norm', 'ffn1_w_gate', 'ffn1_w_up', 'ffn1_w_down', 'mix_norm', 'w_in', 'dn_conv', 'dn_a_log', 'dn_dt_bias', 'dn_out_norm', 'pool_w', 'pool_scale', 'w_proj_a', 'w_proj_b', 'w_proj_c', 'w_gate', 'b_gate', 'w_out', 'ffn2_norm', 'ffn2_w_gate', 'ffn2_w_up', 'ffn2_w_down', 'final_norm', 'loss_target', 'm_ffn1_norm', 'm_ffn1_w_gate', 'm_ffn1_w_up', 'm_ffn1_w_down', 'm_mix_norm', 'm_w_in', 'm_dn_conv', 'm_dn_a_log', 'm_dn_dt_bias', 'm_dn_out_norm', 'm_pool_w', 'm_pool_scale', 'm_w_proj_a', 'm_w_proj_b', 'm_w_proj_c', 'm_w_gate', 'm_b_gate', 'm_w_out', 'm_ffn2_norm', 'm_ffn2_w_gate', 'm_ffn2_w_up', 'm_ffn2_w_down', 'm_final_norm', 'v_ffn1_norm', 'v_ffn1_w_gate', 'v_ffn1_w_up', 'v_ffn1_w_down', 'v_mix_norm', 'v_w_in', 'v_dn_conv', 'v_dn_a_log', 'v_dn_dt_bias', 'v_dn_out_norm', 'v_pool_w', 'v_pool_scale', 'v_w_proj_a', 'v_w_proj_b', 'v_w_proj_c', 'v_w_gate', 'v_b_gate', 'v_w_out', 'v_ffn2_norm', 'v_ffn2_w_gate', 'v_ffn2_w_up', 'v_ffn2_w_down', 'v_final_norm']
TWIN_OUTPUTS = ['loss', 'grad_x', 'grad_ffn1_norm', 'grad_ffn1_w_gate', 'grad_ffn1_w_up', 'grad_ffn1_w_down', 'grad_mix_norm', 'grad_w_in', 'grad_dn_conv', 'grad_dn_a_log', 'grad_dn_dt_bias', 'grad_dn_out_norm', 'grad_pool_w', 'grad_pool_scale', 'grad_w_proj_a', 'grad_w_proj_b', 'grad_w_proj_c', 'grad_w_gate', 'grad_b_gate', 'grad_w_out', 'grad_ffn2_norm', 'grad_ffn2_w_gate', 'grad_ffn2_w_up', 'grad_ffn2_w_down', 'grad_final_norm', 'delta_ffn1_norm', 'delta_ffn1_w_gate', 'delta_ffn1_w_up', 'delta_ffn1_w_down', 'delta_mix_norm', 'delta_w_in', 'delta_dn_conv', 'delta_dn_a_log', 'delta_dn_dt_bias', 'delta_dn_out_norm', 'delta_pool_w', 'delta_pool_scale', 'delta_w_proj_a', 'delta_w_proj_b', 'delta_w_proj_c', 'delta_w_gate', 'delta_b_gate', 'delta_w_out', 'delta_ffn2_norm', 'delta_ffn2_w_gate', 'delta_ffn2_w_up', 'delta_ffn2_w_down', 'delta_final_norm', 'new_m_ffn1_norm', 'new_m_ffn1_w_gate', 'new_m_ffn1_w_up', 'new_m_ffn1_w_down', 'new_m_mix_norm', 'new_m_w_in', 'new_m_dn_conv', 'new_m_dn_a_log', 'new_m_dn_dt_bias', 'new_m_dn_out_norm', 'new_m_pool_w', 'new_m_pool_scale', 'new_m_w_proj_a', 'new_m_w_proj_b', 'new_m_w_proj_c', 'new_m_w_gate', 'new_m_b_gate', 'new_m_w_out', 'new_m_ffn2_norm', 'new_m_ffn2_w_gate', 'new_m_ffn2_w_up', 'new_m_ffn2_w_down', 'new_m_final_norm', 'new_v_ffn1_norm', 'new_v_ffn1_w_gate', 'new_v_ffn1_w_up', 'new_v_ffn1_w_down', 'new_v_mix_norm', 'new_v_w_in', 'new_v_dn_conv', 'new_v_dn_a_log', 'new_v_dn_dt_bias', 'new_v_dn_out_norm', 'new_v_pool_w', 'new_v_pool_scale', 'new_v_w_proj_a', 'new_v_w_proj_b', 'new_v_w_proj_c', 'new_v_w_gate', 'new_v_b_gate', 'new_v_w_out', 'new_v_ffn2_norm', 'new_v_ffn2_w_gate', 'new_v_ffn2_w_up', 'new_v_ffn2_w_down', 'new_v_final_norm']
TWIN_LEAF_KINDS = {'loss': 'loss', 'grad_x': 'grad_x', 'grad_ffn1_norm': 'grad_w', 'grad_ffn1_w_gate': 'grad_w', 'grad_ffn1_w_up': 'grad_w', 'grad_ffn1_w_down': 'grad_w', 'grad_mix_norm': 'grad_w', 'grad_w_in': 'grad_w', 'grad_dn_conv': 'grad_w', 'grad_dn_a_log': 'grad_w', 'grad_dn_dt_bias': 'grad_w', 'grad_dn_out_norm': 'grad_w', 'grad_pool_w': 'grad_w', 'grad_pool_scale': 'grad_w', 'grad_w_proj_a': 'grad_w', 'grad_w_proj_b': 'grad_w', 'grad_w_proj_c': 'grad_w', 'grad_w_gate': 'grad_w', 'grad_b_gate': 'grad_w', 'grad_w_out': 'grad_w', 'grad_ffn2_norm': 'grad_w', 'grad_ffn2_w_gate': 'grad_w', 'grad_ffn2_w_up': 'grad_w', 'grad_ffn2_w_down': 'grad_w', 'grad_final_norm': 'grad_w', 'delta_ffn1_norm': 'delta_w', 'delta_ffn1_w_gate': 'delta_w', 'delta_ffn1_w_up': 'delta_w', 'delta_ffn1_w_down': 'delta_w', 'delta_mix_norm': 'delta_w', 'delta_w_in': 'delta_w', 'delta_dn_conv': 'delta_w', 'delta_dn_a_log': 'delta_w', 'delta_dn_dt_bias': 'delta_w', 'delta_dn_out_norm': 'delta_w', 'delta_pool_w': 'delta_w', 'delta_pool_scale': 'delta_w', 'delta_w_proj_a': 'delta_w', 'delta_w_proj_b': 'delta_w', 'delta_w_proj_c': 'delta_w', 'delta_w_gate': 'delta_w', 'delta_b_gate': 'delta_w', 'delta_w_out': 'delta_w', 'delta_ffn2_norm': 'delta_w', 'delta_ffn2_w_gate': 'delta_w', 'delta_ffn2_w_up': 'delta_w', 'delta_ffn2_w_down': 'delta_w', 'delta_final_norm': 'delta_w', 'new_m_ffn1_norm': 'new_m', 'new_m_ffn1_w_gate': 'new_m', 'new_m_ffn1_w_up': 'new_m', 'new_m_ffn1_w_down': 'new_m', 'new_m_mix_norm': 'new_m', 'new_m_w_in': 'new_m', 'new_m_dn_conv': 'new_m', 'new_m_dn_a_log': 'new_m', 'new_m_dn_dt_bias': 'new_m', 'new_m_dn_out_norm': 'new_m', 'new_m_pool_w': 'new_m', 'new_m_pool_scale': 'new_m', 'new_m_w_proj_a': 'new_m', 'new_m_w_proj_b': 'new_m', 'new_m_w_proj_c': 'new_m', 'new_m_w_gate': 'new_m', 'new_m_b_gate': 'new_m', 'new_m_w_out': 'new_m', 'new_m_ffn2_norm': 'new_m', 'new_m_ffn2_w_gate': 'new_m', 'new_m_ffn2_w_up': 'new_m', 'new_m_ffn2_w_down': 'new_m', 'new_m_final_norm': 'new_m', 'new_v_ffn1_norm': 'new_v', 'new_v_ffn1_w_gate': 'new_v', 'new_v_ffn1_w_up': 'new_v', 'new_v_ffn1_w_down': 'new_v', 'new_v_mix_norm': 'new_v', 'new_v_w_in': 'new_v', 'new_v_dn_conv': 'new_v', 'new_v_dn_a_log': 'new_v', 'new_v_dn_dt_bias': 'new_v', 'new_v_dn_out_norm': 'new_v', 'new_v_pool_w': 'new_v', 'new_v_pool_scale': 'new_v', 'new_v_w_proj_a': 'new_v', 'new_v_w_proj_b': 'new_v', 'new_v_w_proj_c': 'new_v', 'new_v_w_gate': 'new_v', 'new_v_b_gate': 'new_v', 'new_v_w_out': 'new_v', 'new_v_ffn2_norm': 'new_v', 'new_v_ffn2_w_gate': 'new_v', 'new_v_ffn2_w_up': 'new_v', 'new_v_ffn2_w_down': 'new_v', 'new_v_final_norm': 'new_v'}


def _forward(args):
    return _fwd_reference(*[args[k] for k in FWD_PARAMS])


def _output_shape():
    out = _jax.eval_shape(lambda: _forward(_fwd_setup_inputs(0)))
    return out.shape, out.dtype

N_MICROBATCH = 1
ADAM_LR = 0.001
ADAM_B1 = 0.9
ADAM_B2 = 0.999
ADAM_EPS = 1e-08
ADAM_WD = 0.01
ADAM_STEP = 10
PER_EXAMPLE_BATCH_AXIS = {'x': 0, 'loss_target': 0}
SHARED_INPUTS = []
_WEIGHT_DTYPES = {'ffn1_norm': _jnp.float32, 'ffn1_w_gate': _jnp.float32, 'ffn1_w_up': _jnp.float32, 'ffn1_w_down': _jnp.float32, 'mix_norm': _jnp.float32, 'w_in': _jnp.float32, 'dn_conv': _jnp.float32, 'dn_a_log': _jnp.float32, 'dn_dt_bias': _jnp.float32, 'dn_out_norm': _jnp.float32, 'pool_w': _jnp.float32, 'pool_scale': _jnp.float32, 'w_proj_a': _jnp.float32, 'w_proj_b': _jnp.float32, 'w_proj_c': _jnp.float32, 'w_gate': _jnp.float32, 'b_gate': _jnp.float32, 'w_out': _jnp.float32, 'ffn2_norm': _jnp.float32, 'ffn2_w_gate': _jnp.float32, 'ffn2_w_up': _jnp.float32, 'ffn2_w_down': _jnp.float32, 'final_norm': _jnp.float32}
MOMENT_SCALE = {'ffn1_norm': 1.134464e-01, 'ffn1_w_gate': 4.582198e-02, 'ffn1_w_up': 4.436620e-02, 'ffn1_w_down': 7.363392e-02, 'mix_norm': 1.639613e-01, 'w_in': 6.830120e-02, 'dn_conv': 6.862705e-02, 'dn_a_log': 3.191482e-01, 'dn_dt_bias': 3.089595e-01, 'dn_out_norm': 1.852062e-01, 'pool_w': 1.429454e-01, 'pool_scale': 1.448464e-01, 'w_proj_a': 6.801888e-02, 'w_proj_b': 1.010935e-01, 'w_proj_c': 1.438321e-02, 'w_gate': 2.728271e-02, 'b_gate': 2.949580e-02, 'w_out': 1.213946e-01, 'ffn2_norm': 9.061711e-02, 'ffn2_w_gate': 3.705735e-02, 'ffn2_w_up': 3.591159e-02, 'ffn2_w_down': 5.958552e-02, 'final_norm': 6.394334e+01}


def _to_microbatches(a, axis):
    t = _jnp.moveaxis(a, axis, 0)
    t = t.reshape((N_MICROBATCH, t.shape[0] // N_MICROBATCH) + t.shape[1:])
    return _jnp.moveaxis(t, 1, axis + 1)


def setup_inputs(seed: int = 0) -> dict:
    inp = _fwd_setup_inputs(seed)
    key = _jax.random.fold_in(_jax.random.key(seed), 7919)
    shape, _ = _output_shape()
    out = dict(inp)
    out["loss_target"] = _jax.random.normal(_jax.random.fold_in(key, 0), shape, _jnp.float32)
    for i, name in enumerate(TWIN_WEIGHTS):
        w = inp[name].astype(_jnp.float32)
        if MOMENT_SCALE is None:
            s = _jnp.sqrt(_jnp.mean(_jnp.square(w)) + 1e-30)
        else:
            s = MOMENT_SCALE[name]
        km, kv = _jax.random.split(_jax.random.fold_in(key, i + 1))
        out[name] = w
        out["m_" + name] = s * _jax.random.normal(km, w.shape, _jnp.float32)
        out["v_" + name] = (s * s) * _jax.random.uniform(kv, w.shape, _jnp.float32, 0.5, 1.5)
    if N_MICROBATCH > 1:
        for name, axis in PER_EXAMPLE_BATCH_AXIS.items():
            out[name] = _to_microbatches(out[name], axis)
    return {'x': out['x'], 'ffn1_norm': out['ffn1_norm'], 'ffn1_w_gate': out['ffn1_w_gate'], 'ffn1_w_up': out['ffn1_w_up'], 'ffn1_w_down': out['ffn1_w_down'], 'mix_norm': out['mix_norm'], 'w_in': out['w_in'], 'dn_conv': out['dn_conv'], 'dn_a_log': out['dn_a_log'], 'dn_dt_bias': out['dn_dt_bias'], 'dn_out_norm': out['dn_out_norm'], 'pool_w': out['pool_w'], 'pool_scale': out['pool_scale'], 'w_proj_a': out['w_proj_a'], 'w_proj_b': out['w_proj_b'], 'w_proj_c': out['w_proj_c'], 'w_gate': out['w_gate'], 'b_gate': out['b_gate'], 'w_out': out['w_out'], 'ffn2_norm': out['ffn2_norm'], 'ffn2_w_gate': out['ffn2_w_gate'], 'ffn2_w_up': out['ffn2_w_up'], 'ffn2_w_down': out['ffn2_w_down'], 'final_norm': out['final_norm'], 'loss_target': out['loss_target'], 'm_ffn1_norm': out['m_ffn1_norm'], 'm_ffn1_w_gate': out['m_ffn1_w_gate'], 'm_ffn1_w_up': out['m_ffn1_w_up'], 'm_ffn1_w_down': out['m_ffn1_w_down'], 'm_mix_norm': out['m_mix_norm'], 'm_w_in': out['m_w_in'], 'm_dn_conv': out['m_dn_conv'], 'm_dn_a_log': out['m_dn_a_log'], 'm_dn_dt_bias': out['m_dn_dt_bias'], 'm_dn_out_norm': out['m_dn_out_norm'], 'm_pool_w': out['m_pool_w'], 'm_pool_scale': out['m_pool_scale'], 'm_w_proj_a': out['m_w_proj_a'], 'm_w_proj_b': out['m_w_proj_b'], 'm_w_proj_c': out['m_w_proj_c'], 'm_w_gate': out['m_w_gate'], 'm_b_gate': out['m_b_gate'], 'm_w_out': out['m_w_out'], 'm_ffn2_norm': out['m_ffn2_norm'], 'm_ffn2_w_gate': out['m_ffn2_w_gate'], 'm_ffn2_w_up': out['m_ffn2_w_up'], 'm_ffn2_w_down': out['m_ffn2_w_down'], 'm_final_norm': out['m_final_norm'], 'v_ffn1_norm': out['v_ffn1_norm'], 'v_ffn1_w_gate': out['v_ffn1_w_gate'], 'v_ffn1_w_up': out['v_ffn1_w_up'], 'v_ffn1_w_down': out['v_ffn1_w_down'], 'v_mix_norm': out['v_mix_norm'], 'v_w_in': out['v_w_in'], 'v_dn_conv': out['v_dn_conv'], 'v_dn_a_log': out['v_dn_a_log'], 'v_dn_dt_bias': out['v_dn_dt_bias'], 'v_dn_out_norm': out['v_dn_out_norm'], 'v_pool_w': out['v_pool_w'], 'v_pool_scale': out['v_pool_scale'], 'v_w_proj_a': out['v_w_proj_a'], 'v_w_proj_b': out['v_w_proj_b'], 'v_w_proj_c': out['v_w_proj_c'], 'v_w_gate': out['v_w_gate'], 'v_b_gate': out['v_b_gate'], 'v_w_out': out['v_w_out'], 'v_ffn2_norm': out['v_ffn2_norm'], 'v_ffn2_w_gate': out['v_ffn2_w_gate'], 'v_ffn2_w_up': out['v_ffn2_w_up'], 'v_ffn2_w_down': out['v_ffn2_w_down'], 'v_final_norm': out['v_final_norm']}


def _loss(weights, diff, rest, loss_target):
    with _jax.named_scope("forward"):
        args = {**rest, TWIN_DIFF_INPUT: diff, **{k: w.astype(_WEIGHT_DTYPES[k]) for k, w in weights.items()}}
        y = _forward(args)
    with _jax.named_scope("loss_head"):
        err = _jnp.square(y.astype(_jnp.float32) - loss_target)
        return 0.5 * _jnp.sum(_jnp.mean(err, axis=-1)) if err.ndim else 0.5 * err


def _adamw(w, g, m, v):
    m = ADAM_B1 * m + (1.0 - ADAM_B1) * g
    v = ADAM_B2 * v + (1.0 - ADAM_B2) * _jnp.square(g)
    m_hat = m / (1.0 - ADAM_B1 ** ADAM_STEP)
    v_hat = v / (1.0 - ADAM_B2 ** ADAM_STEP)
    delta = -ADAM_LR * (m_hat / (_jnp.sqrt(v_hat) + ADAM_EPS) + ADAM_WD * w)
    return delta, m, v


def reference(x, ffn1_norm, ffn1_w_gate, ffn1_w_up, ffn1_w_down, mix_norm, w_in, dn_conv, dn_a_log, dn_dt_bias, dn_out_norm, pool_w, pool_scale, w_proj_a, w_proj_b, w_proj_c, w_gate, b_gate, w_out, ffn2_norm, ffn2_w_gate, ffn2_w_up, ffn2_w_down, final_norm, loss_target, m_ffn1_norm, m_ffn1_w_gate, m_ffn1_w_up, m_ffn1_w_down, m_mix_norm, m_w_in, m_dn_conv, m_dn_a_log, m_dn_dt_bias, m_dn_out_norm, m_pool_w, m_pool_scale, m_w_proj_a, m_w_proj_b, m_w_proj_c, m_w_gate, m_b_gate, m_w_out, m_ffn2_norm, m_ffn2_w_gate, m_ffn2_w_up, m_ffn2_w_down, m_final_norm, v_ffn1_norm, v_ffn1_w_gate, v_ffn1_w_up, v_ffn1_w_down, v_mix_norm, v_w_in, v_dn_conv, v_dn_a_log, v_dn_dt_bias, v_dn_out_norm, v_pool_w, v_pool_scale, v_w_proj_a, v_w_proj_b, v_w_proj_c, v_w_gate, v_b_gate, v_w_out, v_ffn2_norm, v_ffn2_w_gate, v_ffn2_w_up, v_ffn2_w_down, v_final_norm):
    given = dict(x=x, ffn1_norm=ffn1_norm, ffn1_w_gate=ffn1_w_gate, ffn1_w_up=ffn1_w_up, ffn1_w_down=ffn1_w_down, mix_norm=mix_norm, w_in=w_in, dn_conv=dn_conv, dn_a_log=dn_a_log, dn_dt_bias=dn_dt_bias, dn_out_norm=dn_out_norm, pool_w=pool_w, pool_scale=pool_scale, w_proj_a=w_proj_a, w_proj_b=w_proj_b, w_proj_c=w_proj_c, w_gate=w_gate, b_gate=b_gate, w_out=w_out, ffn2_norm=ffn2_norm, ffn2_w_gate=ffn2_w_gate, ffn2_w_up=ffn2_w_up, ffn2_w_down=ffn2_w_down, final_norm=final_norm, loss_target=loss_target, m_ffn1_norm=m_ffn1_norm, m_ffn1_w_gate=m_ffn1_w_gate, m_ffn1_w_up=m_ffn1_w_up, m_ffn1_w_down=m_ffn1_w_down, m_mix_norm=m_mix_norm, m_w_in=m_w_in, m_dn_conv=m_dn_conv, m_dn_a_log=m_dn_a_log, m_dn_dt_bias=m_dn_dt_bias, m_dn_out_norm=m_dn_out_norm, m_pool_w=m_pool_w, m_pool_scale=m_pool_scale, m_w_proj_a=m_w_proj_a, m_w_proj_b=m_w_proj_b, m_w_proj_c=m_w_proj_c, m_w_gate=m_w_gate, m_b_gate=m_b_gate, m_w_out=m_w_out, m_ffn2_norm=m_ffn2_norm, m_ffn2_w_gate=m_ffn2_w_gate, m_ffn2_w_up=m_ffn2_w_up, m_ffn2_w_down=m_ffn2_w_down, m_final_norm=m_final_norm, v_ffn1_norm=v_ffn1_norm, v_ffn1_w_gate=v_ffn1_w_gate, v_ffn1_w_up=v_ffn1_w_up, v_ffn1_w_down=v_ffn1_w_down, v_mix_norm=v_mix_norm, v_w_in=v_w_in, v_dn_conv=v_dn_conv, v_dn_a_log=v_dn_a_log, v_dn_dt_bias=v_dn_dt_bias, v_dn_out_norm=v_dn_out_norm, v_pool_w=v_pool_w, v_pool_scale=v_pool_scale, v_w_proj_a=v_w_proj_a, v_w_proj_b=v_w_proj_b, v_w_proj_c=v_w_proj_c, v_w_gate=v_w_gate, v_b_gate=v_b_gate, v_w_out=v_w_out, v_ffn2_norm=v_ffn2_norm, v_ffn2_w_gate=v_ffn2_w_gate, v_ffn2_w_up=v_ffn2_w_up, v_ffn2_w_down=v_ffn2_w_down, v_final_norm=v_final_norm)
    weights = {n: given[n] for n in TWIN_WEIGHTS}
    shared = {n: given[n] for n in SHARED_INPUTS}
    per_example = {n: given[n] for n in ['x']}
    grad_fn = _jax.value_and_grad(_loss, argnums=(0, 1))

    def one_microbatch(ex, loss_target):
        ex = dict(ex)
        diff = ex.pop(TWIN_DIFF_INPUT)
        return grad_fn(weights, diff, {**shared, **ex}, loss_target)

    if N_MICROBATCH == 1:
        loss, (grad_w, grad_x) = one_microbatch(per_example, given["loss_target"])
    else:
        def body(carry, xs):
            loss_sum, grad_sum = carry
            l_k, (gw_k, gx_k) = one_microbatch(xs[0], xs[1])
            with _jax.named_scope("update"):
                return (loss_sum + l_k, _jax.tree.map(_jnp.add, grad_sum, gw_k)), gx_k

        init = (_jnp.zeros((), _jnp.float32), _jax.tree.map(_jnp.zeros_like, weights))
        (loss, grad_w), grad_x = _jax.lax.scan(body, init, (per_example, given["loss_target"]))
    with _jax.named_scope("update"):
        delta_w, new_m, new_v = {}, {}, {}
        for n in TWIN_WEIGHTS:
            delta_w[n], new_m[n], new_v[n] = _adamw(weights[n], grad_w[n], given["m_" + n], given["v_" + n])
    return (loss, grad_x, *[grad_w[n] for n in TWIN_WEIGHTS], *[delta_w[n] for n in TWIN_WEIGHTS],
            *[new_m[n] for n in TWIN_WEIGHTS], *[new_v[n] for n in TWIN_WEIGHTS])
```

```python
import functools
import math

import jax
import jax.numpy as jnp
from jax import lax
from jax.experimental import pallas as pl
from jax.experimental.pallas import tpu as pltpu

F32 = jnp.float32
BF16 = jnp.bfloat16

N_DEV = 8
RMS_EPS = 1e-6
L2_EPS = 1e-6
DN_HEADS = 4
DN_DIM = 128
DN_WIDTH = DN_HEADS * DN_DIM
DN_CONV = 5
DN_CHUNK = 64
DN_SUPER = 256
POOL_GROUPS = 4
POOL_DIM = 128
POOL_WIDTH = POOL_GROUPS * POOL_DIM
POOL_MAX_HALF = 8
DA_GROUPS = 3
DA_HEADS = 4
DA_DIM = 64
DA_WIDTH = DA_GROUPS * DA_HEADS * DA_DIM
DA_OUT = DA_HEADS * DA_DIM
DA_DILATIONS = (1, 4, 16)
DA_RADIUS = 64
DA_TQ = 128
ROPE_THETA = 10000.0
MASK_VALUE = -1e30
BA_PAD = 128

ADAM_LR = 0.001
ADAM_B1 = 0.9
ADAM_B2 = 0.999
ADAM_EPS = 1e-08
ADAM_WD = 0.01
ADAM_STEP = 10

VMEM_LIMIT_V7X = 56 * 1024 * 1024
LANES = 1024

SHARDED = ("ffn1_w_gate", "ffn1_w_up", "ffn1_w_down", "w_in", "dn_conv", "w_proj_a", "w_proj_b", "w_proj_c",
           "w_gate", "w_out", "ffn2_w_gate", "ffn2_w_up", "ffn2_w_down")
SHARD_AXIS = {"ffn1_w_gate": 2, "ffn1_w_up": 2, "ffn1_w_down": 1, "w_in": 2, "dn_conv": 2, "w_proj_a": 2,
              "w_proj_b": 2, "w_proj_c": 2, "w_gate": 2, "w_out": 1, "ffn2_w_gate": 2, "ffn2_w_up": 2,
              "ffn2_w_down": 1}
REPLICATED = ("ffn1_norm", "mix_norm", "dn_a_log", "dn_dt_bias", "dn_out_norm", "pool_w", "pool_scale", "b_gate",
              "ffn2_norm", "final_norm")
WEIGHTS = ("ffn1_norm", "ffn1_w_gate", "ffn1_w_up", "ffn1_w_down", "mix_norm", "w_in", "dn_conv", "dn_a_log",
           "dn_dt_bias", "dn_out_norm", "pool_w", "pool_scale", "w_proj_a", "w_proj_b", "w_proj_c", "w_gate",
           "b_gate", "w_out", "ffn2_norm", "ffn2_w_gate", "ffn2_w_up", "ffn2_w_down", "final_norm")


def _params(**kw):
    return pltpu.CompilerParams(vmem_limit_bytes=VMEM_LIMIT_V7X, **kw)


def _pick(n, target, align):
    best = None
    t = align
    while t <= min(n, target):
        if n % t == 0:
            best = t
        t += align
    return best if best is not None else n


_DIMS = {"nn": (((1,), (0,)), ((), ())), "nt": (((1,), (1,)), ((), ())), "tn": (((0,), (0,)), ((), ()))}


def _dg(a, b, mode):
    return lax.dot_general(a, b, _DIMS[mode], preferred_element_type=F32)


def _split2(a):
    hi = a.astype(BF16)
    lo = (a - hi.astype(F32)).astype(BF16)
    return hi, lo


def _dotp(a, b, mode, passes):
    if passes == 1:
        return _dg(a.astype(BF16), b.astype(BF16), mode)
    ah, al = _split2(a.astype(F32))
    bh, bl = _split2(b.astype(F32))
    return _dg(ah, bh, mode) + (_dg(ah, bl, mode) + _dg(al, bh, mode))


@functools.partial(jax.custom_vjp, nondiff_argnums=(2, 3))
def _dot(a, b, mode, passes):
    return _dotp(a, b, mode, passes)


def _dot_fwd(a, b, mode, passes):
    return _dotp(a, b, mode, passes), (a, b)


def _dot_bwd(mode, passes, res, ct):
    a, b = res
    if mode == "nn":
        da, db = _dotp(ct, b, "nt", passes), _dotp(a, ct, "tn", passes)
    elif mode == "nt":
        da, db = _dotp(ct, b, "nn", passes), _dotp(ct, a, "tn", passes)
    else:
        da, db = _dotp(b, ct, "nt", passes), _dotp(a, ct, "nn", passes)
    return da.astype(a.dtype), db.astype(b.dtype)


_dot.defvjp(_dot_fwd, _dot_bwd)


def _split3(x):
    x1 = x.astype(BF16)
    r = x - x1.astype(F32)
    x2 = r.astype(BF16)
    x3 = (r - x2.astype(F32)).astype(BF16)
    return x1, x2, x3


def _mdotp(mask, x, mode):
    x1, x2, x3 = _split3(x)
    return _dg(mask, x1, mode) + (_dg(mask, x2, mode) + _dg(mask, x3, mode))


@jax.custom_vjp
def _mdot(mask, x):
    return _mdotp(mask, x, "nn")


def _mdot_fwd(mask, x):
    return _mdotp(mask, x, "nn"), mask


def _mdot_bwd(mask, ct):
    return jnp.zeros_like(mask), _mdotp(mask, ct, "tn")


_mdot.defvjp(_mdot_fwd, _mdot_bwd)


def _shift_impl(x, o):
    if o == 0:
        return x
    n = x.shape[0]
    y = pltpu.roll(x, (-o) % n, axis=0)
    t = lax.broadcasted_iota(jnp.int32, x.shape, 0) + o
    return jnp.where((t >= 0) & (t < n), y, 0.0)


@functools.partial(jax.custom_vjp, nondiff_argnums=(1,))
def _shift(x, o):
    return _shift_impl(x, o)


def _shift_fwd(x, o):
    return _shift_impl(x, o), None


def _shift_bwd(o, _, ct):
    return (_shift_impl(ct, -o),)


_shift.defvjp(_shift_fwd, _shift_bwd)


def _rot_impl(x):
    w = x.shape[1]
    half = DA_DIM // 2
    lane = lax.broadcasted_iota(jnp.int32, x.shape, 1)
    first = (lane & (DA_DIM - 1)) < half
    return jnp.where(first, -pltpu.roll(x, w - half, axis=1), pltpu.roll(x, half, axis=1))


@jax.custom_vjp
def _rot(x):
    return _rot_impl(x)


def _rot_fwd(x):
    return _rot_impl(x), None


def _rot_bwd(_, ct):
    return (-_rot_impl(ct),)


_rot.defvjp(_rot_fwd, _rot_bwd)


def _sigmoid(x):
    return 1.0 / (1.0 + jnp.exp(-x))


def _silu(x):
    return x * _sigmoid(x)


def _softplus(x):
    return jnp.maximum(x, 0.0) + jnp.log(1.0 + jnp.exp(-jnp.abs(x)))


def _rms(x, gain):
    return x * lax.rsqrt(jnp.mean(x * x, axis=-1, keepdims=True) + RMS_EPS) * gain


class _In:
    def __init__(self, arr, block, imap, kind="t", acc=False, g=None, gdtype=None):
        self.arr, self.block, self.imap, self.kind, self.acc, self.g, self.gdtype = arr, block, imap, kind, acc, g, gdtype


class _Out:
    def __init__(self, shape, dtype, block, imap):
        self.shape, self.dtype, self.block, self.imap = shape, dtype, block, imap


def _first_step(acc_from, ngrid):
    c = None
    for a in range(acc_from, ngrid):
        t = pl.program_id(a) == 0
        c = t if c is None else jnp.logical_and(c, t)
    return c


def _tile_fwd(name, f, grid, ins, outs):
    n_in = len(ins)
    ngrid = len(grid)

    def body(*refs):
        pids = tuple(pl.program_id(a) for a in range(ngrid))
        vals = [r[...] for r in refs[:n_in]]
        res = f(pids, *vals)
        for r, v in zip(refs[n_in:], res):
            r[...] = v.astype(r.dtype)

    return pl.pallas_call(
        body, name=name, grid=grid,
        in_specs=[pl.BlockSpec(i.block, i.imap) for i in ins],
        out_specs=[pl.BlockSpec(o.block, o.imap) for o in outs],
        out_shape=[jax.ShapeDtypeStruct(o.shape, o.dtype) for o in outs],
        compiler_params=_params(),
    )(*[i.arr for i in ins])


def _tile_bwd(name, f, grid, ins, outs, cts, acc_from=None, addends=None):
    n_in, n_out = len(ins), len(outs)
    ngrid = len(grid)
    diff = [k for k, i in enumerate(ins) if i.kind == "t"]
    addends = addends or {}
    add_keys = sorted(addends)

    def body(*refs):
        pids = tuple(pl.program_id(a) for a in range(ngrid))
        in_refs = refs[:n_in]
        ct_refs = refs[n_in:n_in + n_out]
        add_refs = refs[n_in + n_out:n_in + n_out + len(add_keys)]
        g_refs = refs[n_in + n_out + len(add_keys):]
        vals = [r[...] for r in in_refs]
        dvals = [vals[k].astype(F32) for k in diff]

        def g(*d):
            full = list(vals)
            for k, dk in zip(diff, d):
                full[k] = dk
            return tuple(f(pids, *full))

        res, vjp = jax.vjp(g, *dvals)
        grads = vjp(tuple(c[...].astype(r.dtype) for c, r in zip(ct_refs, res)))
        first = _first_step(acc_from, ngrid) if acc_from is not None else None
        for k, gr, gref in zip(diff, grads, g_refs):
            if k in addends:
                gr = gr + add_refs[add_keys.index(k)][...].astype(F32)
            if ins[k].acc:
                @pl.when(first)
                def _(gr=gr, gref=gref):
                    gref[...] = gr.astype(gref.dtype)

                @pl.when(jnp.logical_not(first))
                def _(gr=gr, gref=gref):
                    gref[...] += gr.astype(gref.dtype)
            else:
                gref[...] = gr.astype(gref.dtype)

    g_shapes, g_specs = [], []
    for k in diff:
        i = ins[k]
        if i.g is not None:
            shape, imap = i.g
        else:
            shape, imap = i.arr.shape, i.imap
        dt = i.gdtype or (F32 if i.acc else i.arr.dtype)
        g_shapes.append(jax.ShapeDtypeStruct(shape, dt))
        g_specs.append(pl.BlockSpec(i.block, imap))
    add_specs = [pl.BlockSpec(ins[k].block, ins[k].g[1] if ins[k].g is not None else ins[k].imap) for k in add_keys]
    return pl.pallas_call(
        body, name=name, grid=grid,
        in_specs=[pl.BlockSpec(i.block, i.imap) for i in ins] + [pl.BlockSpec(o.block, o.imap) for o in outs] + add_specs,
        out_specs=g_specs, out_shape=g_shapes,
        compiler_params=_params(),
    )(*[i.arr for i in ins], *cts, *[addends[k] for k in add_keys])


def _mm(name, a, b, mode, out_dtype=F32, add=None, tm=512, tn=512, tk=512):
    if mode == "nn":
        (M, K), N = a.shape, b.shape[1]
    elif mode == "nt":
        (M, K), N = a.shape, b.shape[0]
    else:
        (K, M), N = a.shape, b.shape[1]
    tm, tn, tk = _pick(M, tm, 128), _pick(N, tn, 128), _pick(K, tk, 128)
    nk = K // tk
    a_spec = pl.BlockSpec((tk, tm), lambda i, j, k: (k, i)) if mode == "tn" else pl.BlockSpec((tm, tk), lambda i, j, k: (i, k))
    b_spec = pl.BlockSpec((tn, tk), lambda i, j, k: (j, k)) if mode == "nt" else pl.BlockSpec((tk, tn), lambda i, j, k: (k, j))
    o_spec = pl.BlockSpec((tm, tn), lambda i, j, k: (i, j))

    def body(*refs):
        if add is None:
            a_ref, b_ref, o_ref, acc = refs
            add_ref = None
        else:
            a_ref, b_ref, add_ref, o_ref, acc = refs
        k = pl.program_id(2)

        @pl.when(k == 0)
        def _():
            acc[...] = jnp.zeros_like(acc)

        acc[...] += _dg(a_ref[...].astype(BF16), b_ref[...].astype(BF16), mode)

        @pl.when(k == nk - 1)
        def _():
            r = acc[...]
            if add_ref is not None:
                r = r + add_ref[...].astype(F32)
            o_ref[...] = r.astype(o_ref.dtype)

    ops = (a, b) if add is None else (a, b, add)
    specs = [a_spec, b_spec] + ([] if add is None else [o_spec])
    return pl.pallas_call(
        body, name=name, grid=(M // tm, N // tn, nk), in_specs=specs, out_specs=o_spec,
        out_shape=jax.ShapeDtypeStruct((M, N), out_dtype), scratch_shapes=[pltpu.VMEM((tm, tn), F32)],
        compiler_params=_params(dimension_semantics=("parallel", "parallel", "arbitrary")),
    )(*ops)


def _ffn_fwd(name, x, gain, wg, wu, wd):
    T, D = x.shape
    F = wg.shape[1]
    tm, tf = _pick(T, 512, 8), _pick(F, 256, 128)
    nf = F // tf

    def body(x_ref, g_ref, wg_ref, wu_ref, wd_ref, o_ref, h_ref, acc):
        j = pl.program_id(1)

        @pl.when(j == 0)
        def _():
            h_ref[...] = _rms(x_ref[...], g_ref[...]).astype(BF16)
            acc[...] = jnp.zeros_like(acc)

        h = h_ref[...]
        a = _dg(h, wg_ref[...], "nn")
        b = _dg(h, wu_ref[...], "nn")
        s = (_silu(a) * b).astype(BF16)
        acc[...] += _dg(s, wd_ref[...], "nn")

        @pl.when(j == nf - 1)
        def _():
            o_ref[...] = x_ref[...] + 0.5 * acc[...]

    return pl.pallas_call(
        body, name=name, grid=(T // tm, nf),
        in_specs=[pl.BlockSpec((tm, D), lambda i, j: (i, 0)), pl.BlockSpec((1, D), lambda i, j: (0, 0)),
                  pl.BlockSpec((D, tf), lambda i, j: (0, j)), pl.BlockSpec((D, tf), lambda i, j: (0, j)),
                  pl.BlockSpec((tf, D), lambda i, j: (j, 0))],
        out_specs=pl.BlockSpec((tm, D), lambda i, j: (i, 0)),
        out_shape=jax.ShapeDtypeStruct((T, D), F32),
        scratch_shapes=[pltpu.VMEM((tm, D), BF16), pltpu.VMEM((tm, D), F32)],
        compiler_params=_params(dimension_semantics=("parallel", "arbitrary")),
    )(x, gain, wg, wu, wd)


def _ffn_bwd(name, x, gain, wg, wu, wd, dy):
    T, D = x.shape
    F = wg.shape[1]
    tm, tf = _pick(T, 512, 8), _pick(F, 256, 128)
    nf = F // tf

    def body(x_ref, g_ref, wg_ref, wu_ref, wd_ref, dy_ref, dx_ref, dg_ref, da_ref, db_ref, s_ref, h_ref, dyh_ref, dh):
        i, j = pl.program_id(0), pl.program_id(1)

        @pl.when(j == 0)
        def _():
            h_ref[...] = _rms(x_ref[...], g_ref[...]).astype(BF16)
            dyh_ref[...] = (0.5 * dy_ref[...]).astype(BF16)
            dh[...] = jnp.zeros_like(dh)

        h = h_ref[...]
        a = _dg(h, wg_ref[...], "nn")
        b = _dg(h, wu_ref[...], "nn")
        ds = _dg(dyh_ref[...], wd_ref[...], "nt")
        sig = _sigmoid(a)
        silu = a * sig
        da = (ds * b * (sig * (1.0 + a * (1.0 - sig)))).astype(BF16)
        db = (ds * silu).astype(BF16)
        da_ref[...] = da
        db_ref[...] = db
        s_ref[...] = (silu * b).astype(BF16)
        dh[...] += _dg(da, wg_ref[...], "nt") + _dg(db, wu_ref[...], "nt")

        @pl.when(j == nf - 1)
        def _():
            _, vjp = jax.vjp(_rms, x_ref[...], g_ref[...])
            dxn, dgn = vjp(dh[...])
            dx_ref[...] = dy_ref[...] + dxn

            @pl.when(i == 0)
            def _():
                dg_ref[...] = dgn

            @pl.when(i != 0)
            def _():
                dg_ref[...] += dgn

    row = lambda i, j: (i, 0)
    col = lambda i, j: (i, j)
    dx, dgain, da, db, s, h, dyh = pl.pallas_call(
        body, name=name, grid=(T // tm, nf),
        in_specs=[pl.BlockSpec((tm, D), row), pl.BlockSpec((1, D), lambda i, j: (0, 0)),
                  pl.BlockSpec((D, tf), lambda i, j: (0, j)), pl.BlockSpec((D, tf), lambda i, j: (0, j)),
                  pl.BlockSpec((tf, D), lambda i, j: (j, 0)), pl.BlockSpec((tm, D), row)],
        out_specs=[pl.BlockSpec((tm, D), row), pl.BlockSpec((1, D), lambda i, j: (0, 0)),
                   pl.BlockSpec((tm, tf), col), pl.BlockSpec((tm, tf), col), pl.BlockSpec((tm, tf), col),
                   pl.BlockSpec((tm, D), row), pl.BlockSpec((tm, D), row)],
        out_shape=[jax.ShapeDtypeStruct((T, D), F32), jax.ShapeDtypeStruct((1, D), F32),
                   jax.ShapeDtypeStruct((T, F), BF16), jax.ShapeDtypeStruct((T, F), BF16),
                   jax.ShapeDtypeStruct((T, F), BF16), jax.ShapeDtypeStruct((T, D), BF16),
                   jax.ShapeDtypeStruct((T, D), BF16)],
        scratch_shapes=[pltpu.VMEM((tm, D), F32)],
        compiler_params=_params(),
    )(x, gain, wg, wu, wd, dy)
    dwg = _mm(name + "_dwg", h, da, "tn")
    dwu = _mm(name + "_dwu", h, db, "tn")
    dwd = _mm(name + "_dwd", s, dyh, "tn")
    return dx, dgain, dwg, dwu, dwd


def _norm_f(pids, x, gain):
    return (_rms(x, gain),)


def _dn_conv_f(pids, x, w):
    j = pids[0]
    tap = lax.broadcasted_iota(jnp.int32, w.shape, 0)
    y = jnp.zeros_like(x)
    for t in range(DN_CONV):
        wt = jnp.sum(jnp.where(tap == t, w, 0.0), axis=0, keepdims=True)
        y = y + _shift(x, t - DN_CONV // 2) * wt
    y = _silu(y)
    n = y * lax.rsqrt(jnp.sum(y * y, axis=-1, keepdims=True) + L2_EPS)
    is_q = (j < DN_HEADS).astype(F32)
    is_qk = (j < 2 * DN_HEADS).astype(F32)
    scale = is_q * (DN_DIM ** -0.5) + (1.0 - is_q)
    return ((is_qk * n + (1.0 - is_qk) * y) * scale,)


def _dn_gate_f(pids, braw, araw, a_log, dt_bias):
    beta = _sigmoid(braw)
    g = -jnp.exp(a_log) * _softplus(araw + dt_bias)
    return beta, g


def _dn_prep_f(pids, q, k, v, brow, grow):
    cs = DN_SUPER
    sign = 1 - 2 * pids[2]
    ii = lax.broadcasted_iota(jnp.int32, (cs, cs), 0)
    jj = lax.broadcasted_iota(jnp.int32, (cs, cs), 1)
    shift = int(math.log2(DN_CHUNK))
    same = (ii >> shift) == (jj >> shift)
    d = (ii - jj) * sign
    incl = same & (d >= 0)
    strict = same & (d > 0)
    eye = ii == jj
    g_col = jnp.sum(jnp.where(eye, jnp.broadcast_to(grow, (cs, cs)), 0.0), axis=1, keepdims=True)
    b_col = jnp.sum(jnp.where(eye, jnp.broadcast_to(brow, (cs, cs)), 0.0), axis=1, keepdims=True)
    g128 = jnp.broadcast_to(g_col, (cs, DN_DIM))
    G = _mdot(incl.astype(BF16), g128)
    Gt = _mdot(same.astype(BF16), g128)
    Gc = jnp.concatenate([G, G], axis=1)
    Grow = jnp.sum(jnp.where(eye, Gc, 0.0), axis=0, keepdims=True)
    decay = jnp.exp(jnp.where(incl, Gc - Grow, MASK_VALUE))
    eG = jnp.exp(G)
    kb = k * b_col
    A = jnp.where(strict, _dot(kb, k, "nt", 1) * decay, 0.0)
    X = jnp.concatenate([v * b_col, kb * eG], axis=1)
    X = X - _dot(A, X, "nn", 3)
    P = A
    for _ in range(shift - 1):
        P = _dot(P, P, "nn", 3)
        X = X + _dot(P, X, "nn", 3)
    qk = jnp.where(incl, _dot(q, k, "nt", 1) * decay, 0.0)
    return X, qk, q * eG, k * jnp.exp(Gt - G), jnp.exp(Gt)


def _dn_out_f(pids, of, ob, z, gain):
    return (_rms(of + ob, gain) * _silu(z),)


def _pool_f(pids, u, w, scale):
    g = pids[0]
    half = jnp.left_shift(1, g)
    n = u.shape[0]
    pos = lax.broadcasted_iota(jnp.int32, (n, 1), 0)
    tot = jnp.zeros_like(u)
    cnt = jnp.zeros((n, 1), F32)
    for o in range(-POOL_MAX_HALF, POOL_MAX_HALF):
        use = ((o >= -half) & (o < half)).astype(F32)
        tot = tot + use * _shift(u, o)
        cnt = cnt + use * ((pos + o >= 0) & (pos + o < n)).astype(F32)
    pooled = tot / cnt - u
    return (_dot(pooled, w, "nn", 1) * scale,)


def _rope_f(pids, q, k, v, cos, sin):
    qr = (q * cos + _rot(q) * sin) * (DA_DIM ** -0.5)
    kr = k * cos + _rot(k) * sin
    return qr, kr, v


def _attn_head(q, k, v, qpos0, kpos0):
    s = _dot(q, k, "nt", 1)
    qi = qpos0 + lax.broadcasted_iota(jnp.int32, s.shape, 0)
    kj = kpos0 + lax.broadcasted_iota(jnp.int32, s.shape, 1)
    s = jnp.where(jnp.abs(kj - qi) <= DA_RADIUS, s, MASK_VALUE)
    m = lax.stop_gradient(jnp.max(s, axis=1, keepdims=True))
    p = jnp.exp(s - m)
    l = jnp.sum(p, axis=1, keepdims=True)
    o = _dot(p, v, "nn", 1) / l
    return o, jnp.broadcast_to(m + jnp.log(l), o.shape)


def _merge_f(pids, o0, o1, o2, l0, l1, l2):
    m = jnp.maximum(jnp.maximum(l0, l1), l2)
    e0, e1, e2 = jnp.exp(l0 - m), jnp.exp(l1 - m), jnp.exp(l2 - m)
    return ((e0 * o0 + e1 * o1 + e2 * o2) / (e0 + e1 + e2),)


def _gate_f(pids, g0, g1, g2, ya, yb, yc, b0, b1, b2):
    return (_sigmoid(g0 + b0) * ya + _sigmoid(g1 + b1) * yb + _sigmoid(g2 + b2) * yc,)


def _attn_window(i, L, tq, W):
    k0 = jnp.clip(i * tq - DA_RADIUS, 0, L - W)
    return pl.multiple_of(k0, DA_RADIUS)


def _attn_fwd(name, q, k, v):
    NS, L, HD = q.shape
    tq = min(DA_TQ, L)
    W = min(L, tq + 2 * DA_RADIUS)

    def body(q_ref, k_ref, v_ref, o_ref, l_ref):
        i = pl.program_id(1)
        k0 = _attn_window(i, L, tq, W)
        for h in range(DA_HEADS):
            hs = slice(h * DA_DIM, (h + 1) * DA_DIM)
            o, lse = _attn_head(q_ref[:, hs], k_ref[pl.ds(k0, W), hs], v_ref[pl.ds(k0, W), hs], i * tq, k0)
            o_ref[:, hs] = o
            l_ref[:, hs] = lse

    qs = pl.BlockSpec((None, tq, HD), lambda s, i: (s, i, 0))
    ks = pl.BlockSpec((None, L, HD), lambda s, i: (s, 0, 0))
    return pl.pallas_call(
        body, name=name, grid=(NS, L // tq), in_specs=[qs, ks, ks], out_specs=[qs, qs],
        out_shape=[jax.ShapeDtypeStruct((NS, L, HD), F32)] * 2, compiler_params=_params(),
    )(q, k, v)


def _attn_bwd(name, q, k, v, do, dl):
    NS, L, HD = q.shape
    tq = min(DA_TQ, L)
    W = min(L, tq + 2 * DA_RADIUS)

    def body(q_ref, k_ref, v_ref, do_ref, dl_ref, dq_ref, dk_ref, dv_ref):
        i = pl.program_id(1)
        k0 = _attn_window(i, L, tq, W)

        @pl.when(i == 0)
        def _():
            dk_ref[...] = jnp.zeros_like(dk_ref)
            dv_ref[...] = jnp.zeros_like(dv_ref)

        for h in range(DA_HEADS):
            hs = slice(h * DA_DIM, (h + 1) * DA_DIM)
            f = functools.partial(_attn_head, qpos0=i * tq, kpos0=k0)
            _, vjp = jax.vjp(f, q_ref[:, hs].astype(F32), k_ref[pl.ds(k0, W), hs].astype(F32),
                             v_ref[pl.ds(k0, W), hs].astype(F32))
            dq, dk, dv = vjp((do_ref[:, hs], dl_ref[:, hs]))
            dq_ref[:, hs] = dq
            dk_ref[pl.ds(k0, W), hs] += dk
            dv_ref[pl.ds(k0, W), hs] += dv

    qs = pl.BlockSpec((None, tq, HD), lambda s, i: (s, i, 0))
    ks = pl.BlockSpec((None, L, HD), lambda s, i: (s, 0, 0))
    return pl.pallas_call(
        body, name=name, grid=(NS, L // tq), in_specs=[qs, ks, ks, qs, qs], out_specs=[qs, ks, ks],
        out_shape=[jax.ShapeDtypeStruct((NS, L, HD), F32)] * 3, compiler_params=_params(),
    )(q, k, v, do, dl)


def _scan_chunk(t, rev, N):
    c = jnp.where(rev, N - 1 - t, t)
    per = DN_SUPER // DN_CHUNK
    return c, pl.multiple_of(c * DN_CHUNK, DN_CHUNK), pl.multiple_of((c % per) * DN_CHUNK, DN_CHUNK), \
        pl.multiple_of((c // per) * DN_SUPER, DN_SUPER)


def _dn_scan_fwd(name, uw, qk, qd, kd, gl, B):
    R, T, _ = uw.shape
    S = T // B
    N = S // DN_CHUNK
    C, DK = DN_CHUNK, DN_DIM

    def body(uw_ref, qk_ref, qd_ref, kd_ref, gl_ref, o_ref, st_ref, vn_ref):
        rev = pl.program_id(1) >= DN_HEADS
        vn_ref[...] = jnp.zeros_like(vn_ref)

        def step(t, state):
            c, r0, w0, s0 = _scan_chunk(t, rev, N)
            rows = pl.ds(r0, C)
            st_ref[c] = state
            vnew = uw_ref[rows, 0:DK] - _dotp(uw_ref[rows, DK:2 * DK], state, "nn", 1)
            vn_ref[pl.ds(w0, C), :] = vnew
            o_ref[rows, :] = _dotp(qd_ref[rows, :], state, "nn", 1) + _dotp(qk_ref[rows, :], vn_ref[...], "nn", 1)
            return state * gl_ref[pl.ds(r0, 1), :] + _dotp(kd_ref[rows, :], vnew, "tn", 1)

        lax.fori_loop(0, N, step, jnp.zeros((DK, DK), F32))

    def seq(w):
        return pl.BlockSpec((None, S, w), lambda b, r: (r, b, 0))

    return pl.pallas_call(
        body, name=name, grid=(B, R),
        in_specs=[seq(2 * DK), seq(DN_SUPER), seq(DK), seq(DK), seq(DK)],
        out_specs=[seq(DK), pl.BlockSpec((None, None, N, DK, DK), lambda b, r: (b, r, 0, 0, 0))],
        out_shape=[jax.ShapeDtypeStruct((R, T, DK), F32), jax.ShapeDtypeStruct((B, R, N, DK, DK), F32)],
        scratch_shapes=[pltpu.VMEM((DN_SUPER, DK), F32)], compiler_params=_params(),
    )(uw, qk, qd, kd, gl)


def _dn_scan_bwd(name, uw, qk, qd, kd, gl, st, do, B):
    R, T, _ = uw.shape
    S = T // B
    N = S // DN_CHUNK
    C, DK = DN_CHUNK, DN_DIM

    def body(uw_ref, qk_ref, qd_ref, kd_ref, gl_ref, st_ref, do_ref, duw_ref, dqk_ref, dqd_ref, dkd_ref, dgl_ref,
             vn_ref, tmp_ref):
        rev = pl.program_id(1) >= DN_HEADS
        vn_ref[...] = jnp.zeros_like(vn_ref)
        dgl_ref[...] = jnp.zeros_like(dgl_ref)

        def step(t, dstate):
            c, r0, w0, s0 = _scan_chunk(N - 1 - t, rev, N)
            rows = pl.ds(r0, C)
            state = st_ref[c]
            w = uw_ref[rows, DK:2 * DK]
            vnew = uw_ref[rows, 0:DK] - _dotp(w, state, "nn", 1)
            vn_ref[pl.ds(w0, C), :] = vnew
            do_c = do_ref[rows, :]
            tmp_ref[...] = _dotp(qk_ref[rows, :], do_c, "tn", 1)
            dvn = tmp_ref[pl.ds(w0, C), :] + _dotp(kd_ref[rows, :], dstate, "nn", 1)
            dqk_ref[rows, :] = _dotp(do_c, vn_ref[...], "nt", 1)
            dqd_ref[rows, :] = _dotp(do_c, state, "nt", 1)
            dkd_ref[rows, :] = _dotp(vnew, dstate, "nt", 1)
            dgl_ref[pl.ds(r0, 1), :] = jnp.sum(state * dstate, axis=0, keepdims=True)
            duw_ref[rows, 0:DK] = dvn
            duw_ref[rows, DK:2 * DK] = -_dotp(dvn, state, "nt", 1)
            return (_dotp(qd_ref[rows, :], do_c, "tn", 1) + dstate * gl_ref[pl.ds(r0, 1), :]
                    - _dotp(w, dvn, "tn", 1))

        lax.fori_loop(0, N, step, jnp.zeros((DK, DK), F32))

    def seq(w):
        return pl.BlockSpec((None, S, w), lambda b, r: (r, b, 0))

    return pl.pallas_call(
        body, name=name, grid=(B, R),
        in_specs=[seq(2 * DK), seq(DN_SUPER), seq(DK), seq(DK), seq(DK),
                  pl.BlockSpec((None, None, N, DK, DK), lambda b, r: (b, r, 0, 0, 0)),
                  pl.BlockSpec((None, S, DK), lambda b, r: (r % DN_HEADS, b, 0))],
        out_specs=[seq(2 * DK), seq(DN_SUPER), seq(DK), seq(DK), seq(DK)],
        out_shape=[jax.ShapeDtypeStruct((R, T, 2 * DK), F32), jax.ShapeDtypeStruct((R, T, DN_SUPER), F32),
                   jax.ShapeDtypeStruct((R, T, DK), F32), jax.ShapeDtypeStruct((R, T, DK), F32),
                   jax.ShapeDtypeStruct((R, T, DK), F32)],
        scratch_shapes=[pltpu.VMEM((DN_SUPER, DK), F32), pltpu.VMEM((DN_SUPER, DK), F32)],
        compiler_params=_params(),
    )(uw, qk, qd, kd, gl, st, do)


def _loss_fwd_bwd(name, x, gain, target):
    T, D = x.shape
    tm = _pick(T, 512, 8)

    def body(x_ref, g_ref, t_ref, loss_ref, dx_ref, dg_ref):
        i = pl.program_id(0)

        def f(xv, gv):
            e = _rms(xv, gv) - t_ref[...]
            return 0.5 * jnp.sum(jnp.mean(e * e, axis=-1, keepdims=True))

        val, (dx, dg) = jax.value_and_grad(f, argnums=(0, 1))(x_ref[...], g_ref[...])
        dx_ref[...] = dx
        part = jnp.full(loss_ref.shape, val, F32)

        @pl.when(i == 0)
        def _():
            dg_ref[...] = dg
            loss_ref[...] = part

        @pl.when(i != 0)
        def _():
            dg_ref[...] += dg
            loss_ref[...] += part

    return pl.pallas_call(
        body, name=name, grid=(T // tm,),
        in_specs=[pl.BlockSpec((tm, D), lambda i: (i, 0)), pl.BlockSpec((1, D), lambda i: (0, 0)),
                  pl.BlockSpec((tm, D), lambda i: (i, 0))],
        out_specs=[pl.BlockSpec((8, 128), lambda i: (0, 0)), pl.BlockSpec((tm, D), lambda i: (i, 0)),
                   pl.BlockSpec((1, D), lambda i: (0, 0))],
        out_shape=[jax.ShapeDtypeStruct((8, 128), F32), jax.ShapeDtypeStruct((T, D), F32),
                   jax.ShapeDtypeStruct((1, D), F32)],
        compiler_params=_params(),
    )(x, gain, target)


class _Cols:
    def __init__(self, D):
        assert D % 256 == 0
        self.gate = 0
        self.da = 3 * D
        self.qkv = self.da + 3 * DA_WIDTH
        self.z = self.qkv + 3 * DN_WIDTH
        self.pool = self.z + DN_WIDTH
        self.ba = self.pool + POOL_WIDTH
        self.total = self.ba + BA_PAD


def _rope_tables(S):
    half = DA_DIM // 2
    inv_freq = ROPE_THETA ** (-jnp.arange(half, dtype=F32) / half)
    ang = jnp.arange(S, dtype=F32)[:, None] * inv_freq[None, :]
    reps = DA_WIDTH // DA_DIM
    cos = jnp.tile(jnp.concatenate([jnp.cos(ang), jnp.cos(ang)], axis=1), (1, reps))
    sin = jnp.tile(jnp.concatenate([jnp.sin(ang), jnp.sin(ang)], axis=1), (1, reps))
    return cos, sin


def _to_strided(t, B, dil):
    T, w = t.shape
    L = T // B // dil
    return t.reshape(B, L, dil, w).transpose(0, 2, 1, 3).reshape(B * dil, L, w)


def _from_strided(t, B, dil):
    NS, L, w = t.shape
    return t.reshape(B, dil, L, w).transpose(0, 2, 1, 3).reshape(B * dil * L, w)


def _mixer(l, x1, w, B):
    T, D = x1.shape
    S = T // B
    c = _Cols(D)
    tm = _pick(S, 512, 8)
    nmS = S // tm
    n = f"l{l}_"

    norm_ins = [_In(x1, (tm, D), lambda i: (i, 0)), _In(w["mix_norm"], (1, D), lambda i: (0, 0), acc=True)]
    norm_outs = [_Out((T, D), BF16, (tm, D), lambda i: (i, 0))]
    (h,) = _tile_fwd(n + "norm", _norm_f, (T // tm,), norm_ins, norm_outs)
    P = _mm(n + "proj", h, w["w_cat"], "nn", tn=896, tk=1024)
    baT = P[:, c.ba:c.ba + 16].T

    cb = c.qkv // DN_DIM
    conv_ins = [_In(P, (S, DN_DIM), lambda j, b: (b, cb + j), g=((T, 3 * DN_WIDTH), lambda j, b: (b, j)), gdtype=BF16),
                _In(w["dn_conv"], (DN_CONV, DN_DIM), lambda j, b: (0, j), acc=True)]
    conv_outs = [_Out((T, 3 * DN_WIDTH), F32, (S, DN_DIM), lambda j, b: (b, j))]
    conv_grid = (3 * DN_HEADS, B)
    (qkvc,) = _tile_fwd(n + "dnconv", _dn_conv_f, conv_grid, conv_ins, conv_outs)

    tg = _pick(T, 2048, 128)
    gate_ins = [_In(baT, (8, tg), lambda i: (0, i)), _In(baT, (8, tg), lambda i: (1, i)),
                _In(w["dn_a_log"], (8, 1), lambda i: (0, 0), acc=True),
                _In(w["dn_dt_bias"], (8, 1), lambda i: (0, 0), acc=True)]
    gate_ins[0].g = ((8, T), lambda i: (0, i))
    gate_ins[1].g = ((8, T), lambda i: (0, i))
    gate_outs = [_Out((8, T), F32, (8, tg), lambda i: (0, i))] * 2
    beta, gdec = _tile_fwd(n + "dngate", _dn_gate_f, (T // tg,), gate_ins, gate_outs)

    NSC = T // DN_SUPER
    beta4 = beta.reshape(8, NSC, 1, DN_SUPER)
    gdec4 = gdec.reshape(8, NSC, 1, DN_SUPER)
    R = 2 * DN_HEADS

    def qkv_in(off):
        return _In(qkvc, (DN_SUPER, DN_DIM), lambda hh, m, dd: (m, off + hh), acc=True,
                   g=((T, DN_WIDTH), lambda hh, m, dd: (m, hh)))

    def row_in(a):
        return _In(a, (None, None, 1, DN_SUPER), lambda hh, m, dd: (dd * DN_HEADS + hh, m, 0, 0))

    def chain_out(wd):
        return _Out((R, T, wd), F32, (None, DN_SUPER, wd), lambda hh, m, dd: (dd * DN_HEADS + hh, m, 0))

    prep_ins = [qkv_in(0), qkv_in(DN_HEADS), qkv_in(2 * DN_HEADS), row_in(beta4), row_in(gdec4)]
    prep_outs = [chain_out(2 * DN_DIM), chain_out(DN_SUPER), chain_out(DN_DIM), chain_out(DN_DIM), chain_out(DN_DIM)]
    prep_grid = (DN_HEADS, NSC, 2)
    uw, qk, qd, kd, gl = _tile_fwd(n + "dnprep", _dn_prep_f, prep_grid, prep_ins, prep_outs)
    o_dn, states = _dn_scan_fwd(n + "dnscan", uw, qk, qd, kd, gl, B)

    zb = c.z // DN_DIM
    out_ins = [_In(o_dn, (None, S, DN_DIM), lambda b, hh: (hh, b, 0)),
               _In(o_dn, (None, S, DN_DIM), lambda b, hh: (DN_HEADS + hh, b, 0)),
               _In(P, (S, DN_DIM), lambda b, hh: (b, zb + hh), g=((T, DN_WIDTH), lambda b, hh: (b, hh)), gdtype=BF16),
               _In(w["dn_out_norm"], (1, DN_DIM), lambda b, hh: (0, 0), acc=True)]
    out_ins[0].g = ((DN_HEADS, T, DN_DIM), lambda b, hh: (hh, b, 0))
    out_ins[1].g = ((DN_HEADS, T, DN_DIM), lambda b, hh: (hh, b, 0))
    out_outs = [_Out((T, DN_WIDTH), BF16, (S, DN_DIM), lambda b, hh: (b, hh))]
    (ya_in,) = _tile_fwd(n + "dnout", _dn_out_f, (B, DN_HEADS), out_ins, out_outs)

    pb = c.pool // POOL_DIM
    pool_ins = [_In(P, (S, POOL_DIM), lambda gi, b: (b, pb + gi), g=((T, POOL_WIDTH), lambda gi, b: (b, gi)), gdtype=BF16),
                _In(w["pool_w"], (None, POOL_DIM, POOL_DIM), lambda gi, b: (gi, 0, 0), acc=True),
                _In(w["pool_scale"], (None, 1, POOL_DIM), lambda gi, b: (gi, 0, 0), acc=True)]
    pool_outs = [_Out((T, POOL_WIDTH), BF16, (S, POOL_DIM), lambda gi, b: (b, gi))]
    (yb_in,) = _tile_fwd(n + "pool", _pool_f, (POOL_GROUPS, B), pool_ins, pool_outs)

    cos, sin = _rope_tables(S)
    db = c.da // DA_WIDTH

    def da_in(k):
        return _In(P, (tm, DA_WIDTH), lambda i: (i, db + k), g=((T, DA_WIDTH), lambda i: (i, 0)), gdtype=BF16)

    rope_ins = [da_in(0), da_in(1), da_in(2),
                _In(cos, (tm, DA_WIDTH), lambda i: (i % nmS, 0), kind="c"),
                _In(sin, (tm, DA_WIDTH), lambda i: (i % nmS, 0), kind="c")]
    rope_outs = [_Out((T, DA_WIDTH), BF16, (tm, DA_WIDTH), lambda i: (i, 0))] * 3
    qr, kr, vr = _tile_fwd(n + "rope", _rope_f, (T // tm,), rope_ins, rope_outs)
    strided = []
    o_g, l_g = [], []
    for gi, dil in enumerate(DA_DILATIONS):
        cs_ = slice(gi * DA_OUT, (gi + 1) * DA_OUT)
        qs, ks, vs = (_to_strided(t[:, cs_], B, dil) for t in (qr, kr, vr))
        strided.append((qs, ks, vs))
        o, lse = _attn_fwd(n + f"attn{gi}", qs, ks, vs)
        o_g.append(_from_strided(o, B, dil))
        l_g.append(_from_strided(lse, B, dil))
    mrg_ins = [_In(a, (tm, DA_OUT), lambda i: (i, 0)) for a in o_g + l_g]
    mrg_outs = [_Out((T, DA_OUT), BF16, (tm, DA_OUT), lambda i: (i, 0))]
    (yc_in,) = _tile_fwd(n + "merge", _merge_f, (T // tm,), mrg_ins, mrg_outs)

    ya = _mm(n + "pa", ya_in, w["w_proj_a"], "nn")
    yb = _mm(n + "pb", yb_in, w["w_proj_b"], "nn")
    yc = _mm(n + "pc", yc_in, w["w_proj_c"], "nn")

    def gcol(k):
        return _In(P, (tm, D), lambda i: (i, k), g=((T, D), lambda i: (i, 0)), gdtype=BF16)

    def yin(a):
        return _In(a, (tm, D), lambda i: (i, 0), gdtype=BF16)

    def bin_(k):
        return _In(w["b_gate"][k:k + 1], (1, D), lambda i: (0, 0), acc=True)

    gm_ins = [gcol(0), gcol(1), gcol(2), yin(ya), yin(yb), yin(yc), bin_(0), bin_(1), bin_(2)]
    gm_outs = [_Out((T, D), BF16, (tm, D), lambda i: (i, 0))]
    (merged,) = _tile_fwd(n + "gates", _gate_f, (T // tm,), gm_ins, gm_outs)
    x2 = _mm(n + "out", merged, w["w_out"], "nn", add=x1)

    def backward(dx2):
        return _mixer_bwd(dx2, **{k: v for k, v in locals_.items() if k in _MIXER_BWD_NEEDS})

    locals_ = dict(locals())
    return x2, backward


_MIXER_BWD_NEEDS = ("n", "B", "T", "D", "tm", "w", "h", "merged", "gm_ins", "gm_outs", "ya_in", "yb_in", "yc_in",
                    "mrg_ins", "mrg_outs", "strided", "rope_ins", "rope_outs", "pool_ins", "pool_outs", "out_ins",
                    "out_outs", "uw", "qk", "qd", "kd", "gl", "states", "prep_grid", "prep_ins", "prep_outs", "tg",
                    "gate_ins", "gate_outs", "conv_grid", "conv_ins", "conv_outs", "norm_ins", "norm_outs")


def _mixer_bwd(dx2, *, n, B, T, D, tm, w, h, merged, gm_ins, gm_outs, ya_in, yb_in, yc_in, mrg_ins, mrg_outs, strided,
               rope_ins, rope_outs, pool_ins, pool_outs, out_ins, out_outs, uw, qk, qd, kd, gl, states, prep_grid,
               prep_ins, prep_outs, tg, gate_ins, gate_outs, conv_grid, conv_ins, conv_outs, norm_ins, norm_outs):
    g = {}
    dmerged = _mm(n + "d_merged", dx2, w["w_out"], "nt")
    g["w_out"] = _mm(n + "d_wout", merged, dx2, "tn")
    dg0, dg1, dg2, dya, dyb, dyc, db0, db1, db2 = _tile_bwd(
        n + "gates_b", _gate_f, (T // tm,), gm_ins, gm_outs, [dmerged], acc_from=0)
    g["b_gate"] = jnp.concatenate([db0, db1, db2], axis=0)
    dya_in = _mm(n + "d_pa", dya, w["w_proj_a"], "nt")
    dyb_in = _mm(n + "d_pb", dyb, w["w_proj_b"], "nt")
    dyc_in = _mm(n + "d_pc", dyc, w["w_proj_c"], "nt")
    g["w_proj_a"] = _mm(n + "d_wpa", ya_in, dya, "tn")
    g["w_proj_b"] = _mm(n + "d_wpb", yb_in, dyb, "tn")
    g["w_proj_c"] = _mm(n + "d_wpc", yc_in, dyc, "tn")

    dmrg = _tile_bwd(n + "merge_b", _merge_f, (T // tm,), mrg_ins, mrg_outs, [dyc_in])
    dq_parts, dk_parts, dv_parts = [], [], []
    for gi, dil in enumerate(DA_DILATIONS):
        qs, ks, vs = strided[gi]
        do_s = _to_strided(dmrg[gi], B, dil)
        dl_s = _to_strided(dmrg[DA_GROUPS + gi], B, dil)
        dq, dk, dv = _attn_bwd(n + f"attn{gi}_b", qs, ks, vs, do_s, dl_s)
        dq_parts.append(_from_strided(dq, B, dil))
        dk_parts.append(_from_strided(dk, B, dil))
        dv_parts.append(_from_strided(dv, B, dil))
    dqr, dkr, dvr = (jnp.concatenate(p, axis=1) for p in (dq_parts, dk_parts, dv_parts))
    dPq, dPk, dPv = _tile_bwd(n + "rope_b", _rope_f, (T // tm,), rope_ins, rope_outs, [dqr, dkr, dvr])

    dPpool, g["pool_w"], g["pool_scale"] = _tile_bwd(
        n + "pool_b", _pool_f, (POOL_GROUPS, B), pool_ins, pool_outs, [dyb_in], acc_from=1)

    dof, dob, dPz, g["dn_out_norm"] = _tile_bwd(
        n + "dnout_b", _dn_out_f, (B, DN_HEADS), out_ins, out_outs, [dya_in], acc_from=0)
    del dob
    duw, dqk, dqd, dkd, dgl = _dn_scan_bwd(n + "dnscan_b", uw, qk, qd, kd, gl, states, dof, B)
    dq_, dk_, dv_, dbeta4, dgdec4 = _tile_bwd(
        n + "dnprep_b", _dn_prep_f, prep_grid, prep_ins, prep_outs, [duw, dqk, dqd, dkd, dgl], acc_from=2)
    dqkvc = jnp.concatenate([dq_, dk_, dv_], axis=1)
    dbraw, daraw, g["dn_a_log"], g["dn_dt_bias"] = _tile_bwd(
        n + "dngate_b", _dn_gate_f, (T // tg,), gate_ins, gate_outs,
        [dbeta4.reshape(8, T), dgdec4.reshape(8, T)], acc_from=0)
    dPqkv, g["dn_conv"] = _tile_bwd(n + "dnconv_b", _dn_conv_f, conv_grid, conv_ins, conv_outs, [dqkvc], acc_from=1)
    dba = jnp.concatenate([dbraw, daraw], axis=0).T.astype(BF16)
    dba = jnp.pad(dba, ((0, 0), (0, BA_PAD - 16)))
    dP = jnp.concatenate([dg0, dg1, dg2, dPq, dPk, dPv, dPqkv, dPz, dPpool, dba], axis=1)
    dh = _mm(n + "d_h", dP, w["w_cat"], "nt", tk=896)
    g["w_cat"] = _mm(n + "d_wcat", h, dP, "tn", tn=896)
    dx1, g["mix_norm"] = _tile_bwd(n + "norm_b", _norm_f, (T // tm,), norm_ins, norm_outs, [dh], acc_from=0,
                                   addends={0: dx2})
    return dx1, g


def _layer_weights(full, l, D):
    c = _Cols(D)
    w_in = full["w_in"][l]
    o_z, o_ba, o_pool, o_da = 3 * DN_WIDTH, 4 * DN_WIDTH, 4 * DN_WIDTH + 16, 4 * DN_WIDTH + 16 + POOL_WIDTH
    w_cat = jnp.concatenate(
        [full["w_gate"][l], w_in[:, o_da:], w_in[:, :o_z], w_in[:, o_z:o_ba], w_in[:, o_pool:o_da], w_in[:, o_ba:o_pool],
         jnp.zeros((D, BA_PAD - 16), w_in.dtype)], axis=1).astype(BF16)
    assert w_cat.shape[1] == c.total
    w = {k: full[k][l].astype(BF16) for k in ("ffn1_w_gate", "ffn1_w_up", "ffn1_w_down", "ffn2_w_gate", "ffn2_w_up",
                                              "ffn2_w_down", "w_proj_a", "w_proj_b", "w_proj_c", "w_out")}
    w["w_cat"] = w_cat
    w["ffn1_norm"] = full["ffn1_norm"][l][None].astype(F32)
    w["ffn2_norm"] = full["ffn2_norm"][l][None].astype(F32)
    w["mix_norm"] = full["mix_norm"][l][None].astype(F32)
    w["dn_conv"] = full["dn_conv"][l].astype(F32)
    w["dn_a_log"] = full["dn_a_log"][l].reshape(2 * DN_HEADS, 1).astype(F32)
    w["dn_dt_bias"] = full["dn_dt_bias"][l].reshape(2 * DN_HEADS, 1).astype(F32)
    w["dn_out_norm"] = full["dn_out_norm"][l][None].astype(F32)
    w["pool_w"] = full["pool_w"][l].astype(F32)
    w["pool_scale"] = full["pool_scale"][l].reshape(POOL_GROUPS, 1, POOL_DIM).astype(F32)
    w["b_gate"] = full["b_gate"][l].reshape(3, D).astype(F32)
    return w


def _layer_grads(g, D):
    c = _Cols(D)
    gc = g.pop("w_cat")
    out = dict(g)
    out["w_gate"] = gc[:, :c.da]
    out["w_in"] = jnp.concatenate([gc[:, c.qkv:c.pool], gc[:, c.ba:c.ba + 16], gc[:, c.pool:c.ba], gc[:, c.da:c.qkv]],
                                  axis=1)
    for k in ("ffn1_norm", "ffn2_norm", "mix_norm", "dn_out_norm"):
        out[k] = g[k][0]
    out["dn_a_log"] = g["dn_a_log"].reshape(2, DN_HEADS)
    out["dn_dt_bias"] = g["dn_dt_bias"].reshape(2, DN_HEADS)
    out["pool_scale"] = g["pool_scale"].reshape(POOL_WIDTH)
    out["b_gate"] = g["b_gate"].reshape(3 * D)
    return out


def _local_step(x, target, full):
    B, S, D = x.shape
    T = B * S
    depth = full["w_in"].shape[0]
    xs = x.reshape(T, D)
    tape = []
    for l in range(depth):
        w = _layer_weights(full, l, D)
        x1 = _ffn_fwd(f"l{l}_ffn1", xs, w["ffn1_norm"], w["ffn1_w_gate"], w["ffn1_w_up"], w["ffn1_w_down"])
        x2, mixer_bwd = _mixer(l, x1, w, B)
        x3 = _ffn_fwd(f"l{l}_ffn2", x2, w["ffn2_norm"], w["ffn2_w_gate"], w["ffn2_w_up"], w["ffn2_w_down"])
        tape.append((w, xs, mixer_bwd, x2))
        xs = x3
    loss8, dx, dfinal = _loss_fwd_bwd("loss", xs, full["final_norm"][None].astype(F32), target.reshape(T, D))
    per_layer = [None] * depth
    for l in reversed(range(depth)):
        w, x0, mixer_bwd, x2 = tape[l]
        dx, dn2, dwg2, dwu2, dwd2 = _ffn_bwd(f"l{l}_ffn2b", x2, w["ffn2_norm"], w["ffn2_w_gate"], w["ffn2_w_up"],
                                             w["ffn2_w_down"], dx)
        dx, g = mixer_bwd(dx)
        dx, dn1, dwg1, dwu1, dwd1 = _ffn_bwd(f"l{l}_ffn1b", x0, w["ffn1_norm"], w["ffn1_w_gate"], w["ffn1_w_up"],
                                             w["ffn1_w_down"], dx)
        g.update(ffn1_norm=dn1, ffn1_w_gate=dwg1, ffn1_w_up=dwu1, ffn1_w_down=dwd1,
                 ffn2_norm=dn2, ffn2_w_gate=dwg2, ffn2_w_up=dwu2, ffn2_w_down=dwd2)
        per_layer[l] = _layer_grads(g, D)
    grads = {k: jnp.stack([pg[k] for pg in per_layer]) for k in per_layer[0]}
    grads["final_norm"] = dfinal[0]
    return loss8[0, 0], dx.reshape(B, S, D), grads


def _mesh_position():
    mx, my, mc = lax.axis_index("x"), lax.axis_index("y"), lax.axis_index("c")
    return mx, my, mc, 4 * mx + 2 * my + mc


def _peers(mx, my, mc):
    out = []
    for k in range(1, N_DEV):
        px, py, pc = mx ^ ((k >> 2) & 1), my ^ ((k >> 1) & 1), mc ^ (k & 1)
        out.append(((px, py, pc), 4 * px + 2 * py + pc))
    return out


_ANY = pl.BlockSpec(memory_space=pl.ANY)


def _all_gather(name, x):
    def body(x_ref, o_ref, send_sems, recv_sems, local_sem):
        mx, my, mc, me = _mesh_position()
        mine = pltpu.make_async_copy(x_ref, o_ref.at[me], local_sem)
        mine.start()
        copies = []
        for k, (peer, _) in enumerate(_peers(mx, my, mc)):
            cp = pltpu.make_async_remote_copy(src_ref=x_ref, dst_ref=o_ref.at[me], send_sem=send_sems.at[k],
                                              recv_sem=recv_sems.at[k], device_id=peer,
                                              device_id_type=pl.DeviceIdType.MESH)
            cp.start()
            copies.append(cp)
        for cp in copies:
            cp.wait_send()
            cp.wait_recv()
        mine.wait()

    return pl.pallas_call(
        body, name=name, in_specs=[_ANY], out_specs=_ANY,
        out_shape=jax.ShapeDtypeStruct((N_DEV,) + x.shape, x.dtype),
        scratch_shapes=[pltpu.SemaphoreType.DMA((N_DEV - 1,)), pltpu.SemaphoreType.DMA((N_DEV - 1,)),
                        pltpu.SemaphoreType.DMA],
    )(x)


def _exchange_grads(name, gs, gr):
    def body(gs_ref, gr_ref, os_ref, or_ref, ssend, srecv, rsend, rrecv, local_sems):
        mx, my, mc, me = _mesh_position()
        own_s = pltpu.make_async_copy(gs_ref.at[me], os_ref.at[me], local_sems.at[0])
        own_r = pltpu.make_async_copy(gr_ref, or_ref.at[me], local_sems.at[1])
        own_s.start()
        own_r.start()
        copies = []
        for k, (peer, pid) in enumerate(_peers(mx, my, mc)):
            for src, dst, s, r in ((gs_ref.at[pid], os_ref.at[me], ssend, srecv), (gr_ref, or_ref.at[me], rsend, rrecv)):
                cp = pltpu.make_async_remote_copy(src_ref=src, dst_ref=dst, send_sem=s.at[k], recv_sem=r.at[k],
                                                  device_id=peer, device_id_type=pl.DeviceIdType.MESH)
                cp.start()
                copies.append(cp)
        for cp in copies:
            cp.wait_send()
            cp.wait_recv()
        own_s.wait()
        own_r.wait()

    sems = pltpu.SemaphoreType.DMA((N_DEV - 1,))
    return pl.pallas_call(
        body, name=name, in_specs=[_ANY, _ANY], out_specs=[_ANY, _ANY],
        out_shape=[jax.ShapeDtypeStruct(gs.shape, gs.dtype), jax.ShapeDtypeStruct((N_DEV,) + gr.shape, gr.dtype)],
        scratch_shapes=[sems, sems, sems, sems, pltpu.SemaphoreType.DMA((2,))],
    )(gs, gr)


def _sum_blocks(name, a):
    _, R, L = a.shape
    tr = _pick(R, 256, 16)

    def body(a_ref, o_ref):
        s = a_ref[0].astype(F32)
        for d in range(1, N_DEV):
            s = s + a_ref[d].astype(F32)
        o_ref[...] = s

    return pl.pallas_call(
        body, name=name, grid=(R // tr,), in_specs=[pl.BlockSpec((N_DEV, tr, L), lambda i: (0, i, 0))],
        out_specs=pl.BlockSpec((tr, L), lambda i: (i, 0)), out_shape=jax.ShapeDtypeStruct((R, L), F32),
        compiler_params=_params(),
    )(a)


def _adamw(name, w, g, m, v):
    shape = w.shape
    cols = shape[-1] if len(shape) > 1 else shape[0]
    w2, g2, m2, v2 = (t.reshape(-1, cols) for t in (w, g, m, v))
    rows = w2.shape[0]
    tr = _pick(rows, 512, 8) if rows > 1024 else rows
    c1 = 1.0 - ADAM_B1 ** ADAM_STEP
    c2 = 1.0 - ADAM_B2 ** ADAM_STEP

    def body(w_ref, g_ref, m_ref, v_ref, d_ref, nm_ref, nv_ref):
        gv = g_ref[...]
        nm = ADAM_B1 * m_ref[...] + (1.0 - ADAM_B1) * gv
        nv = ADAM_B2 * v_ref[...] + (1.0 - ADAM_B2) * (gv * gv)
        d_ref[...] = -ADAM_LR * ((nm / c1) / (jnp.sqrt(nv / c2) + ADAM_EPS) + ADAM_WD * w_ref[...])
        nm_ref[...] = nm
        nv_ref[...] = nv

    spec = pl.BlockSpec((tr, cols), lambda i: (i, 0))
    outs = pl.pallas_call(
        body, name=name, grid=(rows // tr,), in_specs=[spec] * 4, out_specs=[spec] * 3,
        out_shape=[jax.ShapeDtypeStruct((rows, cols), F32)] * 3, compiler_params=_params(),
    )(w2, g2, m2, v2)
    return tuple(o.reshape(shape) for o in outs)


def _pad_rows(flat, mult):
    n = flat.shape[-1]
    rows = -(-n // (LANES * mult)) * mult
    pad = rows * LANES - n
    flat = jnp.pad(flat, [(0, 0)] * (flat.ndim - 1) + [(0, pad)])
    return flat.reshape(flat.shape[:-1] + (rows, LANES))


def kernel(x, ffn1_norm, ffn1_w_gate, ffn1_w_up, ffn1_w_down, mix_norm, w_in, dn_conv, dn_a_log, dn_dt_bias, dn_out_norm, pool_w, pool_scale, w_proj_a, w_proj_b, w_proj_c, w_gate, b_gate, w_out, ffn2_norm, ffn2_w_gate, ffn2_w_up, ffn2_w_down, final_norm, loss_target, m_ffn1_norm, m_ffn1_w_gate, m_ffn1_w_up, m_ffn1_w_down, m_mix_norm, m_w_in, m_dn_conv, m_dn_a_log, m_dn_dt_bias, m_dn_out_norm, m_pool_w, m_pool_scale, m_w_proj_a, m_w_proj_b, m_w_proj_c, m_w_gate, m_b_gate, m_w_out, m_ffn2_norm, m_ffn2_w_gate, m_ffn2_w_up, m_ffn2_w_down, m_final_norm, v_ffn1_norm, v_ffn1_w_gate, v_ffn1_w_up, v_ffn1_w_down, v_mix_norm, v_w_in, v_dn_conv, v_dn_a_log, v_dn_dt_bias, v_dn_out_norm, v_pool_w, v_pool_scale, v_w_proj_a, v_w_proj_b, v_w_proj_c, v_w_gate, v_b_gate, v_w_out, v_ffn2_norm, v_ffn2_w_gate, v_ffn2_w_up, v_ffn2_w_down, v_final_norm):
    args = locals()
    wts = {k: args[k] for k in WEIGHTS}
    ms = {k: args["m_" + k] for k in WEIGHTS}
    vs = {k: args["v_" + k] for k in WEIGHTS}

    sizes = [math.prod(wts[k].shape) for k in SHARDED]
    send = _pad_rows(jnp.concatenate([wts[k].astype(BF16).reshape(-1) for k in SHARDED]), 16)
    gathered = _all_gather("gather_weights", send).reshape(N_DEV, -1)
    full = {k: wts[k] for k in REPLICATED}
    off = 0
    for k, n in zip(SHARDED, sizes):
        ax = SHARD_AXIS[k]
        shp = wts[k].shape
        seg = jnp.moveaxis(gathered[:, off:off + n].reshape((N_DEV,) + shp), 0, ax)
        full[k] = seg.reshape(shp[:ax] + (N_DEV * shp[ax],) + shp[ax + 1:])
        off += n

    loss_local, dx, grads = _local_step(x, loss_target, full)
    loss = lax.psum(loss_local, ("x", "y", "c"))

    parts = []
    for k in SHARDED:
        ax = SHARD_AXIS[k]
        shp = grads[k].shape
        t = grads[k].reshape(shp[:ax] + (N_DEV, shp[ax] // N_DEV) + shp[ax + 1:])
        parts.append(jnp.moveaxis(t, ax, 0).reshape(N_DEV, -1).astype(BF16))
    gs = _pad_rows(jnp.concatenate(parts, axis=1), 16)
    rsizes = [math.prod(wts[k].shape) for k in REPLICATED]
    gr = _pad_rows(jnp.concatenate([grads[k].astype(F32).reshape(-1) for k in REPLICATED]), 8)
    got_s, got_r = _exchange_grads("exchange_grads", gs, gr)
    gsum_s = _sum_blocks("sum_sharded", got_s).reshape(-1)
    gsum_r = _sum_blocks("sum_replicated", got_r).reshape(-1)
    g_final = {}
    off = 0
    for k, n in zip(SHARDED, sizes):
        g_final[k] = gsum_s[off:off + n].reshape(wts[k].shape)
        off += n
    off = 0
    for k, n in zip(REPLICATED, rsizes):
        g_final[k] = gsum_r[off:off + n].reshape(wts[k].shape)
        off += n

    deltas, new_m, new_v = {}, {}, {}
    for k in WEIGHTS:
        deltas[k], new_m[k], new_v[k] = _adamw("adamw_" + k, wts[k], g_final[k], ms[k], vs[k])
    return (loss, dx, *[g_final[k] for k in WEIGHTS], *[deltas[k] for k in WEIGHTS], *[new_m[k] for k in WEIGHTS],
            *[new_v[k] for k in WEIGHTS])
```

```python
import functools
import math

import jax
import jax.numpy as jnp
from jax import lax
from jax.experimental import pallas as pl
from jax.experimental.pallas import tpu as pltpu

F32 = jnp.float32
BF16 = jnp.bfloat16

N_DEV = 8
RMS_EPS = 1e-6
L2_EPS = 1e-6
DN_HEADS = 4
DN_DIM = 128
DN_WIDTH = DN_HEADS * DN_DIM
DN_CONV = 5
DN_CHUNK = 64
DN_SUPER = 256
POOL_GROUPS = 4
POOL_DIM = 128
POOL_WIDTH = POOL_GROUPS * POOL_DIM
POOL_MAX_HALF = 8
DA_GROUPS = 3
DA_HEADS = 4
DA_DIM = 64
DA_WIDTH = DA_GROUPS * DA_HEADS * DA_DIM
DA_OUT = DA_HEADS * DA_DIM
DA_DILATIONS = (1, 4, 16)
DA_RADIUS = 64
DA_TQ = 128
ROPE_THETA = 10000.0
MASK_VALUE = -1e30
BA_PAD = 128

ADAM_LR = 0.001
ADAM_B1 = 0.9
ADAM_B2 = 0.999
ADAM_EPS = 1e-08
ADAM_WD = 0.01
ADAM_STEP = 10

VMEM_LIMIT_V7X = 56 * 1024 * 1024
LANES = 1024

SHARDED = ("ffn1_w_gate", "ffn1_w_up", "ffn1_w_down", "w_in", "dn_conv", "w_proj_a", "w_proj_b", "w_proj_c",
           "w_gate", "w_out", "ffn2_w_gate", "ffn2_w_up", "ffn2_w_down")
SHARD_AXIS = {"ffn1_w_gate": 2, "ffn1_w_up": 2, "ffn1_w_down": 1, "w_in": 2, "dn_conv": 2, "w_proj_a": 2,
              "w_proj_b": 2, "w_proj_c": 2, "w_gate": 2, "w_out": 1, "ffn2_w_gate": 2, "ffn2_w_up": 2,
              "ffn2_w_down": 1}
REPLICATED = ("ffn1_norm", "mix_norm", "dn_a_log", "dn_dt_bias", "dn_out_norm", "pool_w", "pool_scale", "b_gate",
              "ffn2_norm", "final_norm")
WEIGHTS = ("ffn1_norm", "ffn1_w_gate", "ffn1_w_up", "ffn1_w_down", "mix_norm", "w_in", "dn_conv", "dn_a_log",
           "dn_dt_bias", "dn_out_norm", "pool_w", "pool_scale", "w_proj_a", "w_proj_b", "w_proj_c", "w_gate",
           "b_gate", "w_out", "ffn2_norm", "ffn2_w_gate", "ffn2_w_up", "ffn2_w_down", "final_norm")


def _params(**kw):
    return pltpu.CompilerParams(vmem_limit_bytes=VMEM_LIMIT_V7X, **kw)


def _pick(n, target, align):
    best = None
    t = align
    while t <= min(n, target):
        if n % t == 0:
            best = t
        t += align
    return best if best is not None else n


_DIMS = {"nn": (((1,), (0,)), ((), ())), "nt": (((1,), (1,)), ((), ())), "tn": (((0,), (0,)), ((), ()))}


def _dg(a, b, mode):
    return lax.dot_general(a, b, _DIMS[mode], preferred_element_type=F32)


def _split2(a):
    hi = a.astype(BF16)
    lo = (a - hi.astype(F32)).astype(BF16)
    return hi, lo


def _dotp(a, b, mode, passes):
    if passes == 1:
        return _dg(a.astype(BF16), b.astype(BF16), mode)
    ah, al = _split2(a.astype(F32))
    bh, bl = _split2(b.astype(F32))
    return _dg(ah, bh, mode) + (_dg(ah, bl, mode) + _dg(al, bh, mode))


@functools.partial(jax.custom_vjp, nondiff_argnums=(2, 3))
def _dot(a, b, mode, passes):
    return _dotp(a, b, mode, passes)


def _dot_fwd(a, b, mode, passes):
    return _dotp(a, b, mode, passes), (a, b)


def _dot_bwd(mode, passes, res, ct):
    a, b = res
    if mode == "nn":
        da, db = _dotp(ct, b, "nt", passes), _dotp(a, ct, "tn", passes)
    elif mode == "nt":
        da, db = _dotp(ct, b, "nn", passes), _dotp(ct, a, "tn", passes)
    else:
        da, db = _dotp(b, ct, "nt", passes), _dotp(a, ct, "nn", passes)
    return da.astype(a.dtype), db.astype(b.dtype)


_dot.defvjp(_dot_fwd, _dot_bwd)


def _split3(x):
    x1 = x.astype(BF16)
    r = x - x1.astype(F32)
    x2 = r.astype(BF16)
    x3 = (r - x2.astype(F32)).astype(BF16)
    return x1, x2, x3


def _mdotp(mask, x, mode):
    x1, x2, x3 = _split3(x)
    return _dg(mask, x1, mode) + (_dg(mask, x2, mode) + _dg(mask, x3, mode))


@jax.custom_vjp
def _mdot(mask, x):
    return _mdotp(mask, x, "nn")


def _mdot_fwd(mask, x):
    return _mdotp(mask, x, "nn"), mask


def _mdot_bwd(mask, ct):
    return jnp.zeros_like(mask), _mdotp(mask, ct, "tn")


_mdot.defvjp(_mdot_fwd, _mdot_bwd)


_SOLVE_SQUARINGS = int(math.log2(DN_CHUNK)) - 1


def _unit_solve_fwd(A, R):
    X = R - _dotp(A, R, "nn", 3)
    P, powers = A, []
    for _ in range(_SOLVE_SQUARINGS):
        P = _dotp(P, P, "nn", 3)
        powers.append(P)
        X = X + _dotp(P, X, "nn", 3)
    return X, (A, tuple(powers), X)


@jax.custom_vjp
def _unit_solve(A, R):
    return _unit_solve_fwd(A, R)[0]


def _unit_solve_bwd(res, dX):
    A, powers, X = res
    Y = dX - _dotp(A, dX, "tn", 3)
    for P in powers:
        Y = Y + _dotp(P, Y, "tn", 3)
    return -_dotp(Y, X, "nt", 3), Y


_unit_solve.defvjp(_unit_solve_fwd, _unit_solve_bwd)


def _shift_impl(x, o):
    if o == 0:
        return x
    n = x.shape[0]
    y = pltpu.roll(x, (-o) % n, axis=0)
    t = lax.broadcasted_iota(jnp.int32, x.shape, 0) + o
    return jnp.where((t >= 0) & (t < n), y, 0.0)


@functools.partial(jax.custom_vjp, nondiff_argnums=(1,))
def _shift(x, o):
    return _shift_impl(x, o)


def _shift_fwd(x, o):
    return _shift_impl(x, o), None


def _shift_bwd(o, _, ct):
    return (_shift_impl(ct, -o),)


_shift.defvjp(_shift_fwd, _shift_bwd)


def _rot_impl(x):
    w = x.shape[1]
    half = DA_DIM // 2
    lane = lax.broadcasted_iota(jnp.int32, x.shape, 1)
    first = (lane & (DA_DIM - 1)) < half
    return jnp.where(first, -pltpu.roll(x, w - half, axis=1), pltpu.roll(x, half, axis=1))


@jax.custom_vjp
def _rot(x):
    return _rot_impl(x)


def _rot_fwd(x):
    return _rot_impl(x), None


def _rot_bwd(_, ct):
    return (-_rot_impl(ct),)


_rot.defvjp(_rot_fwd, _rot_bwd)


def _sigmoid(x):
    return 1.0 / (1.0 + jnp.exp(-x))


def _silu(x):
    return x * _sigmoid(x)


def _softplus(x):
    return jnp.maximum(x, 0.0) + jnp.log(1.0 + jnp.exp(-jnp.abs(x)))


def _rms(x, gain):
    return x * lax.rsqrt(jnp.mean(x * x, axis=-1, keepdims=True) + RMS_EPS) * gain


class _In:
    def __init__(self, arr, block, imap, kind="t", acc=False, g=None, gdtype=None):
        self.arr, self.block, self.imap, self.kind, self.acc, self.g, self.gdtype = arr, block, imap, kind, acc, g, gdtype


class _Out:
    def __init__(self, shape, dtype, block, imap):
        self.shape, self.dtype, self.block, self.imap = shape, dtype, block, imap


def _first_step(acc_from, ngrid):
    c = None
    for a in range(acc_from, ngrid):
        t = pl.program_id(a) == 0
        c = t if c is None else jnp.logical_and(c, t)
    return c


def _tile_fwd(name, f, grid, ins, outs):
    n_in = len(ins)
    ngrid = len(grid)

    def body(*refs):
        pids = tuple(pl.program_id(a) for a in range(ngrid))
        vals = [r[...] for r in refs[:n_in]]
        res = f(pids, *vals)
        for r, v in zip(refs[n_in:], res):
            r[...] = v.astype(r.dtype)

    return pl.pallas_call(
        body, name=name, grid=grid,
        in_specs=[pl.BlockSpec(i.block, i.imap) for i in ins],
        out_specs=[pl.BlockSpec(o.block, o.imap) for o in outs],
        out_shape=[jax.ShapeDtypeStruct(o.shape, o.dtype) for o in outs],
        compiler_params=_params(),
    )(*[i.arr for i in ins])


def _tile_bwd(name, f, grid, ins, outs, cts, acc_from=None, addends=None):
    n_in, n_out = len(ins), len(outs)
    ngrid = len(grid)
    diff = [k for k, i in enumerate(ins) if i.kind == "t"]
    addends = addends or {}
    add_keys = sorted(addends)

    def body(*refs):
        pids = tuple(pl.program_id(a) for a in range(ngrid))
        in_refs = refs[:n_in]
        ct_refs = refs[n_in:n_in + n_out]
        add_refs = refs[n_in + n_out:n_in + n_out + len(add_keys)]
        g_refs = refs[n_in + n_out + len(add_keys):]
        vals = [r[...] for r in in_refs]
        dvals = [vals[k].astype(F32) for k in diff]

        def g(*d):
            full = list(vals)
            for k, dk in zip(diff, d):
                full[k] = dk
            return tuple(f(pids, *full))

        res, vjp = jax.vjp(g, *dvals)
        grads = vjp(tuple(c[...].astype(r.dtype) for c, r in zip(ct_refs, res)))
        first = _first_step(acc_from, ngrid) if acc_from is not None else None
        for k, gr, gref in zip(diff, grads, g_refs):
            if k in addends:
                gr = gr + add_refs[add_keys.index(k)][...].astype(F32)
            if ins[k].acc:
                @pl.when(first)
                def _(gr=gr, gref=gref):
                    gref[...] = gr.astype(gref.dtype)

                @pl.when(jnp.logical_not(first))
                def _(gr=gr, gref=gref):
                    gref[...] += gr.astype(gref.dtype)
            else:
                gref[...] = gr.astype(gref.dtype)

    g_shapes, g_specs = [], []
    for k in diff:
        i = ins[k]
        if i.g is not None:
            shape, imap = i.g
        else:
            shape, imap = i.arr.shape, i.imap
        dt = i.gdtype or (F32 if i.acc else i.arr.dtype)
        g_shapes.append(jax.ShapeDtypeStruct(shape, dt))
        g_specs.append(pl.BlockSpec(i.block, imap))
    add_specs = [pl.BlockSpec(ins[k].block, ins[k].g[1] if ins[k].g is not None else ins[k].imap) for k in add_keys]
    return pl.pallas_call(
        body, name=name, grid=grid,
        in_specs=[pl.BlockSpec(i.block, i.imap) for i in ins] + [pl.BlockSpec(o.block, o.imap) for o in outs] + add_specs,
        out_specs=g_specs, out_shape=g_shapes,
        compiler_params=_params(),
    )(*[i.arr for i in ins], *cts, *[addends[k] for k in add_keys])


def _mm(name, a, b, mode, out_dtype=F32, add=None, tm=512, tn=512, tk=512):
    if mode == "nn":
        (M, K), N = a.shape, b.shape[1]
    elif mode == "nt":
        (M, K), N = a.shape, b.shape[0]
    else:
        (K, M), N = a.shape, b.shape[1]
    tm, tn, tk = _pick(M, tm, 128), _pick(N, tn, 128), _pick(K, tk, 128)
    nk = K // tk
    a_spec = pl.BlockSpec((tk, tm), lambda i, j, k: (k, i)) if mode == "tn" else pl.BlockSpec((tm, tk), lambda i, j, k: (i, k))
    b_spec = pl.BlockSpec((tn, tk), lambda i, j, k: (j, k)) if mode == "nt" else pl.BlockSpec((tk, tn), lambda i, j, k: (k, j))
    o_spec = pl.BlockSpec((tm, tn), lambda i, j, k: (i, j))

    def body(*refs):
        if add is None:
            a_ref, b_ref, o_ref, acc = refs
            add_ref = None
        else:
            a_ref, b_ref, add_ref, o_ref, acc = refs
        k = pl.program_id(2)

        @pl.when(k == 0)
        def _():
            acc[...] = jnp.zeros_like(acc)

        acc[...] += _dg(a_ref[...].astype(BF16), b_ref[...].astype(BF16), mode)

        @pl.when(k == nk - 1)
        def _():
            r = acc[...]
            if add_ref is not None:
                r = r + add_ref[...].astype(F32)
            o_ref[...] = r.astype(o_ref.dtype)

    ops = (a, b) if add is None else (a, b, add)
    specs = [a_spec, b_spec] + ([] if add is None else [o_spec])
    return pl.pallas_call(
        body, name=name, grid=(M // tm, N // tn, nk), in_specs=specs, out_specs=o_spec,
        out_shape=jax.ShapeDtypeStruct((M, N), out_dtype), scratch_shapes=[pltpu.VMEM((tm, tn), F32)],
        compiler_params=_params(dimension_semantics=("parallel", "parallel", "arbitrary")),
    )(*ops)


def _ffn_fwd(name, x, gain, wg, wu, wd):
    T, D = x.shape
    F = wg.shape[1]
    tm, tf = _pick(T, 512, 8), _pick(F, 256, 128)
    nf = F // tf

    def body(x_ref, g_ref, wg_ref, wu_ref, wd_ref, o_ref, h_ref, acc):
        j = pl.program_id(1)

        @pl.when(j == 0)
        def _():
            h_ref[...] = _rms(x_ref[...], g_ref[...]).astype(BF16)
            acc[...] = jnp.zeros_like(acc)

        h = h_ref[...]
        a = _dg(h, wg_ref[...], "nn")
        b = _dg(h, wu_ref[...], "nn")
        s = (_silu(a) * b).astype(BF16)
        acc[...] += _dg(s, wd_ref[...], "nn")

        @pl.when(j == nf - 1)
        def _():
            o_ref[...] = x_ref[...] + 0.5 * acc[...]

    return pl.pallas_call(
        body, name=name, grid=(T // tm, nf),
        in_specs=[pl.BlockSpec((tm, D), lambda i, j: (i, 0)), pl.BlockSpec((1, D), lambda i, j: (0, 0)),
                  pl.BlockSpec((D, tf), lambda i, j: (0, j)), pl.BlockSpec((D, tf), lambda i, j: (0, j)),
                  pl.BlockSpec((tf, D), lambda i, j: (j, 0))],
        out_specs=pl.BlockSpec((tm, D), lambda i, j: (i, 0)),
        out_shape=jax.ShapeDtypeStruct((T, D), F32),
        scratch_shapes=[pltpu.VMEM((tm, D), BF16), pltpu.VMEM((tm, D), F32)],
        compiler_params=_params(dimension_semantics=("parallel", "arbitrary")),
    )(x, gain, wg, wu, wd)


def _ffn_bwd(name, x, gain, wg, wu, wd, dy):
    T, D = x.shape
    F = wg.shape[1]
    tm, tf = _pick(T, 512, 8), _pick(F, 256, 128)
    nf = F // tf

    def body(x_ref, g_ref, wg_ref, wu_ref, wd_ref, dy_ref, dx_ref, dg_ref, da_ref, db_ref, s_ref, h_ref, dyh_ref, dh):
        i, j = pl.program_id(0), pl.program_id(1)

        @pl.when(j == 0)
        def _():
            h_ref[...] = _rms(x_ref[...], g_ref[...]).astype(BF16)
            dyh_ref[...] = (0.5 * dy_ref[...]).astype(BF16)
            dh[...] = jnp.zeros_like(dh)

        h = h_ref[...]
        a = _dg(h, wg_ref[...], "nn")
        b = _dg(h, wu_ref[...], "nn")
        ds = _dg(dyh_ref[...], wd_ref[...], "nt")
        sig = _sigmoid(a)
        silu = a * sig
        da = (ds * b * (sig * (1.0 + a * (1.0 - sig)))).astype(BF16)
        db = (ds * silu).astype(BF16)
        da_ref[...] = da
        db_ref[...] = db
        s_ref[...] = (silu * b).astype(BF16)
        dh[...] += _dg(da, wg_ref[...], "nt") + _dg(db, wu_ref[...], "nt")

        @pl.when(j == nf - 1)
        def _():
            _, vjp = jax.vjp(_rms, x_ref[...], g_ref[...])
            dxn, dgn = vjp(dh[...])
            dx_ref[...] = dy_ref[...] + dxn

            @pl.when(i == 0)
            def _():
                dg_ref[...] = dgn

            @pl.when(i != 0)
            def _():
                dg_ref[...] += dgn

    row = lambda i, j: (i, 0)
    col = lambda i, j: (i, j)
    dx, dgain, da, db, s, h, dyh = pl.pallas_call(
        body, name=name, grid=(T // tm, nf),
        in_specs=[pl.BlockSpec((tm, D), row), pl.BlockSpec((1, D), lambda i, j: (0, 0)),
                  pl.BlockSpec((D, tf), lambda i, j: (0, j)), pl.BlockSpec((D, tf), lambda i, j: (0, j)),
                  pl.BlockSpec((tf, D), lambda i, j: (j, 0)), pl.BlockSpec((tm, D), row)],
        out_specs=[pl.BlockSpec((tm, D), row), pl.BlockSpec((1, D), lambda i, j: (0, 0)),
                   pl.BlockSpec((tm, tf), col), pl.BlockSpec((tm, tf), col), pl.BlockSpec((tm, tf), col),
                   pl.BlockSpec((tm, D), row), pl.BlockSpec((tm, D), row)],
        out_shape=[jax.ShapeDtypeStruct((T, D), F32), jax.ShapeDtypeStruct((1, D), F32),
                   jax.ShapeDtypeStruct((T, F), BF16), jax.ShapeDtypeStruct((T, F), BF16),
                   jax.ShapeDtypeStruct((T, F), BF16), jax.ShapeDtypeStruct((T, D), BF16),
                   jax.ShapeDtypeStruct((T, D), BF16)],
        scratch_shapes=[pltpu.VMEM((tm, D), F32)],
        compiler_params=_params(),
    )(x, gain, wg, wu, wd, dy)
    dwg = _mm(name + "_dwg", h, da, "tn", tm=1024, tn=1408, tk=1024)
    dwu = _mm(name + "_dwu", h, db, "tn", tm=1024, tn=1408, tk=1024)
    dwd = _mm(name + "_dwd", s, dyh, "tn", tm=1408, tn=1024, tk=1024)
    return dx, dgain, dwg, dwu, dwd


def _norm_f(pids, x, gain):
    return (_rms(x, gain),)


def _dn_conv_f(pids, x, w):
    j = pids[0]
    tap = lax.broadcasted_iota(jnp.int32, w.shape, 0)
    y = jnp.zeros_like(x)
    for t in range(DN_CONV):
        wt = jnp.sum(jnp.where(tap == t, w, 0.0), axis=0, keepdims=True)
        y = y + _shift(x, t - DN_CONV // 2) * wt
    y = _silu(y)
    n = y * lax.rsqrt(jnp.sum(y * y, axis=-1, keepdims=True) + L2_EPS)
    is_q = (j < DN_HEADS).astype(F32)
    is_qk = (j < 2 * DN_HEADS).astype(F32)
    scale = is_q * (DN_DIM ** -0.5) + (1.0 - is_q)
    return ((is_qk * n + (1.0 - is_qk) * y) * scale,)


def _dn_gate_f(pids, braw, araw, a_log, dt_bias):
    beta = _sigmoid(braw)
    g = -jnp.exp(a_log) * _softplus(araw + dt_bias)
    return beta, g


def _dn_prep_f(pids, q, k, v, brow, grow):
    cs = DN_SUPER
    sign = 1 - 2 * pids[2]
    ii = lax.broadcasted_iota(jnp.int32, (cs, cs), 0)
    jj = lax.broadcasted_iota(jnp.int32, (cs, cs), 1)
    shift = int(math.log2(DN_CHUNK))
    same = (ii >> shift) == (jj >> shift)
    d = (ii - jj) * sign
    incl = same & (d >= 0)
    strict = same & (d > 0)
    eye = ii == jj
    g_col = jnp.sum(jnp.where(eye, jnp.broadcast_to(grow, (cs, cs)), 0.0), axis=1, keepdims=True)
    b_col = jnp.sum(jnp.where(eye, jnp.broadcast_to(brow, (cs, cs)), 0.0), axis=1, keepdims=True)
    g128 = jnp.broadcast_to(g_col, (cs, DN_DIM))
    G = _mdot(incl.astype(BF16), g128)
    Gt = _mdot(same.astype(BF16), g128)
    Gc = jnp.concatenate([G, G], axis=1)
    Grow = jnp.sum(jnp.where(eye, Gc, 0.0), axis=0, keepdims=True)
    decay = jnp.exp(jnp.where(incl, Gc - Grow, MASK_VALUE))
    eG = jnp.exp(G)
    kb = k * b_col
    A = jnp.where(strict, _dot(kb, k, "nt", 1) * decay, 0.0)
    X = _unit_solve(A, jnp.concatenate([v * b_col, kb * eG], axis=1))
    qk = jnp.where(incl, _dot(q, k, "nt", 1) * decay, 0.0)
    return X, qk, q * eG, k * jnp.exp(Gt - G), jnp.exp(Gt)


def _dn_out_f(pids, of, ob, z, gain):
    return (_rms(of + ob, gain) * _silu(z),)


def _pool_f(pids, u, w, scale):
    g = pids[0]
    half = jnp.left_shift(1, g)
    n = u.shape[0]
    pos = lax.broadcasted_iota(jnp.int32, (n, 1), 0)
    tot = jnp.zeros_like(u)
    cnt = jnp.zeros((n, 1), F32)
    for o in range(-POOL_MAX_HALF, POOL_MAX_HALF):
        use = ((o >= -half) & (o < half)).astype(F32)
        tot = tot + use * _shift(u, o)
        cnt = cnt + use * ((pos + o >= 0) & (pos + o < n)).astype(F32)
    pooled = tot / cnt - u
    return (_dot(pooled, w, "nn", 1) * scale,)


def _rope_f(pids, q, k, v, cos, sin):
    qr = (q * cos + _rot(q) * sin) * (DA_DIM ** -0.5)
    kr = k * cos + _rot(k) * sin
    return qr, kr, v


def _attn_head(q, k, v, qpos0, kpos0):
    s = _dot(q, k, "nt", 1)
    qi = qpos0 + lax.broadcasted_iota(jnp.int32, s.shape, 0)
    kj = kpos0 + lax.broadcasted_iota(jnp.int32, s.shape, 1)
    s = jnp.where(jnp.abs(kj - qi) <= DA_RADIUS, s, MASK_VALUE)
    m = lax.stop_gradient(jnp.max(s, axis=1, keepdims=True))
    p = jnp.exp(s - m)
    l = jnp.sum(p, axis=1, keepdims=True)
    o = _dot(p, v, "nn", 1) / l
    return o, jnp.broadcast_to(m + jnp.log(l), o.shape)


def _merge_f(pids, o0, o1, o2, l0, l1, l2):
    m = jnp.maximum(jnp.maximum(l0, l1), l2)
    e0, e1, e2 = jnp.exp(l0 - m), jnp.exp(l1 - m), jnp.exp(l2 - m)
    return ((e0 * o0 + e1 * o1 + e2 * o2) / (e0 + e1 + e2),)


def _gate_f(pids, g0, g1, g2, ya, yb, yc, b0, b1, b2):
    return (_sigmoid(g0 + b0) * ya + _sigmoid(g1 + b1) * yb + _sigmoid(g2 + b2) * yc,)


def _attn_window(i, L, tq, W):
    k0 = jnp.clip(i * tq - DA_RADIUS, 0, L - W)
    return pl.multiple_of(k0, DA_RADIUS)


def _attn_fwd(name, q, k, v):
    NS, L, HD = q.shape
    tq = min(DA_TQ, L)
    W = min(L, tq + 2 * DA_RADIUS)

    def body(q_ref, k_ref, v_ref, o_ref, l_ref):
        i = pl.program_id(1)
        k0 = _attn_window(i, L, tq, W)
        for h in range(DA_HEADS):
            hs = slice(h * DA_DIM, (h + 1) * DA_DIM)
            o, lse = _attn_head(q_ref[:, hs], k_ref[pl.ds(k0, W), hs], v_ref[pl.ds(k0, W), hs], i * tq, k0)
            o_ref[:, hs] = o
            l_ref[:, hs] = lse

    qs = pl.BlockSpec((None, tq, HD), lambda s, i: (s, i, 0))
    ks = pl.BlockSpec((None, L, HD), lambda s, i: (s, 0, 0))
    return pl.pallas_call(
        body, name=name, grid=(NS, L // tq), in_specs=[qs, ks, ks], out_specs=[qs, qs],
        out_shape=[jax.ShapeDtypeStruct((NS, L, HD), F32)] * 2, compiler_params=_params(),
    )(q, k, v)


def _attn_bwd(name, q, k, v, do, dl):
    NS, L, HD = q.shape
    tq = min(DA_TQ, L)
    W = min(L, tq + 2 * DA_RADIUS)

    def body(q_ref, k_ref, v_ref, do_ref, dl_ref, dq_ref, dk_ref, dv_ref):
        i = pl.program_id(1)
        k0 = _attn_window(i, L, tq, W)

        @pl.when(i == 0)
        def _():
            dk_ref[...] = jnp.zeros_like(dk_ref)
            dv_ref[...] = jnp.zeros_like(dv_ref)

        for h in range(DA_HEADS):
            hs = slice(h * DA_DIM, (h + 1) * DA_DIM)
            f = functools.partial(_attn_head, qpos0=i * tq, kpos0=k0)
            _, vjp = jax.vjp(f, q_ref[:, hs].astype(F32), k_ref[pl.ds(k0, W), hs].astype(F32),
                             v_ref[pl.ds(k0, W), hs].astype(F32))
            dq, dk, dv = vjp((do_ref[:, hs], dl_ref[:, hs]))
            dq_ref[:, hs] = dq
            dk_ref[pl.ds(k0, W), hs] += dk
            dv_ref[pl.ds(k0, W), hs] += dv

    qs = pl.BlockSpec((None, tq, HD), lambda s, i: (s, i, 0))
    ks = pl.BlockSpec((None, L, HD), lambda s, i: (s, 0, 0))
    return pl.pallas_call(
        body, name=name, grid=(NS, L // tq), in_specs=[qs, ks, ks, qs, qs], out_specs=[qs, ks, ks],
        out_shape=[jax.ShapeDtypeStruct((NS, L, HD), F32)] * 3, compiler_params=_params(),
    )(q, k, v, do, dl)


def _scan_chunk(t, rev, N):
    c = jnp.where(rev, N - 1 - t, t)
    per = DN_SUPER // DN_CHUNK
    return c, pl.multiple_of(c * DN_CHUNK, DN_CHUNK), pl.multiple_of((c % per) * DN_CHUNK, DN_CHUNK), \
        pl.multiple_of((c // per) * DN_SUPER, DN_SUPER)


def _dn_scan_fwd(name, uw, qk, qd, kd, gl, B):
    R, T, _ = uw.shape
    S = T // B
    N = S // DN_CHUNK
    C, DK = DN_CHUNK, DN_DIM

    def body(uw_ref, qk_ref, qd_ref, kd_ref, gl_ref, o_ref, st_ref, vn_ref):
        rev = pl.program_id(1) >= DN_HEADS
        vn_ref[...] = jnp.zeros_like(vn_ref)

        def step(t, state):
            c, r0, w0, s0 = _scan_chunk(t, rev, N)
            rows = pl.ds(r0, C)
            st_ref[c] = state
            vnew = uw_ref[rows, 0:DK] - _dotp(uw_ref[rows, DK:2 * DK], state, "nn", 1)
            vn_ref[pl.ds(w0, C), :] = vnew
            o_ref[rows, :] = _dotp(qd_ref[rows, :], state, "nn", 1) + _dotp(qk_ref[rows, :], vn_ref[...], "nn", 1)
            return state * gl_ref[pl.ds(r0, 1), :] + _dotp(kd_ref[rows, :], vnew, "tn", 1)

        lax.fori_loop(0, N, step, jnp.zeros((DK, DK), F32))

    def seq(w):
        return pl.BlockSpec((None, S, w), lambda b, r: (r, b, 0))

    return pl.pallas_call(
        body, name=name, grid=(B, R),
        in_specs=[seq(2 * DK), seq(DN_SUPER), seq(DK), seq(DK), seq(DK)],
        out_specs=[seq(DK), pl.BlockSpec((None, None, N, DK, DK), lambda b, r: (b, r, 0, 0, 0))],
        out_shape=[jax.ShapeDtypeStruct((R, T, DK), F32), jax.ShapeDtypeStruct((B, R, N, DK, DK), F32)],
        scratch_shapes=[pltpu.VMEM((DN_SUPER, DK), F32)], compiler_params=_params(),
    )(uw, qk, qd, kd, gl)


def _dn_scan_bwd(name, uw, qk, qd, kd, gl, st, do, B):
    R, T, _ = uw.shape
    S = T // B
    N = S // DN_CHUNK
    C, DK = DN_CHUNK, DN_DIM

    def body(uw_ref, qk_ref, qd_ref, kd_ref, gl_ref, st_ref, do_ref, duw_ref, dqk_ref, dqd_ref, dkd_ref, dgl_ref,
             vn_ref, tmp_ref):
        rev = pl.program_id(1) >= DN_HEADS
        vn_ref[...] = jnp.zeros_like(vn_ref)
        dgl_ref[...] = jnp.zeros_like(dgl_ref)

        def step(t, dstate):
            c, r0, w0, s0 = _scan_chunk(N - 1 - t, rev, N)
            rows = pl.ds(r0, C)
            state = st_ref[c]
            w = uw_ref[rows, DK:2 * DK]
            vnew = uw_ref[rows, 0:DK] - _dotp(w, state, "nn", 1)
            vn_ref[pl.ds(w0, C), :] = vnew
            do_c = do_ref[rows, :]
            tmp_ref[...] = _dotp(qk_ref[rows, :], do_c, "tn", 1)
            dvn = tmp_ref[pl.ds(w0, C), :] + _dotp(kd_ref[rows, :], dstate, "nn", 1)
            dqk_ref[rows, :] = _dotp(do_c, vn_ref[...], "nt", 1)
            dqd_ref[rows, :] = _dotp(do_c, state, "nt", 1)
            dkd_ref[rows, :] = _dotp(vnew, dstate, "nt", 1)
            dgl_ref[pl.ds(r0, 1), :] = jnp.sum(state * dstate, axis=0, keepdims=True)
            duw_ref[rows, 0:DK] = dvn
            duw_ref[rows, DK:2 * DK] = -_dotp(dvn, state, "nt", 1)
            return (_dotp(qd_ref[rows, :], do_c, "tn", 1) + dstate * gl_ref[pl.ds(r0, 1), :]
                    - _dotp(w, dvn, "tn", 1))

        lax.fori_loop(0, N, step, jnp.zeros((DK, DK), F32))

    def seq(w):
        return pl.BlockSpec((None, S, w), lambda b, r: (r, b, 0))

    return pl.pallas_call(
        body, name=name, grid=(B, R),
        in_specs=[seq(2 * DK), seq(DN_SUPER), seq(DK), seq(DK), seq(DK),
                  pl.BlockSpec((None, None, N, DK, DK), lambda b, r: (b, r, 0, 0, 0)),
                  pl.BlockSpec((None, S, DK), lambda b, r: (r % DN_HEADS, b, 0))],
        out_specs=[seq(2 * DK), seq(DN_SUPER), seq(DK), seq(DK), seq(DK)],
        out_shape=[jax.ShapeDtypeStruct((R, T, 2 * DK), F32), jax.ShapeDtypeStruct((R, T, DN_SUPER), F32),
                   jax.ShapeDtypeStruct((R, T, DK), F32), jax.ShapeDtypeStruct((R, T, DK), F32),
                   jax.ShapeDtypeStruct((R, T, DK), F32)],
        scratch_shapes=[pltpu.VMEM((DN_SUPER, DK), F32), pltpu.VMEM((DN_SUPER, DK), F32)],
        compiler_params=_params(),
    )(uw, qk, qd, kd, gl, st, do)


def _loss_fwd_bwd(name, x, gain, target):
    T, D = x.shape
    tm = _pick(T, 512, 8)

    def body(x_ref, g_ref, t_ref, loss_ref, dx_ref, dg_ref):
        i = pl.program_id(0)

        def f(xv, gv):
            e = _rms(xv, gv) - t_ref[...]
            return 0.5 * jnp.sum(jnp.mean(e * e, axis=-1, keepdims=True))

        val, (dx, dg) = jax.value_and_grad(f, argnums=(0, 1))(x_ref[...], g_ref[...])
        dx_ref[...] = dx
        part = jnp.full(loss_ref.shape, val, F32)

        @pl.when(i == 0)
        def _():
            dg_ref[...] = dg
            loss_ref[...] = part

        @pl.when(i != 0)
        def _():
            dg_ref[...] += dg
            loss_ref[...] += part

    return pl.pallas_call(
        body, name=name, grid=(T // tm,),
        in_specs=[pl.BlockSpec((tm, D), lambda i: (i, 0)), pl.BlockSpec((1, D), lambda i: (0, 0)),
                  pl.BlockSpec((tm, D), lambda i: (i, 0))],
        out_specs=[pl.BlockSpec((8, 128), lambda i: (0, 0)), pl.BlockSpec((tm, D), lambda i: (i, 0)),
                   pl.BlockSpec((1, D), lambda i: (0, 0))],
        out_shape=[jax.ShapeDtypeStruct((8, 128), F32), jax.ShapeDtypeStruct((T, D), F32),
                   jax.ShapeDtypeStruct((1, D), F32)],
        compiler_params=_params(),
    )(x, gain, target)


class _Cols:
    def __init__(self, D):
        assert D % 256 == 0
        self.gate = 0
        self.da = 3 * D
        self.qkv = self.da + 3 * DA_WIDTH
        self.z = self.qkv + 3 * DN_WIDTH
        self.pool = self.z + DN_WIDTH
        self.ba = self.pool + POOL_WIDTH
        self.total = self.ba + BA_PAD


def _rope_tables(S):
    half = DA_DIM // 2
    inv_freq = ROPE_THETA ** (-jnp.arange(half, dtype=F32) / half)
    ang = jnp.arange(S, dtype=F32)[:, None] * inv_freq[None, :]
    reps = DA_WIDTH // DA_DIM
    cos = jnp.tile(jnp.concatenate([jnp.cos(ang), jnp.cos(ang)], axis=1), (1, reps))
    sin = jnp.tile(jnp.concatenate([jnp.sin(ang), jnp.sin(ang)], axis=1), (1, reps))
    return cos, sin


def _to_strided(t, B, dil):
    T, w = t.shape
    L = T // B // dil
    return t.reshape(B, L, dil, w).transpose(0, 2, 1, 3).reshape(B * dil, L, w)


def _from_strided(t, B, dil):
    NS, L, w = t.shape
    return t.reshape(B, dil, L, w).transpose(0, 2, 1, 3).reshape(B * dil * L, w)


def _mixer(l, x1, w, B):
    T, D = x1.shape
    S = T // B
    c = _Cols(D)
    tm = _pick(S, 512, 8)
    nmS = S // tm
    n = f"l{l}_"

    norm_ins = [_In(x1, (tm, D), lambda i: (i, 0)), _In(w["mix_norm"], (1, D), lambda i: (0, 0), acc=True)]
    norm_outs = [_Out((T, D), BF16, (tm, D), lambda i: (i, 0))]
    (h,) = _tile_fwd(n + "norm", _norm_f, (T // tm,), norm_ins, norm_outs)
    P = _mm(n + "proj", h, w["w_cat"], "nn", tn=896, tk=1024)
    baT = P[:, c.ba:c.ba + 16].T

    cb = c.qkv // DN_DIM
    conv_ins = [_In(P, (S, DN_DIM), lambda j, b: (b, cb + j), g=((T, 3 * DN_WIDTH), lambda j, b: (b, j)), gdtype=BF16),
                _In(w["dn_conv"], (DN_CONV, DN_DIM), lambda j, b: (0, j), acc=True)]
    conv_outs = [_Out((T, 3 * DN_WIDTH), F32, (S, DN_DIM), lambda j, b: (b, j))]
    conv_grid = (3 * DN_HEADS, B)
    (qkvc,) = _tile_fwd(n + "dnconv", _dn_conv_f, conv_grid, conv_ins, conv_outs)

    tg = _pick(T, 2048, 128)
    gate_ins = [_In(baT, (8, tg), lambda i: (0, i)), _In(baT, (8, tg), lambda i: (1, i)),
                _In(w["dn_a_log"], (8, 1), lambda i: (0, 0), acc=True),
                _In(w["dn_dt_bias"], (8, 1), lambda i: (0, 0), acc=True)]
    gate_ins[0].g = ((8, T), lambda i: (0, i))
    gate_ins[1].g = ((8, T), lambda i: (0, i))
    gate_outs = [_Out((8, T), F32, (8, tg), lambda i: (0, i))] * 2
    beta, gdec = _tile_fwd(n + "dngate", _dn_gate_f, (T // tg,), gate_ins, gate_outs)

    NSC = T // DN_SUPER
    beta4 = beta.reshape(8, NSC, 1, DN_SUPER)
    gdec4 = gdec.reshape(8, NSC, 1, DN_SUPER)
    R = 2 * DN_HEADS

    def qkv_in(off):
        return _In(qkvc, (DN_SUPER, DN_DIM), lambda hh, m, dd: (m, off + hh), acc=True,
                   g=((T, DN_WIDTH), lambda hh, m, dd: (m, hh)))

    def row_in(a):
        return _In(a, (None, None, 1, DN_SUPER), lambda hh, m, dd: (dd * DN_HEADS + hh, m, 0, 0))

    def chain_out(wd):
        return _Out((R, T, wd), F32, (None, DN_SUPER, wd), lambda hh, m, dd: (dd * DN_HEADS + hh, m, 0))

    prep_ins = [qkv_in(0), qkv_in(DN_HEADS), qkv_in(2 * DN_HEADS), row_in(beta4), row_in(gdec4)]
    prep_outs = [chain_out(2 * DN_DIM), chain_out(DN_SUPER), chain_out(DN_DIM), chain_out(DN_DIM), chain_out(DN_DIM)]
    prep_grid = (DN_HEADS, NSC, 2)
    uw, qk, qd, kd, gl = _tile_fwd(n + "dnprep", _dn_prep_f, prep_grid, prep_ins, prep_outs)
    o_dn, states = _dn_scan_fwd(n + "dnscan", uw, qk, qd, kd, gl, B)

    zb = c.z // DN_DIM
    out_ins = [_In(o_dn, (None, S, DN_DIM), lambda b, hh: (hh, b, 0)),
               _In(o_dn, (None, S, DN_DIM), lambda b, hh: (DN_HEADS + hh, b, 0)),
               _In(P, (S, DN_DIM), lambda b, hh: (b, zb + hh), g=((T, DN_WIDTH), lambda b, hh: (b, hh)), gdtype=BF16),
               _In(w["dn_out_norm"], (1, DN_DIM), lambda b, hh: (0, 0), acc=True)]
    out_ins[0].g = ((DN_HEADS, T, DN_DIM), lambda b, hh: (hh, b, 0))
    out_ins[1].g = ((DN_HEADS, T, DN_DIM), lambda b, hh: (hh, b, 0))
    out_outs = [_Out((T, DN_WIDTH), BF16, (S, DN_DIM), lambda b, hh: (b, hh))]
    (ya_in,) = _tile_fwd(n + "dnout", _dn_out_f, (B, DN_HEADS), out_ins, out_outs)

    pb = c.pool // POOL_DIM
    pool_ins = [_In(P, (S, POOL_DIM), lambda gi, b: (b, pb + gi), g=((T, POOL_WIDTH), lambda gi, b: (b, gi)), gdtype=BF16),
                _In(w["pool_w"], (None, POOL_DIM, POOL_DIM), lambda gi, b: (gi, 0, 0), acc=True),
                _In(w["pool_scale"], (None, 1, POOL_DIM), lambda gi, b: (gi, 0, 0), acc=True)]
    pool_outs = [_Out((T, POOL_WIDTH), BF16, (S, POOL_DIM), lambda gi, b: (b, gi))]
    (yb_in,) = _tile_fwd(n + "pool", _pool_f, (POOL_GROUPS, B), pool_ins, pool_outs)

    cos, sin = _rope_tables(S)
    db = c.da // DA_WIDTH

    def da_in(k):
        return _In(P, (tm, DA_WIDTH), lambda i: (i, db + k), g=((T, DA_WIDTH), lambda i: (i, 0)), gdtype=BF16)

    rope_ins = [da_in(0), da_in(1), da_in(2),
                _In(cos, (tm, DA_WIDTH), lambda i: (i % nmS, 0), kind="c"),
                _In(sin, (tm, DA_WIDTH), lambda i: (i % nmS, 0), kind="c")]
    rope_outs = [_Out((T, DA_WIDTH), BF16, (tm, DA_WIDTH), lambda i: (i, 0))] * 3
    qr, kr, vr = _tile_fwd(n + "rope", _rope_f, (T // tm,), rope_ins, rope_outs)
    strided = []
    o_g, l_g = [], []
    for gi, dil in enumerate(DA_DILATIONS):
        cs_ = slice(gi * DA_OUT, (gi + 1) * DA_OUT)
        qs, ks, vs = (_to_strided(t[:, cs_], B, dil) for t in (qr, kr, vr))
        strided.append((qs, ks, vs))
        o, lse = _attn_fwd(n + f"attn{gi}", qs, ks, vs)
        o_g.append(_from_strided(o, B, dil))
        l_g.append(_from_strided(lse, B, dil))
    mrg_ins = [_In(a, (tm, DA_OUT), lambda i: (i, 0)) for a in o_g + l_g]
    mrg_outs = [_Out((T, DA_OUT), BF16, (tm, DA_OUT), lambda i: (i, 0))]
    (yc_in,) = _tile_fwd(n + "merge", _merge_f, (T // tm,), mrg_ins, mrg_outs)

    ya = _mm(n + "pa", ya_in, w["w_proj_a"], "nn")
    yb = _mm(n + "pb", yb_in, w["w_proj_b"], "nn")
    yc = _mm(n + "pc", yc_in, w["w_proj_c"], "nn")

    def gcol(k):
        return _In(P, (tm, D), lambda i: (i, k), g=((T, D), lambda i: (i, 0)), gdtype=BF16)

    def yin(a):
        return _In(a, (tm, D), lambda i: (i, 0), gdtype=BF16)

    def bin_(k):
        return _In(w["b_gate"][k:k + 1], (1, D), lambda i: (0, 0), acc=True)

    gm_ins = [gcol(0), gcol(1), gcol(2), yin(ya), yin(yb), yin(yc), bin_(0), bin_(1), bin_(2)]
    gm_outs = [_Out((T, D), BF16, (tm, D), lambda i: (i, 0))]
    (merged,) = _tile_fwd(n + "gates", _gate_f, (T // tm,), gm_ins, gm_outs)
    x2 = _mm(n + "out", merged, w["w_out"], "nn", add=x1)

    def backward(dx2):
        return _mixer_bwd(dx2, **{k: v for k, v in locals_.items() if k in _MIXER_BWD_NEEDS})

    locals_ = dict(locals())
    return x2, backward


_MIXER_BWD_NEEDS = ("n", "B", "T", "D", "tm", "w", "h", "merged", "gm_ins", "gm_outs", "ya_in", "yb_in", "yc_in",
                    "mrg_ins", "mrg_outs", "strided", "rope_ins", "rope_outs", "pool_ins", "pool_outs", "out_ins",
                    "out_outs", "uw", "qk", "qd", "kd", "gl", "states", "prep_grid", "prep_ins", "prep_outs", "tg",
                    "gate_ins", "gate_outs", "conv_grid", "conv_ins", "conv_outs", "norm_ins", "norm_outs")


def _mixer_bwd(dx2, *, n, B, T, D, tm, w, h, merged, gm_ins, gm_outs, ya_in, yb_in, yc_in, mrg_ins, mrg_outs, strided,
               rope_ins, rope_outs, pool_ins, pool_outs, out_ins, out_outs, uw, qk, qd, kd, gl, states, prep_grid,
               prep_ins, prep_outs, tg, gate_ins, gate_outs, conv_grid, conv_ins, conv_outs, norm_ins, norm_outs):
    g = {}
    dmerged = _mm(n + "d_merged", dx2, w["w_out"], "nt")
    g["w_out"] = _mm(n + "d_wout", merged, dx2, "tn", tm=1024, tn=1024, tk=1024)
    dg0, dg1, dg2, dya, dyb, dyc, db0, db1, db2 = _tile_bwd(
        n + "gates_b", _gate_f, (T // tm,), gm_ins, gm_outs, [dmerged], acc_from=0)
    g["b_gate"] = jnp.concatenate([db0, db1, db2], axis=0)
    dya_in = _mm(n + "d_pa", dya, w["w_proj_a"], "nt")
    dyb_in = _mm(n + "d_pb", dyb, w["w_proj_b"], "nt")
    dyc_in = _mm(n + "d_pc", dyc, w["w_proj_c"], "nt")
    g["w_proj_a"] = _mm(n + "d_wpa", ya_in, dya, "tn", tn=1024, tk=2048)
    g["w_proj_b"] = _mm(n + "d_wpb", yb_in, dyb, "tn", tn=1024, tk=2048)
    g["w_proj_c"] = _mm(n + "d_wpc", yc_in, dyc, "tn", tn=1024, tk=2048)

    dmrg = _tile_bwd(n + "merge_b", _merge_f, (T // tm,), mrg_ins, mrg_outs, [dyc_in])
    dq_parts, dk_parts, dv_parts = [], [], []
    for gi, dil in enumerate(DA_DILATIONS):
        qs, ks, vs = strided[gi]
        do_s = _to_strided(dmrg[gi], B, dil)
        dl_s = _to_strided(dmrg[DA_GROUPS + gi], B, dil)
        dq, dk, dv = _attn_bwd(n + f"attn{gi}_b", qs, ks, vs, do_s, dl_s)
        dq_parts.append(_from_strided(dq, B, dil))
        dk_parts.append(_from_strided(dk, B, dil))
        dv_parts.append(_from_strided(dv, B, dil))
    dqr, dkr, dvr = (jnp.concatenate(p, axis=1) for p in (dq_parts, dk_parts, dv_parts))
    dPq, dPk, dPv = _tile_bwd(n + "rope_b", _rope_f, (T // tm,), rope_ins, rope_outs, [dqr, dkr, dvr])

    dPpool, g["pool_w"], g["pool_scale"] = _tile_bwd(
        n + "pool_b", _pool_f, (POOL_GROUPS, B), pool_ins, pool_outs, [dyb_in], acc_from=1)

    dof, dob, dPz, g["dn_out_norm"] = _tile_bwd(
        n + "dnout_b", _dn_out_f, (B, DN_HEADS), out_ins, out_outs, [dya_in], acc_from=0)
    del dob
    duw, dqk, dqd, dkd, dgl = _dn_scan_bwd(n + "dnscan_b", uw, qk, qd, kd, gl, states, dof, B)
    dq_, dk_, dv_, dbeta4, dgdec4 = _tile_bwd(
        n + "dnprep_b", _dn_prep_f, prep_grid, prep_ins, prep_outs, [duw, dqk, dqd, dkd, dgl], acc_from=2)
    dqkvc = jnp.concatenate([dq_, dk_, dv_], axis=1)
    dbraw, daraw, g["dn_a_log"], g["dn_dt_bias"] = _tile_bwd(
        n + "dngate_b", _dn_gate_f, (T // tg,), gate_ins, gate_outs,
        [dbeta4.reshape(8, T), dgdec4.reshape(8, T)], acc_from=0)
    dPqkv, g["dn_conv"] = _tile_bwd(n + "dnconv_b", _dn_conv_f, conv_grid, conv_ins, conv_outs, [dqkvc], acc_from=1)
    dba = jnp.concatenate([dbraw, daraw], axis=0).T.astype(BF16)
    dba = jnp.pad(dba, ((0, 0), (0, BA_PAD - 16)))
    dP = jnp.concatenate([dg0, dg1, dg2, dPq, dPk, dPv, dPqkv, dPz, dPpool, dba], axis=1)
    dh = _mm(n + "d_h", dP, w["w_cat"], "nt", tn=1024, tk=2688)
    g["w_cat"] = _mm(n + "d_wcat", h, dP, "tn", tm=1024, tn=896, tk=1024)
    dx1, g["mix_norm"] = _tile_bwd(n + "norm_b", _norm_f, (T // tm,), norm_ins, norm_outs, [dh], acc_from=0,
                                   addends={0: dx2})
    return dx1, g


def _layer_weights(full, l, D):
    c = _Cols(D)
    w_in = full["w_in"][l]
    o_z, o_ba, o_pool, o_da = 3 * DN_WIDTH, 4 * DN_WIDTH, 4 * DN_WIDTH + 16, 4 * DN_WIDTH + 16 + POOL_WIDTH
    w_cat = jnp.concatenate(
        [full["w_gate"][l], w_in[:, o_da:], w_in[:, :o_z], w_in[:, o_z:o_ba], w_in[:, o_pool:o_da], w_in[:, o_ba:o_pool],
         jnp.zeros((D, BA_PAD - 16), w_in.dtype)], axis=1).astype(BF16)
    assert w_cat.shape[1] == c.total
    w = {k: full[k][l].astype(BF16) for k in ("ffn1_w_gate", "ffn1_w_up", "ffn1_w_down", "ffn2_w_gate", "ffn2_w_up",
                                              "ffn2_w_down", "w_proj_a", "w_proj_b", "w_proj_c", "w_out")}
    w["w_cat"] = w_cat
    w["ffn1_norm"] = full["ffn1_norm"][l][None].astype(F32)
    w["ffn2_norm"] = full["ffn2_norm"][l][None].astype(F32)
    w["mix_norm"] = full["mix_norm"][l][None].astype(F32)
    w["dn_conv"] = full["dn_conv"][l].astype(F32)
    w["dn_a_log"] = full["dn_a_log"][l].reshape(2 * DN_HEADS, 1).astype(F32)
    w["dn_dt_bias"] = full["dn_dt_bias"][l].reshape(2 * DN_HEADS, 1).astype(F32)
    w["dn_out_norm"] = full["dn_out_norm"][l][None].astype(F32)
    w["pool_w"] = full["pool_w"][l].astype(F32)
    w["pool_scale"] = full["pool_scale"][l].reshape(POOL_GROUPS, 1, POOL_DIM).astype(F32)
    w["b_gate"] = full["b_gate"][l].reshape(3, D).astype(F32)
    return w


def _layer_grads(g, D):
    c = _Cols(D)
    gc = g.pop("w_cat")
    out = dict(g)
    out["w_gate"] = gc[:, :c.da]
    out["w_in"] = jnp.concatenate([gc[:, c.qkv:c.pool], gc[:, c.ba:c.ba + 16], gc[:, c.pool:c.ba], gc[:, c.da:c.qkv]],
                                  axis=1)
    for k in ("ffn1_norm", "ffn2_norm", "mix_norm", "dn_out_norm"):
        out[k] = g[k][0]
    out["dn_a_log"] = g["dn_a_log"].reshape(2, DN_HEADS)
    out["dn_dt_bias"] = g["dn_dt_bias"].reshape(2, DN_HEADS)
    out["pool_scale"] = g["pool_scale"].reshape(POOL_WIDTH)
    out["b_gate"] = g["b_gate"].reshape(3 * D)
    return out


def _local_step(x, target, full):
    B, S, D = x.shape
    T = B * S
    depth = full["w_in"].shape[0]
    xs = x.reshape(T, D)
    tape = []
    for l in range(depth):
        w = _layer_weights(full, l, D)
        x1 = _ffn_fwd(f"l{l}_ffn1", xs, w["ffn1_norm"], w["ffn1_w_gate"], w["ffn1_w_up"], w["ffn1_w_down"])
        x2, mixer_bwd = _mixer(l, x1, w, B)
        x3 = _ffn_fwd(f"l{l}_ffn2", x2, w["ffn2_norm"], w["ffn2_w_gate"], w["ffn2_w_up"], w["ffn2_w_down"])
        tape.append((w, xs, mixer_bwd, x2))
        xs = x3
    loss8, dx, dfinal = _loss_fwd_bwd("loss", xs, full["final_norm"][None].astype(F32), target.reshape(T, D))
    per_layer = [None] * depth
    for l in reversed(range(depth)):
        w, x0, mixer_bwd, x2 = tape[l]
        dx, dn2, dwg2, dwu2, dwd2 = _ffn_bwd(f"l{l}_ffn2b", x2, w["ffn2_norm"], w["ffn2_w_gate"], w["ffn2_w_up"],
                                             w["ffn2_w_down"], dx)
        dx, g = mixer_bwd(dx)
        dx, dn1, dwg1, dwu1, dwd1 = _ffn_bwd(f"l{l}_ffn1b", x0, w["ffn1_norm"], w["ffn1_w_gate"], w["ffn1_w_up"],
                                             w["ffn1_w_down"], dx)
        g.update(ffn1_norm=dn1, ffn1_w_gate=dwg1, ffn1_w_up=dwu1, ffn1_w_down=dwd1,
                 ffn2_norm=dn2, ffn2_w_gate=dwg2, ffn2_w_up=dwu2, ffn2_w_down=dwd2)
        per_layer[l] = _layer_grads(g, D)
    grads = {k: jnp.stack([pg[k] for pg in per_layer]) for k in per_layer[0]}
    grads["final_norm"] = dfinal[0]
    return loss8[0, 0], dx.reshape(B, S, D), grads


def _mesh_position():
    mx, my, mc = lax.axis_index("x"), lax.axis_index("y"), lax.axis_index("c")
    return mx, my, mc, 4 * mx + 2 * my + mc


def _peers(mx, my, mc):
    out = []
    for k in range(1, N_DEV):
        px, py, pc = mx ^ ((k >> 2) & 1), my ^ ((k >> 1) & 1), mc ^ (k & 1)
        out.append(((px, py, pc), 4 * px + 2 * py + pc))
    return out


_ANY = pl.BlockSpec(memory_space=pl.ANY)


def _all_gather(name, xs):
    n = len(xs)

    def body(*refs):
        x_refs, o_refs = refs[:n], refs[n:2 * n]
        send_sems, recv_sems, local_sems = refs[2 * n:]
        mx, my, mc, me = _mesh_position()
        sibling = (mx, my, 1 - mc)
        chips = [(1 - mx, my), (mx, 1 - my), (1 - mx, 1 - my)]

        def block(px, py, pc):
            return 4 * px + 2 * py + pc

        def copy(a, k, blk, to, src=None):
            dst = o_refs[a].at[blk]
            return pltpu.make_async_remote_copy(src_ref=dst if src is None else src, dst_ref=dst,
                                                send_sem=send_sems.at[a, k], recv_sem=recv_sems.at[a, k],
                                                device_id=to, device_id_type=pl.DeviceIdType.MESH)

        own, sends = [], []
        for a in range(n):
            mine = pltpu.make_async_copy(x_refs[a], o_refs[a].at[me], local_sems.at[a])
            mine.start()
            own.append(mine)
            first = [copy(a, 0, me, sibling, src=x_refs[a])]
            first += [copy(a, 1 + j, me, (*chip, mc), src=x_refs[a]) for j, chip in enumerate(chips)]
            for cp in first:
                cp.start()
            sends += first
        for a in range(n):
            for j, chip in enumerate(chips):
                copy(a, 1 + j, block(*chip, mc), (mx, my, mc)).wait_recv()
                passed = copy(a, 4 + j, block(*chip, mc), sibling)
                passed.start()
                sends.append(passed)
        for a in range(n):
            copy(a, 0, block(mx, my, 1 - mc), (mx, my, mc)).wait_recv()
            for j, chip in enumerate(chips):
                copy(a, 4 + j, block(*chip, 1 - mc), (mx, my, mc)).wait_recv()
        for cp in sends:
            cp.wait_send()
        for cp in own:
            cp.wait()

    sems = pltpu.SemaphoreType.DMA((n, N_DEV - 1))
    return pl.pallas_call(
        body, name=name, in_specs=[_ANY] * n, out_specs=[_ANY] * n,
        out_shape=[jax.ShapeDtypeStruct((N_DEV,) + x.shape, x.dtype) for x in xs],
        scratch_shapes=[sems, sems, pltpu.SemaphoreType.DMA((n,))],
    )(*xs)


def _exchange_grads(name, gs, gr):
    ns, n = len(gs), len(gs) + len(gr)

    def body(*refs):
        in_refs, out_refs = refs[:n], refs[n:2 * n]
        send_sems, recv_sems, local_sems = refs[2 * n:]
        mx, my, mc, me = _mesh_position()
        own, copies = [], []
        for a in range(n):
            src = in_refs[a].at[me] if a < ns else in_refs[a]
            cp = pltpu.make_async_copy(src, out_refs[a].at[me], local_sems.at[a])
            cp.start()
            own.append(cp)
        for k, (peer, pid) in enumerate(_peers(mx, my, mc)):
            for a in range(n):
                src = in_refs[a].at[pid] if a < ns else in_refs[a]
                cp = pltpu.make_async_remote_copy(src_ref=src, dst_ref=out_refs[a].at[me], send_sem=send_sems.at[a, k],
                                                  recv_sem=recv_sems.at[a, k], device_id=peer,
                                                  device_id_type=pl.DeviceIdType.MESH)
                cp.start()
                copies.append(cp)
        for cp in copies:
            cp.wait_send()
            cp.wait_recv()
        for cp in own:
            cp.wait()

    sems = pltpu.SemaphoreType.DMA((n, N_DEV - 1))
    outs = pl.pallas_call(
        body, name=name, in_specs=[_ANY] * n, out_specs=[_ANY] * n,
        out_shape=[jax.ShapeDtypeStruct(a.shape, a.dtype) for a in gs]
        + [jax.ShapeDtypeStruct((N_DEV,) + a.shape, a.dtype) for a in gr],
        scratch_shapes=[sems, sems, pltpu.SemaphoreType.DMA((n,))],
    )(*gs, *gr)
    return outs[:ns], outs[ns:]


def _reduce_adamw(name, parts, w, m, v):
    shape = w.shape
    cols = shape[-1]
    w2, m2, v2 = (t.reshape(-1, cols) for t in (w, m, v))
    p3 = parts.reshape(N_DEV, -1, cols)
    rows = w2.shape[0]
    tr = _pick(rows, 512, 16) if rows > 1024 else rows
    c1 = 1.0 - ADAM_B1 ** ADAM_STEP
    c2 = 1.0 - ADAM_B2 ** ADAM_STEP

    def body(p_ref, w_ref, m_ref, v_ref, g_ref, d_ref, nm_ref, nv_ref):
        gv = p_ref[0].astype(F32)
        for d in range(1, N_DEV):
            gv = gv + p_ref[d].astype(F32)
        nm = ADAM_B1 * m_ref[...] + (1.0 - ADAM_B1) * gv
        nv = ADAM_B2 * v_ref[...] + (1.0 - ADAM_B2) * (gv * gv)
        g_ref[...] = gv
        d_ref[...] = -ADAM_LR * ((nm / c1) / (jnp.sqrt(nv / c2) + ADAM_EPS) + ADAM_WD * w_ref[...])
        nm_ref[...] = nm
        nv_ref[...] = nv

    spec = pl.BlockSpec((tr, cols), lambda i: (i, 0))
    outs = pl.pallas_call(
        body, name=name, grid=(rows // tr,),
        in_specs=[pl.BlockSpec((N_DEV, tr, cols), lambda i: (0, i, 0))] + [spec] * 3, out_specs=[spec] * 4,
        out_shape=[jax.ShapeDtypeStruct((rows, cols), F32)] * 4, compiler_params=_params(),
    )(p3, w2, m2, v2)
    return tuple(o.reshape(shape) for o in outs)


def kernel(x, ffn1_norm, ffn1_w_gate, ffn1_w_up, ffn1_w_down, mix_norm, w_in, dn_conv, dn_a_log, dn_dt_bias, dn_out_norm, pool_w, pool_scale, w_proj_a, w_proj_b, w_proj_c, w_gate, b_gate, w_out, ffn2_norm, ffn2_w_gate, ffn2_w_up, ffn2_w_down, final_norm, loss_target, m_ffn1_norm, m_ffn1_w_gate, m_ffn1_w_up, m_ffn1_w_down, m_mix_norm, m_w_in, m_dn_conv, m_dn_a_log, m_dn_dt_bias, m_dn_out_norm, m_pool_w, m_pool_scale, m_w_proj_a, m_w_proj_b, m_w_proj_c, m_w_gate, m_b_gate, m_w_out, m_ffn2_norm, m_ffn2_w_gate, m_ffn2_w_up, m_ffn2_w_down, m_final_norm, v_ffn1_norm, v_ffn1_w_gate, v_ffn1_w_up, v_ffn1_w_down, v_mix_norm, v_w_in, v_dn_conv, v_dn_a_log, v_dn_dt_bias, v_dn_out_norm, v_pool_w, v_pool_scale, v_w_proj_a, v_w_proj_b, v_w_proj_c, v_w_gate, v_b_gate, v_w_out, v_ffn2_norm, v_ffn2_w_gate, v_ffn2_w_up, v_ffn2_w_down, v_final_norm):
    args = locals()
    wts = {k: args[k] for k in WEIGHTS}
    ms = {k: args["m_" + k] for k in WEIGHTS}
    vs = {k: args["v_" + k] for k in WEIGHTS}

    gathered = _all_gather("gather_weights", [wts[k].astype(BF16) for k in SHARDED])
    full = {k: wts[k] for k in REPLICATED}
    for k, got in zip(SHARDED, gathered):
        ax = SHARD_AXIS[k]
        shp = wts[k].shape
        full[k] = jnp.moveaxis(got, 0, ax).reshape(shp[:ax] + (N_DEV * shp[ax],) + shp[ax + 1:])

    loss_local, dx, grads = _local_step(x, loss_target, full)
    loss = lax.psum(loss_local, ("x", "y", "c"))

    gs = []
    for k in SHARDED:
        ax = SHARD_AXIS[k]
        shp = grads[k].shape
        t = grads[k].reshape(shp[:ax] + (N_DEV, shp[ax] // N_DEV) + shp[ax + 1:])
        gs.append(jnp.moveaxis(t, ax, 0).astype(BF16))
    gr = [grads[k].astype(F32).reshape(wts[k].shape) for k in REPLICATED]
    got_s, got_r = _exchange_grads("exchange_grads", gs, gr)
    parts = dict(zip(SHARDED + REPLICATED, list(got_s) + list(got_r)))

    g_final, deltas, new_m, new_v = {}, {}, {}, {}
    for k in WEIGHTS:
        g_final[k], deltas[k], new_m[k], new_v[k] = _reduce_adamw("adamw_" + k, parts[k], wts[k], ms[k], vs[k])
    return (loss, dx, *[g_final[k] for k in WEIGHTS], *[deltas[k] for k in WEIGHTS], *[new_m[k] for k in WEIGHTS],
            *[new_v[k] for k in WEIGHTS])
```

```python
import functools
import math

import jax
import jax.numpy as jnp
from jax import lax
from jax.experimental import pallas as pl
from jax.experimental.pallas import tpu as pltpu

F32 = jnp.float32
BF16 = jnp.bfloat16

N_DEV = 8
RMS_EPS = 1e-6
L2_EPS = 1e-6
DN_HEADS = 4
DN_DIM = 128
DN_WIDTH = DN_HEADS * DN_DIM
DN_CONV = 5
DN_CHUNK = 64
DN_SUPER = 256
POOL_GROUPS = 4
POOL_DIM = 128
POOL_WIDTH = POOL_GROUPS * POOL_DIM
POOL_MAX_HALF = 8
DA_GROUPS = 3
DA_HEADS = 4
DA_DIM = 64
DA_WIDTH = DA_GROUPS * DA_HEADS * DA_DIM
DA_OUT = DA_HEADS * DA_DIM
DA_DILATIONS = (1, 4, 16)
DA_RADIUS = 64
DA_TQ = 128
ROPE_THETA = 10000.0
MASK_VALUE = -1e30
BA_PAD = 128

ADAM_LR = 0.001
ADAM_B1 = 0.9
ADAM_B2 = 0.999
ADAM_EPS = 1e-08
ADAM_WD = 0.01
ADAM_STEP = 10

VMEM_LIMIT_V7X = 56 * 1024 * 1024
LANES = 1024

SHARDED = ("ffn1_w_gate", "ffn1_w_up", "ffn1_w_down", "w_in", "dn_conv", "w_proj_a", "w_proj_b", "w_proj_c",
           "w_gate", "w_out", "ffn2_w_gate", "ffn2_w_up", "ffn2_w_down")
SHARD_AXIS = {"ffn1_w_gate": 2, "ffn1_w_up": 2, "ffn1_w_down": 1, "w_in": 2, "dn_conv": 2, "w_proj_a": 2,
              "w_proj_b": 2, "w_proj_c": 2, "w_gate": 2, "w_out": 1, "ffn2_w_gate": 2, "ffn2_w_up": 2,
              "ffn2_w_down": 1}
REPLICATED = ("ffn1_norm", "mix_norm", "dn_a_log", "dn_dt_bias", "dn_out_norm", "pool_w", "pool_scale", "b_gate",
              "ffn2_norm", "final_norm")
WEIGHTS = ("ffn1_norm", "ffn1_w_gate", "ffn1_w_up", "ffn1_w_down", "mix_norm", "w_in", "dn_conv", "dn_a_log",
           "dn_dt_bias", "dn_out_norm", "pool_w", "pool_scale", "w_proj_a", "w_proj_b", "w_proj_c", "w_gate",
           "b_gate", "w_out", "ffn2_norm", "ffn2_w_gate", "ffn2_w_up", "ffn2_w_down", "final_norm")


def _params(**kw):
    return pltpu.CompilerParams(vmem_limit_bytes=VMEM_LIMIT_V7X, **kw)


def _pick(n, target, align):
    best = None
    t = align
    while t <= min(n, target):
        if n % t == 0:
            best = t
        t += align
    return best if best is not None else n


_DIMS = {"nn": (((1,), (0,)), ((), ())), "nt": (((1,), (1,)), ((), ())), "tn": (((0,), (0,)), ((), ()))}


def _dg(a, b, mode):
    return lax.dot_general(a, b, _DIMS[mode], preferred_element_type=F32)


def _split2(a):
    hi = a.astype(BF16)
    lo = (a - hi.astype(F32)).astype(BF16)
    return hi, lo


def _dotp(a, b, mode, passes):
    if passes == 1:
        return _dg(a.astype(BF16), b.astype(BF16), mode)
    ah, al = _split2(a.astype(F32))
    bh, bl = _split2(b.astype(F32))
    return _dg(ah, bh, mode) + (_dg(ah, bl, mode) + _dg(al, bh, mode))


@functools.partial(jax.custom_vjp, nondiff_argnums=(2, 3))
def _dot(a, b, mode, passes):
    return _dotp(a, b, mode, passes)


def _dot_fwd(a, b, mode, passes):
    return _dotp(a, b, mode, passes), (a, b)


def _dot_bwd(mode, passes, res, ct):
    a, b = res
    if mode == "nn":
        da, db = _dotp(ct, b, "nt", passes), _dotp(a, ct, "tn", passes)
    elif mode == "nt":
        da, db = _dotp(ct, b, "nn", passes), _dotp(ct, a, "tn", passes)
    else:
        da, db = _dotp(b, ct, "nt", passes), _dotp(a, ct, "nn", passes)
    return da.astype(a.dtype), db.astype(b.dtype)


_dot.defvjp(_dot_fwd, _dot_bwd)


def _split3(x):
    x1 = x.astype(BF16)
    r = x - x1.astype(F32)
    x2 = r.astype(BF16)
    x3 = (r - x2.astype(F32)).astype(BF16)
    return x1, x2, x3


def _mdotp(mask, x, mode):
    x1, x2, x3 = _split3(x)
    return _dg(mask, x1, mode) + (_dg(mask, x2, mode) + _dg(mask, x3, mode))


@jax.custom_vjp
def _mdot(mask, x):
    return _mdotp(mask, x, "nn")


def _mdot_fwd(mask, x):
    return _mdotp(mask, x, "nn"), mask


def _mdot_bwd(mask, ct):
    return jnp.zeros_like(mask), _mdotp(mask, ct, "tn")


_mdot.defvjp(_mdot_fwd, _mdot_bwd)


_SOLVE_SQUARINGS = int(math.log2(DN_CHUNK)) - 1


def _unit_solve_fwd(A, R):
    X = R - _dotp(A, R, "nn", 3)
    P, powers = A, []
    for _ in range(_SOLVE_SQUARINGS):
        P = _dotp(P, P, "nn", 3)
        powers.append(P)
        X = X + _dotp(P, X, "nn", 3)
    return X, (A, tuple(powers), X)


@jax.custom_vjp
def _unit_solve(A, R):
    return _unit_solve_fwd(A, R)[0]


def _unit_solve_bwd(res, dX):
    A, powers, X = res
    Y = dX - _dotp(A, dX, "tn", 3)
    for P in powers:
        Y = Y + _dotp(P, Y, "tn", 3)
    return -_dotp(Y, X, "nt", 3), Y


_unit_solve.defvjp(_unit_solve_fwd, _unit_solve_bwd)


def _shift_impl(x, o):
    if o == 0:
        return x
    n = x.shape[0]
    y = pltpu.roll(x, (-o) % n, axis=0)
    t = lax.broadcasted_iota(jnp.int32, x.shape, 0) + o
    return jnp.where((t >= 0) & (t < n), y, 0.0)


@functools.partial(jax.custom_vjp, nondiff_argnums=(1,))
def _shift(x, o):
    return _shift_impl(x, o)


def _shift_fwd(x, o):
    return _shift_impl(x, o), None


def _shift_bwd(o, _, ct):
    return (_shift_impl(ct, -o),)


_shift.defvjp(_shift_fwd, _shift_bwd)


def _rot_impl(x):
    w = x.shape[1]
    half = DA_DIM // 2
    lane = lax.broadcasted_iota(jnp.int32, x.shape, 1)
    first = (lane & (DA_DIM - 1)) < half
    return jnp.where(first, -pltpu.roll(x, w - half, axis=1), pltpu.roll(x, half, axis=1))


@jax.custom_vjp
def _rot(x):
    return _rot_impl(x)


def _rot_fwd(x):
    return _rot_impl(x), None


def _rot_bwd(_, ct):
    return (-_rot_impl(ct),)


_rot.defvjp(_rot_fwd, _rot_bwd)


def _sigmoid(x):
    return 1.0 / (1.0 + jnp.exp(-x))


def _silu(x):
    return x * _sigmoid(x)


def _softplus(x):
    return jnp.maximum(x, 0.0) + jnp.log(1.0 + jnp.exp(-jnp.abs(x)))


def _rms(x, gain):
    return x * lax.rsqrt(jnp.mean(x * x, axis=-1, keepdims=True) + RMS_EPS) * gain


class _In:
    def __init__(self, arr, block, imap, kind="t", acc=False, g=None, gdtype=None, split=False):
        self.arr, self.block, self.imap, self.kind, self.acc, self.g, self.gdtype = arr, block, imap, kind, acc, g, gdtype
        self.split = split


class _Out:
    def __init__(self, shape, dtype, block, imap, split=False):
        self.shape, self.dtype, self.block, self.imap, self.split = shape, dtype, block, imap, split


def _grid_edges(grid):
    first = last = None
    for a, n in enumerate(grid):
        f, l = pl.program_id(a) == 0, pl.program_id(a) == n - 1
        first = f if first is None else jnp.logical_and(first, f)
        last = l if last is None else jnp.logical_and(last, l)
    return first, last


def _comm_plumbing(comm):
    if comm is None:
        return [], [], [], lambda refs: None, lambda refs: None
    kind, arrs = comm
    n = len(arrs)
    if kind == "gather":
        shapes = [jax.ShapeDtypeStruct((N_DEV,) + a.shape, a.dtype) for a in arrs]
        start, finish = _gather_start, _gather_finish
    else:
        shapes = [jax.ShapeDtypeStruct(a.shape, a.dtype) for a in arrs]
        start, finish = _exchange_start, _exchange_finish
    sems = [pltpu.SemaphoreType.DMA((n, N_DEV - 1)), pltpu.SemaphoreType.DMA((n, N_DEV - 1)),
            pltpu.SemaphoreType.DMA((n,))]
    return list(arrs), shapes, sems, start, finish


def _first_step(acc_from, ngrid):
    c = None
    for a in range(acc_from, ngrid):
        t = pl.program_id(a) == 0
        c = t if c is None else jnp.logical_and(c, t)
    return c


def _tile_fwd(name, f, grid, ins, outs, sub=1, comm=None):
    n_in, n_out = len(ins), len(outs)
    ngrid = len(grid)
    c_arrs, c_shapes, c_sems, c_start, c_finish = _comm_plumbing(comm)
    nc = len(c_arrs)

    def body(*refs):
        in_refs, c_in = refs[:n_in], refs[n_in:n_in + nc]
        out_refs, c_out = refs[n_in + nc:n_in + nc + n_out], refs[n_in + nc + n_out:n_in + 2 * nc + n_out]
        sems = refs[n_in + 2 * nc + n_out:]
        pids = tuple(pl.program_id(a) for a in range(ngrid))
        if nc:
            first, last = _grid_edges(grid)
            pl.when(first)(lambda: c_start(c_in, c_out, *sems))
        for s in range(sub):
            vals = [(r[s] if i.split else r[...]) for r, i in zip(in_refs, ins)]
            res = f(pids + ((s,) if sub > 1 else ()), *vals)
            for r, o, v in zip(out_refs, outs, res):
                if o.split:
                    r[s] = v.astype(r.dtype)
                else:
                    r[...] = v.astype(r.dtype)
        if nc:
            pl.when(last)(lambda: c_finish(c_in, c_out, *sems))

    res = pl.pallas_call(
        body, name=name, grid=grid,
        in_specs=[pl.BlockSpec(i.block, i.imap) for i in ins] + [_ANY] * nc,
        out_specs=[pl.BlockSpec(o.block, o.imap) for o in outs] + [_ANY] * nc,
        out_shape=[jax.ShapeDtypeStruct(o.shape, o.dtype) for o in outs] + c_shapes,
        scratch_shapes=c_sems, compiler_params=_params(),
    )(*[i.arr for i in ins], *c_arrs)
    return (res[:n_out], res[n_out:]) if nc else res


def _tile_bwd(name, f, grid, ins, outs, cts, acc_from=None, addends=None, sub=1, comm=None):
    n_in, n_out = len(ins), len(outs)
    ngrid = len(grid)
    diff = [k for k, i in enumerate(ins) if i.kind == "t"]
    addends = addends or {}
    add_keys = sorted(addends)
    n_add, n_g = len(add_keys), len(diff)
    c_arrs, c_shapes, c_sems, c_start, c_finish = _comm_plumbing(comm)
    nc = len(c_arrs)

    def body(*refs):
        pids = tuple(pl.program_id(a) for a in range(ngrid))
        in_refs = refs[:n_in]
        ct_refs = refs[n_in:n_in + n_out]
        add_refs = refs[n_in + n_out:n_in + n_out + n_add]
        o = n_in + n_out + n_add
        c_in, g_refs, c_out, sems = refs[o:o + nc], refs[o + nc:o + nc + n_g], refs[o + nc + n_g:o + 2 * nc + n_g], \
            refs[o + 2 * nc + n_g:]
        if nc:
            first_step, last_step = _grid_edges(grid)
            pl.when(first_step)(lambda: c_start(c_in, c_out, *sems))
        shared = {}
        for s in range(sub):
            vals = [(r[s] if i.split else r[...]) for r, i in zip(in_refs, ins)]
            dvals = [vals[k].astype(F32) for k in diff]

            def g(*d, vals=vals, s=s):
                full = list(vals)
                for k, dk in zip(diff, d):
                    full[k] = dk
                return tuple(f(pids + ((s,) if sub > 1 else ()), *full))

            res, vjp = jax.vjp(g, *dvals)
            cvals = [(c[s] if o_.split else c[...]).astype(r.dtype) for c, o_, r in zip(ct_refs, outs, res)]
            grads = vjp(tuple(cvals))
            for k, gr, gref in zip(diff, grads, g_refs):
                if ins[k].split:
                    gref[s] = gr.astype(gref.dtype)
                else:
                    shared[k] = gr if k not in shared else shared[k] + gr
        first = _first_step(acc_from, ngrid) if acc_from is not None else None
        for k, gref in zip(diff, g_refs):
            if ins[k].split:
                continue
            gr = shared[k]
            if k in addends:
                gr = gr + add_refs[add_keys.index(k)][...].astype(F32)
            if ins[k].acc and first is not None:
                @pl.when(first)
                def _(gr=gr, gref=gref):
                    gref[...] = gr.astype(gref.dtype)

                @pl.when(jnp.logical_not(first))
                def _(gr=gr, gref=gref):
                    gref[...] += gr.astype(gref.dtype)
            else:
                gref[...] = gr.astype(gref.dtype)
        if nc:
            pl.when(last_step)(lambda: c_finish(c_in, c_out, *sems))

    g_shapes, g_specs = [], []
    for k in diff:
        i = ins[k]
        if i.g is not None:
            shape, imap = i.g
        else:
            shape, imap = i.arr.shape, i.imap
        dt = i.gdtype or (F32 if i.acc else i.arr.dtype)
        g_shapes.append(jax.ShapeDtypeStruct(shape, dt))
        g_specs.append(pl.BlockSpec(i.block, imap))
    add_specs = [pl.BlockSpec(ins[k].block, ins[k].g[1] if ins[k].g is not None else ins[k].imap) for k in add_keys]
    res = pl.pallas_call(
        body, name=name, grid=grid,
        in_specs=[pl.BlockSpec(i.block, i.imap) for i in ins] + [pl.BlockSpec(o.block, o.imap) for o in outs] + add_specs
        + [_ANY] * nc,
        out_specs=g_specs + [_ANY] * nc, out_shape=g_shapes + c_shapes,
        scratch_shapes=c_sems, compiler_params=_params(),
    )(*[i.arr for i in ins], *cts, *[addends[k] for k in add_keys], *c_arrs)
    return (res[:n_g], res[n_g:]) if nc else res


def _mm(name, a, b, mode, out_dtype=F32, add=None, tm=512, tn=512, tk=512):
    if mode == "nn":
        (M, K), N = a.shape, b.shape[1]
    elif mode == "nt":
        (M, K), N = a.shape, b.shape[0]
    else:
        (K, M), N = a.shape, b.shape[1]
    tm, tn, tk = _pick(M, tm, 128), _pick(N, tn, 128), _pick(K, tk, 128)
    nk = K // tk
    a_spec = pl.BlockSpec((tk, tm), lambda i, j, k: (k, i)) if mode == "tn" else pl.BlockSpec((tm, tk), lambda i, j, k: (i, k))
    b_spec = pl.BlockSpec((tn, tk), lambda i, j, k: (j, k)) if mode == "nt" else pl.BlockSpec((tk, tn), lambda i, j, k: (k, j))
    o_spec = pl.BlockSpec((tm, tn), lambda i, j, k: (i, j))

    def body(*refs):
        if add is None:
            a_ref, b_ref, o_ref, acc = refs
            add_ref = None
        else:
            a_ref, b_ref, add_ref, o_ref, acc = refs
        k = pl.program_id(2)

        @pl.when(k == 0)
        def _():
            acc[...] = jnp.zeros_like(acc)

        acc[...] += _dg(a_ref[...].astype(BF16), b_ref[...].astype(BF16), mode)

        @pl.when(k == nk - 1)
        def _():
            r = acc[...]
            if add_ref is not None:
                r = r + add_ref[...].astype(F32)
            o_ref[...] = r.astype(o_ref.dtype)

    ops = (a, b) if add is None else (a, b, add)
    specs = [a_spec, b_spec] + ([] if add is None else [o_spec])
    return pl.pallas_call(
        body, name=name, grid=(M // tm, N // tn, nk), in_specs=specs, out_specs=o_spec,
        out_shape=jax.ShapeDtypeStruct((M, N), out_dtype), scratch_shapes=[pltpu.VMEM((tm, tn), F32)],
        compiler_params=_params(dimension_semantics=("parallel", "parallel", "arbitrary")),
    )(*ops)


def _ffn_fwd(name, x, gain, wg, wu, wd):
    T, D = x.shape
    F = wg.shape[1]
    tm, tf = _pick(T, 512, 8), _pick(F, 256, 128)
    nf = F // tf

    def body(x_ref, g_ref, wg_ref, wu_ref, wd_ref, o_ref, h_ref, acc):
        j = pl.program_id(1)

        @pl.when(j == 0)
        def _():
            h_ref[...] = _rms(x_ref[...], g_ref[...]).astype(BF16)
            acc[...] = jnp.zeros_like(acc)

        h = h_ref[...]
        a = _dg(h, wg_ref[...], "nn")
        b = _dg(h, wu_ref[...], "nn")
        s = (_silu(a) * b).astype(BF16)
        acc[...] += _dg(s, wd_ref[...], "nn")

        @pl.when(j == nf - 1)
        def _():
            o_ref[...] = x_ref[...] + 0.5 * acc[...]

    return pl.pallas_call(
        body, name=name, grid=(T // tm, nf),
        in_specs=[pl.BlockSpec((tm, D), lambda i, j: (i, 0)), pl.BlockSpec((1, D), lambda i, j: (0, 0)),
                  pl.BlockSpec((D, tf), lambda i, j: (0, j)), pl.BlockSpec((D, tf), lambda i, j: (0, j)),
                  pl.BlockSpec((tf, D), lambda i, j: (j, 0))],
        out_specs=pl.BlockSpec((tm, D), lambda i, j: (i, 0)),
        out_shape=jax.ShapeDtypeStruct((T, D), F32),
        scratch_shapes=[pltpu.VMEM((tm, D), BF16), pltpu.VMEM((tm, D), F32)],
        compiler_params=_params(dimension_semantics=("parallel", "arbitrary")),
    )(x, gain, wg, wu, wd)


def _ffn_bwd(name, x, gain, wg, wu, wd, dy):
    T, D = x.shape
    F = wg.shape[1]
    tm, tf = _pick(T, 512, 8), _pick(F, 256, 128)
    nf = F // tf

    def body(x_ref, g_ref, wg_ref, wu_ref, wd_ref, dy_ref, dx_ref, dg_ref, da_ref, db_ref, s_ref, h_ref, dyh_ref, dh):
        i, j = pl.program_id(0), pl.program_id(1)

        @pl.when(j == 0)
        def _():
            h_ref[...] = _rms(x_ref[...], g_ref[...]).astype(BF16)
            dyh_ref[...] = (0.5 * dy_ref[...]).astype(BF16)
            dh[...] = jnp.zeros_like(dh)

        h = h_ref[...]
        a = _dg(h, wg_ref[...], "nn")
        b = _dg(h, wu_ref[...], "nn")
        ds = _dg(dyh_ref[...], wd_ref[...], "nt")
        sig = _sigmoid(a)
        silu = a * sig
        da = (ds * b * (sig * (1.0 + a * (1.0 - sig)))).astype(BF16)
        db = (ds * silu).astype(BF16)
        da_ref[...] = da
        db_ref[...] = db
        s_ref[...] = (silu * b).astype(BF16)
        dh[...] += _dg(da, wg_ref[...], "nt") + _dg(db, wu_ref[...], "nt")

        @pl.when(j == nf - 1)
        def _():
            _, vjp = jax.vjp(_rms, x_ref[...], g_ref[...])
            dxn, dgn = vjp(dh[...])
            dx_ref[...] = dy_ref[...] + dxn

            @pl.when(i == 0)
            def _():
                dg_ref[...] = dgn

            @pl.when(i != 0)
            def _():
                dg_ref[...] += dgn

    row = lambda i, j: (i, 0)
    col = lambda i, j: (i, j)
    dx, dgain, da, db, s, h, dyh = pl.pallas_call(
        body, name=name, grid=(T // tm, nf),
        in_specs=[pl.BlockSpec((tm, D), row), pl.BlockSpec((1, D), lambda i, j: (0, 0)),
                  pl.BlockSpec((D, tf), lambda i, j: (0, j)), pl.BlockSpec((D, tf), lambda i, j: (0, j)),
                  pl.BlockSpec((tf, D), lambda i, j: (j, 0)), pl.BlockSpec((tm, D), row)],
        out_specs=[pl.BlockSpec((tm, D), row), pl.BlockSpec((1, D), lambda i, j: (0, 0)),
                   pl.BlockSpec((tm, tf), col), pl.BlockSpec((tm, tf), col), pl.BlockSpec((tm, tf), col),
                   pl.BlockSpec((tm, D), row), pl.BlockSpec((tm, D), row)],
        out_shape=[jax.ShapeDtypeStruct((T, D), F32), jax.ShapeDtypeStruct((1, D), F32),
                   jax.ShapeDtypeStruct((T, F), BF16), jax.ShapeDtypeStruct((T, F), BF16),
                   jax.ShapeDtypeStruct((T, F), BF16), jax.ShapeDtypeStruct((T, D), BF16),
                   jax.ShapeDtypeStruct((T, D), BF16)],
        scratch_shapes=[pltpu.VMEM((tm, D), F32)],
        compiler_params=_params(),
    )(x, gain, wg, wu, wd, dy)
    dwg = _mm(name + "_dwg", h, da, "tn", tm=1024, tn=1408, tk=1024)
    dwu = _mm(name + "_dwu", h, db, "tn", tm=1024, tn=1408, tk=1024)
    dwd = _mm(name + "_dwd", s, dyh, "tn", tm=1408, tn=1024, tk=1024)
    return dx, dgain, dwg, dwu, dwd


def _norm_f(pids, x, gain):
    return (_rms(x, gain),)


def _dn_conv_f(pids, x, w):
    j = pids[0]
    tap = lax.broadcasted_iota(jnp.int32, w.shape, 0)
    y = jnp.zeros_like(x)
    for t in range(DN_CONV):
        wt = jnp.sum(jnp.where(tap == t, w, 0.0), axis=0, keepdims=True)
        y = y + _shift(x, t - DN_CONV // 2) * wt
    y = _silu(y)
    n = y * lax.rsqrt(jnp.sum(y * y, axis=-1, keepdims=True) + L2_EPS)
    is_q = (j < DN_HEADS).astype(F32)
    is_qk = (j < 2 * DN_HEADS).astype(F32)
    scale = is_q * (DN_DIM ** -0.5) + (1.0 - is_q)
    return ((is_qk * n + (1.0 - is_qk) * y) * scale,)


def _dn_gate_f(pids, braw, araw, a_log, dt_bias):
    beta = _sigmoid(braw)
    g = -jnp.exp(a_log) * _softplus(araw + dt_bias)
    return beta, g


def _dn_prep_f(pids, q, k, v, brow, grow):
    cs = DN_SUPER
    sign = 1 - 2 * pids[2]
    ii = lax.broadcasted_iota(jnp.int32, (cs, cs), 0)
    jj = lax.broadcasted_iota(jnp.int32, (cs, cs), 1)
    shift = int(math.log2(DN_CHUNK))
    same = (ii >> shift) == (jj >> shift)
    d = (ii - jj) * sign
    incl = same & (d >= 0)
    strict = same & (d > 0)
    eye = ii == jj
    g_col = jnp.sum(jnp.where(eye, jnp.broadcast_to(grow, (cs, cs)), 0.0), axis=1, keepdims=True)
    b_col = jnp.sum(jnp.where(eye, jnp.broadcast_to(brow, (cs, cs)), 0.0), axis=1, keepdims=True)
    g128 = jnp.broadcast_to(g_col, (cs, DN_DIM))
    G = _mdot(incl.astype(BF16), g128)
    Gt = _mdot(same.astype(BF16), g128)
    Gc = jnp.concatenate([G, G], axis=1)
    Grow = jnp.sum(jnp.where(eye, Gc, 0.0), axis=0, keepdims=True)
    decay = jnp.exp(jnp.where(incl, Gc - Grow, MASK_VALUE))
    eG = jnp.exp(G)
    kb = k * b_col
    A = jnp.where(strict, _dot(kb, k, "nt", 1) * decay, 0.0)
    X = _unit_solve(A, jnp.concatenate([v * b_col, kb * eG], axis=1))
    qk = jnp.where(incl, _dot(q, k, "nt", 1) * decay, 0.0)
    return X, qk, q * eG, k * jnp.exp(Gt - G), jnp.exp(Gt)


def _dn_out_f(pids, of, ob, z, gain):
    return (_rms(of + ob, gain) * _silu(z),)


def _pool_f(pids, u, w, scale):
    g = pids[0]
    half = jnp.left_shift(1, g)
    n = u.shape[0]
    pos = lax.broadcasted_iota(jnp.int32, (n, 1), 0)
    tot = jnp.zeros_like(u)
    cnt = jnp.zeros((n, 1), F32)
    for o in range(-POOL_MAX_HALF, POOL_MAX_HALF):
        use = ((o >= -half) & (o < half)).astype(F32)
        tot = tot + use * _shift(u, o)
        cnt = cnt + use * ((pos + o >= 0) & (pos + o < n)).astype(F32)
    pooled = tot / cnt - u
    return (_dot(pooled, w, "nn", 1) * scale,)


def _rope_f(pids, q, k, v, cos, sin):
    qr = (q * cos + _rot(q) * sin) * (DA_DIM ** -0.5)
    kr = k * cos + _rot(k) * sin
    return qr, kr, v


def _attn_head(q, k, v, qpos0, kpos0):
    s = _dot(q, k, "nt", 1)
    qi = qpos0 + lax.broadcasted_iota(jnp.int32, s.shape, 0)
    kj = kpos0 + lax.broadcasted_iota(jnp.int32, s.shape, 1)
    s = jnp.where(jnp.abs(kj - qi) <= DA_RADIUS, s, MASK_VALUE)
    m = lax.stop_gradient(jnp.max(s, axis=1, keepdims=True))
    p = jnp.exp(s - m)
    l = jnp.sum(p, axis=1, keepdims=True)
    o = _dot(p, v, "nn", 1) / l
    return o, jnp.broadcast_to(m + jnp.log(l), o.shape)


def _merge_f(pids, o0, o1, o2, l0, l1, l2):
    m = jnp.maximum(jnp.maximum(l0, l1), l2)
    e0, e1, e2 = jnp.exp(l0 - m), jnp.exp(l1 - m), jnp.exp(l2 - m)
    return ((e0 * o0 + e1 * o1 + e2 * o2) / (e0 + e1 + e2),)


def _gate_f(pids, g0, g1, g2, ya, yb, yc, b0, b1, b2):
    return (_sigmoid(g0 + b0) * ya + _sigmoid(g1 + b1) * yb + _sigmoid(g2 + b2) * yc,)


def _attn_window(i, L, tq, W):
    k0 = jnp.clip(i * tq - DA_RADIUS, 0, L - W)
    return pl.multiple_of(k0, DA_RADIUS)


def _attn_fwd(name, q, k, v):
    NS, L, HD = q.shape
    tq = min(DA_TQ, L)
    W = min(L, tq + 2 * DA_RADIUS)

    def body(q_ref, k_ref, v_ref, o_ref, l_ref):
        i = pl.program_id(1)
        k0 = _attn_window(i, L, tq, W)
        for h in range(DA_HEADS):
            hs = slice(h * DA_DIM, (h + 1) * DA_DIM)
            o, lse = _attn_head(q_ref[:, hs], k_ref[pl.ds(k0, W), hs], v_ref[pl.ds(k0, W), hs], i * tq, k0)
            o_ref[:, hs] = o
            l_ref[:, hs] = lse

    qs = pl.BlockSpec((None, tq, HD), lambda s, i: (s, i, 0))
    ks = pl.BlockSpec((None, L, HD), lambda s, i: (s, 0, 0))
    return pl.pallas_call(
        body, name=name, grid=(NS, L // tq), in_specs=[qs, ks, ks], out_specs=[qs, qs],
        out_shape=[jax.ShapeDtypeStruct((NS, L, HD), F32)] * 2, compiler_params=_params(),
    )(q, k, v)


def _attn_bwd(name, q, k, v, do, dl):
    NS, L, HD = q.shape
    tq = min(DA_TQ, L)
    W = min(L, tq + 2 * DA_RADIUS)

    def body(q_ref, k_ref, v_ref, do_ref, dl_ref, dq_ref, dk_ref, dv_ref):
        i = pl.program_id(1)
        k0 = _attn_window(i, L, tq, W)

        @pl.when(i == 0)
        def _():
            dk_ref[...] = jnp.zeros_like(dk_ref)
            dv_ref[...] = jnp.zeros_like(dv_ref)

        for h in range(DA_HEADS):
            hs = slice(h * DA_DIM, (h + 1) * DA_DIM)
            f = functools.partial(_attn_head, qpos0=i * tq, kpos0=k0)
            _, vjp = jax.vjp(f, q_ref[:, hs].astype(F32), k_ref[pl.ds(k0, W), hs].astype(F32),
                             v_ref[pl.ds(k0, W), hs].astype(F32))
            dq, dk, dv = vjp((do_ref[:, hs], dl_ref[:, hs]))
            dq_ref[:, hs] = dq
            dk_ref[pl.ds(k0, W), hs] += dk
            dv_ref[pl.ds(k0, W), hs] += dv

    qs = pl.BlockSpec((None, tq, HD), lambda s, i: (s, i, 0))
    ks = pl.BlockSpec((None, L, HD), lambda s, i: (s, 0, 0))
    return pl.pallas_call(
        body, name=name, grid=(NS, L // tq), in_specs=[qs, ks, ks, qs, qs], out_specs=[qs, ks, ks],
        out_shape=[jax.ShapeDtypeStruct((NS, L, HD), F32)] * 3, compiler_params=_params(),
    )(q, k, v, do, dl)


def _scan_chunk(t, rev, N):
    c = jnp.where(rev, N - 1 - t, t)
    per = DN_SUPER // DN_CHUNK
    return c, pl.multiple_of(c * DN_CHUNK, DN_CHUNK), pl.multiple_of((c % per) * DN_CHUNK, DN_CHUNK), \
        pl.multiple_of((c // per) * DN_SUPER, DN_SUPER)


def _dn_scan_fwd(name, uw, qk, qd, kd, gl, B):
    R, T, _ = uw.shape
    S = T // B
    N = S // DN_CHUNK
    C, DK = DN_CHUNK, DN_DIM

    def body(uw_ref, qk_ref, qd_ref, kd_ref, gl_ref, o_ref, st_ref, vn_ref):
        rev = pl.program_id(1) >= DN_HEADS
        vn_ref[...] = jnp.zeros_like(vn_ref)

        def step(t, state):
            c, r0, w0, s0 = _scan_chunk(t, rev, N)
            rows = pl.ds(r0, C)
            st_ref[c] = state
            vnew = uw_ref[rows, 0:DK] - _dotp(uw_ref[rows, DK:2 * DK], state, "nn", 1)
            vn_ref[pl.ds(w0, C), :] = vnew
            o_ref[rows, :] = _dotp(qd_ref[rows, :], state, "nn", 1) + _dotp(qk_ref[rows, :], vn_ref[...], "nn", 1)
            return state * gl_ref[pl.ds(r0, 1), :] + _dotp(kd_ref[rows, :], vnew, "tn", 1)

        lax.fori_loop(0, N, step, jnp.zeros((DK, DK), F32))

    def seq(w):
        return pl.BlockSpec((None, S, w), lambda b, r: (r, b, 0))

    return pl.pallas_call(
        body, name=name, grid=(B, R),
        in_specs=[seq(2 * DK), seq(DN_SUPER), seq(DK), seq(DK), seq(DK)],
        out_specs=[seq(DK), pl.BlockSpec((None, None, N, DK, DK), lambda b, r: (b, r, 0, 0, 0))],
        out_shape=[jax.ShapeDtypeStruct((R, T, DK), F32), jax.ShapeDtypeStruct((B, R, N, DK, DK), F32)],
        scratch_shapes=[pltpu.VMEM((DN_SUPER, DK), F32)], compiler_params=_params(),
    )(uw, qk, qd, kd, gl)


def _dn_scan_bwd(name, uw, qk, qd, kd, gl, st, do, B):
    R, T, _ = uw.shape
    S = T // B
    N = S // DN_CHUNK
    C, DK = DN_CHUNK, DN_DIM

    def body(uw_ref, qk_ref, qd_ref, kd_ref, gl_ref, st_ref, do_ref, duw_ref, dqk_ref, dqd_ref, dkd_ref, dgl_ref,
             vn_ref, tmp_ref):
        rev = pl.program_id(1) >= DN_HEADS
        vn_ref[...] = jnp.zeros_like(vn_ref)
        dgl_ref[...] = jnp.zeros_like(dgl_ref)

        def step(t, dstate):
            c, r0, w0, s0 = _scan_chunk(N - 1 - t, rev, N)
            rows = pl.ds(r0, C)
            state = st_ref[c]
            w = uw_ref[rows, DK:2 * DK]
            vnew = uw_ref[rows, 0:DK] - _dotp(w, state, "nn", 1)
            vn_ref[pl.ds(w0, C), :] = vnew
            do_c = do_ref[rows, :]
            tmp_ref[...] = _dotp(qk_ref[rows, :], do_c, "tn", 1)
            dvn = tmp_ref[pl.ds(w0, C), :] + _dotp(kd_ref[rows, :], dstate, "nn", 1)
            dqk_ref[rows, :] = _dotp(do_c, vn_ref[...], "nt", 1)
            dqd_ref[rows, :] = _dotp(do_c, state, "nt", 1)
            dkd_ref[rows, :] = _dotp(vnew, dstate, "nt", 1)
            dgl_ref[pl.ds(r0, 1), :] = jnp.sum(state * dstate, axis=0, keepdims=True)
            duw_ref[rows, 0:DK] = dvn
            duw_ref[rows, DK:2 * DK] = -_dotp(dvn, state, "nt", 1)
            return (_dotp(qd_ref[rows, :], do_c, "tn", 1) + dstate * gl_ref[pl.ds(r0, 1), :]
                    - _dotp(w, dvn, "tn", 1))

        lax.fori_loop(0, N, step, jnp.zeros((DK, DK), F32))

    def seq(w):
        return pl.BlockSpec((None, S, w), lambda b, r: (r, b, 0))

    return pl.pallas_call(
        body, name=name, grid=(B, R),
        in_specs=[seq(2 * DK), seq(DN_SUPER), seq(DK), seq(DK), seq(DK),
                  pl.BlockSpec((None, None, N, DK, DK), lambda b, r: (b, r, 0, 0, 0)),
                  pl.BlockSpec((None, S, DK), lambda b, r: (r % DN_HEADS, b, 0))],
        out_specs=[seq(2 * DK), seq(DN_SUPER), seq(DK), seq(DK), seq(DK)],
        out_shape=[jax.ShapeDtypeStruct((R, T, 2 * DK), F32), jax.ShapeDtypeStruct((R, T, DN_SUPER), F32),
                   jax.ShapeDtypeStruct((R, T, DK), F32), jax.ShapeDtypeStruct((R, T, DK), F32),
                   jax.ShapeDtypeStruct((R, T, DK), F32)],
        scratch_shapes=[pltpu.VMEM((DN_SUPER, DK), F32), pltpu.VMEM((DN_SUPER, DK), F32)],
        compiler_params=_params(),
    )(uw, qk, qd, kd, gl, st, do)


def _loss_fwd_bwd(name, x, gain, target):
    T, D = x.shape
    tm = _pick(T, 512, 8)

    def body(x_ref, g_ref, t_ref, loss_ref, dx_ref, dg_ref):
        i = pl.program_id(0)

        def f(xv, gv):
            e = _rms(xv, gv) - t_ref[...]
            return 0.5 * jnp.sum(jnp.mean(e * e, axis=-1, keepdims=True))

        val, (dx, dg) = jax.value_and_grad(f, argnums=(0, 1))(x_ref[...], g_ref[...])
        dx_ref[...] = dx
        part = jnp.full(loss_ref.shape, val, F32)

        @pl.when(i == 0)
        def _():
            dg_ref[...] = dg
            loss_ref[...] = part

        @pl.when(i != 0)
        def _():
            dg_ref[...] += dg
            loss_ref[...] += part

    return pl.pallas_call(
        body, name=name, grid=(T // tm,),
        in_specs=[pl.BlockSpec((tm, D), lambda i: (i, 0)), pl.BlockSpec((1, D), lambda i: (0, 0)),
                  pl.BlockSpec((tm, D), lambda i: (i, 0))],
        out_specs=[pl.BlockSpec((8, 128), lambda i: (0, 0)), pl.BlockSpec((tm, D), lambda i: (i, 0)),
                   pl.BlockSpec((1, D), lambda i: (0, 0))],
        out_shape=[jax.ShapeDtypeStruct((8, 128), F32), jax.ShapeDtypeStruct((T, D), F32),
                   jax.ShapeDtypeStruct((1, D), F32)],
        compiler_params=_params(),
    )(x, gain, target)


class _Cols:
    def __init__(self, D):
        assert D % 256 == 0
        self.gate = 0
        self.da = 3 * D
        self.qkv = self.da + 3 * DA_WIDTH
        self.z = self.qkv + 3 * DN_WIDTH
        self.pool = self.z + DN_WIDTH
        self.ba = self.pool + POOL_WIDTH
        self.total = self.ba + BA_PAD


def _rope_tables(S):
    half = DA_DIM // 2
    inv_freq = ROPE_THETA ** (-jnp.arange(half, dtype=F32) / half)
    ang = jnp.arange(S, dtype=F32)[:, None] * inv_freq[None, :]
    reps = DA_WIDTH // DA_DIM
    cos = jnp.tile(jnp.concatenate([jnp.cos(ang), jnp.cos(ang)], axis=1), (1, reps))
    sin = jnp.tile(jnp.concatenate([jnp.sin(ang), jnp.sin(ang)], axis=1), (1, reps))
    return cos, sin


def _to_strided(t, B, dil):
    T, w = t.shape
    L = T // B // dil
    return t.reshape(B, L, dil, w).transpose(0, 2, 1, 3).reshape(B * dil, L, w)


def _from_strided(t, B, dil):
    NS, L, w = t.shape
    return t.reshape(B, dil, L, w).transpose(0, 2, 1, 3).reshape(B * dil * L, w)


def _mixer(l, x1, w, B, host_gather=None):
    T, D = x1.shape
    S = T // B
    c = _Cols(D)
    tm = _pick(S, 512, 8)
    nmS = S // tm
    n = f"l{l}_"

    norm_ins = [_In(x1, (tm, D), lambda i: (i, 0)), _In(w["mix_norm"], (1, D), lambda i: (0, 0), acc=True)]
    norm_outs = [_Out((T, D), BF16, (tm, D), lambda i: (i, 0))]
    (h,) = _tile_fwd(n + "norm", _norm_f, (T // tm,), norm_ins, norm_outs)
    P = _mm(n + "proj", h, w["w_cat"], "nn", tn=896, tk=1024)
    baT = P[:, c.ba:c.ba + 16].T

    cb = c.qkv // DN_DIM
    conv_ins = [_In(P, (S, DN_DIM), lambda j, b: (b, cb + j), g=((T, 3 * DN_WIDTH), lambda j, b: (b, j)), gdtype=BF16),
                _In(w["dn_conv"], (DN_CONV, DN_DIM), lambda j, b: (0, j), acc=True)]
    conv_outs = [_Out((T, 3 * DN_WIDTH), F32, (S, DN_DIM), lambda j, b: (b, j))]
    conv_grid = (3 * DN_HEADS, B)
    (qkvc,) = _tile_fwd(n + "dnconv", _dn_conv_f, conv_grid, conv_ins, conv_outs)

    tg = _pick(T, 2048, 128)
    gate_ins = [_In(baT, (8, tg), lambda i: (0, i)), _In(baT, (8, tg), lambda i: (1, i)),
                _In(w["dn_a_log"], (8, 1), lambda i: (0, 0), acc=True),
                _In(w["dn_dt_bias"], (8, 1), lambda i: (0, 0), acc=True)]
    gate_ins[0].g = ((8, T), lambda i: (0, i))
    gate_ins[1].g = ((8, T), lambda i: (0, i))
    gate_outs = [_Out((8, T), F32, (8, tg), lambda i: (0, i))] * 2
    beta, gdec = _tile_fwd(n + "dngate", _dn_gate_f, (T // tg,), gate_ins, gate_outs)

    NSC = T // DN_SUPER
    beta4 = beta.reshape(2, DN_HEADS, NSC, 1, DN_SUPER)
    gdec4 = gdec.reshape(2, DN_HEADS, NSC, 1, DN_SUPER)
    R = 2 * DN_HEADS

    def qkv_in(off):
        return _In(qkvc, (DN_SUPER, DN_DIM), lambda hh, m: (m, off + hh), acc=True,
                   g=((T, DN_WIDTH), lambda hh, m: (m, hh)))

    def row_in(a):
        return _In(a, (2, None, None, 1, DN_SUPER), lambda hh, m: (0, hh, m, 0, 0), split=True)

    def chain_out(wd):
        return _Out((2, DN_HEADS, T, wd), F32, (2, None, DN_SUPER, wd), lambda hh, m: (0, hh, m, 0), split=True)

    prep_ins = [qkv_in(0), qkv_in(DN_HEADS), qkv_in(2 * DN_HEADS), row_in(beta4), row_in(gdec4)]
    prep_outs = [chain_out(2 * DN_DIM), chain_out(DN_SUPER), chain_out(DN_DIM), chain_out(DN_DIM), chain_out(DN_DIM)]
    prep_grid = (DN_HEADS, NSC)
    prep_res = _tile_fwd(n + "dnprep", _dn_prep_f, prep_grid, prep_ins, prep_outs, sub=2, comm=host_gather)
    gathered_next = None
    if host_gather is not None:
        prep_res, gathered_next = prep_res
    uw, qk, qd, kd, gl = (t.reshape((R,) + t.shape[2:]) for t in prep_res)
    o_dn, states = _dn_scan_fwd(n + "dnscan", uw, qk, qd, kd, gl, B)

    zb = c.z // DN_DIM
    out_ins = [_In(o_dn, (None, S, DN_DIM), lambda b, hh: (hh, b, 0)),
               _In(o_dn, (None, S, DN_DIM), lambda b, hh: (DN_HEADS + hh, b, 0)),
               _In(P, (S, DN_DIM), lambda b, hh: (b, zb + hh), g=((T, DN_WIDTH), lambda b, hh: (b, hh)), gdtype=BF16),
               _In(w["dn_out_norm"], (1, DN_DIM), lambda b, hh: (0, 0), acc=True)]
    out_ins[0].g = ((DN_HEADS, T, DN_DIM), lambda b, hh: (hh, b, 0))
    out_ins[1].g = ((DN_HEADS, T, DN_DIM), lambda b, hh: (hh, b, 0))
    out_outs = [_Out((T, DN_WIDTH), BF16, (S, DN_DIM), lambda b, hh: (b, hh))]
    (ya_in,) = _tile_fwd(n + "dnout", _dn_out_f, (B, DN_HEADS), out_ins, out_outs)

    pb = c.pool // POOL_DIM
    pool_ins = [_In(P, (S, POOL_DIM), lambda gi, b: (b, pb + gi), g=((T, POOL_WIDTH), lambda gi, b: (b, gi)), gdtype=BF16),
                _In(w["pool_w"], (None, POOL_DIM, POOL_DIM), lambda gi, b: (gi, 0, 0), acc=True),
                _In(w["pool_scale"], (None, 1, POOL_DIM), lambda gi, b: (gi, 0, 0), acc=True)]
    pool_outs = [_Out((T, POOL_WIDTH), BF16, (S, POOL_DIM), lambda gi, b: (b, gi))]
    (yb_in,) = _tile_fwd(n + "pool", _pool_f, (POOL_GROUPS, B), pool_ins, pool_outs)

    cos, sin = _rope_tables(S)
    db = c.da // DA_WIDTH

    def da_in(k):
        return _In(P, (tm, DA_WIDTH), lambda i: (i, db + k), g=((T, DA_WIDTH), lambda i: (i, 0)), gdtype=BF16)

    rope_ins = [da_in(0), da_in(1), da_in(2),
                _In(cos, (tm, DA_WIDTH), lambda i: (i % nmS, 0), kind="c"),
                _In(sin, (tm, DA_WIDTH), lambda i: (i % nmS, 0), kind="c")]
    rope_outs = [_Out((T, DA_WIDTH), BF16, (tm, DA_WIDTH), lambda i: (i, 0))] * 3
    qr, kr, vr = _tile_fwd(n + "rope", _rope_f, (T // tm,), rope_ins, rope_outs)
    strided = []
    o_g, l_g = [], []
    for gi, dil in enumerate(DA_DILATIONS):
        cs_ = slice(gi * DA_OUT, (gi + 1) * DA_OUT)
        qs, ks, vs = (_to_strided(t[:, cs_], B, dil) for t in (qr, kr, vr))
        strided.append((qs, ks, vs))
        o, lse = _attn_fwd(n + f"attn{gi}", qs, ks, vs)
        o_g.append(_from_strided(o, B, dil))
        l_g.append(_from_strided(lse, B, dil))
    mrg_ins = [_In(a, (tm, DA_OUT), lambda i: (i, 0)) for a in o_g + l_g]
    mrg_outs = [_Out((T, DA_OUT), BF16, (tm, DA_OUT), lambda i: (i, 0))]
    (yc_in,) = _tile_fwd(n + "merge", _merge_f, (T // tm,), mrg_ins, mrg_outs)

    ya = _mm(n + "pa", ya_in, w["w_proj_a"], "nn")
    yb = _mm(n + "pb", yb_in, w["w_proj_b"], "nn")
    yc = _mm(n + "pc", yc_in, w["w_proj_c"], "nn")

    def gcol(k):
        return _In(P, (tm, D), lambda i: (i, k), g=((T, D), lambda i: (i, 0)), gdtype=BF16)

    def yin(a):
        return _In(a, (tm, D), lambda i: (i, 0), gdtype=BF16)

    def bin_(k):
        return _In(w["b_gate"][k:k + 1], (1, D), lambda i: (0, 0), acc=True)

    gm_ins = [gcol(0), gcol(1), gcol(2), yin(ya), yin(yb), yin(yc), bin_(0), bin_(1), bin_(2)]
    gm_outs = [_Out((T, D), BF16, (tm, D), lambda i: (i, 0))]
    (merged,) = _tile_fwd(n + "gates", _gate_f, (T // tm,), gm_ins, gm_outs)
    x2 = _mm(n + "out", merged, w["w_out"], "nn", add=x1)

    def backward(dx2, host_exchange=None):
        return _mixer_bwd(dx2, host_exchange, **{k: v for k, v in locals_.items() if k in _MIXER_BWD_NEEDS})

    locals_ = dict(locals())
    return x2, backward, gathered_next


_MIXER_BWD_NEEDS = ("n", "B", "T", "D", "tm", "w", "h", "merged", "gm_ins", "gm_outs", "ya_in", "yb_in", "yc_in",
                    "mrg_ins", "mrg_outs", "strided", "rope_ins", "rope_outs", "pool_ins", "pool_outs", "out_ins",
                    "out_outs", "uw", "qk", "qd", "kd", "gl", "states", "prep_grid", "prep_ins", "prep_outs", "tg",
                    "gate_ins", "gate_outs", "conv_grid", "conv_ins", "conv_outs", "norm_ins", "norm_outs")


def _mixer_bwd(dx2, host_exchange, *, n, B, T, D, tm, w, h, merged, gm_ins, gm_outs, ya_in, yb_in, yc_in, mrg_ins, mrg_outs, strided,
               rope_ins, rope_outs, pool_ins, pool_outs, out_ins, out_outs, uw, qk, qd, kd, gl, states, prep_grid,
               prep_ins, prep_outs, tg, gate_ins, gate_outs, conv_grid, conv_ins, conv_outs, norm_ins, norm_outs):
    g = {}
    dmerged = _mm(n + "d_merged", dx2, w["w_out"], "nt")
    g["w_out"] = _mm(n + "d_wout", merged, dx2, "tn", tm=1024, tn=1024, tk=1024)
    dg0, dg1, dg2, dya, dyb, dyc, db0, db1, db2 = _tile_bwd(
        n + "gates_b", _gate_f, (T // tm,), gm_ins, gm_outs, [dmerged], acc_from=0)
    g["b_gate"] = jnp.concatenate([db0, db1, db2], axis=0)
    dya_in = _mm(n + "d_pa", dya, w["w_proj_a"], "nt")
    dyb_in = _mm(n + "d_pb", dyb, w["w_proj_b"], "nt")
    dyc_in = _mm(n + "d_pc", dyc, w["w_proj_c"], "nt")
    g["w_proj_a"] = _mm(n + "d_wpa", ya_in, dya, "tn", tn=1024, tk=2048)
    g["w_proj_b"] = _mm(n + "d_wpb", yb_in, dyb, "tn", tn=1024, tk=2048)
    g["w_proj_c"] = _mm(n + "d_wpc", yc_in, dyc, "tn", tn=1024, tk=2048)

    dmrg = _tile_bwd(n + "merge_b", _merge_f, (T // tm,), mrg_ins, mrg_outs, [dyc_in])
    dq_parts, dk_parts, dv_parts = [], [], []
    for gi, dil in enumerate(DA_DILATIONS):
        qs, ks, vs = strided[gi]
        do_s = _to_strided(dmrg[gi], B, dil)
        dl_s = _to_strided(dmrg[DA_GROUPS + gi], B, dil)
        dq, dk, dv = _attn_bwd(n + f"attn{gi}_b", qs, ks, vs, do_s, dl_s)
        dq_parts.append(_from_strided(dq, B, dil))
        dk_parts.append(_from_strided(dk, B, dil))
        dv_parts.append(_from_strided(dv, B, dil))
    dqr, dkr, dvr = (jnp.concatenate(p, axis=1) for p in (dq_parts, dk_parts, dv_parts))
    dPq, dPk, dPv = _tile_bwd(n + "rope_b", _rope_f, (T // tm,), rope_ins, rope_outs, [dqr, dkr, dvr])

    dPpool, g["pool_w"], g["pool_scale"] = _tile_bwd(
        n + "pool_b", _pool_f, (POOL_GROUPS, B), pool_ins, pool_outs, [dyb_in], acc_from=1)

    dof, dob, dPz, g["dn_out_norm"] = _tile_bwd(
        n + "dnout_b", _dn_out_f, (B, DN_HEADS), out_ins, out_outs, [dya_in], acc_from=0)
    del dob
    duw, dqk, dqd, dkd, dgl = _dn_scan_bwd(n + "dnscan_b", uw, qk, qd, kd, gl, states, dof, B)
    prep_cts = [t.reshape((2, DN_HEADS) + t.shape[1:]) for t in (duw, dqk, dqd, dkd, dgl)]
    prep_res = _tile_bwd(n + "dnprep_b", _dn_prep_f, prep_grid, prep_ins, prep_outs, prep_cts, sub=2,
                         comm=host_exchange)
    exchanged = None
    if host_exchange is not None:
        prep_res, exchanged = prep_res
    dq_, dk_, dv_, dbeta4, dgdec4 = prep_res
    dqkvc = jnp.concatenate([dq_, dk_, dv_], axis=1)
    dbraw, daraw, g["dn_a_log"], g["dn_dt_bias"] = _tile_bwd(
        n + "dngate_b", _dn_gate_f, (T // tg,), gate_ins, gate_outs,
        [dbeta4.reshape(8, T), dgdec4.reshape(8, T)], acc_from=0)
    dPqkv, g["dn_conv"] = _tile_bwd(n + "dnconv_b", _dn_conv_f, conv_grid, conv_ins, conv_outs, [dqkvc], acc_from=1)
    dba = jnp.concatenate([dbraw, daraw], axis=0).T.astype(BF16)
    dba = jnp.pad(dba, ((0, 0), (0, BA_PAD - 16)))
    dP = jnp.concatenate([dg0, dg1, dg2, dPq, dPk, dPv, dPqkv, dPz, dPpool, dba], axis=1)
    dh = _mm(n + "d_h", dP, w["w_cat"], "nt", tn=1024, tk=2688)
    g["w_cat"] = _mm(n + "d_wcat", h, dP, "tn", tm=1024, tn=896, tk=1024)
    dx1, g["mix_norm"] = _tile_bwd(n + "norm_b", _norm_f, (T // tm,), norm_ins, norm_outs, [dh], acc_from=0,
                                   addends={0: dx2})
    return dx1, g, exchanged


def _layer_weights(full, l, D):
    c = _Cols(D)
    w_in = full["w_in"][l]
    o_z, o_ba, o_pool, o_da = 3 * DN_WIDTH, 4 * DN_WIDTH, 4 * DN_WIDTH + 16, 4 * DN_WIDTH + 16 + POOL_WIDTH
    w_cat = jnp.concatenate(
        [full["w_gate"][l], w_in[:, o_da:], w_in[:, :o_z], w_in[:, o_z:o_ba], w_in[:, o_pool:o_da], w_in[:, o_ba:o_pool],
         jnp.zeros((D, BA_PAD - 16), w_in.dtype)], axis=1).astype(BF16)
    assert w_cat.shape[1] == c.total
    w = {k: full[k][l].astype(BF16) for k in ("ffn1_w_gate", "ffn1_w_up", "ffn1_w_down", "ffn2_w_gate", "ffn2_w_up",
                                              "ffn2_w_down", "w_proj_a", "w_proj_b", "w_proj_c", "w_out")}
    w["w_cat"] = w_cat
    w["ffn1_norm"] = full["ffn1_norm"][l][None].astype(F32)
    w["ffn2_norm"] = full["ffn2_norm"][l][None].astype(F32)
    w["mix_norm"] = full["mix_norm"][l][None].astype(F32)
    w["dn_conv"] = full["dn_conv"][l].astype(F32)
    w["dn_a_log"] = full["dn_a_log"][l].reshape(2 * DN_HEADS, 1).astype(F32)
    w["dn_dt_bias"] = full["dn_dt_bias"][l].reshape(2 * DN_HEADS, 1).astype(F32)
    w["dn_out_norm"] = full["dn_out_norm"][l][None].astype(F32)
    w["pool_w"] = full["pool_w"][l].astype(F32)
    w["pool_scale"] = full["pool_scale"][l].reshape(POOL_GROUPS, 1, POOL_DIM).astype(F32)
    w["b_gate"] = full["b_gate"][l].reshape(3, D).astype(F32)
    return w


def _layer_grads(g, D):
    c = _Cols(D)
    gc = g.pop("w_cat")
    out = dict(g)
    out["w_gate"] = gc[:, :c.da]
    out["w_in"] = jnp.concatenate([gc[:, c.qkv:c.pool], gc[:, c.ba:c.ba + 16], gc[:, c.pool:c.ba], gc[:, c.da:c.qkv]],
                                  axis=1)
    for k in ("ffn1_norm", "ffn2_norm", "mix_norm", "dn_out_norm"):
        out[k] = g[k][0]
    out["dn_a_log"] = g["dn_a_log"].reshape(2, DN_HEADS)
    out["dn_dt_bias"] = g["dn_dt_bias"].reshape(2, DN_HEADS)
    out["pool_scale"] = g["pool_scale"].reshape(POOL_WIDTH)
    out["b_gate"] = g["b_gate"].reshape(3 * D)
    return out


def _unshard(got):
    full = {}
    for k, t in zip(SHARDED, got):
        ax = SHARD_AXIS[k] - 1
        shp = t.shape[1:]
        full[k] = jnp.moveaxis(t, 0, ax).reshape(shp[:ax] + (N_DEV * shp[ax],) + shp[ax + 1:])
    return full


def _to_owner_blocks(grads):
    out = []
    for k in SHARDED:
        ax = SHARD_AXIS[k] - 1
        shp = grads[k].shape
        t = grads[k].reshape(shp[:ax] + (N_DEV, shp[ax] // N_DEV) + shp[ax + 1:])
        out.append(jnp.moveaxis(t, ax, 0).astype(BF16))
    return out


def _local_step(x, target, rep, shards, distributed):
    B, S, D = x.shape
    T = B * S
    depth = len(shards)
    xs = x.reshape(T, D)
    tape = []
    if distributed:
        sharded_now = _unshard(_all_gather("gather_l0", [shards[0][k] for k in SHARDED]))
    else:
        sharded_now = shards[0]
    for l in range(depth):
        full = {k: [v] * (l + 1) for k, v in sharded_now.items()}
        full.update({k: v for k, v in rep.items() if k != "final_norm"})
        w = _layer_weights(full, l, D)
        host = ("gather", [shards[l + 1][k] for k in SHARDED]) if distributed and l + 1 < depth else None
        x1 = _ffn_fwd(f"l{l}_ffn1", xs, w["ffn1_norm"], w["ffn1_w_gate"], w["ffn1_w_up"], w["ffn1_w_down"])
        x2, mixer_bwd, got = _mixer(l, x1, w, B, host)
        x3 = _ffn_fwd(f"l{l}_ffn2", x2, w["ffn2_norm"], w["ffn2_w_gate"], w["ffn2_w_up"], w["ffn2_w_down"])
        tape.append((w, xs, mixer_bwd, x2))
        xs = x3
        if l + 1 < depth:
            sharded_now = _unshard(got) if distributed else shards[l + 1]
    loss8, dx, dfinal = _loss_fwd_bwd("loss", xs, rep["final_norm"][None].astype(F32), target.reshape(T, D))
    per_layer = [None] * depth
    exchanged = [None] * depth
    pending = None
    for l in reversed(range(depth)):
        w, x0, mixer_bwd, x2 = tape[l]
        dx, dn2, dwg2, dwu2, dwd2 = _ffn_bwd(f"l{l}_ffn2b", x2, w["ffn2_norm"], w["ffn2_w_gate"], w["ffn2_w_up"],
                                             w["ffn2_w_down"], dx)
        dx, g, got = mixer_bwd(dx, ("exchange", pending) if pending is not None else None)
        if pending is not None:
            exchanged[l + 1] = got
        dx, dn1, dwg1, dwu1, dwd1 = _ffn_bwd(f"l{l}_ffn1b", x0, w["ffn1_norm"], w["ffn1_w_gate"], w["ffn1_w_up"],
                                             w["ffn1_w_down"], dx)
        g.update(ffn1_norm=dn1, ffn1_w_gate=dwg1, ffn1_w_up=dwu1, ffn1_w_down=dwd1,
                 ffn2_norm=dn2, ffn2_w_gate=dwg2, ffn2_w_up=dwu2, ffn2_w_down=dwd2)
        per_layer[l] = _layer_grads(g, D)
        if distributed:
            pending = _to_owner_blocks(per_layer[l])
    return loss8[0, 0], dx.reshape(B, S, D), per_layer, dfinal[0], exchanged, pending


def _mesh_position():
    mx, my, mc = lax.axis_index("x"), lax.axis_index("y"), lax.axis_index("c")
    return mx, my, mc, 4 * mx + 2 * my + mc


def _peers(mx, my, mc):
    out = []
    for k in range(1, N_DEV):
        px, py, pc = mx ^ ((k >> 2) & 1), my ^ ((k >> 1) & 1), mc ^ (k & 1)
        out.append(((px, py, pc), 4 * px + 2 * py + pc))
    return out


_ANY = pl.BlockSpec(memory_space=pl.ANY)


def _all_gather(name, xs):
    n = len(xs)

    def body(*refs):
        _gather_start(refs[:n], refs[n:2 * n], *refs[2 * n:])
        _gather_finish(refs[:n], refs[n:2 * n], *refs[2 * n:])

    sems = pltpu.SemaphoreType.DMA((n, N_DEV - 1))
    return pl.pallas_call(
        body, name=name, in_specs=[_ANY] * n, out_specs=[_ANY] * n,
        out_shape=[jax.ShapeDtypeStruct((N_DEV,) + x.shape, x.dtype) for x in xs],
        scratch_shapes=[sems, sems, pltpu.SemaphoreType.DMA((n,))],
    )(*xs)


class _GatherPlan:
    def __init__(self, x_refs, o_refs, send_sems, recv_sems, local_sems):
        self.x, self.o, self.ss, self.rs, self.ls = x_refs, o_refs, send_sems, recv_sems, local_sems
        self.mx, self.my, self.mc, self.me = _mesh_position()
        self.self_id = (self.mx, self.my, self.mc)
        self.sibling = (self.mx, self.my, 1 - self.mc)
        self.chips = [(1 - self.mx, self.my), (self.mx, 1 - self.my), (1 - self.mx, 1 - self.my)]

    def copy(self, a, k, blk, to, from_input=False):
        dst = self.o[a].at[blk]
        return pltpu.make_async_remote_copy(src_ref=self.x[a] if from_input else dst, dst_ref=dst,
                                            send_sem=self.ss.at[a, k], recv_sem=self.rs.at[a, k],
                                            device_id=to, device_id_type=pl.DeviceIdType.MESH)

    def own(self, a):
        return pltpu.make_async_copy(self.x[a], self.o[a].at[self.me], self.ls.at[a])

    def first_sends(self, a):
        cps = [self.copy(a, 0, self.me, self.sibling, from_input=True)]
        return cps + [self.copy(a, 1 + j, self.me, (*chip, self.mc), from_input=True) for j, chip in enumerate(self.chips)]

    def passed_on(self, a, j):
        cx, cy = self.chips[j]
        return self.copy(a, 4 + j, 4 * cx + 2 * cy + self.mc, self.sibling)


def _gather_start(x_refs, o_refs, send_sems, recv_sems, local_sems):
    p = _GatherPlan(x_refs, o_refs, send_sems, recv_sems, local_sems)
    for a in range(len(x_refs)):
        p.own(a).start()
        for cp in p.first_sends(a):
            cp.start()


def _gather_finish(x_refs, o_refs, send_sems, recv_sems, local_sems):
    p = _GatherPlan(x_refs, o_refs, send_sems, recv_sems, local_sems)
    n = len(x_refs)
    for a in range(n):
        for j, (cx, cy) in enumerate(p.chips):
            p.copy(a, 1 + j, 4 * cx + 2 * cy + p.mc, p.self_id).wait_recv()
            p.passed_on(a, j).start()
    for a in range(n):
        p.copy(a, 0, 4 * p.mx + 2 * p.my + 1 - p.mc, p.self_id).wait_recv()
        for j, (cx, cy) in enumerate(p.chips):
            p.copy(a, 4 + j, 4 * cx + 2 * cy + 1 - p.mc, p.self_id).wait_recv()
    for a in range(n):
        for cp in p.first_sends(a):
            cp.wait_send()
        for j in range(len(p.chips)):
            p.passed_on(a, j).wait_send()
        p.own(a).wait()


def _exchange_grads(name, gs, gr):
    ns, n = len(gs), len(gs) + len(gr)

    def body(*refs):
        _exchange_start(refs[:n], refs[n:2 * n], *refs[2 * n:], n_sharded=ns)
        _exchange_finish(refs[:n], refs[n:2 * n], *refs[2 * n:], n_sharded=ns)

    sems = pltpu.SemaphoreType.DMA((n, N_DEV - 1))
    outs = pl.pallas_call(
        body, name=name, in_specs=[_ANY] * n, out_specs=[_ANY] * n,
        out_shape=[jax.ShapeDtypeStruct(a.shape, a.dtype) for a in gs]
        + [jax.ShapeDtypeStruct((N_DEV,) + a.shape, a.dtype) for a in gr],
        scratch_shapes=[sems, sems, pltpu.SemaphoreType.DMA((n,))],
    )(*gs, *gr)
    return outs[:ns], outs[ns:]


def _exchange_copies(in_refs, out_refs, send_sems, recv_sems, local_sems, n_sharded):
    mx, my, mc, me = _mesh_position()
    n = len(in_refs)
    own = [pltpu.make_async_copy(in_refs[a].at[me] if a < n_sharded else in_refs[a], out_refs[a].at[me],
                                 local_sems.at[a]) for a in range(n)]
    remote = []
    for k, (peer, pid) in enumerate(_peers(mx, my, mc)):
        for a in range(n):
            src = in_refs[a].at[pid] if a < n_sharded else in_refs[a]
            remote.append(pltpu.make_async_remote_copy(
                src_ref=src, dst_ref=out_refs[a].at[me], send_sem=send_sems.at[a, k], recv_sem=recv_sems.at[a, k],
                device_id=peer, device_id_type=pl.DeviceIdType.MESH))
    return own, remote


def _exchange_start(in_refs, out_refs, send_sems, recv_sems, local_sems, n_sharded=None):
    ns = len(in_refs) if n_sharded is None else n_sharded
    own, remote = _exchange_copies(in_refs, out_refs, send_sems, recv_sems, local_sems, ns)
    for cp in own + remote:
        cp.start()


def _exchange_finish(in_refs, out_refs, send_sems, recv_sems, local_sems, n_sharded=None):
    ns = len(in_refs) if n_sharded is None else n_sharded
    own, remote = _exchange_copies(in_refs, out_refs, send_sems, recv_sems, local_sems, ns)
    for cp in remote:
        cp.wait_send()
        cp.wait_recv()
    for cp in own:
        cp.wait()


def _reduce_adamw(name, parts, w, m, v):
    shape = w.shape
    cols = shape[-1]
    w2, m2, v2 = (t.reshape(-1, cols) for t in (w, m, v))
    p3 = parts.reshape(N_DEV, -1, cols)
    rows = w2.shape[0]
    tr = _pick(rows, 512, 16) if rows > 1024 else rows
    c1 = 1.0 - ADAM_B1 ** ADAM_STEP
    c2 = 1.0 - ADAM_B2 ** ADAM_STEP

    def body(p_ref, w_ref, m_ref, v_ref, g_ref, d_ref, nm_ref, nv_ref):
        gv = p_ref[0].astype(F32)
        for d in range(1, N_DEV):
            gv = gv + p_ref[d].astype(F32)
        nm = ADAM_B1 * m_ref[...] + (1.0 - ADAM_B1) * gv
        nv = ADAM_B2 * v_ref[...] + (1.0 - ADAM_B2) * (gv * gv)
        g_ref[...] = gv
        d_ref[...] = -ADAM_LR * ((nm / c1) / (jnp.sqrt(nv / c2) + ADAM_EPS) + ADAM_WD * w_ref[...])
        nm_ref[...] = nm
        nv_ref[...] = nv

    spec = pl.BlockSpec((tr, cols), lambda i: (i, 0))
    outs = pl.pallas_call(
        body, name=name, grid=(rows // tr,),
        in_specs=[pl.BlockSpec((N_DEV, tr, cols), lambda i: (0, i, 0))] + [spec] * 3, out_specs=[spec] * 4,
        out_shape=[jax.ShapeDtypeStruct((rows, cols), F32)] * 4, compiler_params=_params(),
    )(p3, w2, m2, v2)
    return tuple(o.reshape(shape) for o in outs)


def kernel(x, ffn1_norm, ffn1_w_gate, ffn1_w_up, ffn1_w_down, mix_norm, w_in, dn_conv, dn_a_log, dn_dt_bias, dn_out_norm, pool_w, pool_scale, w_proj_a, w_proj_b, w_proj_c, w_gate, b_gate, w_out, ffn2_norm, ffn2_w_gate, ffn2_w_up, ffn2_w_down, final_norm, loss_target, m_ffn1_norm, m_ffn1_w_gate, m_ffn1_w_up, m_ffn1_w_down, m_mix_norm, m_w_in, m_dn_conv, m_dn_a_log, m_dn_dt_bias, m_dn_out_norm, m_pool_w, m_pool_scale, m_w_proj_a, m_w_proj_b, m_w_proj_c, m_w_gate, m_b_gate, m_w_out, m_ffn2_norm, m_ffn2_w_gate, m_ffn2_w_up, m_ffn2_w_down, m_final_norm, v_ffn1_norm, v_ffn1_w_gate, v_ffn1_w_up, v_ffn1_w_down, v_mix_norm, v_w_in, v_dn_conv, v_dn_a_log, v_dn_dt_bias, v_dn_out_norm, v_pool_w, v_pool_scale, v_w_proj_a, v_w_proj_b, v_w_proj_c, v_w_gate, v_b_gate, v_w_out, v_ffn2_norm, v_ffn2_w_gate, v_ffn2_w_up, v_ffn2_w_down, v_final_norm):
    args = locals()
    wts = {k: args[k] for k in WEIGHTS}
    ms = {k: args["m_" + k] for k in WEIGHTS}
    vs = {k: args["v_" + k] for k in WEIGHTS}

    depth = w_in.shape[0]
    rep = {k: wts[k] for k in REPLICATED}
    shards = [{k: wts[k][l].astype(BF16) for k in SHARDED} for l in range(depth)]
    loss_local, dx, per_layer, dfinal, exchanged, pending = _local_step(x, loss_target, rep, shards, True)
    loss = lax.psum(loss_local, ("x", "y", "c"))

    gr = [dfinal if k == "final_norm" else jnp.stack([pg[k] for pg in per_layer]).astype(F32).reshape(wts[k].shape)
          for k in REPLICATED]
    exchanged[0], got_r = _exchange_grads("exchange_grads", pending, gr)
    parts = {k: jnp.stack([exchanged[l][j] for l in range(depth)], axis=1) for j, k in enumerate(SHARDED)}
    parts.update(zip(REPLICATED, got_r))

    g_final, deltas, new_m, new_v = {}, {}, {}, {}
    for k in WEIGHTS:
        g_final[k], deltas[k], new_m[k], new_v[k] = _reduce_adamw("adamw_" + k, parts[k], wts[k], ms[k], vs[k])
    return (loss, dx, *[g_final[k] for k in WEIGHTS], *[deltas[k] for k in WEIGHTS], *[new_m[k] for k in WEIGHTS],
            *[new_v[k] for k in WEIGHTS])
```

```python
import functools
import math

import jax
import jax.numpy as jnp
from jax import lax
from jax.experimental import pallas as pl
from jax.experimental.pallas import tpu as pltpu

F32 = jnp.float32
BF16 = jnp.bfloat16

N_DEV = 8
RMS_EPS = 1e-6
L2_EPS = 1e-6
DN_HEADS = 4
DN_DIM = 128
DN_WIDTH = DN_HEADS * DN_DIM
DN_CONV = 5
DN_CHUNK = 64
DN_SUPER = 256
POOL_GROUPS = 4
POOL_DIM = 128
POOL_WIDTH = POOL_GROUPS * POOL_DIM
POOL_MAX_HALF = 8
DA_GROUPS = 3
DA_HEADS = 4
DA_DIM = 64
DA_WIDTH = DA_GROUPS * DA_HEADS * DA_DIM
DA_OUT = DA_HEADS * DA_DIM
DA_DILATIONS = (1, 4, 16)
DA_RADIUS = 64
DA_TQ = 128
ROPE_THETA = 10000.0
MASK_VALUE = -1e30
BA_PAD = 128

ADAM_LR = 0.001
ADAM_B1 = 0.9
ADAM_B2 = 0.999
ADAM_EPS = 1e-08
ADAM_WD = 0.01
ADAM_STEP = 10

VMEM_LIMIT_V7X = 56 * 1024 * 1024
LANES = 1024

SHARDED = ("ffn1_w_gate", "ffn1_w_up", "ffn1_w_down", "w_in", "dn_conv", "w_proj_a", "w_proj_b", "w_proj_c",
           "w_gate", "w_out", "ffn2_w_gate", "ffn2_w_up", "ffn2_w_down")
SHARD_AXIS = {"ffn1_w_gate": 2, "ffn1_w_up": 2, "ffn1_w_down": 1, "w_in": 2, "dn_conv": 2, "w_proj_a": 2,
              "w_proj_b": 2, "w_proj_c": 2, "w_gate": 2, "w_out": 1, "ffn2_w_gate": 2, "ffn2_w_up": 2,
              "ffn2_w_down": 1}
REPLICATED = ("ffn1_norm", "mix_norm", "dn_a_log", "dn_dt_bias", "dn_out_norm", "pool_w", "pool_scale", "b_gate",
              "ffn2_norm", "final_norm")
WEIGHTS = ("ffn1_norm", "ffn1_w_gate", "ffn1_w_up", "ffn1_w_down", "mix_norm", "w_in", "dn_conv", "dn_a_log",
           "dn_dt_bias", "dn_out_norm", "pool_w", "pool_scale", "w_proj_a", "w_proj_b", "w_proj_c", "w_gate",
           "b_gate", "w_out", "ffn2_norm", "ffn2_w_gate", "ffn2_w_up", "ffn2_w_down", "final_norm")


def _params(**kw):
    return pltpu.CompilerParams(vmem_limit_bytes=VMEM_LIMIT_V7X, **kw)


def _pick(n, target, align):
    best = None
    t = align
    while t <= min(n, target):
        if n % t == 0:
            best = t
        t += align
    return best if best is not None else n


_DIMS = {"nn": (((1,), (0,)), ((), ())), "nt": (((1,), (1,)), ((), ())), "tn": (((0,), (0,)), ((), ()))}


def _dg(a, b, mode):
    return lax.dot_general(a, b, _DIMS[mode], preferred_element_type=F32)


def _split2(a):
    hi = a.astype(BF16)
    lo = (a - hi.astype(F32)).astype(BF16)
    return hi, lo


def _dotp(a, b, mode, passes):
    if passes == 1:
        return _dg(a.astype(BF16), b.astype(BF16), mode)
    ah, al = _split2(a.astype(F32))
    bh, bl = _split2(b.astype(F32))
    return _dg(ah, bh, mode) + (_dg(ah, bl, mode) + _dg(al, bh, mode))


@functools.partial(jax.custom_vjp, nondiff_argnums=(2, 3))
def _dot(a, b, mode, passes):
    return _dotp(a, b, mode, passes)


def _dot_fwd(a, b, mode, passes):
    return _dotp(a, b, mode, passes), (a, b)


def _dot_bwd(mode, passes, res, ct):
    a, b = res
    if mode == "nn":
        da, db = _dotp(ct, b, "nt", passes), _dotp(a, ct, "tn", passes)
    elif mode == "nt":
        da, db = _dotp(ct, b, "nn", passes), _dotp(ct, a, "tn", passes)
    else:
        da, db = _dotp(b, ct, "nt", passes), _dotp(a, ct, "nn", passes)
    return da.astype(a.dtype), db.astype(b.dtype)


_dot.defvjp(_dot_fwd, _dot_bwd)


def _split3(x):
    x1 = x.astype(BF16)
    r = x - x1.astype(F32)
    x2 = r.astype(BF16)
    x3 = (r - x2.astype(F32)).astype(BF16)
    return x1, x2, x3


def _mdotp(mask, x, mode):
    x1, x2, x3 = _split3(x)
    return _dg(mask, x1, mode) + (_dg(mask, x2, mode) + _dg(mask, x3, mode))


@jax.custom_vjp
def _mdot(mask, x):
    return _mdotp(mask, x, "nn")


def _mdot_fwd(mask, x):
    return _mdotp(mask, x, "nn"), mask


def _mdot_bwd(mask, ct):
    return jnp.zeros_like(mask), _mdotp(mask, ct, "tn")


_mdot.defvjp(_mdot_fwd, _mdot_bwd)


_SOLVE_SQUARINGS = int(math.log2(DN_CHUNK)) - 1


def _unit_solve_fwd(A, R):
    Ab = A.astype(BF16)
    X = R - _dg(Ab, R.astype(BF16), "nn")
    P, powers = Ab, []
    for _ in range(_SOLVE_SQUARINGS):
        P = _dg(P, P, "nn").astype(BF16)
        powers.append(P)
        X = X + _dg(P, X.astype(BF16), "nn")
    return X, (Ab, tuple(powers), X)


@jax.custom_vjp
def _unit_solve(A, R):
    return _unit_solve_fwd(A, R)[0]


def _unit_solve_bwd(res, dX):
    Ab, powers, X = res
    Y = dX - _dg(Ab, dX.astype(BF16), "tn")
    for P in powers:
        Y = Y + _dg(P, Y.astype(BF16), "tn")
    return -_dg(Y.astype(BF16), X.astype(BF16), "nt"), Y


_unit_solve.defvjp(_unit_solve_fwd, _unit_solve_bwd)


def _shift_impl(x, o):
    if o == 0:
        return x
    n = x.shape[0]
    y = pltpu.roll(x, (-o) % n, axis=0)
    t = lax.broadcasted_iota(jnp.int32, x.shape, 0) + o
    return jnp.where((t >= 0) & (t < n), y, 0.0)


@functools.partial(jax.custom_vjp, nondiff_argnums=(1,))
def _shift(x, o):
    return _shift_impl(x, o)


def _shift_fwd(x, o):
    return _shift_impl(x, o), None


def _shift_bwd(o, _, ct):
    return (_shift_impl(ct, -o),)


_shift.defvjp(_shift_fwd, _shift_bwd)


def _rot_impl(x):
    w = x.shape[1]
    half = DA_DIM // 2
    lane = lax.broadcasted_iota(jnp.int32, x.shape, 1)
    first = (lane & (DA_DIM - 1)) < half
    return jnp.where(first, -pltpu.roll(x, w - half, axis=1), pltpu.roll(x, half, axis=1))


@jax.custom_vjp
def _rot(x):
    return _rot_impl(x)


def _rot_fwd(x):
    return _rot_impl(x), None


def _rot_bwd(_, ct):
    return (-_rot_impl(ct),)


_rot.defvjp(_rot_fwd, _rot_bwd)


def _sigmoid(x):
    return 1.0 / (1.0 + jnp.exp(-x))


def _silu(x):
    return x * _sigmoid(x)


def _softplus(x):
    return jnp.maximum(x, 0.0) + jnp.log(1.0 + jnp.exp(-jnp.abs(x)))


def _rms(x, gain):
    return x * lax.rsqrt(jnp.mean(x * x, axis=-1, keepdims=True) + RMS_EPS) * gain


class _In:
    def __init__(self, arr, block, imap, kind="t", acc=False, g=None, gdtype=None, split=False):
        self.arr, self.block, self.imap, self.kind, self.acc, self.g, self.gdtype = arr, block, imap, kind, acc, g, gdtype
        self.split = split


class _Out:
    def __init__(self, shape, dtype, block, imap, split=False):
        self.shape, self.dtype, self.block, self.imap, self.split = shape, dtype, block, imap, split


def _grid_edges(grid):
    first = last = None
    for a, n in enumerate(grid):
        f, l = pl.program_id(a) == 0, pl.program_id(a) == n - 1
        first = f if first is None else jnp.logical_and(first, f)
        last = l if last is None else jnp.logical_and(last, l)
    return first, last


def _comm_plumbing(comm):
    if comm is None:
        return [], [], [], lambda refs: None, lambda refs: None
    kind, arrs = comm
    n = len(arrs)
    if kind == "gather":
        shapes = [jax.ShapeDtypeStruct((N_DEV,) + a.shape, a.dtype) for a in arrs]
        start, finish = _gather_start, _gather_finish
    else:
        shapes = [jax.ShapeDtypeStruct(a.shape, a.dtype) for a in arrs]
        start, finish = _exchange_start, _exchange_finish
    sems = [pltpu.SemaphoreType.DMA((n, N_DEV - 1)), pltpu.SemaphoreType.DMA((n, N_DEV - 1)),
            pltpu.SemaphoreType.DMA((n,))]
    return list(arrs), shapes, sems, start, finish


def _first_step(acc_from, ngrid):
    c = None
    for a in range(acc_from, ngrid):
        t = pl.program_id(a) == 0
        c = t if c is None else jnp.logical_and(c, t)
    return c


def _tile_fwd(name, f, grid, ins, outs, sub=1, comm=None):
    n_in, n_out = len(ins), len(outs)
    ngrid = len(grid)
    c_arrs, c_shapes, c_sems, c_start, c_finish = _comm_plumbing(comm)
    nc = len(c_arrs)

    def body(*refs):
        in_refs, c_in = refs[:n_in], refs[n_in:n_in + nc]
        out_refs, c_out = refs[n_in + nc:n_in + nc + n_out], refs[n_in + nc + n_out:n_in + 2 * nc + n_out]
        sems = refs[n_in + 2 * nc + n_out:]
        pids = tuple(pl.program_id(a) for a in range(ngrid))
        if nc:
            first, last = _grid_edges(grid)
            pl.when(first)(lambda: c_start(c_in, c_out, *sems))
        for s in range(sub):
            vals = [(r[s] if i.split else r[...]) for r, i in zip(in_refs, ins)]
            res = f(pids + ((s,) if sub > 1 else ()), *vals)
            for r, o, v in zip(out_refs, outs, res):
                if o.split:
                    r[s] = v.astype(r.dtype)
                else:
                    r[...] = v.astype(r.dtype)
        if nc:
            pl.when(last)(lambda: c_finish(c_in, c_out, *sems))

    res = pl.pallas_call(
        body, name=name, grid=grid,
        in_specs=[pl.BlockSpec(i.block, i.imap) for i in ins] + [_ANY] * nc,
        out_specs=[pl.BlockSpec(o.block, o.imap) for o in outs] + [_ANY] * nc,
        out_shape=[jax.ShapeDtypeStruct(o.shape, o.dtype) for o in outs] + c_shapes,
        scratch_shapes=c_sems, compiler_params=_params(),
    )(*[i.arr for i in ins], *c_arrs)
    return (res[:n_out], res[n_out:]) if nc else res


def _tile_bwd(name, f, grid, ins, outs, cts, acc_from=None, addends=None, sub=1, comm=None):
    n_in, n_out = len(ins), len(outs)
    ngrid = len(grid)
    diff = [k for k, i in enumerate(ins) if i.kind == "t"]
    addends = addends or {}
    add_keys = sorted(addends)
    n_add, n_g = len(add_keys), len(diff)
    c_arrs, c_shapes, c_sems, c_start, c_finish = _comm_plumbing(comm)
    nc = len(c_arrs)

    def body(*refs):
        pids = tuple(pl.program_id(a) for a in range(ngrid))
        in_refs = refs[:n_in]
        ct_refs = refs[n_in:n_in + n_out]
        add_refs = refs[n_in + n_out:n_in + n_out + n_add]
        o = n_in + n_out + n_add
        c_in, g_refs, c_out, sems = refs[o:o + nc], refs[o + nc:o + nc + n_g], refs[o + nc + n_g:o + 2 * nc + n_g], \
            refs[o + 2 * nc + n_g:]
        if nc:
            first_step, last_step = _grid_edges(grid)
            pl.when(first_step)(lambda: c_start(c_in, c_out, *sems))
        shared = {}
        for s in range(sub):
            vals = [(r[s] if i.split else r[...]) for r, i in zip(in_refs, ins)]
            dvals = [vals[k].astype(F32) for k in diff]

            def g(*d, vals=vals, s=s):
                full = list(vals)
                for k, dk in zip(diff, d):
                    full[k] = dk
                return tuple(f(pids + ((s,) if sub > 1 else ()), *full))

            res, vjp = jax.vjp(g, *dvals)
            cvals = [(c[s] if o_.split else c[...]).astype(r.dtype) for c, o_, r in zip(ct_refs, outs, res)]
            grads = vjp(tuple(cvals))
            for k, gr, gref in zip(diff, grads, g_refs):
                if ins[k].split:
                    gref[s] = gr.astype(gref.dtype)
                else:
                    shared[k] = gr if k not in shared else shared[k] + gr
        first = _first_step(acc_from, ngrid) if acc_from is not None else None
        for k, gref in zip(diff, g_refs):
            if ins[k].split:
                continue
            gr = shared[k]
            if k in addends:
                gr = gr + add_refs[add_keys.index(k)][...].astype(F32)
            if ins[k].acc and first is not None:
                @pl.when(first)
                def _(gr=gr, gref=gref):
                    gref[...] = gr.astype(gref.dtype)

                @pl.when(jnp.logical_not(first))
                def _(gr=gr, gref=gref):
                    gref[...] += gr.astype(gref.dtype)
            else:
                gref[...] = gr.astype(gref.dtype)
        if nc:
            pl.when(last_step)(lambda: c_finish(c_in, c_out, *sems))

    g_shapes, g_specs = [], []
    for k in diff:
        i = ins[k]
        if i.g is not None:
            shape, imap = i.g
        else:
            shape, imap = i.arr.shape, i.imap
        dt = i.gdtype or (F32 if i.acc else i.arr.dtype)
        g_shapes.append(jax.ShapeDtypeStruct(shape, dt))
        g_specs.append(pl.BlockSpec(i.block, imap))
    add_specs = [pl.BlockSpec(ins[k].block, ins[k].g[1] if ins[k].g is not None else ins[k].imap) for k in add_keys]
    res = pl.pallas_call(
        body, name=name, grid=grid,
        in_specs=[pl.BlockSpec(i.block, i.imap) for i in ins] + [pl.BlockSpec(o.block, o.imap) for o in outs] + add_specs
        + [_ANY] * nc,
        out_specs=g_specs + [_ANY] * nc, out_shape=g_shapes + c_shapes,
        scratch_shapes=c_sems, compiler_params=_params(),
    )(*[i.arr for i in ins], *cts, *[addends[k] for k in add_keys], *c_arrs)
    return (res[:n_g], res[n_g:]) if nc else res


def _mm(name, a, b, mode, out_dtype=F32, add=None, tm=512, tn=512, tk=512):
    if mode == "nn":
        (M, K), N = a.shape, b.shape[1]
    elif mode == "nt":
        (M, K), N = a.shape, b.shape[0]
    else:
        (K, M), N = a.shape, b.shape[1]
    tm, tn, tk = _pick(M, tm, 128), _pick(N, tn, 128), _pick(K, tk, 128)
    nk = K // tk
    a_spec = pl.BlockSpec((tk, tm), lambda i, j, k: (k, i)) if mode == "tn" else pl.BlockSpec((tm, tk), lambda i, j, k: (i, k))
    b_spec = pl.BlockSpec((tn, tk), lambda i, j, k: (j, k)) if mode == "nt" else pl.BlockSpec((tk, tn), lambda i, j, k: (k, j))
    o_spec = pl.BlockSpec((tm, tn), lambda i, j, k: (i, j))

    def body(*refs):
        if add is None:
            a_ref, b_ref, o_ref, acc = refs
            add_ref = None
        else:
            a_ref, b_ref, add_ref, o_ref, acc = refs
        k = pl.program_id(2)

        @pl.when(k == 0)
        def _():
            acc[...] = jnp.zeros_like(acc)

        acc[...] += _dg(a_ref[...].astype(BF16), b_ref[...].astype(BF16), mode)

        @pl.when(k == nk - 1)
        def _():
            r = acc[...]
            if add_ref is not None:
                r = r + add_ref[...].astype(F32)
            o_ref[...] = r.astype(o_ref.dtype)

    ops = (a, b) if add is None else (a, b, add)
    specs = [a_spec, b_spec] + ([] if add is None else [o_spec])
    return pl.pallas_call(
        body, name=name, grid=(M // tm, N // tn, nk), in_specs=specs, out_specs=o_spec,
        out_shape=jax.ShapeDtypeStruct((M, N), out_dtype), scratch_shapes=[pltpu.VMEM((tm, tn), F32)],
        compiler_params=_params(dimension_semantics=("parallel", "parallel", "arbitrary")),
    )(*ops)


def _ffn_fwd(name, x, gain, wg, wu, wd):
    T, D = x.shape
    F = wg.shape[1]
    tm, tf = _pick(T, 512, 8), _pick(F, 256, 128)
    nf = F // tf

    def body(x_ref, g_ref, wg_ref, wu_ref, wd_ref, o_ref, h_ref, acc):
        j = pl.program_id(1)

        @pl.when(j == 0)
        def _():
            h_ref[...] = _rms(x_ref[...], g_ref[...]).astype(BF16)
            acc[...] = jnp.zeros_like(acc)

        h = h_ref[...]
        a = _dg(h, wg_ref[...], "nn")
        b = _dg(h, wu_ref[...], "nn")
        s = (_silu(a) * b).astype(BF16)
        acc[...] += _dg(s, wd_ref[...], "nn")

        @pl.when(j == nf - 1)
        def _():
            o_ref[...] = x_ref[...] + 0.5 * acc[...]

    return pl.pallas_call(
        body, name=name, grid=(T // tm, nf),
        in_specs=[pl.BlockSpec((tm, D), lambda i, j: (i, 0)), pl.BlockSpec((1, D), lambda i, j: (0, 0)),
                  pl.BlockSpec((D, tf), lambda i, j: (0, j)), pl.BlockSpec((D, tf), lambda i, j: (0, j)),
                  pl.BlockSpec((tf, D), lambda i, j: (j, 0))],
        out_specs=pl.BlockSpec((tm, D), lambda i, j: (i, 0)),
        out_shape=jax.ShapeDtypeStruct((T, D), F32),
        scratch_shapes=[pltpu.VMEM((tm, D), BF16), pltpu.VMEM((tm, D), F32)],
        compiler_params=_params(dimension_semantics=("parallel", "arbitrary")),
    )(x, gain, wg, wu, wd)


def _ffn_bwd(name, x, gain, wg, wu, wd, dy):
    T, D = x.shape
    F = wg.shape[1]
    tm, tf = _pick(T, 512, 8), _pick(F, 256, 128)
    nf = F // tf

    def body(x_ref, g_ref, wg_ref, wu_ref, wd_ref, dy_ref, dx_ref, dg_ref, da_ref, db_ref, s_ref, h_ref, dyh_ref, dh):
        i, j = pl.program_id(0), pl.program_id(1)

        @pl.when(j == 0)
        def _():
            h_ref[...] = _rms(x_ref[...], g_ref[...]).astype(BF16)
            dyh_ref[...] = (0.5 * dy_ref[...]).astype(BF16)
            dh[...] = jnp.zeros_like(dh)

        h = h_ref[...]
        a = _dg(h, wg_ref[...], "nn")
        b = _dg(h, wu_ref[...], "nn")
        ds = _dg(dyh_ref[...], wd_ref[...], "nt")
        sig = _sigmoid(a)
        silu = a * sig
        da = (ds * b * (sig * (1.0 + a * (1.0 - sig)))).astype(BF16)
        db = (ds * silu).astype(BF16)
        da_ref[...] = da
        db_ref[...] = db
        s_ref[...] = (silu * b).astype(BF16)
        dh[...] += _dg(da, wg_ref[...], "nt") + _dg(db, wu_ref[...], "nt")

        @pl.when(j == nf - 1)
        def _():
            _, vjp = jax.vjp(_rms, x_ref[...], g_ref[...])
            dxn, dgn = vjp(dh[...])
            dx_ref[...] = dy_ref[...] + dxn

            @pl.when(i == 0)
            def _():
                dg_ref[...] = dgn

            @pl.when(i != 0)
            def _():
                dg_ref[...] += dgn

    row = lambda i, j: (i, 0)
    col = lambda i, j: (i, j)
    dx, dgain, da, db, s, h, dyh = pl.pallas_call(
        body, name=name, grid=(T // tm, nf),
        in_specs=[pl.BlockSpec((tm, D), row), pl.BlockSpec((1, D), lambda i, j: (0, 0)),
                  pl.BlockSpec((D, tf), lambda i, j: (0, j)), pl.BlockSpec((D, tf), lambda i, j: (0, j)),
                  pl.BlockSpec((tf, D), lambda i, j: (j, 0)), pl.BlockSpec((tm, D), row)],
        out_specs=[pl.BlockSpec((tm, D), row), pl.BlockSpec((1, D), lambda i, j: (0, 0)),
                   pl.BlockSpec((tm, tf), col), pl.BlockSpec((tm, tf), col), pl.BlockSpec((tm, tf), col),
                   pl.BlockSpec((tm, D), row), pl.BlockSpec((tm, D), row)],
        out_shape=[jax.ShapeDtypeStruct((T, D), F32), jax.ShapeDtypeStruct((1, D), F32),
                   jax.ShapeDtypeStruct((T, F), BF16), jax.ShapeDtypeStruct((T, F), BF16),
                   jax.ShapeDtypeStruct((T, F), BF16), jax.ShapeDtypeStruct((T, D), BF16),
                   jax.ShapeDtypeStruct((T, D), BF16)],
        scratch_shapes=[pltpu.VMEM((tm, D), F32)],
        compiler_params=_params(),
    )(x, gain, wg, wu, wd, dy)
    dwg = _mm(name + "_dwg", h, da, "tn", tm=1024, tn=1408, tk=1024)
    dwu = _mm(name + "_dwu", h, db, "tn", tm=1024, tn=1408, tk=1024)
    dwd = _mm(name + "_dwd", s, dyh, "tn", tm=1408, tn=1024, tk=1024)
    return dx, dgain, dwg, dwu, dwd


def _norm_f(pids, x, gain):
    return (_rms(x, gain),)


def _dn_conv_f(pids, x, w):
    j = pids[0]
    tap = lax.broadcasted_iota(jnp.int32, w.shape, 0)
    y = jnp.zeros_like(x)
    for t in range(DN_CONV):
        wt = jnp.sum(jnp.where(tap == t, w, 0.0), axis=0, keepdims=True)
        y = y + _shift(x, t - DN_CONV // 2) * wt
    y = _silu(y)
    n = y * lax.rsqrt(jnp.sum(y * y, axis=-1, keepdims=True) + L2_EPS)
    is_q = (j < DN_HEADS).astype(F32)
    is_qk = (j < 2 * DN_HEADS).astype(F32)
    scale = is_q * (DN_DIM ** -0.5) + (1.0 - is_q)
    return ((is_qk * n + (1.0 - is_qk) * y) * scale,)


def _dn_gate_f(pids, braw, araw, a_log, dt_bias):
    beta = _sigmoid(braw)
    g = -jnp.exp(a_log) * _softplus(araw + dt_bias)
    return beta, g


def _dn_prep_f(pids, q, k, v, brow, grow):
    cs = DN_SUPER
    sign = 1 - 2 * pids[2]
    ii = lax.broadcasted_iota(jnp.int32, (cs, cs), 0)
    jj = lax.broadcasted_iota(jnp.int32, (cs, cs), 1)
    shift = int(math.log2(DN_CHUNK))
    same = (ii >> shift) == (jj >> shift)
    d = (ii - jj) * sign
    incl = same & (d >= 0)
    strict = same & (d > 0)
    eye = ii == jj
    g_col = jnp.sum(jnp.where(eye, jnp.broadcast_to(grow, (cs, cs)), 0.0), axis=1, keepdims=True)
    b_col = jnp.sum(jnp.where(eye, jnp.broadcast_to(brow, (cs, cs)), 0.0), axis=1, keepdims=True)
    g128 = jnp.broadcast_to(g_col, (cs, DN_DIM))
    G = _mdot(incl.astype(BF16), g128)
    Gt = _mdot(same.astype(BF16), g128)
    Gc = jnp.concatenate([G, G], axis=1)
    Grow = jnp.sum(jnp.where(eye, Gc, 0.0), axis=0, keepdims=True)
    decay = jnp.exp(jnp.where(incl, Gc - Grow, MASK_VALUE))
    eG = jnp.exp(G)
    kb = k * b_col
    A = jnp.where(strict, _dot(kb, k, "nt", 1) * decay, 0.0)
    X = _unit_solve(A, jnp.concatenate([v * b_col, kb * eG], axis=1))
    qk = jnp.where(incl, _dot(q, k, "nt", 1) * decay, 0.0)
    return X, qk, q * eG, k * jnp.exp(Gt - G), jnp.exp(Gt)


def _dn_out_f(pids, of, ob, z, gain):
    return (_rms(of + ob, gain) * _silu(z),)


def _pool_f(pids, u, w, scale):
    g = pids[0]
    half = jnp.left_shift(1, g)
    n = u.shape[0]
    pos = lax.broadcasted_iota(jnp.int32, (n, 1), 0)
    tot = jnp.zeros_like(u)
    cnt = jnp.zeros((n, 1), F32)
    for o in range(-POOL_MAX_HALF, POOL_MAX_HALF):
        use = ((o >= -half) & (o < half)).astype(F32)
        tot = tot + use * _shift(u, o)
        cnt = cnt + use * ((pos + o >= 0) & (pos + o < n)).astype(F32)
    pooled = tot / cnt - u
    return (_dot(pooled, w, "nn", 1) * scale,)


def _rope_f(pids, *args):
    cos, sin = args[-2:]
    qs, ks, vs = args[:DA_GROUPS], args[DA_GROUPS:2 * DA_GROUPS], args[2 * DA_GROUPS:3 * DA_GROUPS]
    qr = [(q * cos + _rot(q) * sin) * (DA_DIM ** -0.5) for q in qs]
    kr = [k * cos + _rot(k) * sin for k in ks]
    return (*qr, *kr, *vs)


def _attn_head(q, k, v, qpos0, kpos0):
    s = _dot(q, k, "nt", 1)
    qi = qpos0 + lax.broadcasted_iota(jnp.int32, s.shape, 0)
    kj = kpos0 + lax.broadcasted_iota(jnp.int32, s.shape, 1)
    s = jnp.where(jnp.abs(kj - qi) <= DA_RADIUS, s, MASK_VALUE)
    m = lax.stop_gradient(jnp.max(s, axis=1, keepdims=True))
    p = jnp.exp(s - m)
    l = jnp.sum(p, axis=1, keepdims=True)
    o = _dot(p, v, "nn", 1) / l
    return o, jnp.broadcast_to(m + jnp.log(l), o.shape)


def _merge_f(pids, o0, o1, o2, l0, l1, l2):
    m = jnp.maximum(jnp.maximum(l0, l1), l2)
    e0, e1, e2 = jnp.exp(l0 - m), jnp.exp(l1 - m), jnp.exp(l2 - m)
    return ((e0 * o0 + e1 * o1 + e2 * o2) / (e0 + e1 + e2),)


def _gate_f(pids, g0, g1, g2, ya, yb, yc, b0, b1, b2):
    return (_sigmoid(g0 + b0) * ya + _sigmoid(g1 + b1) * yb + _sigmoid(g2 + b2) * yc,)


def _attn_window(i, L, tq, W):
    k0 = jnp.clip(i * tq - DA_RADIUS, 0, L - W)
    return pl.multiple_of(k0, DA_RADIUS)


def _strided_view(t, B, dil):
    T, HD = t.shape
    return t.reshape(B, T // B // dil, dil * HD)


def _attn_fwd(name, q, k, v, B, dil):
    T, HD = q.shape
    NS, L = B * dil, T // B // dil
    tq = min(DA_TQ, L)
    W = min(L, tq + 2 * DA_RADIUS)

    def body(q_ref, k_ref, v_ref, o_ref, l_ref):
        i = pl.program_id(1)
        k0 = _attn_window(i, L, tq, W)
        for h in range(DA_HEADS):
            hs = slice(h * DA_DIM, (h + 1) * DA_DIM)
            o, lse = _attn_head(q_ref[:, hs], k_ref[pl.ds(k0, W), hs], v_ref[pl.ds(k0, W), hs], i * tq, k0)
            o_ref[:, hs] = o
            l_ref[:, hs] = lse

    qs = pl.BlockSpec((None, tq, HD), lambda s, i: (s // dil, i, s % dil))
    ks = pl.BlockSpec((None, L, HD), lambda s, i: (s // dil, 0, s % dil))
    o, lse = pl.pallas_call(
        body, name=name, grid=(NS, L // tq), in_specs=[qs, ks, ks], out_specs=[qs, qs],
        out_shape=[jax.ShapeDtypeStruct((B, L, dil * HD), F32)] * 2, compiler_params=_params(),
    )(*[_strided_view(t, B, dil) for t in (q, k, v)])
    return o.reshape(T, HD), lse.reshape(T, HD)


def _attn_bwd(name, q, k, v, do, dl, B, dil):
    T, HD = q.shape
    NS, L = B * dil, T // B // dil
    tq = min(DA_TQ, L)
    W = min(L, tq + 2 * DA_RADIUS)

    def body(q_ref, k_ref, v_ref, do_ref, dl_ref, dq_ref, dk_ref, dv_ref):
        i = pl.program_id(1)
        k0 = _attn_window(i, L, tq, W)

        @pl.when(i == 0)
        def _():
            dk_ref[...] = jnp.zeros_like(dk_ref)
            dv_ref[...] = jnp.zeros_like(dv_ref)

        for h in range(DA_HEADS):
            hs = slice(h * DA_DIM, (h + 1) * DA_DIM)
            f = functools.partial(_attn_head, qpos0=i * tq, kpos0=k0)
            _, vjp = jax.vjp(f, q_ref[:, hs].astype(F32), k_ref[pl.ds(k0, W), hs].astype(F32),
                             v_ref[pl.ds(k0, W), hs].astype(F32))
            dq, dk, dv = vjp((do_ref[:, hs], dl_ref[:, hs]))
            dq_ref[:, hs] = dq
            dk_ref[pl.ds(k0, W), hs] += dk
            dv_ref[pl.ds(k0, W), hs] += dv

    qs = pl.BlockSpec((None, tq, HD), lambda s, i: (s // dil, i, s % dil))
    ks = pl.BlockSpec((None, L, HD), lambda s, i: (s // dil, 0, s % dil))
    res = pl.pallas_call(
        body, name=name, grid=(NS, L // tq), in_specs=[qs, ks, ks, qs, qs], out_specs=[qs, ks, ks],
        out_shape=[jax.ShapeDtypeStruct((B, L, dil * HD), F32)] * 3, compiler_params=_params(),
    )(*[_strided_view(t, B, dil) for t in (q, k, v, do, dl)])
    return tuple(t.reshape(T, HD) for t in res)


def _scan_chunk(t, rev, N):
    c = jnp.where(rev, N - 1 - t, t)
    per = DN_SUPER // DN_CHUNK
    return c, pl.multiple_of(c * DN_CHUNK, DN_CHUNK), pl.multiple_of((c % per) * DN_CHUNK, DN_CHUNK), \
        pl.multiple_of((c // per) * DN_SUPER, DN_SUPER)


def _dn_scan_fwd(name, uw, qk, qd, kd, gl, B):
    R, T, _ = uw.shape
    S = T // B
    N = S // DN_CHUNK
    C, DK = DN_CHUNK, DN_DIM

    PAIR = 2

    def body(uw_ref, qk_ref, qd_ref, kd_ref, gl_ref, o_ref, st_ref, vn_ref):
        rev = pl.program_id(1) * PAIR >= DN_HEADS
        vn_ref[...] = jnp.zeros_like(vn_ref)

        def step(t, states):
            c, r0, w0, s0 = _scan_chunk(t, rev, N)
            rows = pl.ds(r0, C)
            new = []
            for p, state in enumerate(states):
                st_ref[p, c] = state
                vnew = uw_ref[p, rows, 0:DK] - _dotp(uw_ref[p, rows, DK:2 * DK], state, "nn", 1)
                vn_ref[p, pl.ds(w0, C), :] = vnew
                o_ref[p, rows, :] = (_dotp(qd_ref[p, rows, :], state, "nn", 1)
                                     + _dotp(qk_ref[p, rows, :], vn_ref[p], "nn", 1))
                new.append(state * gl_ref[p, pl.ds(r0, 1), :] + _dotp(kd_ref[p, rows, :], vnew, "tn", 1))
            return tuple(new)

        lax.fori_loop(0, N, step, tuple(jnp.zeros((DK, DK), F32) for _ in range(PAIR)))

    def seq(w):
        return pl.BlockSpec((PAIR, S, w), lambda b, r: (r, b, 0))

    return pl.pallas_call(
        body, name=name, grid=(B, R // PAIR),
        in_specs=[seq(2 * DK), seq(DN_SUPER), seq(DK), seq(DK), seq(DK)],
        out_specs=[seq(DK), pl.BlockSpec((None, PAIR, N, DK, DK), lambda b, r: (b, r, 0, 0, 0))],
        out_shape=[jax.ShapeDtypeStruct((R, T, DK), F32), jax.ShapeDtypeStruct((B, R, N, DK, DK), F32)],
        scratch_shapes=[pltpu.VMEM((PAIR, DN_SUPER, DK), F32)], compiler_params=_params(),
    )(uw, qk, qd, kd, gl)


def _dn_scan_bwd(name, uw, qk, qd, kd, gl, st, do, B):
    R, T, _ = uw.shape
    S = T // B
    N = S // DN_CHUNK
    C, DK = DN_CHUNK, DN_DIM

    def body(uw_ref, qk_ref, qd_ref, kd_ref, gl_ref, st_ref, do_ref, duw_ref, dqk_ref, dqd_ref, dkd_ref, dgl_ref,
             vn_ref, tmp_ref):
        rev = pl.program_id(1) >= DN_HEADS
        vn_ref[...] = jnp.zeros_like(vn_ref)
        dgl_ref[...] = jnp.zeros_like(dgl_ref)

        def step(t, dstate):
            c, r0, w0, s0 = _scan_chunk(N - 1 - t, rev, N)
            rows = pl.ds(r0, C)
            state = st_ref[c]
            w = uw_ref[rows, DK:2 * DK]
            vnew = uw_ref[rows, 0:DK] - _dotp(w, state, "nn", 1)
            vn_ref[pl.ds(w0, C), :] = vnew
            do_c = do_ref[rows, :]
            tmp_ref[...] = _dotp(qk_ref[rows, :], do_c, "tn", 1)
            dvn = tmp_ref[pl.ds(w0, C), :] + _dotp(kd_ref[rows, :], dstate, "nn", 1)
            dqk_ref[rows, :] = _dotp(do_c, vn_ref[...], "nt", 1)
            dqd_ref[rows, :] = _dotp(do_c, state, "nt", 1)
            dkd_ref[rows, :] = _dotp(vnew, dstate, "nt", 1)
            dgl_ref[pl.ds(r0, 1), :] = jnp.sum(state * dstate, axis=0, keepdims=True)
            duw_ref[rows, 0:DK] = dvn
            duw_ref[rows, DK:2 * DK] = -_dotp(dvn, state, "nt", 1)
            return (_dotp(qd_ref[rows, :], do_c, "tn", 1) + dstate * gl_ref[pl.ds(r0, 1), :]
                    - _dotp(w, dvn, "tn", 1))

        lax.fori_loop(0, N, step, jnp.zeros((DK, DK), F32))

    def seq(w):
        return pl.BlockSpec((None, S, w), lambda b, r: (r, b, 0))

    return pl.pallas_call(
        body, name=name, grid=(B, R),
        in_specs=[seq(2 * DK), seq(DN_SUPER), seq(DK), seq(DK), seq(DK),
                  pl.BlockSpec((None, None, N, DK, DK), lambda b, r: (b, r, 0, 0, 0)),
                  pl.BlockSpec((None, S, DK), lambda b, r: (r % DN_HEADS, b, 0))],
        out_specs=[seq(2 * DK), seq(DN_SUPER), seq(DK), seq(DK), seq(DK)],
        out_shape=[jax.ShapeDtypeStruct((R, T, 2 * DK), F32), jax.ShapeDtypeStruct((R, T, DN_SUPER), F32),
                   jax.ShapeDtypeStruct((R, T, DK), F32), jax.ShapeDtypeStruct((R, T, DK), F32),
                   jax.ShapeDtypeStruct((R, T, DK), F32)],
        scratch_shapes=[pltpu.VMEM((DN_SUPER, DK), F32), pltpu.VMEM((DN_SUPER, DK), F32)],
        compiler_params=_params(),
    )(uw, qk, qd, kd, gl, st, do)


def _loss_fwd_bwd(name, x, gain, target):
    T, D = x.shape
    tm = _pick(T, 512, 8)

    def body(x_ref, g_ref, t_ref, loss_ref, dx_ref, dg_ref):
        i = pl.program_id(0)

        def f(xv, gv):
            e = _rms(xv, gv) - t_ref[...]
            return 0.5 * jnp.sum(jnp.mean(e * e, axis=-1, keepdims=True))

        val, (dx, dg) = jax.value_and_grad(f, argnums=(0, 1))(x_ref[...], g_ref[...])
        dx_ref[...] = dx
        part = jnp.full(loss_ref.shape, val, F32)

        @pl.when(i == 0)
        def _():
            dg_ref[...] = dg
            loss_ref[...] = part

        @pl.when(i != 0)
        def _():
            dg_ref[...] += dg
            loss_ref[...] += part

    return pl.pallas_call(
        body, name=name, grid=(T // tm,),
        in_specs=[pl.BlockSpec((tm, D), lambda i: (i, 0)), pl.BlockSpec((1, D), lambda i: (0, 0)),
                  pl.BlockSpec((tm, D), lambda i: (i, 0))],
        out_specs=[pl.BlockSpec((8, 128), lambda i: (0, 0)), pl.BlockSpec((tm, D), lambda i: (i, 0)),
                   pl.BlockSpec((1, D), lambda i: (0, 0))],
        out_shape=[jax.ShapeDtypeStruct((8, 128), F32), jax.ShapeDtypeStruct((T, D), F32),
                   jax.ShapeDtypeStruct((1, D), F32)],
        compiler_params=_params(),
    )(x, gain, target)


class _Cols:
    def __init__(self, D):
        assert D % 256 == 0
        self.gate = 0
        self.da = 3 * D
        self.qkv = self.da + 3 * DA_WIDTH
        self.z = self.qkv + 3 * DN_WIDTH
        self.pool = self.z + DN_WIDTH
        self.ba = self.pool + POOL_WIDTH
        self.total = self.ba + BA_PAD


def _rope_tables(S):
    half = DA_DIM // 2
    inv_freq = ROPE_THETA ** (-jnp.arange(half, dtype=F32) / half)
    ang = jnp.arange(S, dtype=F32)[:, None] * inv_freq[None, :]
    reps = DA_OUT // DA_DIM
    cos = jnp.tile(jnp.concatenate([jnp.cos(ang), jnp.cos(ang)], axis=1), (1, reps))
    sin = jnp.tile(jnp.concatenate([jnp.sin(ang), jnp.sin(ang)], axis=1), (1, reps))
    return cos, sin


def _to_strided(t, B, dil):
    T, w = t.shape
    L = T // B // dil
    return t.reshape(B, L, dil, w).transpose(0, 2, 1, 3).reshape(B * dil, L, w)


def _from_strided(t, B, dil):
    NS, L, w = t.shape
    return t.reshape(B, dil, L, w).transpose(0, 2, 1, 3).reshape(B * dil * L, w)


def _mixer(l, x1, w, B, host_gather=None):
    T, D = x1.shape
    S = T // B
    c = _Cols(D)
    tm = _pick(S, 512, 8)
    nmS = S // tm
    n = f"l{l}_"

    norm_ins = [_In(x1, (tm, D), lambda i: (i, 0)), _In(w["mix_norm"], (1, D), lambda i: (0, 0), acc=True)]
    norm_outs = [_Out((T, D), BF16, (tm, D), lambda i: (i, 0))]
    (h,) = _tile_fwd(n + "norm", _norm_f, (T // tm,), norm_ins, norm_outs)
    P = _mm(n + "proj", h, w["w_cat"], "nn", tn=896, tk=1024)
    baT = P[:, c.ba:c.ba + 16].T

    cb = c.qkv // DN_DIM
    conv_ins = [_In(P, (S, DN_DIM), lambda j, b: (b, cb + j), g=((T, 3 * DN_WIDTH), lambda j, b: (b, j)), gdtype=BF16),
                _In(w["dn_conv"], (DN_CONV, DN_DIM), lambda j, b: (0, j), acc=True)]
    conv_outs = [_Out((T, 3 * DN_WIDTH), F32, (S, DN_DIM), lambda j, b: (b, j))]
    conv_grid = (3 * DN_HEADS, B)
    (qkvc,) = _tile_fwd(n + "dnconv", _dn_conv_f, conv_grid, conv_ins, conv_outs)

    tg = _pick(T, 2048, 128)
    gate_ins = [_In(baT, (8, tg), lambda i: (0, i)), _In(baT, (8, tg), lambda i: (1, i)),
                _In(w["dn_a_log"], (8, 1), lambda i: (0, 0), acc=True),
                _In(w["dn_dt_bias"], (8, 1), lambda i: (0, 0), acc=True)]
    gate_ins[0].g = ((8, T), lambda i: (0, i))
    gate_ins[1].g = ((8, T), lambda i: (0, i))
    gate_outs = [_Out((8, T), F32, (8, tg), lambda i: (0, i))] * 2
    beta, gdec = _tile_fwd(n + "dngate", _dn_gate_f, (T // tg,), gate_ins, gate_outs)

    NSC = T // DN_SUPER
    beta4 = beta.reshape(2, DN_HEADS, NSC, 1, DN_SUPER)
    gdec4 = gdec.reshape(2, DN_HEADS, NSC, 1, DN_SUPER)
    R = 2 * DN_HEADS

    def qkv_in(off):
        return _In(qkvc, (DN_SUPER, DN_DIM), lambda hh, m: (m, off + hh), acc=True,
                   g=((T, DN_WIDTH), lambda hh, m: (m, hh)))

    def row_in(a):
        return _In(a, (2, None, None, 1, DN_SUPER), lambda hh, m: (0, hh, m, 0, 0), split=True)

    def chain_out(wd):
        return _Out((2, DN_HEADS, T, wd), F32, (2, None, DN_SUPER, wd), lambda hh, m: (0, hh, m, 0), split=True)

    prep_ins = [qkv_in(0), qkv_in(DN_HEADS), qkv_in(2 * DN_HEADS), row_in(beta4), row_in(gdec4)]
    prep_outs = [chain_out(2 * DN_DIM), chain_out(DN_SUPER), chain_out(DN_DIM), chain_out(DN_DIM), chain_out(DN_DIM)]
    prep_grid = (DN_HEADS, NSC)
    prep_res = _tile_fwd(n + "dnprep", _dn_prep_f, prep_grid, prep_ins, prep_outs, sub=2, comm=host_gather)
    gathered_next = None
    if host_gather is not None:
        prep_res, gathered_next = prep_res
    uw, qk, qd, kd, gl = (t.reshape((R,) + t.shape[2:]) for t in prep_res)
    o_dn, states = _dn_scan_fwd(n + "dnscan", uw, qk, qd, kd, gl, B)

    zb = c.z // DN_DIM
    out_ins = [_In(o_dn, (None, S, DN_DIM), lambda b, hh: (hh, b, 0)),
               _In(o_dn, (None, S, DN_DIM), lambda b, hh: (DN_HEADS + hh, b, 0)),
               _In(P, (S, DN_DIM), lambda b, hh: (b, zb + hh), g=((T, DN_WIDTH), lambda b, hh: (b, hh)), gdtype=BF16),
               _In(w["dn_out_norm"], (1, DN_DIM), lambda b, hh: (0, 0), acc=True)]
    out_ins[0].g = ((DN_HEADS, T, DN_DIM), lambda b, hh: (hh, b, 0))
    out_ins[1].g = ((DN_HEADS, T, DN_DIM), lambda b, hh: (hh, b, 0))
    out_outs = [_Out((T, DN_WIDTH), BF16, (S, DN_DIM), lambda b, hh: (b, hh))]
    (ya_in,) = _tile_fwd(n + "dnout", _dn_out_f, (B, DN_HEADS), out_ins, out_outs)

    pb = c.pool // POOL_DIM
    pool_ins = [_In(P, (S, POOL_DIM), lambda gi, b: (b, pb + gi), g=((T, POOL_WIDTH), lambda gi, b: (b, gi)), gdtype=BF16),
                _In(w["pool_w"], (None, POOL_DIM, POOL_DIM), lambda gi, b: (gi, 0, 0), acc=True),
                _In(w["pool_scale"], (None, 1, POOL_DIM), lambda gi, b: (gi, 0, 0), acc=True)]
    pool_outs = [_Out((T, POOL_WIDTH), BF16, (S, POOL_DIM), lambda gi, b: (b, gi))]
    (yb_in,) = _tile_fwd(n + "pool", _pool_f, (POOL_GROUPS, B), pool_ins, pool_outs)

    cos, sin = _rope_tables(S)
    db = c.da // DA_OUT

    def da_in(k):
        return _In(P, (tm, DA_OUT), lambda i: (i, db + k), g=((T, DA_OUT), lambda i: (i, 0)), gdtype=BF16)

    rope_ins = [da_in(k) for k in range(3 * DA_GROUPS)]
    rope_ins += [_In(cos, (tm, DA_OUT), lambda i: (i % nmS, 0), kind="c"),
                 _In(sin, (tm, DA_OUT), lambda i: (i % nmS, 0), kind="c")]
    rope_outs = [_Out((T, DA_OUT), BF16, (tm, DA_OUT), lambda i: (i, 0))] * (3 * DA_GROUPS)
    roped = _tile_fwd(n + "rope", _rope_f, (T // tm,), rope_ins, rope_outs)
    strided = []
    o_g, l_g = [], []
    for gi, dil in enumerate(DA_DILATIONS):
        qs, ks, vs = roped[gi], roped[DA_GROUPS + gi], roped[2 * DA_GROUPS + gi]
        strided.append((qs, ks, vs))
        o, lse = _attn_fwd(n + f"attn{gi}", qs, ks, vs, B, dil)
        o_g.append(o)
        l_g.append(lse)
    mrg_ins = [_In(a, (tm, DA_OUT), lambda i: (i, 0)) for a in o_g + l_g]
    mrg_outs = [_Out((T, DA_OUT), BF16, (tm, DA_OUT), lambda i: (i, 0))]
    (yc_in,) = _tile_fwd(n + "merge", _merge_f, (T // tm,), mrg_ins, mrg_outs)

    ya = _mm(n + "pa", ya_in, w["w_proj_a"], "nn")
    yb = _mm(n + "pb", yb_in, w["w_proj_b"], "nn")
    yc = _mm(n + "pc", yc_in, w["w_proj_c"], "nn")

    def gcol(k):
        return _In(P, (tm, D), lambda i: (i, k), g=((T, D), lambda i: (i, 0)), gdtype=BF16)

    def yin(a):
        return _In(a, (tm, D), lambda i: (i, 0), gdtype=BF16)

    def bin_(k):
        return _In(w["b_gate"][k:k + 1], (1, D), lambda i: (0, 0), acc=True)

    gm_ins = [gcol(0), gcol(1), gcol(2), yin(ya), yin(yb), yin(yc), bin_(0), bin_(1), bin_(2)]
    gm_outs = [_Out((T, D), BF16, (tm, D), lambda i: (i, 0))]
    (merged,) = _tile_fwd(n + "gates", _gate_f, (T // tm,), gm_ins, gm_outs)
    x2 = _mm(n + "out", merged, w["w_out"], "nn", add=x1)

    def backward(dx2, host_exchange=None):
        return _mixer_bwd(dx2, host_exchange, **{k: v for k, v in locals_.items() if k in _MIXER_BWD_NEEDS})

    locals_ = dict(locals())
    return x2, backward, gathered_next


_MIXER_BWD_NEEDS = ("n", "B", "T", "D", "tm", "w", "h", "merged", "gm_ins", "gm_outs", "ya_in", "yb_in", "yc_in",
                    "mrg_ins", "mrg_outs", "strided", "rope_ins", "rope_outs", "pool_ins", "pool_outs", "out_ins",
                    "out_outs", "uw", "qk", "qd", "kd", "gl", "states", "prep_grid", "prep_ins", "prep_outs", "tg",
                    "gate_ins", "gate_outs", "conv_grid", "conv_ins", "conv_outs", "norm_ins", "norm_outs")


def _mixer_bwd(dx2, host_exchange, *, n, B, T, D, tm, w, h, merged, gm_ins, gm_outs, ya_in, yb_in, yc_in, mrg_ins, mrg_outs, strided,
               rope_ins, rope_outs, pool_ins, pool_outs, out_ins, out_outs, uw, qk, qd, kd, gl, states, prep_grid,
               prep_ins, prep_outs, tg, gate_ins, gate_outs, conv_grid, conv_ins, conv_outs, norm_ins, norm_outs):
    g = {}
    dmerged = _mm(n + "d_merged", dx2, w["w_out"], "nt")
    g["w_out"] = _mm(n + "d_wout", merged, dx2, "tn", tm=1024, tn=1024, tk=1024)
    dg0, dg1, dg2, dya, dyb, dyc, db0, db1, db2 = _tile_bwd(
        n + "gates_b", _gate_f, (T // tm,), gm_ins, gm_outs, [dmerged], acc_from=0)
    g["b_gate"] = jnp.concatenate([db0, db1, db2], axis=0)
    dya_in = _mm(n + "d_pa", dya, w["w_proj_a"], "nt")
    dyb_in = _mm(n + "d_pb", dyb, w["w_proj_b"], "nt")
    dyc_in = _mm(n + "d_pc", dyc, w["w_proj_c"], "nt")
    g["w_proj_a"] = _mm(n + "d_wpa", ya_in, dya, "tn", tn=1024, tk=2048)
    g["w_proj_b"] = _mm(n + "d_wpb", yb_in, dyb, "tn", tn=1024, tk=2048)
    g["w_proj_c"] = _mm(n + "d_wpc", yc_in, dyc, "tn", tn=1024, tk=2048)

    dmrg = _tile_bwd(n + "merge_b", _merge_f, (T // tm,), mrg_ins, mrg_outs, [dyc_in])
    dq_parts, dk_parts, dv_parts = [], [], []
    for gi, dil in enumerate(DA_DILATIONS):
        qs, ks, vs = strided[gi]
        dq, dk, dv = _attn_bwd(n + f"attn{gi}_b", qs, ks, vs, dmrg[gi], dmrg[DA_GROUPS + gi], B, dil)
        dq_parts.append(dq)
        dk_parts.append(dk)
        dv_parts.append(dv)
    dP_da = _tile_bwd(n + "rope_b", _rope_f, (T // tm,), rope_ins, rope_outs, dq_parts + dk_parts + dv_parts)

    dPpool, g["pool_w"], g["pool_scale"] = _tile_bwd(
        n + "pool_b", _pool_f, (POOL_GROUPS, B), pool_ins, pool_outs, [dyb_in], acc_from=1)

    dof, dob, dPz, g["dn_out_norm"] = _tile_bwd(
        n + "dnout_b", _dn_out_f, (B, DN_HEADS), out_ins, out_outs, [dya_in], acc_from=0)
    del dob
    duw, dqk, dqd, dkd, dgl = _dn_scan_bwd(n + "dnscan_b", uw, qk, qd, kd, gl, states, dof, B)
    prep_cts = [t.reshape((2, DN_HEADS) + t.shape[1:]) for t in (duw, dqk, dqd, dkd, dgl)]
    prep_res = _tile_bwd(n + "dnprep_b", _dn_prep_f, prep_grid, prep_ins, prep_outs, prep_cts, sub=2,
                         comm=host_exchange)
    exchanged = None
    if host_exchange is not None:
        prep_res, exchanged = prep_res
    dq_, dk_, dv_, dbeta4, dgdec4 = prep_res
    dqkvc = jnp.concatenate([dq_, dk_, dv_], axis=1)
    dbraw, daraw, g["dn_a_log"], g["dn_dt_bias"] = _tile_bwd(
        n + "dngate_b", _dn_gate_f, (T // tg,), gate_ins, gate_outs,
        [dbeta4.reshape(8, T), dgdec4.reshape(8, T)], acc_from=0)
    dPqkv, g["dn_conv"] = _tile_bwd(n + "dnconv_b", _dn_conv_f, conv_grid, conv_ins, conv_outs, [dqkvc], acc_from=1)
    dba = jnp.concatenate([dbraw, daraw], axis=0).T.astype(BF16)
    dba = jnp.pad(dba, ((0, 0), (0, BA_PAD - 16)))
    dP = jnp.concatenate([dg0, dg1, dg2, *dP_da, dPqkv, dPz, dPpool, dba], axis=1)
    dh = _mm(n + "d_h", dP, w["w_cat"], "nt", tn=1024, tk=2688)
    g["w_cat"] = _mm(n + "d_wcat", h, dP, "tn", tm=1024, tn=896, tk=1024)
    dx1, g["mix_norm"] = _tile_bwd(n + "norm_b", _norm_f, (T // tm,), norm_ins, norm_outs, [dh], acc_from=0,
                                   addends={0: dx2})
    return dx1, g, exchanged


def _layer_weights(full, l, D):
    c = _Cols(D)
    w_in = full["w_in"][l]
    o_z, o_ba, o_pool, o_da = 3 * DN_WIDTH, 4 * DN_WIDTH, 4 * DN_WIDTH + 16, 4 * DN_WIDTH + 16 + POOL_WIDTH
    w_cat = jnp.concatenate(
        [full["w_gate"][l], w_in[:, o_da:], w_in[:, :o_z], w_in[:, o_z:o_ba], w_in[:, o_pool:o_da], w_in[:, o_ba:o_pool],
         jnp.zeros((D, BA_PAD - 16), w_in.dtype)], axis=1).astype(BF16)
    assert w_cat.shape[1] == c.total
    w = {k: full[k][l].astype(BF16) for k in ("ffn1_w_gate", "ffn1_w_up", "ffn1_w_down", "ffn2_w_gate", "ffn2_w_up",
                                              "ffn2_w_down", "w_proj_a", "w_proj_b", "w_proj_c", "w_out")}
    w["w_cat"] = w_cat
    w["ffn1_norm"] = full["ffn1_norm"][l][None].astype(F32)
    w["ffn2_norm"] = full["ffn2_norm"][l][None].astype(F32)
    w["mix_norm"] = full["mix_norm"][l][None].astype(F32)
    w["dn_conv"] = full["dn_conv"][l].astype(F32)
    w["dn_a_log"] = full["dn_a_log"][l].reshape(2 * DN_HEADS, 1).astype(F32)
    w["dn_dt_bias"] = full["dn_dt_bias"][l].reshape(2 * DN_HEADS, 1).astype(F32)
    w["dn_out_norm"] = full["dn_out_norm"][l][None].astype(F32)
    w["pool_w"] = full["pool_w"][l].astype(F32)
    w["pool_scale"] = full["pool_scale"][l].reshape(POOL_GROUPS, 1, POOL_DIM).astype(F32)
    w["b_gate"] = full["b_gate"][l].reshape(3, D).astype(F32)
    return w


def _layer_grads(g, D):
    c = _Cols(D)
    gc = g.pop("w_cat")
    out = dict(g)
    out["w_gate"] = gc[:, :c.da]
    out["w_in"] = jnp.concatenate([gc[:, c.qkv:c.pool], gc[:, c.ba:c.ba + 16], gc[:, c.pool:c.ba], gc[:, c.da:c.qkv]],
                                  axis=1)
    for k in ("ffn1_norm", "ffn2_norm", "mix_norm", "dn_out_norm"):
        out[k] = g[k][0]
    out["dn_a_log"] = g["dn_a_log"].reshape(2, DN_HEADS)
    out["dn_dt_bias"] = g["dn_dt_bias"].reshape(2, DN_HEADS)
    out["pool_scale"] = g["pool_scale"].reshape(POOL_WIDTH)
    out["b_gate"] = g["b_gate"].reshape(3 * D)
    return out


def _unshard(got):
    full = {}
    for k, t in zip(SHARDED, got):
        ax = SHARD_AXIS[k] - 1
        shp = t.shape[1:]
        full[k] = jnp.moveaxis(t, 0, ax).reshape(shp[:ax] + (N_DEV * shp[ax],) + shp[ax + 1:])
    return full


def _to_owner_blocks(grads):
    out = []
    for k in SHARDED:
        ax = SHARD_AXIS[k] - 1
        shp = grads[k].shape
        t = grads[k].reshape(shp[:ax] + (N_DEV, shp[ax] // N_DEV) + shp[ax + 1:])
        out.append(jnp.moveaxis(t, ax, 0).astype(BF16))
    return out


def _local_step(x, target, rep, shards, distributed):
    B, S, D = x.shape
    T = B * S
    depth = len(shards)
    xs = x.reshape(T, D)
    tape = []
    if distributed:
        sharded_now = _unshard(_all_gather("gather_l0", [shards[0][k] for k in SHARDED]))
    else:
        sharded_now = shards[0]
    for l in range(depth):
        full = {k: [v] * (l + 1) for k, v in sharded_now.items()}
        full.update({k: v for k, v in rep.items() if k != "final_norm"})
        w = _layer_weights(full, l, D)
        host = ("gather", [shards[l + 1][k] for k in SHARDED]) if distributed and l + 1 < depth else None
        x1 = _ffn_fwd(f"l{l}_ffn1", xs, w["ffn1_norm"], w["ffn1_w_gate"], w["ffn1_w_up"], w["ffn1_w_down"])
        x2, mixer_bwd, got = _mixer(l, x1, w, B, host)
        x3 = _ffn_fwd(f"l{l}_ffn2", x2, w["ffn2_norm"], w["ffn2_w_gate"], w["ffn2_w_up"], w["ffn2_w_down"])
        tape.append((w, xs, mixer_bwd, x2))
        xs = x3
        if l + 1 < depth:
            sharded_now = _unshard(got) if distributed else shards[l + 1]
    loss8, dx, dfinal = _loss_fwd_bwd("loss", xs, rep["final_norm"][None].astype(F32), target.reshape(T, D))
    per_layer = [None] * depth
    exchanged = [None] * depth
    pending = None
    for l in reversed(range(depth)):
        w, x0, mixer_bwd, x2 = tape[l]
        dx, dn2, dwg2, dwu2, dwd2 = _ffn_bwd(f"l{l}_ffn2b", x2, w["ffn2_norm"], w["ffn2_w_gate"], w["ffn2_w_up"],
                                             w["ffn2_w_down"], dx)
        dx, g, got = mixer_bwd(dx, ("exchange", pending) if pending is not None else None)
        if pending is not None:
            exchanged[l + 1] = got
        dx, dn1, dwg1, dwu1, dwd1 = _ffn_bwd(f"l{l}_ffn1b", x0, w["ffn1_norm"], w["ffn1_w_gate"], w["ffn1_w_up"],
                                             w["ffn1_w_down"], dx)
        g.update(ffn1_norm=dn1, ffn1_w_gate=dwg1, ffn1_w_up=dwu1, ffn1_w_down=dwd1,
                 ffn2_norm=dn2, ffn2_w_gate=dwg2, ffn2_w_up=dwu2, ffn2_w_down=dwd2)
        per_layer[l] = _layer_grads(g, D)
        if distributed:
            pending = _to_owner_blocks(per_layer[l])
    return loss8[0, 0], dx.reshape(B, S, D), per_layer, dfinal[0], exchanged, pending


def _mesh_position():
    mx, my, mc = lax.axis_index("x"), lax.axis_index("y"), lax.axis_index("c")
    return mx, my, mc, 4 * mx + 2 * my + mc


def _peers(mx, my, mc):
    out = []
    for k in range(1, N_DEV):
        px, py, pc = mx ^ ((k >> 2) & 1), my ^ ((k >> 1) & 1), mc ^ (k & 1)
        out.append(((px, py, pc), 4 * px + 2 * py + pc))
    return out


_ANY = pl.BlockSpec(memory_space=pl.ANY)


def _all_gather(name, xs):
    n = len(xs)

    def body(*refs):
        _gather_start(refs[:n], refs[n:2 * n], *refs[2 * n:])
        _gather_finish(refs[:n], refs[n:2 * n], *refs[2 * n:])

    sems = pltpu.SemaphoreType.DMA((n, N_DEV - 1))
    return pl.pallas_call(
        body, name=name, in_specs=[_ANY] * n, out_specs=[_ANY] * n,
        out_shape=[jax.ShapeDtypeStruct((N_DEV,) + x.shape, x.dtype) for x in xs],
        scratch_shapes=[sems, sems, pltpu.SemaphoreType.DMA((n,))],
    )(*xs)


class _GatherPlan:
    def __init__(self, x_refs, o_refs, send_sems, recv_sems, local_sems):
        self.x, self.o, self.ss, self.rs, self.ls = x_refs, o_refs, send_sems, recv_sems, local_sems
        self.mx, self.my, self.mc, self.me = _mesh_position()
        self.self_id = (self.mx, self.my, self.mc)
        self.sibling = (self.mx, self.my, 1 - self.mc)
        self.chips = [(1 - self.mx, self.my), (self.mx, 1 - self.my), (1 - self.mx, 1 - self.my)]

    def copy(self, a, k, blk, to, from_input=False):
        dst = self.o[a].at[blk]
        return pltpu.make_async_remote_copy(src_ref=self.x[a] if from_input else dst, dst_ref=dst,
                                            send_sem=self.ss.at[a, k], recv_sem=self.rs.at[a, k],
                                            device_id=to, device_id_type=pl.DeviceIdType.MESH)

    def own(self, a):
        return pltpu.make_async_copy(self.x[a], self.o[a].at[self.me], self.ls.at[a])

    def first_sends(self, a):
        cps = [self.copy(a, 0, self.me, self.sibling, from_input=True)]
        return cps + [self.copy(a, 1 + j, self.me, (*chip, self.mc), from_input=True) for j, chip in enumerate(self.chips)]

    def passed_on(self, a, j):
        cx, cy = self.chips[j]
        return self.copy(a, 4 + j, 4 * cx + 2 * cy + self.mc, self.sibling)


def _gather_start(x_refs, o_refs, send_sems, recv_sems, local_sems):
    p = _GatherPlan(x_refs, o_refs, send_sems, recv_sems, local_sems)
    for a in range(len(x_refs)):
        p.own(a).start()
        for cp in p.first_sends(a):
            cp.start()


def _gather_finish(x_refs, o_refs, send_sems, recv_sems, local_sems):
    p = _GatherPlan(x_refs, o_refs, send_sems, recv_sems, local_sems)
    n = len(x_refs)
    for a in range(n):
        for j, (cx, cy) in enumerate(p.chips):
            p.copy(a, 1 + j, 4 * cx + 2 * cy + p.mc, p.self_id).wait_recv()
            p.passed_on(a, j).start()
    for a in range(n):
        p.copy(a, 0, 4 * p.mx + 2 * p.my + 1 - p.mc, p.self_id).wait_recv()
        for j, (cx, cy) in enumerate(p.chips):
            p.copy(a, 4 + j, 4 * cx + 2 * cy + 1 - p.mc, p.self_id).wait_recv()
    for a in range(n):
        for cp in p.first_sends(a):
            cp.wait_send()
        for j in range(len(p.chips)):
            p.passed_on(a, j).wait_send()
        p.own(a).wait()


def _exchange_grads(name, gs, gr):
    ns, n = len(gs), len(gs) + len(gr)

    def body(*refs):
        _exchange_start(refs[:n], refs[n:2 * n], *refs[2 * n:], n_sharded=ns)
        _exchange_finish(refs[:n], refs[n:2 * n], *refs[2 * n:], n_sharded=ns)

    sems = pltpu.SemaphoreType.DMA((n, N_DEV - 1))
    outs = pl.pallas_call(
        body, name=name, in_specs=[_ANY] * n, out_specs=[_ANY] * n,
        out_shape=[jax.ShapeDtypeStruct(a.shape, a.dtype) for a in gs]
        + [jax.ShapeDtypeStruct((N_DEV,) + a.shape, a.dtype) for a in gr],
        scratch_shapes=[sems, sems, pltpu.SemaphoreType.DMA((n,))],
    )(*gs, *gr)
    return outs[:ns], outs[ns:]


def _exchange_copies(in_refs, out_refs, send_sems, recv_sems, local_sems, n_sharded):
    mx, my, mc, me = _mesh_position()
    n = len(in_refs)
    own = [pltpu.make_async_copy(in_refs[a].at[me] if a < n_sharded else in_refs[a], out_refs[a].at[me],
                                 local_sems.at[a]) for a in range(n)]
    remote = []
    for k, (peer, pid) in enumerate(_peers(mx, my, mc)):
        for a in range(n):
            src = in_refs[a].at[pid] if a < n_sharded else in_refs[a]
            remote.append(pltpu.make_async_remote_copy(
                src_ref=src, dst_ref=out_refs[a].at[me], send_sem=send_sems.at[a, k], recv_sem=recv_sems.at[a, k],
                device_id=peer, device_id_type=pl.DeviceIdType.MESH))
    return own, remote


def _exchange_start(in_refs, out_refs, send_sems, recv_sems, local_sems, n_sharded=None):
    ns = len(in_refs) if n_sharded is None else n_sharded
    own, remote = _exchange_copies(in_refs, out_refs, send_sems, recv_sems, local_sems, ns)
    for cp in own + remote:
        cp.start()


def _exchange_finish(in_refs, out_refs, send_sems, recv_sems, local_sems, n_sharded=None):
    ns = len(in_refs) if n_sharded is None else n_sharded
    own, remote = _exchange_copies(in_refs, out_refs, send_sems, recv_sems, local_sems, ns)
    for cp in remote:
        cp.wait_send()
        cp.wait_recv()
    for cp in own:
        cp.wait()


def _reduce_adamw(name, parts, w, m, v):
    shape = w.shape
    cols = shape[-1]
    w2, m2, v2 = (t.reshape(-1, cols) for t in (w, m, v))
    p3 = parts.reshape(N_DEV, -1, cols)
    rows = w2.shape[0]
    tr = _pick(rows, 512, 16) if rows > 1024 else rows
    c1 = 1.0 - ADAM_B1 ** ADAM_STEP
    c2 = 1.0 - ADAM_B2 ** ADAM_STEP

    def body(p_ref, w_ref, m_ref, v_ref, g_ref, d_ref, nm_ref, nv_ref):
        gv = p_ref[0].astype(F32)
        for d in range(1, N_DEV):
            gv = gv + p_ref[d].astype(F32)
        nm = ADAM_B1 * m_ref[...] + (1.0 - ADAM_B1) * gv
        nv = ADAM_B2 * v_ref[...] + (1.0 - ADAM_B2) * (gv * gv)
        g_ref[...] = gv
        d_ref[...] = -ADAM_LR * ((nm / c1) / (jnp.sqrt(nv / c2) + ADAM_EPS) + ADAM_WD * w_ref[...])
        nm_ref[...] = nm
        nv_ref[...] = nv

    spec = pl.BlockSpec((tr, cols), lambda i: (i, 0))
    outs = pl.pallas_call(
        body, name=name, grid=(rows // tr,),
        in_specs=[pl.BlockSpec((N_DEV, tr, cols), lambda i: (0, i, 0))] + [spec] * 3, out_specs=[spec] * 4,
        out_shape=[jax.ShapeDtypeStruct((rows, cols), F32)] * 4, compiler_params=_params(),
    )(p3, w2, m2, v2)
    return tuple(o.reshape(shape) for o in outs)


def kernel(x, ffn1_norm, ffn1_w_gate, ffn1_w_up, ffn1_w_down, mix_norm, w_in, dn_conv, dn_a_log, dn_dt_bias, dn_out_norm, pool_w, pool_scale, w_proj_a, w_proj_b, w_proj_c, w_gate, b_gate, w_out, ffn2_norm, ffn2_w_gate, ffn2_w_up, ffn2_w_down, final_norm, loss_target, m_ffn1_norm, m_ffn1_w_gate, m_ffn1_w_up, m_ffn1_w_down, m_mix_norm, m_w_in, m_dn_conv, m_dn_a_log, m_dn_dt_bias, m_dn_out_norm, m_pool_w, m_pool_scale, m_w_proj_a, m_w_proj_b, m_w_proj_c, m_w_gate, m_b_gate, m_w_out, m_ffn2_norm, m_ffn2_w_gate, m_ffn2_w_up, m_ffn2_w_down, m_final_norm, v_ffn1_norm, v_ffn1_w_gate, v_ffn1_w_up, v_ffn1_w_down, v_mix_norm, v_w_in, v_dn_conv, v_dn_a_log, v_dn_dt_bias, v_dn_out_norm, v_pool_w, v_pool_scale, v_w_proj_a, v_w_proj_b, v_w_proj_c, v_w_gate, v_b_gate, v_w_out, v_ffn2_norm, v_ffn2_w_gate, v_ffn2_w_up, v_ffn2_w_down, v_final_norm):
    args = locals()
    wts = {k: args[k] for k in WEIGHTS}
    ms = {k: args["m_" + k] for k in WEIGHTS}
    vs = {k: args["v_" + k] for k in WEIGHTS}

    depth = w_in.shape[0]
    rep = {k: wts[k] for k in REPLICATED}
    shards = [{k: wts[k][l].astype(BF16) for k in SHARDED} for l in range(depth)]
    loss_local, dx, per_layer, dfinal, exchanged, pending = _local_step(x, loss_target, rep, shards, True)
    loss = lax.psum(loss_local, ("x", "y", "c"))

    gr = [dfinal if k == "final_norm" else jnp.stack([pg[k] for pg in per_layer]).astype(F32).reshape(wts[k].shape)
          for k in REPLICATED]
    exchanged[0], got_r = _exchange_grads("exchange_grads", pending, gr)
    parts = {k: jnp.stack([exchanged[l][j] for l in range(depth)], axis=1) for j, k in enumerate(SHARDED)}
    parts.update(zip(REPLICATED, got_r))

    g_final, deltas, new_m, new_v = {}, {}, {}, {}
    for k in WEIGHTS:
        g_final[k], deltas[k], new_m[k], new_v[k] = _reduce_adamw("adamw_" + k, parts[k], wts[k], ms[k], vs[k])
    return (loss, dx, *[g_final[k] for k in WEIGHTS], *[deltas[k] for k in WEIGHTS], *[new_m[k] for k in WEIGHTS],
            *[new_v[k] for k in WEIGHTS])
```

```python
import functools
import math

import jax
import jax.numpy as jnp
from jax import lax
from jax.experimental import pallas as pl
from jax.experimental.pallas import tpu as pltpu

F32 = jnp.float32
BF16 = jnp.bfloat16

N_DEV = 8
RMS_EPS = 1e-6
L2_EPS = 1e-6
DN_HEADS = 4
DN_DIM = 128
DN_WIDTH = DN_HEADS * DN_DIM
DN_CONV = 5
DN_CHUNK = 64
DN_SUPER = 256
POOL_GROUPS = 4
POOL_DIM = 128
POOL_WIDTH = POOL_GROUPS * POOL_DIM
POOL_MAX_HALF = 8
DA_GROUPS = 3
DA_HEADS = 4
DA_DIM = 64
DA_WIDTH = DA_GROUPS * DA_HEADS * DA_DIM
DA_OUT = DA_HEADS * DA_DIM
DA_DILATIONS = (1, 4, 16)
DA_RADIUS = 64
DA_TQ = 256
FFN_BWD_TF = 512
ROPE_THETA = 10000.0
MASK_VALUE = -1e30
BA_PAD = 128

ADAM_LR = 0.001
ADAM_B1 = 0.9
ADAM_B2 = 0.999
ADAM_EPS = 1e-08
ADAM_WD = 0.01
ADAM_STEP = 10

VMEM_LIMIT_V7X = 56 * 1024 * 1024
LANES = 1024

SHARDED = ("ffn1_w_gate", "ffn1_w_up", "ffn1_w_down", "w_in", "dn_conv", "w_proj_a", "w_proj_b", "w_proj_c",
           "w_gate", "w_out", "ffn2_w_gate", "ffn2_w_up", "ffn2_w_down")
SHARD_AXIS = {"ffn1_w_gate": 2, "ffn1_w_up": 2, "ffn1_w_down": 1, "w_in": 2, "dn_conv": 2, "w_proj_a": 2,
              "w_proj_b": 2, "w_proj_c": 2, "w_gate": 2, "w_out": 1, "ffn2_w_gate": 2, "ffn2_w_up": 2,
              "ffn2_w_down": 1}
REPLICATED = ("ffn1_norm", "mix_norm", "dn_a_log", "dn_dt_bias", "dn_out_norm", "pool_w", "pool_scale", "b_gate",
              "ffn2_norm", "final_norm")
WEIGHTS = ("ffn1_norm", "ffn1_w_gate", "ffn1_w_up", "ffn1_w_down", "mix_norm", "w_in", "dn_conv", "dn_a_log",
           "dn_dt_bias", "dn_out_norm", "pool_w", "pool_scale", "w_proj_a", "w_proj_b", "w_proj_c", "w_gate",
           "b_gate", "w_out", "ffn2_norm", "ffn2_w_gate", "ffn2_w_up", "ffn2_w_down", "final_norm")


def _params(**kw):
    return pltpu.CompilerParams(vmem_limit_bytes=VMEM_LIMIT_V7X, **kw)


def _pick(n, target, align):
    best = None
    t = align
    while t <= min(n, target):
        if n % t == 0:
            best = t
        t += align
    return best if best is not None else n


_DIMS = {"nn": (((1,), (0,)), ((), ())), "nt": (((1,), (1,)), ((), ())), "tn": (((0,), (0,)), ((), ()))}


def _dg(a, b, mode):
    return lax.dot_general(a, b, _DIMS[mode], preferred_element_type=F32)


def _split2(a):
    hi = a.astype(BF16)
    lo = (a - hi.astype(F32)).astype(BF16)
    return hi, lo


def _dotp(a, b, mode, passes):
    if passes == 1:
        return _dg(a.astype(BF16), b.astype(BF16), mode)
    ah, al = _split2(a.astype(F32))
    bh, bl = _split2(b.astype(F32))
    return _dg(ah, bh, mode) + (_dg(ah, bl, mode) + _dg(al, bh, mode))


@functools.partial(jax.custom_vjp, nondiff_argnums=(2, 3))
def _dot(a, b, mode, passes):
    return _dotp(a, b, mode, passes)


def _dot_fwd(a, b, mode, passes):
    return _dotp(a, b, mode, passes), (a, b)


def _dot_bwd(mode, passes, res, ct):
    a, b = res
    if mode == "nn":
        da, db = _dotp(ct, b, "nt", passes), _dotp(a, ct, "tn", passes)
    elif mode == "nt":
        da, db = _dotp(ct, b, "nn", passes), _dotp(ct, a, "tn", passes)
    else:
        da, db = _dotp(b, ct, "nt", passes), _dotp(a, ct, "nn", passes)
    return da.astype(a.dtype), db.astype(b.dtype)


_dot.defvjp(_dot_fwd, _dot_bwd)


def _split3(x):
    x1 = x.astype(BF16)
    r = x - x1.astype(F32)
    x2 = r.astype(BF16)
    x3 = (r - x2.astype(F32)).astype(BF16)
    return x1, x2, x3


def _mdotp(mask, x, mode):
    x1, x2, x3 = _split3(x)
    return _dg(mask, x1, mode) + (_dg(mask, x2, mode) + _dg(mask, x3, mode))


@jax.custom_vjp
def _mdot(mask, x):
    return _mdotp(mask, x, "nn")


def _mdot_fwd(mask, x):
    return _mdotp(mask, x, "nn"), mask


def _mdot_bwd(mask, ct):
    return jnp.zeros_like(mask), _mdotp(mask, ct, "tn")


_mdot.defvjp(_mdot_fwd, _mdot_bwd)


_SOLVE_SQUARINGS = int(math.log2(DN_CHUNK)) - 1


def _unit_solve_fwd(A, R):
    Ab = A.astype(BF16)
    X = R - _dg(Ab, R.astype(BF16), "nn")
    P, powers = Ab, []
    for _ in range(_SOLVE_SQUARINGS):
        P = _dg(P, P, "nn").astype(BF16)
        powers.append(P)
        X = X + _dg(P, X.astype(BF16), "nn")
    return X, (Ab, tuple(powers), X)


@jax.custom_vjp
def _unit_solve(A, R):
    return _unit_solve_fwd(A, R)[0]


def _unit_solve_bwd(res, dX):
    Ab, powers, X = res
    Y = dX - _dg(Ab, dX.astype(BF16), "tn")
    for P in powers:
        Y = Y + _dg(P, Y.astype(BF16), "tn")
    return -_dg(Y.astype(BF16), X.astype(BF16), "nt"), Y


_unit_solve.defvjp(_unit_solve_fwd, _unit_solve_bwd)


def _shift_impl(x, o):
    if o == 0:
        return x
    n = x.shape[0]
    y = pltpu.roll(x, (-o) % n, axis=0)
    t = lax.broadcasted_iota(jnp.int32, x.shape, 0) + o
    return jnp.where((t >= 0) & (t < n), y, 0.0)


@functools.partial(jax.custom_vjp, nondiff_argnums=(1,))
def _shift(x, o):
    return _shift_impl(x, o)


def _shift_fwd(x, o):
    return _shift_impl(x, o), None


def _shift_bwd(o, _, ct):
    return (_shift_impl(ct, -o),)


_shift.defvjp(_shift_fwd, _shift_bwd)


def _rot_impl(x):
    w = x.shape[1]
    half = DA_DIM // 2
    lane = lax.broadcasted_iota(jnp.int32, x.shape, 1)
    first = (lane & (DA_DIM - 1)) < half
    return jnp.where(first, -pltpu.roll(x, w - half, axis=1), pltpu.roll(x, half, axis=1))


@jax.custom_vjp
def _rot(x):
    return _rot_impl(x)


def _rot_fwd(x):
    return _rot_impl(x), None


def _rot_bwd(_, ct):
    return (-_rot_impl(ct),)


_rot.defvjp(_rot_fwd, _rot_bwd)


def _sigmoid(x):
    return 1.0 / (1.0 + jnp.exp(-x))


def _silu(x):
    return x * _sigmoid(x)


def _softplus(x):
    return jnp.maximum(x, 0.0) + jnp.log(1.0 + jnp.exp(-jnp.abs(x)))


def _rms(x, gain):
    return x * lax.rsqrt(jnp.mean(x * x, axis=-1, keepdims=True) + RMS_EPS) * gain


class _In:
    def __init__(self, arr, block, imap, kind="t", acc=False, g=None, gdtype=None, split=False):
        self.arr, self.block, self.imap, self.kind, self.acc, self.g, self.gdtype = arr, block, imap, kind, acc, g, gdtype
        self.split = split


class _Out:
    def __init__(self, shape, dtype, block, imap, split=False):
        self.shape, self.dtype, self.block, self.imap, self.split = shape, dtype, block, imap, split


def _sub_index(split, s):
    if not split:
        return Ellipsis
    return (s,) if split is True else split(s)


def _grid_edges(grid):
    first = last = None
    for a, n in enumerate(grid):
        f, l = pl.program_id(a) == 0, pl.program_id(a) == n - 1
        first = f if first is None else jnp.logical_and(first, f)
        last = l if last is None else jnp.logical_and(last, l)
    return first, last


def _comm_plumbing(comm):
    if comm is None:
        return [], [], [], lambda refs: None, lambda refs: None
    kind, arrs = comm
    n = len(arrs)
    if kind == "gather":
        shapes = [jax.ShapeDtypeStruct((N_DEV,) + a.shape, a.dtype) for a in arrs]
        start, finish = _gather_start, _gather_finish
    else:
        shapes = [jax.ShapeDtypeStruct(a.shape, a.dtype) for a in arrs]
        start, finish = _exchange_start, _exchange_finish
    sems = [pltpu.SemaphoreType.DMA((n, N_DEV - 1)), pltpu.SemaphoreType.DMA((n, N_DEV - 1)),
            pltpu.SemaphoreType.DMA((n,))]
    return list(arrs), shapes, sems, start, finish


def _first_step(acc_from, ngrid):
    c = None
    for a in range(acc_from, ngrid):
        t = pl.program_id(a) == 0
        c = t if c is None else jnp.logical_and(c, t)
    return c


def _tile_fwd(name, f, grid, ins, outs, sub=1, comm=None):
    n_in, n_out = len(ins), len(outs)
    ngrid = len(grid)
    c_arrs, c_shapes, c_sems, c_start, c_finish = _comm_plumbing(comm)
    nc = len(c_arrs)

    def body(*refs):
        in_refs, c_in = refs[:n_in], refs[n_in:n_in + nc]
        out_refs, c_out = refs[n_in + nc:n_in + nc + n_out], refs[n_in + nc + n_out:n_in + 2 * nc + n_out]
        sems = refs[n_in + 2 * nc + n_out:]
        pids = tuple(pl.program_id(a) for a in range(ngrid))
        if nc:
            first, last = _grid_edges(grid)
            pl.when(first)(lambda: c_start(c_in, c_out, *sems))
        for s in range(sub):
            vals = [r[_sub_index(i.split, s)] for r, i in zip(in_refs, ins)]
            res = f(pids + ((s,) if sub > 1 else ()), *vals)
            for r, o, v in zip(out_refs, outs, res):
                r[_sub_index(o.split, s)] = v.astype(r.dtype)
        if nc:
            pl.when(last)(lambda: c_finish(c_in, c_out, *sems))

    res = pl.pallas_call(
        body, name=name, grid=grid,
        in_specs=[pl.BlockSpec(i.block, i.imap) for i in ins] + [_ANY] * nc,
        out_specs=[pl.BlockSpec(o.block, o.imap) for o in outs] + [_ANY] * nc,
        out_shape=[jax.ShapeDtypeStruct(o.shape, o.dtype) for o in outs] + c_shapes,
        scratch_shapes=c_sems, compiler_params=_params(),
    )(*[i.arr for i in ins], *c_arrs)
    return (res[:n_out], res[n_out:]) if nc else res


def _tile_bwd(name, f, grid, ins, outs, cts, acc_from=None, addends=None, sub=1, comm=None):
    n_in, n_out = len(ins), len(outs)
    ngrid = len(grid)
    diff = [k for k, i in enumerate(ins) if i.kind == "t"]
    addends = addends or {}
    add_keys = sorted(addends)
    n_add, n_g = len(add_keys), len(diff)
    c_arrs, c_shapes, c_sems, c_start, c_finish = _comm_plumbing(comm)
    nc = len(c_arrs)

    def body(*refs):
        pids = tuple(pl.program_id(a) for a in range(ngrid))
        in_refs = refs[:n_in]
        ct_refs = refs[n_in:n_in + n_out]
        add_refs = refs[n_in + n_out:n_in + n_out + n_add]
        o = n_in + n_out + n_add
        c_in, g_refs, c_out, sems = refs[o:o + nc], refs[o + nc:o + nc + n_g], refs[o + nc + n_g:o + 2 * nc + n_g], \
            refs[o + 2 * nc + n_g:]
        if nc:
            first_step, last_step = _grid_edges(grid)
            pl.when(first_step)(lambda: c_start(c_in, c_out, *sems))
        sums = {}
        for s in range(sub):
            vals = [r[_sub_index(i.split, s)] for r, i in zip(in_refs, ins)]
            dvals = [vals[k].astype(F32) for k in diff]

            def g(*d, vals=vals, s=s):
                full = list(vals)
                for k, dk in zip(diff, d):
                    full[k] = dk
                return tuple(f(pids + ((s,) if sub > 1 else ()), *full))

            res, vjp = jax.vjp(g, *dvals)
            cvals = [c[_sub_index(o_.split, s)].astype(r.dtype) for c, o_, r in zip(ct_refs, outs, res)]
            grads = vjp(tuple(cvals))
            for k, gr in zip(diff, grads):
                idx = _sub_index(ins[k].split, s)
                key = (k, str(idx))
                sums[key] = (idx, gr if key not in sums else sums[key][1] + gr)
        first = _first_step(acc_from, ngrid) if acc_from is not None else None
        for (k, _), (idx, gr) in sums.items():
            gref = g_refs[diff.index(k)]
            if idx is not Ellipsis:
                gref[idx] = gr.astype(gref.dtype)
                continue
            if k in addends:
                gr = gr + add_refs[add_keys.index(k)][...].astype(F32)
            if ins[k].acc and first is not None:
                @pl.when(first)
                def _(gr=gr, gref=gref):
                    gref[...] = gr.astype(gref.dtype)

                @pl.when(jnp.logical_not(first))
                def _(gr=gr, gref=gref):
                    gref[...] += gr.astype(gref.dtype)
            else:
                gref[...] = gr.astype(gref.dtype)
        if nc:
            pl.when(last_step)(lambda: c_finish(c_in, c_out, *sems))

    g_shapes, g_specs = [], []
    for k in diff:
        i = ins[k]
        if i.g is not None:
            shape, imap = i.g
        else:
            shape, imap = i.arr.shape, i.imap
        dt = i.gdtype or (F32 if i.acc else i.arr.dtype)
        g_shapes.append(jax.ShapeDtypeStruct(shape, dt))
        g_specs.append(pl.BlockSpec(i.block, imap))
    add_specs = [pl.BlockSpec(ins[k].block, ins[k].g[1] if ins[k].g is not None else ins[k].imap) for k in add_keys]
    res = pl.pallas_call(
        body, name=name, grid=grid,
        in_specs=[pl.BlockSpec(i.block, i.imap) for i in ins] + [pl.BlockSpec(o.block, o.imap) for o in outs] + add_specs
        + [_ANY] * nc,
        out_specs=g_specs + [_ANY] * nc, out_shape=g_shapes + c_shapes,
        scratch_shapes=c_sems, compiler_params=_params(),
    )(*[i.arr for i in ins], *cts, *[addends[k] for k in add_keys], *c_arrs)
    return (res[:n_g], res[n_g:]) if nc else res


def _mm(name, a, b, mode, out_dtype=F32, add=None, tm=512, tn=512, tk=512, m=None, n=None):
    if mode == "nn":
        (M, K), N = a.shape, b.shape[1]
    elif mode == "nt":
        (M, K), N = a.shape, b.shape[0]
    else:
        (K, M), N = a.shape, b.shape[1]
    M, N = m or M, n or N
    tm, tn, tk = _pick(M, tm, 128), _pick(N, tn, 128), _pick(K, tk, 128)
    nk = K // tk
    a_spec = pl.BlockSpec((tk, tm), lambda i, j, k: (k, i)) if mode == "tn" else pl.BlockSpec((tm, tk), lambda i, j, k: (i, k))
    b_spec = pl.BlockSpec((tn, tk), lambda i, j, k: (j, k)) if mode == "nt" else pl.BlockSpec((tk, tn), lambda i, j, k: (k, j))
    o_spec = pl.BlockSpec((tm, tn), lambda i, j, k: (i, j))

    def body(*refs):
        if add is None:
            a_ref, b_ref, o_ref, acc = refs
            add_ref = None
        else:
            a_ref, b_ref, add_ref, o_ref, acc = refs
        k = pl.program_id(2)

        @pl.when(k == 0)
        def _():
            acc[...] = jnp.zeros_like(acc)

        acc[...] += _dg(a_ref[...].astype(BF16), b_ref[...].astype(BF16), mode)

        @pl.when(k == nk - 1)
        def _():
            r = acc[...]
            if add_ref is not None:
                r = r + add_ref[...].astype(F32)
            o_ref[...] = r.astype(o_ref.dtype)

    ops = (a, b) if add is None else (a, b, add)
    specs = [a_spec, b_spec] + ([] if add is None else [o_spec])
    return pl.pallas_call(
        body, name=name, grid=(M // tm, N // tn, nk), in_specs=specs, out_specs=o_spec,
        out_shape=jax.ShapeDtypeStruct((M, N), out_dtype), scratch_shapes=[pltpu.VMEM((tm, tn), F32)],
        compiler_params=_params(dimension_semantics=("parallel", "parallel", "arbitrary")),
    )(*ops)


def _ffn_fwd(name, x, gain, wg, wu, wd):
    T, D = x.shape
    F = wg.shape[1]
    tm, tf = _pick(T, 512, 8), _pick(F, 256, 128)
    nf = F // tf

    def body(x_ref, g_ref, wg_ref, wu_ref, wd_ref, o_ref, h_ref, acc):
        j = pl.program_id(1)

        @pl.when(j == 0)
        def _():
            h_ref[...] = _rms(x_ref[...], g_ref[...]).astype(BF16)
            acc[...] = jnp.zeros_like(acc)

        h = h_ref[...]
        a = _dg(h, wg_ref[...], "nn")
        b = _dg(h, wu_ref[...], "nn")
        s = (_silu(a) * b).astype(BF16)
        acc[...] += _dg(s, wd_ref[...], "nn")

        @pl.when(j == nf - 1)
        def _():
            o_ref[...] = x_ref[...] + 0.5 * acc[...]

    return pl.pallas_call(
        body, name=name, grid=(T // tm, nf),
        in_specs=[pl.BlockSpec((tm, D), lambda i, j: (i, 0)), pl.BlockSpec((1, D), lambda i, j: (0, 0)),
                  pl.BlockSpec((D, tf), lambda i, j: (0, j)), pl.BlockSpec((D, tf), lambda i, j: (0, j)),
                  pl.BlockSpec((tf, D), lambda i, j: (j, 0))],
        out_specs=pl.BlockSpec((tm, D), lambda i, j: (i, 0)),
        out_shape=jax.ShapeDtypeStruct((T, D), F32),
        scratch_shapes=[pltpu.VMEM((tm, D), BF16), pltpu.VMEM((tm, D), F32)],
        compiler_params=_params(dimension_semantics=("parallel", "arbitrary")),
    )(x, gain, wg, wu, wd)


def _ffn_bwd(name, x, gain, wg, wu, wd, dy, f_true):
    T, D = x.shape
    F = wg.shape[1]
    tm, tf = _pick(T, 512, 8), _pick(F, FFN_BWD_TF, 128)
    nf = F // tf

    def body(x_ref, g_ref, wg_ref, wu_ref, wd_ref, dy_ref, dx_ref, dg_ref, da_ref, db_ref, s_ref, h_ref, dyh_ref, dh):
        i, j = pl.program_id(0), pl.program_id(1)

        @pl.when(j == 0)
        def _():
            h_ref[...] = _rms(x_ref[...], g_ref[...]).astype(BF16)
            dyh_ref[...] = (0.5 * dy_ref[...]).astype(BF16)
            dh[...] = jnp.zeros_like(dh)

        h = h_ref[...]
        a = _dg(h, wg_ref[...], "nn")
        b = _dg(h, wu_ref[...], "nn")
        ds = _dg(dyh_ref[...], wd_ref[...], "nt")
        sig = _sigmoid(a)
        silu = a * sig
        da = (ds * b * (sig * (1.0 + a * (1.0 - sig)))).astype(BF16)
        db = (ds * silu).astype(BF16)
        da_ref[...] = da
        db_ref[...] = db
        s_ref[...] = (silu * b).astype(BF16)
        dh[...] += _dg(da, wg_ref[...], "nt") + _dg(db, wu_ref[...], "nt")

        @pl.when(j == nf - 1)
        def _():
            _, vjp = jax.vjp(_rms, x_ref[...], g_ref[...])
            dxn, dgn = vjp(dh[...])
            dx_ref[...] = dy_ref[...] + dxn

            @pl.when(i == 0)
            def _():
                dg_ref[...] = dgn

            @pl.when(i != 0)
            def _():
                dg_ref[...] += dgn

    row = lambda i, j: (i, 0)
    col = lambda i, j: (i, j)
    dx, dgain, da, db, s, h, dyh = pl.pallas_call(
        body, name=name, grid=(T // tm, nf),
        in_specs=[pl.BlockSpec((tm, D), row), pl.BlockSpec((1, D), lambda i, j: (0, 0)),
                  pl.BlockSpec((D, tf), lambda i, j: (0, j)), pl.BlockSpec((D, tf), lambda i, j: (0, j)),
                  pl.BlockSpec((tf, D), lambda i, j: (j, 0)), pl.BlockSpec((tm, D), row)],
        out_specs=[pl.BlockSpec((tm, D), row), pl.BlockSpec((1, D), lambda i, j: (0, 0)),
                   pl.BlockSpec((tm, tf), col), pl.BlockSpec((tm, tf), col), pl.BlockSpec((tm, tf), col),
                   pl.BlockSpec((tm, D), row), pl.BlockSpec((tm, D), row)],
        out_shape=[jax.ShapeDtypeStruct((T, D), F32), jax.ShapeDtypeStruct((1, D), F32),
                   jax.ShapeDtypeStruct((T, F), BF16), jax.ShapeDtypeStruct((T, F), BF16),
                   jax.ShapeDtypeStruct((T, F), BF16), jax.ShapeDtypeStruct((T, D), BF16),
                   jax.ShapeDtypeStruct((T, D), BF16)],
        scratch_shapes=[pltpu.VMEM((tm, D), F32)],
        compiler_params=_params(),
    )(x, gain, wg, wu, wd, dy)
    dwg = _mm(name + "_dwg", h, da, "tn", tm=1024, tn=1408, tk=1024, n=f_true)
    dwu = _mm(name + "_dwu", h, db, "tn", tm=1024, tn=1408, tk=1024, n=f_true)
    dwd = _mm(name + "_dwd", s, dyh, "tn", tm=1408, tn=1024, tk=1024, m=f_true)
    return dx, dgain, dwg, dwu, dwd


def _norm_f(pids, x, gain):
    return (_rms(x, gain),)


def _dn_conv_f(pids, x, w):
    j = pids[0]
    tap = lax.broadcasted_iota(jnp.int32, w.shape, 0)
    y = jnp.zeros_like(x)
    for t in range(DN_CONV):
        wt = jnp.sum(jnp.where(tap == t, w, 0.0), axis=0, keepdims=True)
        y = y + _shift(x, t - DN_CONV // 2) * wt
    y = _silu(y)
    n = y * lax.rsqrt(jnp.sum(y * y, axis=-1, keepdims=True) + L2_EPS)
    is_q = (j < DN_HEADS).astype(F32)
    is_qk = (j < 2 * DN_HEADS).astype(F32)
    scale = is_q * (DN_DIM ** -0.5) + (1.0 - is_q)
    return ((is_qk * n + (1.0 - is_qk) * y) * scale,)


def _dn_gate_f(pids, braw, araw, a_log, dt_bias):
    beta = _sigmoid(braw)
    g = -jnp.exp(a_log) * _softplus(araw + dt_bias)
    return beta, g


def _dn_prep_f(pids, q, k, v, brow, grow):
    cs = DN_SUPER
    sign = 1 - 2 * pids[2]
    ii = lax.broadcasted_iota(jnp.int32, (cs, cs), 0)
    jj = lax.broadcasted_iota(jnp.int32, (cs, cs), 1)
    shift = int(math.log2(DN_CHUNK))
    same = (ii >> shift) == (jj >> shift)
    d = (ii - jj) * sign
    incl = same & (d >= 0)
    strict = same & (d > 0)
    eye = ii == jj
    g_col = jnp.sum(jnp.where(eye, jnp.broadcast_to(grow, (cs, cs)), 0.0), axis=1, keepdims=True)
    b_col = jnp.sum(jnp.where(eye, jnp.broadcast_to(brow, (cs, cs)), 0.0), axis=1, keepdims=True)
    g128 = jnp.broadcast_to(g_col, (cs, DN_DIM))
    G = _mdot(incl.astype(BF16), g128)
    Gt = _mdot(same.astype(BF16), g128)
    Gc = jnp.concatenate([G, G], axis=1)
    Grow = jnp.sum(jnp.where(eye, Gc, 0.0), axis=0, keepdims=True)
    decay = jnp.exp(jnp.where(incl, Gc - Grow, MASK_VALUE))
    eG = jnp.exp(G)
    kb = k * b_col
    A = jnp.where(strict, _dot(kb, k, "nt", 1) * decay, 0.0)
    X = _unit_solve(A, jnp.concatenate([v * b_col, kb * eG], axis=1))
    qk = jnp.where(incl, _dot(q, k, "nt", 1) * decay, 0.0)
    return X, qk, q * eG, k * jnp.exp(Gt - G), jnp.exp(Gt)


def _dn_out_f(pids, of, ob, z, gain):
    return (_rms(of + ob, gain) * _silu(z),)


def _pool_f(pids, u, w, scale):
    g = pids[0]
    half = jnp.left_shift(1, g)
    n = u.shape[0]
    pos = lax.broadcasted_iota(jnp.int32, (n, 1), 0)
    tot = jnp.zeros_like(u)
    cnt = jnp.zeros((n, 1), F32)
    for o in range(-POOL_MAX_HALF, POOL_MAX_HALF):
        use = ((o >= -half) & (o < half)).astype(F32)
        tot = tot + use * _shift(u, o)
        cnt = cnt + use * ((pos + o >= 0) & (pos + o < n)).astype(F32)
    pooled = tot / cnt - u
    return (_dot(pooled, w, "nn", 1) * scale,)


def _rope_f(pids, *args):
    cos, sin = args[-2:]
    qs, ks, vs = args[:DA_GROUPS], args[DA_GROUPS:2 * DA_GROUPS], args[2 * DA_GROUPS:3 * DA_GROUPS]
    qr = [(q * cos + _rot(q) * sin) * (DA_DIM ** -0.5) for q in qs]
    kr = [k * cos + _rot(k) * sin for k in ks]
    return (*qr, *kr, *vs)


def _attn_head(q, k, v, qpos0, kpos0):
    s = _dot(q, k, "nt", 1)
    qi = qpos0 + lax.broadcasted_iota(jnp.int32, s.shape, 0)
    kj = kpos0 + lax.broadcasted_iota(jnp.int32, s.shape, 1)
    s = jnp.where(jnp.abs(kj - qi) <= DA_RADIUS, s, MASK_VALUE)
    m = lax.stop_gradient(jnp.max(s, axis=1, keepdims=True))
    p = jnp.exp(s - m)
    l = jnp.sum(p, axis=1, keepdims=True)
    o = _dot(p, v, "nn", 1) / l
    return o, jnp.broadcast_to(m + jnp.log(l), o.shape)


def _merge_f(pids, o0, o1, o2, l0, l1, l2):
    m = jnp.maximum(jnp.maximum(l0, l1), l2)
    e0, e1, e2 = jnp.exp(l0 - m), jnp.exp(l1 - m), jnp.exp(l2 - m)
    return ((e0 * o0 + e1 * o1 + e2 * o2) / (e0 + e1 + e2),)


def _gate_f(pids, g0, g1, g2, ya, yb, yc, b0, b1, b2):
    return (_sigmoid(g0 + b0) * ya + _sigmoid(g1 + b1) * yb + _sigmoid(g2 + b2) * yc,)


def _attn_window(i, L, tq, W):
    k0 = jnp.clip(i * tq - DA_RADIUS, 0, L - W)
    return pl.multiple_of(k0, DA_RADIUS)


def _strided_view(t, B, dil):
    T, HD = t.shape
    return t.reshape(B, T // B // dil, dil * HD)


def _attn_fwd(name, q, k, v, B, dil):
    T, HD = q.shape
    NS, L = B * dil, T // B // dil
    tq = min(DA_TQ, L)
    W = min(L, tq + 2 * DA_RADIUS)

    def body(q_ref, k_ref, v_ref, o_ref, l_ref):
        i = pl.program_id(1)
        k0 = _attn_window(i, L, tq, W)
        for h in range(DA_HEADS):
            hs = slice(h * DA_DIM, (h + 1) * DA_DIM)
            o, lse = _attn_head(q_ref[:, hs], k_ref[pl.ds(k0, W), hs], v_ref[pl.ds(k0, W), hs], i * tq, k0)
            o_ref[:, hs] = o
            l_ref[:, hs] = lse

    qs = pl.BlockSpec((None, tq, HD), lambda s, i: (s // dil, i, s % dil))
    ks = pl.BlockSpec((None, L, HD), lambda s, i: (s // dil, 0, s % dil))
    o, lse = pl.pallas_call(
        body, name=name, grid=(NS, L // tq), in_specs=[qs, ks, ks], out_specs=[qs, qs],
        out_shape=[jax.ShapeDtypeStruct((B, L, dil * HD), F32)] * 2, compiler_params=_params(),
    )(*[_strided_view(t, B, dil) for t in (q, k, v)])
    return o.reshape(T, HD), lse.reshape(T, HD)


def _attn_bwd(name, q, k, v, do, dl, B, dil):
    T, HD = q.shape
    NS, L = B * dil, T // B // dil
    tq = min(DA_TQ, L)
    W = min(L, tq + 2 * DA_RADIUS)

    def body(q_ref, k_ref, v_ref, do_ref, dl_ref, dq_ref, dk_ref, dv_ref):
        i = pl.program_id(1)
        k0 = _attn_window(i, L, tq, W)

        @pl.when(i == 0)
        def _():
            dk_ref[...] = jnp.zeros_like(dk_ref)
            dv_ref[...] = jnp.zeros_like(dv_ref)

        for h in range(DA_HEADS):
            hs = slice(h * DA_DIM, (h + 1) * DA_DIM)
            f = functools.partial(_attn_head, qpos0=i * tq, kpos0=k0)
            _, vjp = jax.vjp(f, q_ref[:, hs].astype(F32), k_ref[pl.ds(k0, W), hs].astype(F32),
                             v_ref[pl.ds(k0, W), hs].astype(F32))
            dq, dk, dv = vjp((do_ref[:, hs], dl_ref[:, hs]))
            dq_ref[:, hs] = dq
            dk_ref[pl.ds(k0, W), hs] += dk
            dv_ref[pl.ds(k0, W), hs] += dv

    qs = pl.BlockSpec((None, tq, HD), lambda s, i: (s // dil, i, s % dil))
    ks = pl.BlockSpec((None, L, HD), lambda s, i: (s // dil, 0, s % dil))
    res = pl.pallas_call(
        body, name=name, grid=(NS, L // tq), in_specs=[qs, ks, ks, qs, qs], out_specs=[qs, ks, ks],
        out_shape=[jax.ShapeDtypeStruct((B, L, dil * HD), F32)] * 3, compiler_params=_params(),
    )(*[_strided_view(t, B, dil) for t in (q, k, v, do, dl)])
    return tuple(t.reshape(T, HD) for t in res)


def _scan_chunk(t, rev, N):
    c = jnp.where(rev, N - 1 - t, t)
    per = DN_SUPER // DN_CHUNK
    return c, pl.multiple_of(c * DN_CHUNK, DN_CHUNK), pl.multiple_of((c % per) * DN_CHUNK, DN_CHUNK), \
        pl.multiple_of((c // per) * DN_SUPER, DN_SUPER)


def _dn_scan_fwd(name, uw, qk, qd, kd, gl, B):
    R, T, _ = uw.shape
    S = T // B
    N = S // DN_CHUNK
    C, DK = DN_CHUNK, DN_DIM

    PAIR = 2

    def body(uw_ref, qk_ref, qd_ref, kd_ref, gl_ref, o_ref, st_ref, vn_ref):
        rev = pl.program_id(1) * PAIR >= DN_HEADS
        vn_ref[...] = jnp.zeros_like(vn_ref)

        def step(t, states):
            c, r0, w0, s0 = _scan_chunk(t, rev, N)
            rows = pl.ds(r0, C)
            new = []
            for p, state in enumerate(states):
                st_ref[p, c] = state
                vnew = uw_ref[p, rows, 0:DK] - _dotp(uw_ref[p, rows, DK:2 * DK], state, "nn", 1)
                vn_ref[p, pl.ds(w0, C), :] = vnew
                o_ref[p, rows, :] = (_dotp(qd_ref[p, rows, :], state, "nn", 1)
                                     + _dotp(qk_ref[p, rows, :], vn_ref[p], "nn", 1))
                new.append(state * gl_ref[p, pl.ds(r0, 1), :] + _dotp(kd_ref[p, rows, :], vnew, "tn", 1))
            return tuple(new)

        lax.fori_loop(0, N, step, tuple(jnp.zeros((DK, DK), F32) for _ in range(PAIR)))

    def seq(w):
        return pl.BlockSpec((PAIR, S, w), lambda b, r: (r, b, 0))

    return pl.pallas_call(
        body, name=name, grid=(B, R // PAIR),
        in_specs=[seq(2 * DK), seq(DN_SUPER), seq(DK), seq(DK), seq(DK)],
        out_specs=[seq(DK), pl.BlockSpec((None, PAIR, N, DK, DK), lambda b, r: (b, r, 0, 0, 0))],
        out_shape=[jax.ShapeDtypeStruct((R, T, DK), F32), jax.ShapeDtypeStruct((B, R, N, DK, DK), F32)],
        scratch_shapes=[pltpu.VMEM((PAIR, DN_SUPER, DK), F32)], compiler_params=_params(),
    )(uw, qk, qd, kd, gl)


def _dn_scan_bwd(name, uw, qk, qd, kd, gl, st, do, B):
    R, T, _ = uw.shape
    S = T // B
    N = S // DN_CHUNK
    C, DK = DN_CHUNK, DN_DIM

    def body(uw_ref, qk_ref, qd_ref, kd_ref, gl_ref, st_ref, do_ref, duw_ref, dqk_ref, dqd_ref, dkd_ref, dgl_ref,
             vn_ref, tmp_ref):
        rev = pl.program_id(1) >= DN_HEADS
        vn_ref[...] = jnp.zeros_like(vn_ref)
        dgl_ref[...] = jnp.zeros_like(dgl_ref)

        def step(t, dstate):
            c, r0, w0, s0 = _scan_chunk(N - 1 - t, rev, N)
            rows = pl.ds(r0, C)
            state = st_ref[c]
            w = uw_ref[rows, DK:2 * DK]
            vnew = uw_ref[rows, 0:DK] - _dotp(w, state, "nn", 1)
            vn_ref[pl.ds(w0, C), :] = vnew
            do_c = do_ref[rows, :]
            tmp_ref[...] = _dotp(qk_ref[rows, :], do_c, "tn", 1)
            dvn = tmp_ref[pl.ds(w0, C), :] + _dotp(kd_ref[rows, :], dstate, "nn", 1)
            dqk_ref[rows, :] = _dotp(do_c, vn_ref[...], "nt", 1)
            dqd_ref[rows, :] = _dotp(do_c, state, "nt", 1)
            dkd_ref[rows, :] = _dotp(vnew, dstate, "nt", 1)
            dgl_ref[pl.ds(r0, 1), :] = jnp.sum(state * dstate, axis=0, keepdims=True)
            duw_ref[rows, 0:DK] = dvn
            duw_ref[rows, DK:2 * DK] = -_dotp(dvn, state, "nt", 1)
            return (_dotp(qd_ref[rows, :], do_c, "tn", 1) + dstate * gl_ref[pl.ds(r0, 1), :]
                    - _dotp(w, dvn, "tn", 1))

        lax.fori_loop(0, N, step, jnp.zeros((DK, DK), F32))

    def seq(w):
        return pl.BlockSpec((None, S, w), lambda b, r: (r, b, 0))

    return pl.pallas_call(
        body, name=name, grid=(B, R),
        in_specs=[seq(2 * DK), seq(DN_SUPER), seq(DK), seq(DK), seq(DK),
                  pl.BlockSpec((None, None, N, DK, DK), lambda b, r: (b, r, 0, 0, 0)),
                  pl.BlockSpec((None, S, DK), lambda b, r: (r % DN_HEADS, b, 0))],
        out_specs=[seq(2 * DK), seq(DN_SUPER), seq(DK), seq(DK), seq(DK)],
        out_shape=[jax.ShapeDtypeStruct((R, T, 2 * DK), F32), jax.ShapeDtypeStruct((R, T, DN_SUPER), F32),
                   jax.ShapeDtypeStruct((R, T, DK), F32), jax.ShapeDtypeStruct((R, T, DK), F32),
                   jax.ShapeDtypeStruct((R, T, DK), F32)],
        scratch_shapes=[pltpu.VMEM((DN_SUPER, DK), F32), pltpu.VMEM((DN_SUPER, DK), F32)],
        compiler_params=_params(),
    )(uw, qk, qd, kd, gl, st, do)


def _loss_fwd_bwd(name, x, gain, target):
    T, D = x.shape
    tm = _pick(T, 512, 8)

    def body(x_ref, g_ref, t_ref, loss_ref, dx_ref, dg_ref):
        i = pl.program_id(0)

        def f(xv, gv):
            e = _rms(xv, gv) - t_ref[...]
            return 0.5 * jnp.sum(jnp.mean(e * e, axis=-1, keepdims=True))

        val, (dx, dg) = jax.value_and_grad(f, argnums=(0, 1))(x_ref[...], g_ref[...])
        dx_ref[...] = dx
        part = jnp.full(loss_ref.shape, val, F32)

        @pl.when(i == 0)
        def _():
            dg_ref[...] = dg
            loss_ref[...] = part

        @pl.when(i != 0)
        def _():
            dg_ref[...] += dg
            loss_ref[...] += part

    return pl.pallas_call(
        body, name=name, grid=(T // tm,),
        in_specs=[pl.BlockSpec((tm, D), lambda i: (i, 0)), pl.BlockSpec((1, D), lambda i: (0, 0)),
                  pl.BlockSpec((tm, D), lambda i: (i, 0))],
        out_specs=[pl.BlockSpec((8, 128), lambda i: (0, 0)), pl.BlockSpec((tm, D), lambda i: (i, 0)),
                   pl.BlockSpec((1, D), lambda i: (0, 0))],
        out_shape=[jax.ShapeDtypeStruct((8, 128), F32), jax.ShapeDtypeStruct((T, D), F32),
                   jax.ShapeDtypeStruct((1, D), F32)],
        compiler_params=_params(),
    )(x, gain, target)


class _Cols:
    def __init__(self, D):
        assert D % 256 == 0
        self.gate = 0
        self.da = 3 * D
        self.qkv = self.da + 3 * DA_WIDTH
        self.z = self.qkv + 3 * DN_WIDTH
        self.pool = self.z + DN_WIDTH
        self.ba = self.pool + POOL_WIDTH
        self.total = self.ba + BA_PAD


def _rope_tables(S):
    half = DA_DIM // 2
    inv_freq = ROPE_THETA ** (-jnp.arange(half, dtype=F32) / half)
    ang = jnp.arange(S, dtype=F32)[:, None] * inv_freq[None, :]
    reps = DA_OUT // DA_DIM
    cos = jnp.tile(jnp.concatenate([jnp.cos(ang), jnp.cos(ang)], axis=1), (1, reps))
    sin = jnp.tile(jnp.concatenate([jnp.sin(ang), jnp.sin(ang)], axis=1), (1, reps))
    return cos, sin


def _to_strided(t, B, dil):
    T, w = t.shape
    L = T // B // dil
    return t.reshape(B, L, dil, w).transpose(0, 2, 1, 3).reshape(B * dil, L, w)


def _from_strided(t, B, dil):
    NS, L, w = t.shape
    return t.reshape(B, dil, L, w).transpose(0, 2, 1, 3).reshape(B * dil * L, w)


def _mixer(l, x1, w, B, host_gather=None):
    T, D = x1.shape
    S = T // B
    c = _Cols(D)
    tm = _pick(S, 512, 8)
    nmS = S // tm
    n = f"l{l}_"

    norm_ins = [_In(x1, (tm, D), lambda i: (i, 0)), _In(w["mix_norm"], (1, D), lambda i: (0, 0), acc=True)]
    norm_outs = [_Out((T, D), BF16, (tm, D), lambda i: (i, 0))]
    (h,) = _tile_fwd(n + "norm", _norm_f, (T // tm,), norm_ins, norm_outs)
    P = _mm(n + "proj", h, w["w_cat"], "nn", tn=896, tk=1024)
    baT = P[:, c.ba:c.ba + 16].T

    cb = c.qkv // DN_DIM
    conv_ins = [_In(P, (S, DN_DIM), lambda j, b: (b, cb + j), g=((T, 3 * DN_WIDTH), lambda j, b: (b, j)), gdtype=BF16),
                _In(w["dn_conv"], (DN_CONV, DN_DIM), lambda j, b: (0, j), acc=True)]
    conv_outs = [_Out((T, 3 * DN_WIDTH), F32, (S, DN_DIM), lambda j, b: (b, j))]
    conv_grid = (3 * DN_HEADS, B)
    (qkvc,) = _tile_fwd(n + "dnconv", _dn_conv_f, conv_grid, conv_ins, conv_outs)

    tg = _pick(T, 2048, 128)
    gate_ins = [_In(baT, (8, tg), lambda i: (0, i)), _In(baT, (8, tg), lambda i: (1, i)),
                _In(w["dn_a_log"], (8, 1), lambda i: (0, 0), acc=True),
                _In(w["dn_dt_bias"], (8, 1), lambda i: (0, 0), acc=True)]
    gate_ins[0].g = ((8, T), lambda i: (0, i))
    gate_ins[1].g = ((8, T), lambda i: (0, i))
    gate_outs = [_Out((8, T), F32, (8, tg), lambda i: (0, i))] * 2
    beta, gdec = _tile_fwd(n + "dngate", _dn_gate_f, (T // tg,), gate_ins, gate_outs)

    NSC = T // DN_SUPER
    beta4 = beta.reshape(2, DN_HEADS, NSC, 1, DN_SUPER)
    gdec4 = gdec.reshape(2, DN_HEADS, NSC, 1, DN_SUPER)
    R = 2 * DN_HEADS

    def qkv_in(off):
        return _In(qkvc, (DN_SUPER, DN_DIM), lambda hh, m: (m, off + hh), acc=True,
                   g=((T, DN_WIDTH), lambda hh, m: (m, hh)))

    def row_in(a):
        return _In(a, (2, None, None, 1, DN_SUPER), lambda hh, m: (0, hh, m, 0, 0), split=True)

    def chain_out(wd):
        return _Out((2, DN_HEADS, T, wd), F32, (2, None, DN_SUPER, wd), lambda hh, m: (0, hh, m, 0), split=True)

    prep_ins = [qkv_in(0), qkv_in(DN_HEADS), qkv_in(2 * DN_HEADS), row_in(beta4), row_in(gdec4)]
    prep_outs = [chain_out(2 * DN_DIM), chain_out(DN_SUPER), chain_out(DN_DIM), chain_out(DN_DIM), chain_out(DN_DIM)]
    prep_grid = (DN_HEADS, NSC)
    prep_res = _tile_fwd(n + "dnprep", _dn_prep_f, prep_grid, prep_ins, prep_outs, sub=2, comm=host_gather)
    gathered_next = None
    if host_gather is not None:
        prep_res, gathered_next = prep_res
    uw, qk, qd, kd, gl = (t.reshape((R,) + t.shape[2:]) for t in prep_res)
    o_dn, states = _dn_scan_fwd(n + "dnscan", uw, qk, qd, kd, gl, B)

    zb = c.z // DN_DIM
    out_ins = [_In(o_dn, (None, S, DN_DIM), lambda b, hh: (hh, b, 0)),
               _In(o_dn, (None, S, DN_DIM), lambda b, hh: (DN_HEADS + hh, b, 0)),
               _In(P, (S, DN_DIM), lambda b, hh: (b, zb + hh), g=((T, DN_WIDTH), lambda b, hh: (b, hh)), gdtype=BF16),
               _In(w["dn_out_norm"], (1, DN_DIM), lambda b, hh: (0, 0), acc=True)]
    out_ins[0].g = ((DN_HEADS, T, DN_DIM), lambda b, hh: (hh, b, 0))
    out_ins[1].g = ((DN_HEADS, T, DN_DIM), lambda b, hh: (hh, b, 0))
    out_outs = [_Out((T, DN_WIDTH), BF16, (S, DN_DIM), lambda b, hh: (b, hh))]
    (ya_in,) = _tile_fwd(n + "dnout", _dn_out_f, (B, DN_HEADS), out_ins, out_outs)

    pb = c.pool // POOL_DIM
    pool_ins = [_In(P, (S, POOL_DIM), lambda gi, b: (b, pb + gi), g=((T, POOL_WIDTH), lambda gi, b: (b, gi)), gdtype=BF16),
                _In(w["pool_w"], (None, POOL_DIM, POOL_DIM), lambda gi, b: (gi, 0, 0), acc=True),
                _In(w["pool_scale"], (None, 1, POOL_DIM), lambda gi, b: (gi, 0, 0), acc=True)]
    pool_outs = [_Out((T, POOL_WIDTH), BF16, (S, POOL_DIM), lambda gi, b: (b, gi))]
    (yb_in,) = _tile_fwd(n + "pool", _pool_f, (POOL_GROUPS, B), pool_ins, pool_outs)

    cos, sin = _rope_tables(S)
    db = c.da // DA_OUT

    def da_in(k):
        return _In(P, (tm, DA_OUT), lambda i: (i, db + k), g=((T, DA_OUT), lambda i: (i, 0)), gdtype=BF16)

    rope_ins = [da_in(k) for k in range(3 * DA_GROUPS)]
    rope_ins += [_In(cos, (tm, DA_OUT), lambda i: (i % nmS, 0), kind="c"),
                 _In(sin, (tm, DA_OUT), lambda i: (i % nmS, 0), kind="c")]
    rope_outs = [_Out((T, DA_OUT), BF16, (tm, DA_OUT), lambda i: (i, 0))] * (3 * DA_GROUPS)
    roped = _tile_fwd(n + "rope", _rope_f, (T // tm,), rope_ins, rope_outs)
    strided = []
    o_g, l_g = [], []
    for gi, dil in enumerate(DA_DILATIONS):
        qs, ks, vs = roped[gi], roped[DA_GROUPS + gi], roped[2 * DA_GROUPS + gi]
        strided.append((qs, ks, vs))
        o, lse = _attn_fwd(n + f"attn{gi}", qs, ks, vs, B, dil)
        o_g.append(o)
        l_g.append(lse)
    mrg_ins = [_In(a, (tm, DA_OUT), lambda i: (i, 0)) for a in o_g + l_g]
    mrg_outs = [_Out((T, DA_OUT), BF16, (tm, DA_OUT), lambda i: (i, 0))]
    (yc_in,) = _tile_fwd(n + "merge", _merge_f, (T // tm,), mrg_ins, mrg_outs)

    ya = _mm(n + "pa", ya_in, w["w_proj_a"], "nn")
    yb = _mm(n + "pb", yb_in, w["w_proj_b"], "nn")
    yc = _mm(n + "pc", yc_in, w["w_proj_c"], "nn")

    def gcol(k):
        return _In(P, (tm, D), lambda i: (i, k), g=((T, D), lambda i: (i, 0)), gdtype=BF16)

    def yin(a):
        return _In(a, (tm, D), lambda i: (i, 0), gdtype=BF16)

    def bin_(k):
        return _In(w["b_gate"][k:k + 1], (1, D), lambda i: (0, 0), acc=True)

    gm_ins = [gcol(0), gcol(1), gcol(2), yin(ya), yin(yb), yin(yc), bin_(0), bin_(1), bin_(2)]
    gm_outs = [_Out((T, D), BF16, (tm, D), lambda i: (i, 0))]
    (merged,) = _tile_fwd(n + "gates", _gate_f, (T // tm,), gm_ins, gm_outs)
    x2 = _mm(n + "out", merged, w["w_out"], "nn", add=x1)

    def backward(dx2, host_exchange=None):
        return _mixer_bwd(dx2, host_exchange, **{k: v for k, v in locals_.items() if k in _MIXER_BWD_NEEDS})

    locals_ = dict(locals())
    return x2, backward, gathered_next


_MIXER_BWD_NEEDS = ("n", "B", "T", "D", "tm", "w", "h", "merged", "gm_ins", "gm_outs", "ya_in", "yb_in", "yc_in",
                    "mrg_ins", "mrg_outs", "strided", "rope_ins", "rope_outs", "pool_ins", "pool_outs", "out_ins",
                    "out_outs", "uw", "qk", "qd", "kd", "gl", "states", "prep_grid", "prep_ins", "prep_outs", "tg",
                    "gate_ins", "gate_outs", "conv_grid", "conv_ins", "conv_outs", "norm_ins", "norm_outs")


def _mixer_bwd(dx2, host_exchange, *, n, B, T, D, tm, w, h, merged, gm_ins, gm_outs, ya_in, yb_in, yc_in, mrg_ins, mrg_outs, strided,
               rope_ins, rope_outs, pool_ins, pool_outs, out_ins, out_outs, uw, qk, qd, kd, gl, states, prep_grid,
               prep_ins, prep_outs, tg, gate_ins, gate_outs, conv_grid, conv_ins, conv_outs, norm_ins, norm_outs):
    g = {}
    dmerged = _mm(n + "d_merged", dx2, w["w_out"], "nt")
    g["w_out"] = _mm(n + "d_wout", merged, dx2, "tn", tm=1024, tn=1024, tk=1024)
    dg0, dg1, dg2, dya, dyb, dyc, db0, db1, db2 = _tile_bwd(
        n + "gates_b", _gate_f, (T // tm,), gm_ins, gm_outs, [dmerged], acc_from=0)
    g["b_gate"] = jnp.concatenate([db0, db1, db2], axis=0)
    dya_in = _mm(n + "d_pa", dya, w["w_proj_a"], "nt")
    dyb_in = _mm(n + "d_pb", dyb, w["w_proj_b"], "nt")
    dyc_in = _mm(n + "d_pc", dyc, w["w_proj_c"], "nt")
    g["w_proj_a"] = _mm(n + "d_wpa", ya_in, dya, "tn", tn=1024, tk=2048)
    g["w_proj_b"] = _mm(n + "d_wpb", yb_in, dyb, "tn", tn=1024, tk=2048)
    g["w_proj_c"] = _mm(n + "d_wpc", yc_in, dyc, "tn", tn=1024, tk=2048)

    dmrg = _tile_bwd(n + "merge_b", _merge_f, (T // tm,), mrg_ins, mrg_outs, [dyc_in])
    dq_parts, dk_parts, dv_parts = [], [], []
    for gi, dil in enumerate(DA_DILATIONS):
        qs, ks, vs = strided[gi]
        dq, dk, dv = _attn_bwd(n + f"attn{gi}_b", qs, ks, vs, dmrg[gi], dmrg[DA_GROUPS + gi], B, dil)
        dq_parts.append(dq)
        dk_parts.append(dk)
        dv_parts.append(dv)
    dP_da = _tile_bwd(n + "rope_b", _rope_f, (T // tm,), rope_ins, rope_outs, dq_parts + dk_parts + dv_parts)

    dPpool, g["pool_w"], g["pool_scale"] = _tile_bwd(
        n + "pool_b", _pool_f, (POOL_GROUPS, B), pool_ins, pool_outs, [dyb_in], acc_from=1)

    dof, dob, dPz, g["dn_out_norm"] = _tile_bwd(
        n + "dnout_b", _dn_out_f, (B, DN_HEADS), out_ins, out_outs, [dya_in], acc_from=0)
    del dob
    duw, dqk, dqd, dkd, dgl = _dn_scan_bwd(n + "dnscan_b", uw, qk, qd, kd, gl, states, dof, B)
    prep_cts = [t.reshape((2, DN_HEADS) + t.shape[1:]) for t in (duw, dqk, dqd, dkd, dgl)]
    prep_res = _tile_bwd(n + "dnprep_b", _dn_prep_f, prep_grid, prep_ins, prep_outs, prep_cts, sub=2,
                         comm=host_exchange)
    exchanged = None
    if host_exchange is not None:
        prep_res, exchanged = prep_res
    dq_, dk_, dv_, dbeta4, dgdec4 = prep_res
    dqkvc = jnp.concatenate([dq_, dk_, dv_], axis=1)
    dbraw, daraw, g["dn_a_log"], g["dn_dt_bias"] = _tile_bwd(
        n + "dngate_b", _dn_gate_f, (T // tg,), gate_ins, gate_outs,
        [dbeta4.reshape(8, T), dgdec4.reshape(8, T)], acc_from=0)
    dPqkv, g["dn_conv"] = _tile_bwd(n + "dnconv_b", _dn_conv_f, conv_grid, conv_ins, conv_outs, [dqkvc], acc_from=1)
    dba = jnp.concatenate([dbraw, daraw], axis=0).T.astype(BF16)
    dba = jnp.pad(dba, ((0, 0), (0, BA_PAD - 16)))
    dP = jnp.concatenate([dg0, dg1, dg2, *dP_da, dPqkv, dPz, dPpool, dba], axis=1)
    dh = _mm(n + "d_h", dP, w["w_cat"], "nt", tn=1024, tk=2688)
    g["w_cat"] = _mm(n + "d_wcat", h, dP, "tn", tm=1024, tn=896, tk=1024)
    dx1, g["mix_norm"] = _tile_bwd(n + "norm_b", _norm_f, (T // tm,), norm_ins, norm_outs, [dh], acc_from=0,
                                   addends={0: dx2})
    return dx1, g, exchanged


def _layer_weights(full, l, D):
    c = _Cols(D)
    w_in = full["w_in"][l]
    o_z, o_ba, o_pool, o_da = 3 * DN_WIDTH, 4 * DN_WIDTH, 4 * DN_WIDTH + 16, 4 * DN_WIDTH + 16 + POOL_WIDTH
    w_cat = jnp.concatenate(
        [full["w_gate"][l], w_in[:, o_da:], w_in[:, :o_z], w_in[:, o_z:o_ba], w_in[:, o_pool:o_da], w_in[:, o_ba:o_pool],
         jnp.zeros((D, BA_PAD - 16), w_in.dtype)], axis=1).astype(BF16)
    assert w_cat.shape[1] == c.total
    w = {k: full[k][l].astype(BF16) for k in ("ffn1_w_gate", "ffn1_w_up", "ffn1_w_down", "ffn2_w_gate", "ffn2_w_up",
                                              "ffn2_w_down", "w_proj_a", "w_proj_b", "w_proj_c", "w_out")}
    w["w_cat"] = w_cat
    f_true = w["ffn1_w_gate"].shape[1]
    pad = -f_true % FFN_BWD_TF
    for k in ("ffn1", "ffn2"):
        w[k + "_bwd"] = (jnp.pad(w[k + "_w_gate"], ((0, 0), (0, pad))), jnp.pad(w[k + "_w_up"], ((0, 0), (0, pad))),
                         jnp.pad(w[k + "_w_down"], ((0, pad), (0, 0))), f_true)
    w["ffn1_norm"] = full["ffn1_norm"][l][None].astype(F32)
    w["ffn2_norm"] = full["ffn2_norm"][l][None].astype(F32)
    w["mix_norm"] = full["mix_norm"][l][None].astype(F32)
    w["dn_conv"] = full["dn_conv"][l].astype(F32)
    w["dn_a_log"] = full["dn_a_log"][l].reshape(2 * DN_HEADS, 1).astype(F32)
    w["dn_dt_bias"] = full["dn_dt_bias"][l].reshape(2 * DN_HEADS, 1).astype(F32)
    w["dn_out_norm"] = full["dn_out_norm"][l][None].astype(F32)
    w["pool_w"] = full["pool_w"][l].astype(F32)
    w["pool_scale"] = full["pool_scale"][l].reshape(POOL_GROUPS, 1, POOL_DIM).astype(F32)
    w["b_gate"] = full["b_gate"][l].reshape(3, D).astype(F32)
    return w


def _layer_grads(g, D):
    c = _Cols(D)
    gc = g.pop("w_cat")
    out = dict(g)
    out["w_gate"] = gc[:, :c.da]
    out["w_in"] = jnp.concatenate([gc[:, c.qkv:c.pool], gc[:, c.ba:c.ba + 16], gc[:, c.pool:c.ba], gc[:, c.da:c.qkv]],
                                  axis=1)
    for k in ("ffn1_norm", "ffn2_norm", "mix_norm", "dn_out_norm"):
        out[k] = g[k][0]
    out["dn_a_log"] = g["dn_a_log"].reshape(2, DN_HEADS)
    out["dn_dt_bias"] = g["dn_dt_bias"].reshape(2, DN_HEADS)
    out["pool_scale"] = g["pool_scale"].reshape(POOL_WIDTH)
    out["b_gate"] = g["b_gate"].reshape(3 * D)
    return out


def _unshard(got):
    full = {}
    for k, t in zip(SHARDED, got):
        ax = SHARD_AXIS[k] - 1
        shp = t.shape[1:]
        full[k] = jnp.moveaxis(t, 0, ax).reshape(shp[:ax] + (N_DEV * shp[ax],) + shp[ax + 1:])
    return full


def _to_owner_blocks(grads):
    out = []
    for k in SHARDED:
        ax = SHARD_AXIS[k] - 1
        shp = grads[k].shape
        t = grads[k].reshape(shp[:ax] + (N_DEV, shp[ax] // N_DEV) + shp[ax + 1:])
        out.append(jnp.moveaxis(t, ax, 0).astype(BF16))
    return out


def _local_step(x, target, rep, shards, distributed):
    B, S, D = x.shape
    T = B * S
    depth = len(shards)
    xs = x.reshape(T, D)
    tape = []
    if distributed:
        sharded_now = _unshard(_all_gather("gather_l0", [shards[0][k] for k in SHARDED]))
    else:
        sharded_now = shards[0]
    for l in range(depth):
        full = {k: [v] * (l + 1) for k, v in sharded_now.items()}
        full.update({k: v for k, v in rep.items() if k != "final_norm"})
        w = _layer_weights(full, l, D)
        host = ("gather", [shards[l + 1][k] for k in SHARDED]) if distributed and l + 1 < depth else None
        x1 = _ffn_fwd(f"l{l}_ffn1", xs, w["ffn1_norm"], w["ffn1_w_gate"], w["ffn1_w_up"], w["ffn1_w_down"])
        x2, mixer_bwd, got = _mixer(l, x1, w, B, host)
        x3 = _ffn_fwd(f"l{l}_ffn2", x2, w["ffn2_norm"], w["ffn2_w_gate"], w["ffn2_w_up"], w["ffn2_w_down"])
        tape.append((w, xs, mixer_bwd, x2))
        xs = x3
        if l + 1 < depth:
            sharded_now = _unshard(got) if distributed else shards[l + 1]
    loss8, dx, dfinal = _loss_fwd_bwd("loss", xs, rep["final_norm"][None].astype(F32), target.reshape(T, D))
    per_layer = [None] * depth
    exchanged = [None] * depth
    pending = None
    for l in reversed(range(depth)):
        w, x0, mixer_bwd, x2 = tape[l]
        wg_p, wu_p, wd_p, f_true = w["ffn2_bwd"]
        dx, dn2, dwg2, dwu2, dwd2 = _ffn_bwd(f"l{l}_ffn2b", x2, w["ffn2_norm"], wg_p, wu_p, wd_p, dx, f_true)
        dx, g, got = mixer_bwd(dx, ("exchange", pending) if pending is not None else None)
        if pending is not None:
            exchanged[l + 1] = got
        wg_p, wu_p, wd_p, f_true = w["ffn1_bwd"]
        dx, dn1, dwg1, dwu1, dwd1 = _ffn_bwd(f"l{l}_ffn1b", x0, w["ffn1_norm"], wg_p, wu_p, wd_p, dx, f_true)
        g.update(ffn1_norm=dn1, ffn1_w_gate=dwg1, ffn1_w_up=dwu1, ffn1_w_down=dwd1,
                 ffn2_norm=dn2, ffn2_w_gate=dwg2, ffn2_w_up=dwu2, ffn2_w_down=dwd2)
        per_layer[l] = _layer_grads(g, D)
        if distributed:
            pending = _to_owner_blocks(per_layer[l])
    return loss8[0, 0], dx.reshape(B, S, D), per_layer, dfinal[0], exchanged, pending


def _mesh_position():
    mx, my, mc = lax.axis_index("x"), lax.axis_index("y"), lax.axis_index("c")
    return mx, my, mc, 4 * mx + 2 * my + mc


def _peers(mx, my, mc):
    out = []
    for k in range(1, N_DEV):
        px, py, pc = mx ^ ((k >> 2) & 1), my ^ ((k >> 1) & 1), mc ^ (k & 1)
        out.append(((px, py, pc), 4 * px + 2 * py + pc))
    return out


_ANY = pl.BlockSpec(memory_space=pl.ANY)


def _all_gather(name, xs):
    n = len(xs)

    def body(*refs):
        _gather_start(refs[:n], refs[n:2 * n], *refs[2 * n:])
        _gather_finish(refs[:n], refs[n:2 * n], *refs[2 * n:])

    sems = pltpu.SemaphoreType.DMA((n, N_DEV - 1))
    return pl.pallas_call(
        body, name=name, in_specs=[_ANY] * n, out_specs=[_ANY] * n,
        out_shape=[jax.ShapeDtypeStruct((N_DEV,) + x.shape, x.dtype) for x in xs],
        scratch_shapes=[sems, sems, pltpu.SemaphoreType.DMA((n,))],
    )(*xs)


class _GatherPlan:
    def __init__(self, x_refs, o_refs, send_sems, recv_sems, local_sems):
        self.x, self.o, self.ss, self.rs, self.ls = x_refs, o_refs, send_sems, recv_sems, local_sems
        self.mx, self.my, self.mc, self.me = _mesh_position()
        self.self_id = (self.mx, self.my, self.mc)
        self.sibling = (self.mx, self.my, 1 - self.mc)
        self.chips = [(1 - self.mx, self.my), (self.mx, 1 - self.my), (1 - self.mx, 1 - self.my)]

    def copy(self, a, k, blk, to, from_input=False):
        dst = self.o[a].at[blk]
        return pltpu.make_async_remote_copy(src_ref=self.x[a] if from_input else dst, dst_ref=dst,
                                            send_sem=self.ss.at[a, k], recv_sem=self.rs.at[a, k],
                                            device_id=to, device_id_type=pl.DeviceIdType.MESH)

    def own(self, a):
        return pltpu.make_async_copy(self.x[a], self.o[a].at[self.me], self.ls.at[a])

    def first_sends(self, a):
        cps = [self.copy(a, 0, self.me, self.sibling, from_input=True)]
        return cps + [self.copy(a, 1 + j, self.me, (*chip, self.mc), from_input=True) for j, chip in enumerate(self.chips)]

    def passed_on(self, a, j):
        cx, cy = self.chips[j]
        return self.copy(a, 4 + j, 4 * cx + 2 * cy + self.mc, self.sibling)


def _gather_start(x_refs, o_refs, send_sems, recv_sems, local_sems):
    p = _GatherPlan(x_refs, o_refs, send_sems, recv_sems, local_sems)
    for a in range(len(x_refs)):
        p.own(a).start()
        for cp in p.first_sends(a):
            cp.start()


def _gather_finish(x_refs, o_refs, send_sems, recv_sems, local_sems):
    p = _GatherPlan(x_refs, o_refs, send_sems, recv_sems, local_sems)
    n = len(x_refs)
    for a in range(n):
        for j, (cx, cy) in enumerate(p.chips):
            p.copy(a, 1 + j, 4 * cx + 2 * cy + p.mc, p.self_id).wait_recv()
            p.passed_on(a, j).start()
    for a in range(n):
        p.copy(a, 0, 4 * p.mx + 2 * p.my + 1 - p.mc, p.self_id).wait_recv()
        for j, (cx, cy) in enumerate(p.chips):
            p.copy(a, 4 + j, 4 * cx + 2 * cy + 1 - p.mc, p.self_id).wait_recv()
    for a in range(n):
        for cp in p.first_sends(a):
            cp.wait_send()
        for j in range(len(p.chips)):
            p.passed_on(a, j).wait_send()
        p.own(a).wait()


def _exchange_grads(name, gs, gr):
    ns, n = len(gs), len(gs) + len(gr)

    def body(*refs):
        _exchange_start(refs[:n], refs[n:2 * n], *refs[2 * n:], n_sharded=ns)
        _exchange_finish(refs[:n], refs[n:2 * n], *refs[2 * n:], n_sharded=ns)

    sems = pltpu.SemaphoreType.DMA((n, N_DEV - 1))
    outs = pl.pallas_call(
        body, name=name, in_specs=[_ANY] * n, out_specs=[_ANY] * n,
        out_shape=[jax.ShapeDtypeStruct(a.shape, a.dtype) for a in gs]
        + [jax.ShapeDtypeStruct((N_DEV,) + a.shape, a.dtype) for a in gr],
        scratch_shapes=[sems, sems, pltpu.SemaphoreType.DMA((n,))],
    )(*gs, *gr)
    return outs[:ns], outs[ns:]


def _exchange_copies(in_refs, out_refs, send_sems, recv_sems, local_sems, n_sharded):
    mx, my, mc, me = _mesh_position()
    n = len(in_refs)
    own = [pltpu.make_async_copy(in_refs[a].at[me] if a < n_sharded else in_refs[a], out_refs[a].at[me],
                                 local_sems.at[a]) for a in range(n)]
    remote = []
    for k, (peer, pid) in enumerate(_peers(mx, my, mc)):
        for a in range(n):
            src = in_refs[a].at[pid] if a < n_sharded else in_refs[a]
            remote.append(pltpu.make_async_remote_copy(
                src_ref=src, dst_ref=out_refs[a].at[me], send_sem=send_sems.at[a, k], recv_sem=recv_sems.at[a, k],
                device_id=peer, device_id_type=pl.DeviceIdType.MESH))
    return own, remote


def _exchange_start(in_refs, out_refs, send_sems, recv_sems, local_sems, n_sharded=None):
    ns = len(in_refs) if n_sharded is None else n_sharded
    own, remote = _exchange_copies(in_refs, out_refs, send_sems, recv_sems, local_sems, ns)
    for cp in own + remote:
        cp.start()


def _exchange_finish(in_refs, out_refs, send_sems, recv_sems, local_sems, n_sharded=None):
    ns = len(in_refs) if n_sharded is None else n_sharded
    own, remote = _exchange_copies(in_refs, out_refs, send_sems, recv_sems, local_sems, ns)
    for cp in remote:
        cp.wait_send()
        cp.wait_recv()
    for cp in own:
        cp.wait()


def _reduce_adamw(name, parts, w, m, v):
    shape = w.shape
    cols = shape[-1]
    w2, m2, v2 = (t.reshape(-1, cols) for t in (w, m, v))
    p3 = parts.reshape(N_DEV, -1, cols)
    rows = w2.shape[0]
    tr = _pick(rows, 512, 16) if rows > 1024 else rows
    c1 = 1.0 - ADAM_B1 ** ADAM_STEP
    c2 = 1.0 - ADAM_B2 ** ADAM_STEP

    def body(p_ref, w_ref, m_ref, v_ref, g_ref, d_ref, nm_ref, nv_ref):
        gv = p_ref[0].astype(F32)
        for d in range(1, N_DEV):
            gv = gv + p_ref[d].astype(F32)
        nm = ADAM_B1 * m_ref[...] + (1.0 - ADAM_B1) * gv
        nv = ADAM_B2 * v_ref[...] + (1.0 - ADAM_B2) * (gv * gv)
        g_ref[...] = gv
        d_ref[...] = -ADAM_LR * ((nm / c1) / (jnp.sqrt(nv / c2) + ADAM_EPS) + ADAM_WD * w_ref[...])
        nm_ref[...] = nm
        nv_ref[...] = nv

    spec = pl.BlockSpec((tr, cols), lambda i: (i, 0))
    outs = pl.pallas_call(
        body, name=name, grid=(rows // tr,),
        in_specs=[pl.BlockSpec((N_DEV, tr, cols), lambda i: (0, i, 0))] + [spec] * 3, out_specs=[spec] * 4,
        out_shape=[jax.ShapeDtypeStruct((rows, cols), F32)] * 4, compiler_params=_params(),
    )(p3, w2, m2, v2)
    return tuple(o.reshape(shape) for o in outs)


def kernel(x, ffn1_norm, ffn1_w_gate, ffn1_w_up, ffn1_w_down, mix_norm, w_in, dn_conv, dn_a_log, dn_dt_bias, dn_out_norm, pool_w, pool_scale, w_proj_a, w_proj_b, w_proj_c, w_gate, b_gate, w_out, ffn2_norm, ffn2_w_gate, ffn2_w_up, ffn2_w_down, final_norm, loss_target, m_ffn1_norm, m_ffn1_w_gate, m_ffn1_w_up, m_ffn1_w_down, m_mix_norm, m_w_in, m_dn_conv, m_dn_a_log, m_dn_dt_bias, m_dn_out_norm, m_pool_w, m_pool_scale, m_w_proj_a, m_w_proj_b, m_w_proj_c, m_w_gate, m_b_gate, m_w_out, m_ffn2_norm, m_ffn2_w_gate, m_ffn2_w_up, m_ffn2_w_down, m_final_norm, v_ffn1_norm, v_ffn1_w_gate, v_ffn1_w_up, v_ffn1_w_down, v_mix_norm, v_w_in, v_dn_conv, v_dn_a_log, v_dn_dt_bias, v_dn_out_norm, v_pool_w, v_pool_scale, v_w_proj_a, v_w_proj_b, v_w_proj_c, v_w_gate, v_b_gate, v_w_out, v_ffn2_norm, v_ffn2_w_gate, v_ffn2_w_up, v_ffn2_w_down, v_final_norm):
    args = locals()
    wts = {k: args[k] for k in WEIGHTS}
    ms = {k: args["m_" + k] for k in WEIGHTS}
    vs = {k: args["v_" + k] for k in WEIGHTS}

    depth = w_in.shape[0]
    rep = {k: wts[k] for k in REPLICATED}
    shards = [{k: wts[k][l].astype(BF16) for k in SHARDED} for l in range(depth)]
    loss_local, dx, per_layer, dfinal, exchanged, pending = _local_step(x, loss_target, rep, shards, True)
    loss = lax.psum(loss_local, ("x", "y", "c"))

    gr = [dfinal if k == "final_norm" else jnp.stack([pg[k] for pg in per_layer]).astype(F32).reshape(wts[k].shape)
          for k in REPLICATED]
    exchanged[0], got_r = _exchange_grads("exchange_grads", pending, gr)
    parts = {k: jnp.stack([exchanged[l][j] for l in range(depth)], axis=1) for j, k in enumerate(SHARDED)}
    parts.update(zip(REPLICATED, got_r))

    g_final, deltas, new_m, new_v = {}, {}, {}, {}
    for k in WEIGHTS:
        g_final[k], deltas[k], new_m[k], new_v[k] = _reduce_adamw("adamw_" + k, parts[k], wts[k], ms[k], vs[k])
    return (loss, dx, *[g_final[k] for k in WEIGHTS], *[deltas[k] for k in WEIGHTS], *[new_m[k] for k in WEIGHTS],
            *[new_v[k] for k in WEIGHTS])
```

```python
import functools
import math

import jax
import jax.numpy as jnp
from jax import lax
from jax.experimental import pallas as pl
from jax.experimental.pallas import tpu as pltpu

F32 = jnp.float32
BF16 = jnp.bfloat16

N_DEV = 8
RMS_EPS = 1e-6
L2_EPS = 1e-6
DN_HEADS = 4
DN_DIM = 128
DN_WIDTH = DN_HEADS * DN_DIM
DN_CONV = 5
DN_CHUNK = 64
DN_SUPER = 256
POOL_GROUPS = 4
POOL_DIM = 128
POOL_WIDTH = POOL_GROUPS * POOL_DIM
POOL_MAX_HALF = 8
DA_GROUPS = 3
DA_HEADS = 4
DA_DIM = 64
DA_WIDTH = DA_GROUPS * DA_HEADS * DA_DIM
DA_OUT = DA_HEADS * DA_DIM
DA_DILATIONS = (1, 4, 16)
DA_RADIUS = 64
DA_TQ = 256
FFN_BWD_TF = 512
ROPE_THETA = 10000.0
MASK_VALUE = -1e30
BA_PAD = 128

ADAM_LR = 0.001
ADAM_B1 = 0.9
ADAM_B2 = 0.999
ADAM_EPS = 1e-08
ADAM_WD = 0.01
ADAM_STEP = 10

VMEM_LIMIT_V7X = 56 * 1024 * 1024
LANES = 1024

SHARDED = ("ffn1_w_gate", "ffn1_w_up", "ffn1_w_down", "w_in", "dn_conv", "w_proj_a", "w_proj_b", "w_proj_c",
           "w_gate", "w_out", "ffn2_w_gate", "ffn2_w_up", "ffn2_w_down")
SHARD_AXIS = {"ffn1_w_gate": 2, "ffn1_w_up": 2, "ffn1_w_down": 1, "w_in": 2, "dn_conv": 2, "w_proj_a": 2,
              "w_proj_b": 2, "w_proj_c": 2, "w_gate": 2, "w_out": 1, "ffn2_w_gate": 2, "ffn2_w_up": 2,
              "ffn2_w_down": 1}
REPLICATED = ("ffn1_norm", "mix_norm", "dn_a_log", "dn_dt_bias", "dn_out_norm", "pool_w", "pool_scale", "b_gate",
              "ffn2_norm", "final_norm")
WEIGHTS = ("ffn1_norm", "ffn1_w_gate", "ffn1_w_up", "ffn1_w_down", "mix_norm", "w_in", "dn_conv", "dn_a_log",
           "dn_dt_bias", "dn_out_norm", "pool_w", "pool_scale", "w_proj_a", "w_proj_b", "w_proj_c", "w_gate",
           "b_gate", "w_out", "ffn2_norm", "ffn2_w_gate", "ffn2_w_up", "ffn2_w_down", "final_norm")


def _params(**kw):
    return pltpu.CompilerParams(vmem_limit_bytes=VMEM_LIMIT_V7X, **kw)


def _pick(n, target, align):
    best = None
    t = align
    while t <= min(n, target):
        if n % t == 0:
            best = t
        t += align
    return best if best is not None else n


_DIMS = {"nn": (((1,), (0,)), ((), ())), "nt": (((1,), (1,)), ((), ())), "tn": (((0,), (0,)), ((), ()))}


def _dg(a, b, mode):
    return lax.dot_general(a, b, _DIMS[mode], preferred_element_type=F32)


def _split2(a):
    hi = a.astype(BF16)
    lo = (a - hi.astype(F32)).astype(BF16)
    return hi, lo


def _dotp(a, b, mode, passes):
    if passes == 1:
        return _dg(a.astype(BF16), b.astype(BF16), mode)
    ah, al = _split2(a.astype(F32))
    bh, bl = _split2(b.astype(F32))
    return _dg(ah, bh, mode) + (_dg(ah, bl, mode) + _dg(al, bh, mode))


@functools.partial(jax.custom_vjp, nondiff_argnums=(2, 3))
def _dot(a, b, mode, passes):
    return _dotp(a, b, mode, passes)


def _dot_fwd(a, b, mode, passes):
    return _dotp(a, b, mode, passes), (a, b)


def _dot_bwd(mode, passes, res, ct):
    a, b = res
    if mode == "nn":
        da, db = _dotp(ct, b, "nt", passes), _dotp(a, ct, "tn", passes)
    elif mode == "nt":
        da, db = _dotp(ct, b, "nn", passes), _dotp(ct, a, "tn", passes)
    else:
        da, db = _dotp(b, ct, "nt", passes), _dotp(a, ct, "nn", passes)
    return da.astype(a.dtype), db.astype(b.dtype)


_dot.defvjp(_dot_fwd, _dot_bwd)


def _split3(x):
    x1 = x.astype(BF16)
    r = x - x1.astype(F32)
    x2 = r.astype(BF16)
    x3 = (r - x2.astype(F32)).astype(BF16)
    return x1, x2, x3


def _mdotp(mask, x, mode):
    x1, x2, x3 = _split3(x)
    return _dg(mask, x1, mode) + (_dg(mask, x2, mode) + _dg(mask, x3, mode))


@jax.custom_vjp
def _mdot(mask, x):
    return _mdotp(mask, x, "nn")


def _mdot_fwd(mask, x):
    return _mdotp(mask, x, "nn"), mask


def _mdot_bwd(mask, ct):
    return jnp.zeros_like(mask), _mdotp(mask, ct, "tn")


_mdot.defvjp(_mdot_fwd, _mdot_bwd)


_SOLVE_SQUARINGS = int(math.log2(DN_CHUNK)) - 1


def _unit_solve_fwd(A, R):
    Ab = A.astype(BF16)
    n = A.shape[0]
    eye = (lax.broadcasted_iota(jnp.int32, A.shape, 0) == lax.broadcasted_iota(jnp.int32, A.shape, 1)).astype(F32)
    Tm = eye - A
    P = _dg(Ab, Ab, "nn").astype(BF16)
    for _ in range(_SOLVE_SQUARINGS - 1):
        M = _dg(P, jnp.concatenate([Tm.astype(BF16), P], axis=1), "nn")
        Tm, P = Tm + M[:, :n], M[:, n:].astype(BF16)
    Tb = (Tm + _dg(P, Tm.astype(BF16), "nn")).astype(BF16)
    X = _dg(Tb, R.astype(BF16), "nn")
    return X, (Tb, X)


@jax.custom_vjp
def _unit_solve(A, R):
    return _unit_solve_fwd(A, R)[0]


def _unit_solve_bwd(res, dX):
    Tb, X = res
    Y = _dg(Tb, dX.astype(BF16), "tn")
    return -_dg(Y.astype(BF16), X.astype(BF16), "nt"), Y


_unit_solve.defvjp(_unit_solve_fwd, _unit_solve_bwd)


def _shift_impl(x, o):
    if o == 0:
        return x
    n = x.shape[0]
    y = pltpu.roll(x, (-o) % n, axis=0)
    t = lax.broadcasted_iota(jnp.int32, x.shape, 0) + o
    return jnp.where((t >= 0) & (t < n), y, 0.0)


@functools.partial(jax.custom_vjp, nondiff_argnums=(1,))
def _shift(x, o):
    return _shift_impl(x, o)


def _shift_fwd(x, o):
    return _shift_impl(x, o), None


def _shift_bwd(o, _, ct):
    return (_shift_impl(ct, -o),)


_shift.defvjp(_shift_fwd, _shift_bwd)


def _rot_impl(x):
    w = x.shape[1]
    half = DA_DIM // 2
    lane = lax.broadcasted_iota(jnp.int32, x.shape, 1)
    first = (lane & (DA_DIM - 1)) < half
    return jnp.where(first, -pltpu.roll(x, w - half, axis=1), pltpu.roll(x, half, axis=1))


@jax.custom_vjp
def _rot(x):
    return _rot_impl(x)


def _rot_fwd(x):
    return _rot_impl(x), None


def _rot_bwd(_, ct):
    return (-_rot_impl(ct),)


_rot.defvjp(_rot_fwd, _rot_bwd)


def _sigmoid(x):
    return 1.0 / (1.0 + jnp.exp(-x))


def _silu(x):
    return x * _sigmoid(x)


def _softplus(x):
    return jnp.maximum(x, 0.0) + jnp.log(1.0 + jnp.exp(-jnp.abs(x)))


def _rms(x, gain):
    return x * lax.rsqrt(jnp.mean(x * x, axis=-1, keepdims=True) + RMS_EPS) * gain


class _In:
    def __init__(self, arr, block, imap, kind="t", acc=False, g=None, gdtype=None, split=False):
        self.arr, self.block, self.imap, self.kind, self.acc, self.g, self.gdtype = arr, block, imap, kind, acc, g, gdtype
        self.split = split


class _Out:
    def __init__(self, shape, dtype, block, imap, split=False):
        self.shape, self.dtype, self.block, self.imap, self.split = shape, dtype, block, imap, split


def _sub_index(split, s):
    if not split:
        return Ellipsis
    return (s,) if split is True else split(s)


def _grid_edges(grid):
    first = last = None
    for a, n in enumerate(grid):
        f, l = pl.program_id(a) == 0, pl.program_id(a) == n - 1
        first = f if first is None else jnp.logical_and(first, f)
        last = l if last is None else jnp.logical_and(last, l)
    return first, last


def _comm_plumbing(comm):
    if comm is None:
        return [], [], [], lambda refs: None, lambda refs: None
    kind, arrs = comm
    n = len(arrs)
    if kind == "gather":
        shapes = [jax.ShapeDtypeStruct((N_DEV,) + a.shape, a.dtype) for a in arrs]
        start, finish = _gather_start, _gather_finish
    else:
        shapes = [jax.ShapeDtypeStruct(a.shape, a.dtype) for a in arrs]
        start, finish = _exchange_start, _exchange_finish
    sems = [pltpu.SemaphoreType.DMA((n, N_DEV - 1)), pltpu.SemaphoreType.DMA((n, N_DEV - 1)),
            pltpu.SemaphoreType.DMA((n,))]
    return list(arrs), shapes, sems, start, finish


def _first_step(acc_from, ngrid):
    c = None
    for a in range(acc_from, ngrid):
        t = pl.program_id(a) == 0
        c = t if c is None else jnp.logical_and(c, t)
    return c


def _tile_fwd(name, f, grid, ins, outs, sub=1, comm=None):
    n_in, n_out = len(ins), len(outs)
    ngrid = len(grid)
    c_arrs, c_shapes, c_sems, c_start, c_finish = _comm_plumbing(comm)
    nc = len(c_arrs)

    def body(*refs):
        in_refs, c_in = refs[:n_in], refs[n_in:n_in + nc]
        out_refs, c_out = refs[n_in + nc:n_in + nc + n_out], refs[n_in + nc + n_out:n_in + 2 * nc + n_out]
        sems = refs[n_in + 2 * nc + n_out:]
        pids = tuple(pl.program_id(a) for a in range(ngrid))
        if nc:
            first, last = _grid_edges(grid)
            pl.when(first)(lambda: c_start(c_in, c_out, *sems))
        for s in range(sub):
            vals = [r[_sub_index(i.split, s)] for r, i in zip(in_refs, ins)]
            res = f(pids + ((s,) if sub > 1 else ()), *vals)
            for r, o, v in zip(out_refs, outs, res):
                r[_sub_index(o.split, s)] = v.astype(r.dtype)
        if nc:
            pl.when(last)(lambda: c_finish(c_in, c_out, *sems))

    res = pl.pallas_call(
        body, name=name, grid=grid,
        in_specs=[pl.BlockSpec(i.block, i.imap) for i in ins] + [_ANY] * nc,
        out_specs=[pl.BlockSpec(o.block, o.imap) for o in outs] + [_ANY] * nc,
        out_shape=[jax.ShapeDtypeStruct(o.shape, o.dtype) for o in outs] + c_shapes,
        scratch_shapes=c_sems, compiler_params=_params(),
    )(*[i.arr for i in ins], *c_arrs)
    return (res[:n_out], res[n_out:]) if nc else res


def _tile_bwd(name, f, grid, ins, outs, cts, acc_from=None, addends=None, sub=1, comm=None):
    n_in, n_out = len(ins), len(outs)
    ngrid = len(grid)
    diff = [k for k, i in enumerate(ins) if i.kind == "t"]
    addends = addends or {}
    add_keys = sorted(addends)
    n_add, n_g = len(add_keys), len(diff)
    c_arrs, c_shapes, c_sems, c_start, c_finish = _comm_plumbing(comm)
    nc = len(c_arrs)

    def body(*refs):
        pids = tuple(pl.program_id(a) for a in range(ngrid))
        in_refs = refs[:n_in]
        ct_refs = refs[n_in:n_in + n_out]
        add_refs = refs[n_in + n_out:n_in + n_out + n_add]
        o = n_in + n_out + n_add
        c_in, g_refs, c_out, sems = refs[o:o + nc], refs[o + nc:o + nc + n_g], refs[o + nc + n_g:o + 2 * nc + n_g], \
            refs[o + 2 * nc + n_g:]
        if nc:
            first_step, last_step = _grid_edges(grid)
            pl.when(first_step)(lambda: c_start(c_in, c_out, *sems))
        sums = {}
        for s in range(sub):
            vals = [r[_sub_index(i.split, s)] for r, i in zip(in_refs, ins)]
            dvals = [vals[k].astype(F32) for k in diff]

            def g(*d, vals=vals, s=s):
                full = list(vals)
                for k, dk in zip(diff, d):
                    full[k] = dk
                return tuple(f(pids + ((s,) if sub > 1 else ()), *full))

            res, vjp = jax.vjp(g, *dvals)
            cvals = [c[_sub_index(o_.split, s)].astype(r.dtype) for c, o_, r in zip(ct_refs, outs, res)]
            grads = vjp(tuple(cvals))
            for k, gr in zip(diff, grads):
                idx = _sub_index(ins[k].split, s)
                key = (k, str(idx))
                sums[key] = (idx, gr if key not in sums else sums[key][1] + gr)
        first = _first_step(acc_from, ngrid) if acc_from is not None else None
        for (k, _), (idx, gr) in sums.items():
            gref = g_refs[diff.index(k)]
            if idx is not Ellipsis:
                gref[idx] = gr.astype(gref.dtype)
                continue
            if k in addends:
                gr = gr + add_refs[add_keys.index(k)][...].astype(F32)
            if ins[k].acc and first is not None:
                @pl.when(first)
                def _(gr=gr, gref=gref):
                    gref[...] = gr.astype(gref.dtype)

                @pl.when(jnp.logical_not(first))
                def _(gr=gr, gref=gref):
                    gref[...] += gr.astype(gref.dtype)
            else:
                gref[...] = gr.astype(gref.dtype)
        if nc:
            pl.when(last_step)(lambda: c_finish(c_in, c_out, *sems))

    g_shapes, g_specs = [], []
    for k in diff:
        i = ins[k]
        if i.g is not None:
            shape, imap = i.g
        else:
            shape, imap = i.arr.shape, i.imap
        dt = i.gdtype or (F32 if i.acc else i.arr.dtype)
        g_shapes.append(jax.ShapeDtypeStruct(shape, dt))
        g_specs.append(pl.BlockSpec(i.block, imap))
    add_specs = [pl.BlockSpec(ins[k].block, ins[k].g[1] if ins[k].g is not None else ins[k].imap) for k in add_keys]
    res = pl.pallas_call(
        body, name=name, grid=grid,
        in_specs=[pl.BlockSpec(i.block, i.imap) for i in ins] + [pl.BlockSpec(o.block, o.imap) for o in outs] + add_specs
        + [_ANY] * nc,
        out_specs=g_specs + [_ANY] * nc, out_shape=g_shapes + c_shapes,
        scratch_shapes=c_sems, compiler_params=_params(),
    )(*[i.arr for i in ins], *cts, *[addends[k] for k in add_keys], *c_arrs)
    return (res[:n_g], res[n_g:]) if nc else res


def _mm(name, a, b, mode, out_dtype=F32, add=None, tm=512, tn=512, tk=512, m=None, n=None):
    if mode == "nn":
        (M, K), N = a.shape, b.shape[1]
    elif mode == "nt":
        (M, K), N = a.shape, b.shape[0]
    else:
        (K, M), N = a.shape, b.shape[1]
    M, N = m or M, n or N
    tm, tn, tk = _pick(M, tm, 128), _pick(N, tn, 128), _pick(K, tk, 128)
    nk = K // tk
    a_spec = pl.BlockSpec((tk, tm), lambda i, j, k: (k, i)) if mode == "tn" else pl.BlockSpec((tm, tk), lambda i, j, k: (i, k))
    b_spec = pl.BlockSpec((tn, tk), lambda i, j, k: (j, k)) if mode == "nt" else pl.BlockSpec((tk, tn), lambda i, j, k: (k, j))
    o_spec = pl.BlockSpec((tm, tn), lambda i, j, k: (i, j))

    def body(*refs):
        if add is None:
            a_ref, b_ref, o_ref, acc = refs
            add_ref = None
        else:
            a_ref, b_ref, add_ref, o_ref, acc = refs
        k = pl.program_id(2)

        @pl.when(k == 0)
        def _():
            acc[...] = jnp.zeros_like(acc)

        acc[...] += _dg(a_ref[...].astype(BF16), b_ref[...].astype(BF16), mode)

        @pl.when(k == nk - 1)
        def _():
            r = acc[...]
            if add_ref is not None:
                r = r + add_ref[...].astype(F32)
            o_ref[...] = r.astype(o_ref.dtype)

    ops = (a, b) if add is None else (a, b, add)
    specs = [a_spec, b_spec] + ([] if add is None else [o_spec])
    return pl.pallas_call(
        body, name=name, grid=(M // tm, N // tn, nk), in_specs=specs, out_specs=o_spec,
        out_shape=jax.ShapeDtypeStruct((M, N), out_dtype), scratch_shapes=[pltpu.VMEM((tm, tn), F32)],
        compiler_params=_params(dimension_semantics=("parallel", "parallel", "arbitrary")),
    )(*ops)


def _ffn_fwd(name, x, gain, wg, wu, wd):
    T, D = x.shape
    F = wg.shape[1]
    tm, tf = _pick(T, 1024, 8), _pick(F, 256, 128)
    nf = F // tf

    def body(x_ref, g_ref, wg_ref, wu_ref, wd_ref, o_ref, h_ref, acc):
        j = pl.program_id(1)

        @pl.when(j == 0)
        def _():
            h_ref[...] = _rms(x_ref[...], g_ref[...]).astype(BF16)
            acc[...] = jnp.zeros_like(acc)

        h = h_ref[...]
        a = _dg(h, wg_ref[...], "nn")
        b = _dg(h, wu_ref[...], "nn")
        s = (_silu(a) * b).astype(BF16)
        acc[...] += _dg(s, wd_ref[...], "nn")

        @pl.when(j == nf - 1)
        def _():
            o_ref[...] = x_ref[...] + 0.5 * acc[...]

    return pl.pallas_call(
        body, name=name, grid=(T // tm, nf),
        in_specs=[pl.BlockSpec((tm, D), lambda i, j: (i, 0)), pl.BlockSpec((1, D), lambda i, j: (0, 0)),
                  pl.BlockSpec((D, tf), lambda i, j: (0, j)), pl.BlockSpec((D, tf), lambda i, j: (0, j)),
                  pl.BlockSpec((tf, D), lambda i, j: (j, 0))],
        out_specs=pl.BlockSpec((tm, D), lambda i, j: (i, 0)),
        out_shape=jax.ShapeDtypeStruct((T, D), F32),
        scratch_shapes=[pltpu.VMEM((tm, D), BF16), pltpu.VMEM((tm, D), F32)],
        compiler_params=_params(dimension_semantics=("parallel", "arbitrary")),
    )(x, gain, wg, wu, wd)


def _ffn_bwd(name, x, gain, wg, wu, wd, dy, f_true):
    T, D = x.shape
    F = wg.shape[1]
    tm, tf = _pick(T, 512, 8), _pick(F, FFN_BWD_TF, 128)
    nf = F // tf

    def body(x_ref, g_ref, wg_ref, wu_ref, wd_ref, dy_ref, dx_ref, dg_ref, da_ref, db_ref, s_ref, h_ref, dyh_ref, dh):
        i, j = pl.program_id(0), pl.program_id(1)

        @pl.when(j == 0)
        def _():
            h_ref[...] = _rms(x_ref[...], g_ref[...]).astype(BF16)
            dyh_ref[...] = (0.5 * dy_ref[...]).astype(BF16)
            dh[...] = jnp.zeros_like(dh)

        h = h_ref[...]
        a = _dg(h, wg_ref[...], "nn")
        b = _dg(h, wu_ref[...], "nn")
        ds = _dg(dyh_ref[...], wd_ref[...], "nt")
        sig = _sigmoid(a)
        silu = a * sig
        da = (ds * b * (sig * (1.0 + a * (1.0 - sig)))).astype(BF16)
        db = (ds * silu).astype(BF16)
        da_ref[...] = da
        db_ref[...] = db
        s_ref[...] = (silu * b).astype(BF16)
        dh[...] += _dg(da, wg_ref[...], "nt") + _dg(db, wu_ref[...], "nt")

        @pl.when(j == nf - 1)
        def _():
            _, vjp = jax.vjp(_rms, x_ref[...], g_ref[...])
            dxn, dgn = vjp(dh[...])
            dx_ref[...] = dy_ref[...] + dxn

            @pl.when(i == 0)
            def _():
                dg_ref[...] = dgn

            @pl.when(i != 0)
            def _():
                dg_ref[...] += dgn

    row = lambda i, j: (i, 0)
    col = lambda i, j: (i, j)
    dx, dgain, da, db, s, h, dyh = pl.pallas_call(
        body, name=name, grid=(T // tm, nf),
        in_specs=[pl.BlockSpec((tm, D), row), pl.BlockSpec((1, D), lambda i, j: (0, 0)),
                  pl.BlockSpec((D, tf), lambda i, j: (0, j)), pl.BlockSpec((D, tf), lambda i, j: (0, j)),
                  pl.BlockSpec((tf, D), lambda i, j: (j, 0)), pl.BlockSpec((tm, D), row)],
        out_specs=[pl.BlockSpec((tm, D), row), pl.BlockSpec((1, D), lambda i, j: (0, 0)),
                   pl.BlockSpec((tm, tf), col), pl.BlockSpec((tm, tf), col), pl.BlockSpec((tm, tf), col),
                   pl.BlockSpec((tm, D), row), pl.BlockSpec((tm, D), row)],
        out_shape=[jax.ShapeDtypeStruct((T, D), F32), jax.ShapeDtypeStruct((1, D), F32),
                   jax.ShapeDtypeStruct((T, F), BF16), jax.ShapeDtypeStruct((T, F), BF16),
                   jax.ShapeDtypeStruct((T, F), BF16), jax.ShapeDtypeStruct((T, D), BF16),
                   jax.ShapeDtypeStruct((T, D), BF16)],
        scratch_shapes=[pltpu.VMEM((tm, D), F32)],
        compiler_params=_params(),
    )(x, gain, wg, wu, wd, dy)
    dwg = _mm(name + "_dwg", h, da, "tn", tm=1024, tn=1408, tk=1024, n=f_true)
    dwu = _mm(name + "_dwu", h, db, "tn", tm=1024, tn=1408, tk=1024, n=f_true)
    dwd = _mm(name + "_dwd", s, dyh, "tn", tm=1408, tn=1024, tk=1024, m=f_true)
    return dx, dgain, dwg, dwu, dwd


def _norm_f(pids, x, gain):
    return (_rms(x, gain),)


def _dn_conv_f(pids, x, w):
    j = pids[0]
    tap = lax.broadcasted_iota(jnp.int32, w.shape, 0)
    y = jnp.zeros_like(x)
    for t in range(DN_CONV):
        wt = jnp.sum(jnp.where(tap == t, w, 0.0), axis=0, keepdims=True)
        y = y + _shift(x, t - DN_CONV // 2) * wt
    y = _silu(y)
    n = y * lax.rsqrt(jnp.sum(y * y, axis=-1, keepdims=True) + L2_EPS)
    is_q = (j < DN_HEADS).astype(F32)
    is_qk = (j < 2 * DN_HEADS).astype(F32)
    scale = is_q * (DN_DIM ** -0.5) + (1.0 - is_q)
    return ((is_qk * n + (1.0 - is_qk) * y) * scale,)


def _dn_gate_f(pids, braw, araw, a_log, dt_bias):
    beta = _sigmoid(braw)
    g = -jnp.exp(a_log) * _softplus(araw + dt_bias)
    return beta, g


def _dn_prep_f(pids, q, k, v, brow, grow):
    cs = DN_SUPER
    sign = 1 - 2 * pids[2]
    ii = lax.broadcasted_iota(jnp.int32, (cs, cs), 0)
    jj = lax.broadcasted_iota(jnp.int32, (cs, cs), 1)
    shift = int(math.log2(DN_CHUNK))
    same = (ii >> shift) == (jj >> shift)
    d = (ii - jj) * sign
    incl = same & (d >= 0)
    strict = same & (d > 0)
    eye = ii == jj
    g_col = jnp.sum(jnp.where(eye, jnp.broadcast_to(grow, (cs, cs)), 0.0), axis=1, keepdims=True)
    b_col = jnp.sum(jnp.where(eye, jnp.broadcast_to(brow, (cs, cs)), 0.0), axis=1, keepdims=True)
    g128 = jnp.broadcast_to(g_col, (cs, DN_DIM))
    G = _mdot(incl.astype(BF16), g128)
    Gt = _mdot(same.astype(BF16), g128)
    Gc = jnp.concatenate([G, G], axis=1)
    Grow = jnp.sum(jnp.where(eye, Gc, 0.0), axis=0, keepdims=True)
    decay = jnp.exp(jnp.where(incl, Gc - Grow, MASK_VALUE))
    eG = jnp.exp(G)
    kb = k * b_col
    A = jnp.where(strict, _dot(kb, k, "nt", 1) * decay, 0.0)
    X = _unit_solve(A, jnp.concatenate([v * b_col, kb * eG], axis=1))
    qk = jnp.where(incl, _dot(q, k, "nt", 1) * decay, 0.0)
    return X, qk, q * eG, k * jnp.exp(Gt - G), jnp.exp(Gt)


def _dn_out_f(pids, of, ob, z, gain):
    return (_rms(of + ob, gain) * _silu(z),)


def _pool_f(pids, u, w, scale):
    g = pids[0]
    half = jnp.left_shift(1, g)
    n = u.shape[0]
    pos = lax.broadcasted_iota(jnp.int32, (n, 1), 0)
    tot = jnp.zeros_like(u)
    cnt = jnp.zeros((n, 1), F32)
    for o in range(-POOL_MAX_HALF, POOL_MAX_HALF):
        use = ((o >= -half) & (o < half)).astype(F32)
        tot = tot + use * _shift(u, o)
        cnt = cnt + use * ((pos + o >= 0) & (pos + o < n)).astype(F32)
    pooled = tot / cnt - u
    return (_dot(pooled, w, "nn", 1) * scale,)


def _rope_f(pids, *args):
    cos, sin = args[-2:]
    qs, ks, vs = args[:DA_GROUPS], args[DA_GROUPS:2 * DA_GROUPS], args[2 * DA_GROUPS:3 * DA_GROUPS]
    qr = [(q * cos + _rot(q) * sin) * (DA_DIM ** -0.5) for q in qs]
    kr = [k * cos + _rot(k) * sin for k in ks]
    return (*qr, *kr, *vs)


def _attn_head(q, k, v, qpos0, kpos0):
    s = _dot(q, k, "nt", 1)
    qi = qpos0 + lax.broadcasted_iota(jnp.int32, s.shape, 0)
    kj = kpos0 + lax.broadcasted_iota(jnp.int32, s.shape, 1)
    s = jnp.where(jnp.abs(kj - qi) <= DA_RADIUS, s, MASK_VALUE)
    m = lax.stop_gradient(jnp.max(s, axis=1, keepdims=True))
    p = jnp.exp(s - m)
    l = jnp.sum(p, axis=1, keepdims=True)
    o = _dot(p, v, "nn", 1) / l
    return o, jnp.broadcast_to(m + jnp.log(l), o.shape)


def _merge_f(pids, o0, o1, o2, l0, l1, l2):
    m = jnp.maximum(jnp.maximum(l0, l1), l2)
    e0, e1, e2 = jnp.exp(l0 - m), jnp.exp(l1 - m), jnp.exp(l2 - m)
    return ((e0 * o0 + e1 * o1 + e2 * o2) / (e0 + e1 + e2),)


def _gate_f(pids, g0, g1, g2, ya, yb, yc, b0, b1, b2):
    return (_sigmoid(g0 + b0) * ya + _sigmoid(g1 + b1) * yb + _sigmoid(g2 + b2) * yc,)


def _attn_window(i, L, tq, W):
    k0 = jnp.clip(i * tq - DA_RADIUS, 0, L - W)
    return pl.multiple_of(k0, DA_RADIUS)


def _strided_view(t, B, dil):
    T, HD = t.shape
    return t.reshape(B, T // B // dil, dil * HD)


def _attn_fwd(name, q, k, v, B, dil):
    T, HD = q.shape
    NS, L = B * dil, T // B // dil
    tq = min(DA_TQ, L)
    W = min(L, tq + 2 * DA_RADIUS)

    def body(q_ref, k_ref, v_ref, o_ref, l_ref):
        i = pl.program_id(1)
        k0 = _attn_window(i, L, tq, W)
        for h in range(DA_HEADS):
            hs = slice(h * DA_DIM, (h + 1) * DA_DIM)
            o, lse = _attn_head(q_ref[:, hs], k_ref[pl.ds(k0, W), hs], v_ref[pl.ds(k0, W), hs], i * tq, k0)
            o_ref[:, hs] = o
            l_ref[:, hs] = lse

    qs = pl.BlockSpec((None, tq, HD), lambda s, i: (s // dil, i, s % dil))
    ks = pl.BlockSpec((None, L, HD), lambda s, i: (s // dil, 0, s % dil))
    o, lse = pl.pallas_call(
        body, name=name, grid=(NS, L // tq), in_specs=[qs, ks, ks], out_specs=[qs, qs],
        out_shape=[jax.ShapeDtypeStruct((B, L, dil * HD), F32)] * 2, compiler_params=_params(),
    )(*[_strided_view(t, B, dil) for t in (q, k, v)])
    return o.reshape(T, HD), lse.reshape(T, HD)


def _attn_bwd(name, q, k, v, do, dl, B, dil):
    T, HD = q.shape
    NS, L = B * dil, T // B // dil
    tq = min(DA_TQ, L)
    W = min(L, tq + 2 * DA_RADIUS)

    def body(q_ref, k_ref, v_ref, do_ref, dl_ref, dq_ref, dk_ref, dv_ref):
        i = pl.program_id(1)
        k0 = _attn_window(i, L, tq, W)

        @pl.when(i == 0)
        def _():
            dk_ref[...] = jnp.zeros_like(dk_ref)
            dv_ref[...] = jnp.zeros_like(dv_ref)

        for h in range(DA_HEADS):
            hs = slice(h * DA_DIM, (h + 1) * DA_DIM)
            f = functools.partial(_attn_head, qpos0=i * tq, kpos0=k0)
            _, vjp = jax.vjp(f, q_ref[:, hs].astype(F32), k_ref[pl.ds(k0, W), hs].astype(F32),
                             v_ref[pl.ds(k0, W), hs].astype(F32))
            dq, dk, dv = vjp((do_ref[:, hs], dl_ref[:, hs]))
            dq_ref[:, hs] = dq
            dk_ref[pl.ds(k0, W), hs] += dk
            dv_ref[pl.ds(k0, W), hs] += dv

    qs = pl.BlockSpec((None, tq, HD), lambda s, i: (s // dil, i, s % dil))
    ks = pl.BlockSpec((None, L, HD), lambda s, i: (s // dil, 0, s % dil))
    res = pl.pallas_call(
        body, name=name, grid=(NS, L // tq), in_specs=[qs, ks, ks, qs, qs], out_specs=[qs, ks, ks],
        out_shape=[jax.ShapeDtypeStruct((B, L, dil * HD), F32)] * 3, compiler_params=_params(),
    )(*[_strided_view(t, B, dil) for t in (q, k, v, do, dl)])
    return tuple(t.reshape(T, HD) for t in res)


def _scan_chunk(t, rev, N):
    c = jnp.where(rev, N - 1 - t, t)
    per = DN_SUPER // DN_CHUNK
    return c, pl.multiple_of(c * DN_CHUNK, DN_CHUNK), pl.multiple_of((c % per) * DN_CHUNK, DN_CHUNK), \
        pl.multiple_of((c // per) * DN_SUPER, DN_SUPER)


def _dn_scan_fwd(name, uw, qk, qd, kd, gl, B):
    R, T, _ = uw.shape
    S = T // B
    N = S // DN_CHUNK
    C, DK = DN_CHUNK, DN_DIM

    PAIR = 2

    def body(uw_ref, qk_ref, qd_ref, kd_ref, gl_ref, o_ref, st_ref, vn_ref):
        rev = pl.program_id(1) * PAIR >= DN_HEADS
        vn_ref[...] = jnp.zeros_like(vn_ref)

        def step(t, states):
            c, r0, w0, s0 = _scan_chunk(t, rev, N)
            rows = pl.ds(r0, C)
            new = []
            for p, state in enumerate(states):
                st_ref[p, c] = state
                vnew = uw_ref[p, rows, 0:DK] - _dotp(uw_ref[p, rows, DK:2 * DK], state, "nn", 1)
                vn_ref[p, pl.ds(w0, C), :] = vnew
                o_ref[p, rows, :] = (_dotp(qd_ref[p, rows, :], state, "nn", 1)
                                     + _dotp(qk_ref[p, rows, :], vn_ref[p], "nn", 1))
                new.append(state * gl_ref[p, pl.ds(r0, 1), :] + _dotp(kd_ref[p, rows, :], vnew, "tn", 1))
            return tuple(new)

        lax.fori_loop(0, N, step, tuple(jnp.zeros((DK, DK), F32) for _ in range(PAIR)))

    def seq(w):
        return pl.BlockSpec((PAIR, S, w), lambda b, r: (r, b, 0))

    return pl.pallas_call(
        body, name=name, grid=(B, R // PAIR),
        in_specs=[seq(2 * DK), seq(DN_SUPER), seq(DK), seq(DK), seq(DK)],
        out_specs=[seq(DK), pl.BlockSpec((None, PAIR, N, DK, DK), lambda b, r: (b, r, 0, 0, 0))],
        out_shape=[jax.ShapeDtypeStruct((R, T, DK), F32), jax.ShapeDtypeStruct((B, R, N, DK, DK), F32)],
        scratch_shapes=[pltpu.VMEM((PAIR, DN_SUPER, DK), F32)], compiler_params=_params(),
    )(uw, qk, qd, kd, gl)


def _dn_scan_bwd(name, uw, qk, qd, kd, gl, st, do, B):
    R, T, _ = uw.shape
    S = T // B
    N = S // DN_CHUNK
    C, DK = DN_CHUNK, DN_DIM

    def body(uw_ref, qk_ref, qd_ref, kd_ref, gl_ref, st_ref, do_ref, duw_ref, dqk_ref, dqd_ref, dkd_ref, dgl_ref,
             vn_ref, tmp_ref):
        rev = pl.program_id(1) >= DN_HEADS
        vn_ref[...] = jnp.zeros_like(vn_ref)
        dgl_ref[...] = jnp.zeros_like(dgl_ref)

        def step(t, dstate):
            c, r0, w0, s0 = _scan_chunk(N - 1 - t, rev, N)
            rows = pl.ds(r0, C)
            state = st_ref[c]
            w = uw_ref[rows, DK:2 * DK]
            vnew = uw_ref[rows, 0:DK] - _dotp(w, state, "nn", 1)
            vn_ref[pl.ds(w0, C), :] = vnew
            do_c = do_ref[rows, :]
            tmp_ref[...] = _dotp(qk_ref[rows, :], do_c, "tn", 1)
            dvn = tmp_ref[pl.ds(w0, C), :] + _dotp(kd_ref[rows, :], dstate, "nn", 1)
            dqk_ref[rows, :] = _dotp(do_c, vn_ref[...], "nt", 1)
            dqd_ref[rows, :] = _dotp(do_c, state, "nt", 1)
            dkd_ref[rows, :] = _dotp(vnew, dstate, "nt", 1)
            dgl_ref[pl.ds(r0, 1), :] = jnp.sum(state * dstate, axis=0, keepdims=True)
            duw_ref[rows, 0:DK] = dvn
            duw_ref[rows, DK:2 * DK] = -_dotp(dvn, state, "nt", 1)
            return (_dotp(qd_ref[rows, :], do_c, "tn", 1) + dstate * gl_ref[pl.ds(r0, 1), :]
                    - _dotp(w, dvn, "tn", 1))

        lax.fori_loop(0, N, step, jnp.zeros((DK, DK), F32))

    def seq(w):
        return pl.BlockSpec((None, S, w), lambda b, r: (r, b, 0))

    return pl.pallas_call(
        body, name=name, grid=(B, R),
        in_specs=[seq(2 * DK), seq(DN_SUPER), seq(DK), seq(DK), seq(DK),
                  pl.BlockSpec((None, None, N, DK, DK), lambda b, r: (b, r, 0, 0, 0)),
                  pl.BlockSpec((None, S, DK), lambda b, r: (r % DN_HEADS, b, 0))],
        out_specs=[seq(2 * DK), seq(DN_SUPER), seq(DK), seq(DK), seq(DK)],
        out_shape=[jax.ShapeDtypeStruct((R, T, 2 * DK), F32), jax.ShapeDtypeStruct((R, T, DN_SUPER), F32),
                   jax.ShapeDtypeStruct((R, T, DK), F32), jax.ShapeDtypeStruct((R, T, DK), F32),
                   jax.ShapeDtypeStruct((R, T, DK), F32)],
        scratch_shapes=[pltpu.VMEM((DN_SUPER, DK), F32), pltpu.VMEM((DN_SUPER, DK), F32)],
        compiler_params=_params(),
    )(uw, qk, qd, kd, gl, st, do)


def _loss_fwd_bwd(name, x, gain, target):
    T, D = x.shape
    tm = _pick(T, 512, 8)

    def body(x_ref, g_ref, t_ref, loss_ref, dx_ref, dg_ref):
        i = pl.program_id(0)

        def f(xv, gv):
            e = _rms(xv, gv) - t_ref[...]
            return 0.5 * jnp.sum(jnp.mean(e * e, axis=-1, keepdims=True))

        val, (dx, dg) = jax.value_and_grad(f, argnums=(0, 1))(x_ref[...], g_ref[...])
        dx_ref[...] = dx
        part = jnp.full(loss_ref.shape, val, F32)

        @pl.when(i == 0)
        def _():
            dg_ref[...] = dg
            loss_ref[...] = part

        @pl.when(i != 0)
        def _():
            dg_ref[...] += dg
            loss_ref[...] += part

    return pl.pallas_call(
        body, name=name, grid=(T // tm,),
        in_specs=[pl.BlockSpec((tm, D), lambda i: (i, 0)), pl.BlockSpec((1, D), lambda i: (0, 0)),
                  pl.BlockSpec((tm, D), lambda i: (i, 0))],
        out_specs=[pl.BlockSpec((8, 128), lambda i: (0, 0)), pl.BlockSpec((tm, D), lambda i: (i, 0)),
                   pl.BlockSpec((1, D), lambda i: (0, 0))],
        out_shape=[jax.ShapeDtypeStruct((8, 128), F32), jax.ShapeDtypeStruct((T, D), F32),
                   jax.ShapeDtypeStruct((1, D), F32)],
        compiler_params=_params(),
    )(x, gain, target)


class _Cols:
    def __init__(self, D):
        assert D % 256 == 0
        self.gate = 0
        self.da = 3 * D
        self.qkv = self.da + 3 * DA_WIDTH
        self.z = self.qkv + 3 * DN_WIDTH
        self.pool = self.z + DN_WIDTH
        self.ba = self.pool + POOL_WIDTH
        self.total = self.ba + BA_PAD


def _rope_tables(S):
    half = DA_DIM // 2
    inv_freq = ROPE_THETA ** (-jnp.arange(half, dtype=F32) / half)
    ang = jnp.arange(S, dtype=F32)[:, None] * inv_freq[None, :]
    reps = DA_OUT // DA_DIM
    cos = jnp.tile(jnp.concatenate([jnp.cos(ang), jnp.cos(ang)], axis=1), (1, reps))
    sin = jnp.tile(jnp.concatenate([jnp.sin(ang), jnp.sin(ang)], axis=1), (1, reps))
    return cos, sin


def _to_strided(t, B, dil):
    T, w = t.shape
    L = T // B // dil
    return t.reshape(B, L, dil, w).transpose(0, 2, 1, 3).reshape(B * dil, L, w)


def _from_strided(t, B, dil):
    NS, L, w = t.shape
    return t.reshape(B, dil, L, w).transpose(0, 2, 1, 3).reshape(B * dil * L, w)


def _mixer(l, x1, w, B, host_gather=None):
    T, D = x1.shape
    S = T // B
    c = _Cols(D)
    tm = _pick(S, 512, 8)
    nmS = S // tm
    n = f"l{l}_"

    norm_ins = [_In(x1, (tm, D), lambda i: (i, 0)), _In(w["mix_norm"], (1, D), lambda i: (0, 0), acc=True)]
    norm_outs = [_Out((T, D), BF16, (tm, D), lambda i: (i, 0))]
    (h,) = _tile_fwd(n + "norm", _norm_f, (T // tm,), norm_ins, norm_outs)
    P = _mm(n + "proj", h, w["w_cat"], "nn", tn=2688, tk=1024)
    baT = P[:, c.ba:c.ba + 16].T

    cb = c.qkv // DN_DIM
    conv_ins = [_In(P, (S, DN_DIM), lambda j, b: (b, cb + j), g=((T, 3 * DN_WIDTH), lambda j, b: (b, j)), gdtype=BF16),
                _In(w["dn_conv"], (DN_CONV, DN_DIM), lambda j, b: (0, j), acc=True)]
    conv_outs = [_Out((T, 3 * DN_WIDTH), F32, (S, DN_DIM), lambda j, b: (b, j))]
    conv_grid = (3 * DN_HEADS, B)
    (qkvc,) = _tile_fwd(n + "dnconv", _dn_conv_f, conv_grid, conv_ins, conv_outs)

    tg = _pick(T, 2048, 128)
    gate_ins = [_In(baT, (8, tg), lambda i: (0, i)), _In(baT, (8, tg), lambda i: (1, i)),
                _In(w["dn_a_log"], (8, 1), lambda i: (0, 0), acc=True),
                _In(w["dn_dt_bias"], (8, 1), lambda i: (0, 0), acc=True)]
    gate_ins[0].g = ((8, T), lambda i: (0, i))
    gate_ins[1].g = ((8, T), lambda i: (0, i))
    gate_outs = [_Out((8, T), F32, (8, tg), lambda i: (0, i))] * 2
    beta, gdec = _tile_fwd(n + "dngate", _dn_gate_f, (T // tg,), gate_ins, gate_outs)

    NSC = T // DN_SUPER
    beta4 = beta.reshape(2, DN_HEADS, NSC, 1, DN_SUPER)
    gdec4 = gdec.reshape(2, DN_HEADS, NSC, 1, DN_SUPER)
    R = 2 * DN_HEADS

    def qkv_in(off):
        return _In(qkvc, (DN_SUPER, DN_DIM), lambda hh, m: (m, off + hh), acc=True,
                   g=((T, DN_WIDTH), lambda hh, m: (m, hh)))

    def row_in(a):
        return _In(a, (2, None, None, 1, DN_SUPER), lambda hh, m: (0, hh, m, 0, 0), split=True)

    def chain_out(wd):
        return _Out((2, DN_HEADS, T, wd), F32, (2, None, DN_SUPER, wd), lambda hh, m: (0, hh, m, 0), split=True)

    prep_ins = [qkv_in(0), qkv_in(DN_HEADS), qkv_in(2 * DN_HEADS), row_in(beta4), row_in(gdec4)]
    prep_outs = [chain_out(2 * DN_DIM), chain_out(DN_SUPER), chain_out(DN_DIM), chain_out(DN_DIM), chain_out(DN_DIM)]
    prep_grid = (DN_HEADS, NSC)
    prep_res = _tile_fwd(n + "dnprep", _dn_prep_f, prep_grid, prep_ins, prep_outs, sub=2, comm=host_gather)
    gathered_next = None
    if host_gather is not None:
        prep_res, gathered_next = prep_res
    uw, qk, qd, kd, gl = (t.reshape((R,) + t.shape[2:]) for t in prep_res)
    o_dn, states = _dn_scan_fwd(n + "dnscan", uw, qk, qd, kd, gl, B)

    zb = c.z // DN_DIM
    out_ins = [_In(o_dn, (None, S, DN_DIM), lambda b, hh: (hh, b, 0)),
               _In(o_dn, (None, S, DN_DIM), lambda b, hh: (DN_HEADS + hh, b, 0)),
               _In(P, (S, DN_DIM), lambda b, hh: (b, zb + hh), g=((T, DN_WIDTH), lambda b, hh: (b, hh)), gdtype=BF16),
               _In(w["dn_out_norm"], (1, DN_DIM), lambda b, hh: (0, 0), acc=True)]
    out_ins[0].g = ((DN_HEADS, T, DN_DIM), lambda b, hh: (hh, b, 0))
    out_ins[1].g = ((DN_HEADS, T, DN_DIM), lambda b, hh: (hh, b, 0))
    out_outs = [_Out((T, DN_WIDTH), BF16, (S, DN_DIM), lambda b, hh: (b, hh))]
    (ya_in,) = _tile_fwd(n + "dnout", _dn_out_f, (B, DN_HEADS), out_ins, out_outs)

    pb = c.pool // POOL_DIM
    pool_ins = [_In(P, (S, POOL_DIM), lambda gi, b: (b, pb + gi), g=((T, POOL_WIDTH), lambda gi, b: (b, gi)), gdtype=BF16),
                _In(w["pool_w"], (None, POOL_DIM, POOL_DIM), lambda gi, b: (gi, 0, 0), acc=True),
                _In(w["pool_scale"], (None, 1, POOL_DIM), lambda gi, b: (gi, 0, 0), acc=True)]
    pool_outs = [_Out((T, POOL_WIDTH), BF16, (S, POOL_DIM), lambda gi, b: (b, gi))]
    (yb_in,) = _tile_fwd(n + "pool", _pool_f, (POOL_GROUPS, B), pool_ins, pool_outs)

    cos, sin = _rope_tables(S)
    db = c.da // DA_OUT

    def da_in(k):
        return _In(P, (tm, DA_OUT), lambda i: (i, db + k), g=((T, DA_OUT), lambda i: (i, 0)), gdtype=BF16)

    rope_ins = [da_in(k) for k in range(3 * DA_GROUPS)]
    rope_ins += [_In(cos, (tm, DA_OUT), lambda i: (i % nmS, 0), kind="c"),
                 _In(sin, (tm, DA_OUT), lambda i: (i % nmS, 0), kind="c")]
    rope_outs = [_Out((T, DA_OUT), BF16, (tm, DA_OUT), lambda i: (i, 0))] * (3 * DA_GROUPS)
    roped = _tile_fwd(n + "rope", _rope_f, (T // tm,), rope_ins, rope_outs)
    strided = []
    o_g, l_g = [], []
    for gi, dil in enumerate(DA_DILATIONS):
        qs, ks, vs = roped[gi], roped[DA_GROUPS + gi], roped[2 * DA_GROUPS + gi]
        strided.append((qs, ks, vs))
        o, lse = _attn_fwd(n + f"attn{gi}", qs, ks, vs, B, dil)
        o_g.append(o)
        l_g.append(lse)
    mrg_ins = [_In(a, (tm, DA_OUT), lambda i: (i, 0)) for a in o_g + l_g]
    mrg_outs = [_Out((T, DA_OUT), BF16, (tm, DA_OUT), lambda i: (i, 0))]
    (yc_in,) = _tile_fwd(n + "merge", _merge_f, (T // tm,), mrg_ins, mrg_outs)

    ya = _mm(n + "pa", ya_in, w["w_proj_a"], "nn")
    yb = _mm(n + "pb", yb_in, w["w_proj_b"], "nn")
    yc = _mm(n + "pc", yc_in, w["w_proj_c"], "nn")

    def gcol(k):
        return _In(P, (tm, D), lambda i: (i, k), g=((T, D), lambda i: (i, 0)), gdtype=BF16)

    def yin(a):
        return _In(a, (tm, D), lambda i: (i, 0), gdtype=BF16)

    def bin_(k):
        return _In(w["b_gate"][k:k + 1], (1, D), lambda i: (0, 0), acc=True)

    gm_ins = [gcol(0), gcol(1), gcol(2), yin(ya), yin(yb), yin(yc), bin_(0), bin_(1), bin_(2)]
    gm_outs = [_Out((T, D), BF16, (tm, D), lambda i: (i, 0))]
    (merged,) = _tile_fwd(n + "gates", _gate_f, (T // tm,), gm_ins, gm_outs)
    x2 = _mm(n + "out", merged, w["w_out"], "nn", add=x1)

    def backward(dx2, host_exchange=None):
        return _mixer_bwd(dx2, host_exchange, **{k: v for k, v in locals_.items() if k in _MIXER_BWD_NEEDS})

    locals_ = dict(locals())
    return x2, backward, gathered_next


_MIXER_BWD_NEEDS = ("n", "B", "T", "D", "tm", "w", "h", "merged", "gm_ins", "gm_outs", "ya_in", "yb_in", "yc_in",
                    "mrg_ins", "mrg_outs", "strided", "rope_ins", "rope_outs", "pool_ins", "pool_outs", "out_ins",
                    "out_outs", "uw", "qk", "qd", "kd", "gl", "states", "prep_grid", "prep_ins", "prep_outs", "tg",
                    "gate_ins", "gate_outs", "conv_grid", "conv_ins", "conv_outs", "norm_ins", "norm_outs")


def _mixer_bwd(dx2, host_exchange, *, n, B, T, D, tm, w, h, merged, gm_ins, gm_outs, ya_in, yb_in, yc_in, mrg_ins, mrg_outs, strided,
               rope_ins, rope_outs, pool_ins, pool_outs, out_ins, out_outs, uw, qk, qd, kd, gl, states, prep_grid,
               prep_ins, prep_outs, tg, gate_ins, gate_outs, conv_grid, conv_ins, conv_outs, norm_ins, norm_outs):
    g = {}
    dmerged = _mm(n + "d_merged", dx2, w["w_out"], "nt")
    g["w_out"] = _mm(n + "d_wout", merged, dx2, "tn", tm=1024, tn=1024, tk=1024)
    dg0, dg1, dg2, dya, dyb, dyc, db0, db1, db2 = _tile_bwd(
        n + "gates_b", _gate_f, (T // tm,), gm_ins, gm_outs, [dmerged], acc_from=0)
    g["b_gate"] = jnp.concatenate([db0, db1, db2], axis=0)
    dya_in = _mm(n + "d_pa", dya, w["w_proj_a"], "nt")
    dyb_in = _mm(n + "d_pb", dyb, w["w_proj_b"], "nt")
    dyc_in = _mm(n + "d_pc", dyc, w["w_proj_c"], "nt")
    g["w_proj_a"] = _mm(n + "d_wpa", ya_in, dya, "tn", tn=1024, tk=2048)
    g["w_proj_b"] = _mm(n + "d_wpb", yb_in, dyb, "tn", tn=1024, tk=2048)
    g["w_proj_c"] = _mm(n + "d_wpc", yc_in, dyc, "tn", tn=1024, tk=2048)

    dmrg = _tile_bwd(n + "merge_b", _merge_f, (T // tm,), mrg_ins, mrg_outs, [dyc_in])
    dq_parts, dk_parts, dv_parts = [], [], []
    for gi, dil in enumerate(DA_DILATIONS):
        qs, ks, vs = strided[gi]
        dq, dk, dv = _attn_bwd(n + f"attn{gi}_b", qs, ks, vs, dmrg[gi], dmrg[DA_GROUPS + gi], B, dil)
        dq_parts.append(dq)
        dk_parts.append(dk)
        dv_parts.append(dv)
    dP_da = _tile_bwd(n + "rope_b", _rope_f, (T // tm,), rope_ins, rope_outs, dq_parts + dk_parts + dv_parts)

    dPpool, g["pool_w"], g["pool_scale"] = _tile_bwd(
        n + "pool_b", _pool_f, (POOL_GROUPS, B), pool_ins, pool_outs, [dyb_in], acc_from=1)

    dof, dob, dPz, g["dn_out_norm"] = _tile_bwd(
        n + "dnout_b", _dn_out_f, (B, DN_HEADS), out_ins, out_outs, [dya_in], acc_from=0)
    del dob
    duw, dqk, dqd, dkd, dgl = _dn_scan_bwd(n + "dnscan_b", uw, qk, qd, kd, gl, states, dof, B)
    prep_cts = [t.reshape((2, DN_HEADS) + t.shape[1:]) for t in (duw, dqk, dqd, dkd, dgl)]
    prep_res = _tile_bwd(n + "dnprep_b", _dn_prep_f, prep_grid, prep_ins, prep_outs, prep_cts, sub=2,
                         comm=host_exchange)
    exchanged = None
    if host_exchange is not None:
        prep_res, exchanged = prep_res
    dq_, dk_, dv_, dbeta4, dgdec4 = prep_res
    dqkvc = jnp.concatenate([dq_, dk_, dv_], axis=1)
    dbraw, daraw, g["dn_a_log"], g["dn_dt_bias"] = _tile_bwd(
        n + "dngate_b", _dn_gate_f, (T // tg,), gate_ins, gate_outs,
        [dbeta4.reshape(8, T), dgdec4.reshape(8, T)], acc_from=0)
    dPqkv, g["dn_conv"] = _tile_bwd(n + "dnconv_b", _dn_conv_f, conv_grid, conv_ins, conv_outs, [dqkvc], acc_from=1)
    dba = jnp.concatenate([dbraw, daraw], axis=0).T.astype(BF16)
    dba = jnp.pad(dba, ((0, 0), (0, BA_PAD - 16)))
    dP = jnp.concatenate([dg0, dg1, dg2, *dP_da, dPqkv, dPz, dPpool, dba], axis=1)
    dh = _mm(n + "d_h", dP, w["w_cat"], "nt", tn=1024, tk=2688)
    g["w_cat"] = _mm(n + "d_wcat", h, dP, "tn", tm=1024, tn=896, tk=1024)
    dx1, g["mix_norm"] = _tile_bwd(n + "norm_b", _norm_f, (T // tm,), norm_ins, norm_outs, [dh], acc_from=0,
                                   addends={0: dx2})
    return dx1, g, exchanged


def _layer_weights(full, l, D):
    c = _Cols(D)
    w_in = full["w_in"][l]
    o_z, o_ba, o_pool, o_da = 3 * DN_WIDTH, 4 * DN_WIDTH, 4 * DN_WIDTH + 16, 4 * DN_WIDTH + 16 + POOL_WIDTH
    w_cat = jnp.concatenate(
        [full["w_gate"][l], w_in[:, o_da:], w_in[:, :o_z], w_in[:, o_z:o_ba], w_in[:, o_pool:o_da], w_in[:, o_ba:o_pool],
         jnp.zeros((D, BA_PAD - 16), w_in.dtype)], axis=1).astype(BF16)
    assert w_cat.shape[1] == c.total
    w = {k: full[k][l].astype(BF16) for k in ("ffn1_w_gate", "ffn1_w_up", "ffn1_w_down", "ffn2_w_gate", "ffn2_w_up",
                                              "ffn2_w_down", "w_proj_a", "w_proj_b", "w_proj_c", "w_out")}
    w["w_cat"] = w_cat
    f_true = w["ffn1_w_gate"].shape[1]
    pad = -f_true % FFN_BWD_TF
    for k in ("ffn1", "ffn2"):
        w[k + "_bwd"] = (jnp.pad(w[k + "_w_gate"], ((0, 0), (0, pad))), jnp.pad(w[k + "_w_up"], ((0, 0), (0, pad))),
                         jnp.pad(w[k + "_w_down"], ((0, pad), (0, 0))), f_true)
    w["ffn1_norm"] = full["ffn1_norm"][l][None].astype(F32)
    w["ffn2_norm"] = full["ffn2_norm"][l][None].astype(F32)
    w["mix_norm"] = full["mix_norm"][l][None].astype(F32)
    w["dn_conv"] = full["dn_conv"][l].astype(F32)
    w["dn_a_log"] = full["dn_a_log"][l].reshape(2 * DN_HEADS, 1).astype(F32)
    w["dn_dt_bias"] = full["dn_dt_bias"][l].reshape(2 * DN_HEADS, 1).astype(F32)
    w["dn_out_norm"] = full["dn_out_norm"][l][None].astype(F32)
    w["pool_w"] = full["pool_w"][l].astype(F32)
    w["pool_scale"] = full["pool_scale"][l].reshape(POOL_GROUPS, 1, POOL_DIM).astype(F32)
    w["b_gate"] = full["b_gate"][l].reshape(3, D).astype(F32)
    return w


def _layer_grads(g, D):
    c = _Cols(D)
    gc = g.pop("w_cat")
    out = dict(g)
    out["w_gate"] = gc[:, :c.da]
    out["w_in"] = jnp.concatenate([gc[:, c.qkv:c.pool], gc[:, c.ba:c.ba + 16], gc[:, c.pool:c.ba], gc[:, c.da:c.qkv]],
                                  axis=1)
    for k in ("ffn1_norm", "ffn2_norm", "mix_norm", "dn_out_norm"):
        out[k] = g[k][0]
    out["dn_a_log"] = g["dn_a_log"].reshape(2, DN_HEADS)
    out["dn_dt_bias"] = g["dn_dt_bias"].reshape(2, DN_HEADS)
    out["pool_scale"] = g["pool_scale"].reshape(POOL_WIDTH)
    out["b_gate"] = g["b_gate"].reshape(3 * D)
    return out


def _unshard(got):
    full = {}
    for k, t in zip(SHARDED, got):
        ax = SHARD_AXIS[k] - 1
        shp = t.shape[1:]
        full[k] = jnp.moveaxis(t, 0, ax).reshape(shp[:ax] + (N_DEV * shp[ax],) + shp[ax + 1:])
    return full


def _to_owner_blocks(grads):
    out = []
    for k in SHARDED:
        ax = SHARD_AXIS[k] - 1
        shp = grads[k].shape
        t = grads[k].reshape(shp[:ax] + (N_DEV, shp[ax] // N_DEV) + shp[ax + 1:])
        out.append(jnp.moveaxis(t, ax, 0).astype(BF16))
    return out


def _local_step(x, target, rep, shards, distributed):
    B, S, D = x.shape
    T = B * S
    depth = len(shards)
    xs = x.reshape(T, D)
    tape = []
    if distributed:
        sharded_now = _unshard(_all_gather("gather_l0", [shards[0][k] for k in SHARDED]))
    else:
        sharded_now = shards[0]
    for l in range(depth):
        full = {k: [v] * (l + 1) for k, v in sharded_now.items()}
        full.update({k: v for k, v in rep.items() if k != "final_norm"})
        w = _layer_weights(full, l, D)
        host = ("gather", [shards[l + 1][k] for k in SHARDED]) if distributed and l + 1 < depth else None
        x1 = _ffn_fwd(f"l{l}_ffn1", xs, w["ffn1_norm"], w["ffn1_w_gate"], w["ffn1_w_up"], w["ffn1_w_down"])
        x2, mixer_bwd, got = _mixer(l, x1, w, B, host)
        x3 = _ffn_fwd(f"l{l}_ffn2", x2, w["ffn2_norm"], w["ffn2_w_gate"], w["ffn2_w_up"], w["ffn2_w_down"])
        tape.append((w, xs, mixer_bwd, x2))
        xs = x3
        if l + 1 < depth:
            sharded_now = _unshard(got) if distributed else shards[l + 1]
    loss8, dx, dfinal = _loss_fwd_bwd("loss", xs, rep["final_norm"][None].astype(F32), target.reshape(T, D))
    per_layer = [None] * depth
    exchanged = [None] * depth
    pending = None
    for l in reversed(range(depth)):
        w, x0, mixer_bwd, x2 = tape[l]
        wg_p, wu_p, wd_p, f_true = w["ffn2_bwd"]
        dx, dn2, dwg2, dwu2, dwd2 = _ffn_bwd(f"l{l}_ffn2b", x2, w["ffn2_norm"], wg_p, wu_p, wd_p, dx, f_true)
        dx, g, got = mixer_bwd(dx, ("exchange", pending) if pending is not None else None)
        if pending is not None:
            exchanged[l + 1] = got
        wg_p, wu_p, wd_p, f_true = w["ffn1_bwd"]
        dx, dn1, dwg1, dwu1, dwd1 = _ffn_bwd(f"l{l}_ffn1b", x0, w["ffn1_norm"], wg_p, wu_p, wd_p, dx, f_true)
        g.update(ffn1_norm=dn1, ffn1_w_gate=dwg1, ffn1_w_up=dwu1, ffn1_w_down=dwd1,
                 ffn2_norm=dn2, ffn2_w_gate=dwg2, ffn2_w_up=dwu2, ffn2_w_down=dwd2)
        per_layer[l] = _layer_grads(g, D)
        if distributed:
            pending = _to_owner_blocks(per_layer[l])
    return loss8[0, 0], dx.reshape(B, S, D), per_layer, dfinal[0], exchanged, pending


def _mesh_position():
    mx, my, mc = lax.axis_index("x"), lax.axis_index("y"), lax.axis_index("c")
    return mx, my, mc, 4 * mx + 2 * my + mc


def _peers(mx, my, mc):
    out = []
    for k in range(1, N_DEV):
        px, py, pc = mx ^ ((k >> 2) & 1), my ^ ((k >> 1) & 1), mc ^ (k & 1)
        out.append(((px, py, pc), 4 * px + 2 * py + pc))
    return out


_ANY = pl.BlockSpec(memory_space=pl.ANY)


def _all_gather(name, xs):
    n = len(xs)

    def body(*refs):
        _gather_start(refs[:n], refs[n:2 * n], *refs[2 * n:])
        _gather_finish(refs[:n], refs[n:2 * n], *refs[2 * n:])

    sems = pltpu.SemaphoreType.DMA((n, N_DEV - 1))
    return pl.pallas_call(
        body, name=name, in_specs=[_ANY] * n, out_specs=[_ANY] * n,
        out_shape=[jax.ShapeDtypeStruct((N_DEV,) + x.shape, x.dtype) for x in xs],
        scratch_shapes=[sems, sems, pltpu.SemaphoreType.DMA((n,))],
    )(*xs)


class _GatherPlan:
    def __init__(self, x_refs, o_refs, send_sems, recv_sems, local_sems):
        self.x, self.o, self.ss, self.rs, self.ls = x_refs, o_refs, send_sems, recv_sems, local_sems
        self.mx, self.my, self.mc, self.me = _mesh_position()
        self.self_id = (self.mx, self.my, self.mc)
        self.sibling = (self.mx, self.my, 1 - self.mc)
        self.chips = [(1 - self.mx, self.my), (self.mx, 1 - self.my), (1 - self.mx, 1 - self.my)]

    def copy(self, a, k, blk, to, from_input=False):
        dst = self.o[a].at[blk]
        return pltpu.make_async_remote_copy(src_ref=self.x[a] if from_input else dst, dst_ref=dst,
                                            send_sem=self.ss.at[a, k], recv_sem=self.rs.at[a, k],
                                            device_id=to, device_id_type=pl.DeviceIdType.MESH)

    def own(self, a):
        return pltpu.make_async_copy(self.x[a], self.o[a].at[self.me], self.ls.at[a])

    def first_sends(self, a):
        cps = [self.copy(a, 0, self.me, self.sibling, from_input=True)]
        return cps + [self.copy(a, 1 + j, self.me, (*chip, self.mc), from_input=True) for j, chip in enumerate(self.chips)]

    def passed_on(self, a, j):
        cx, cy = self.chips[j]
        return self.copy(a, 4 + j, 4 * cx + 2 * cy + self.mc, self.sibling)


def _gather_start(x_refs, o_refs, send_sems, recv_sems, local_sems):
    p = _GatherPlan(x_refs, o_refs, send_sems, recv_sems, local_sems)
    for a in range(len(x_refs)):
        p.own(a).start()
        for cp in p.first_sends(a):
            cp.start()


def _gather_finish(x_refs, o_refs, send_sems, recv_sems, local_sems):
    p = _GatherPlan(x_refs, o_refs, send_sems, recv_sems, local_sems)
    n = len(x_refs)
    for a in range(n):
        for j, (cx, cy) in enumerate(p.chips):
            p.copy(a, 1 + j, 4 * cx + 2 * cy + p.mc, p.self_id).wait_recv()
            p.passed_on(a, j).start()
    for a in range(n):
        p.copy(a, 0, 4 * p.mx + 2 * p.my + 1 - p.mc, p.self_id).wait_recv()
        for j, (cx, cy) in enumerate(p.chips):
            p.copy(a, 4 + j, 4 * cx + 2 * cy + 1 - p.mc, p.self_id).wait_recv()
    for a in range(n):
        for cp in p.first_sends(a):
            cp.wait_send()
        for j in range(len(p.chips)):
            p.passed_on(a, j).wait_send()
        p.own(a).wait()


def _exchange_grads(name, gs, gr):
    ns, n = len(gs), len(gs) + len(gr)

    def body(*refs):
        _exchange_start(refs[:n], refs[n:2 * n], *refs[2 * n:], n_sharded=ns)
        _exchange_finish(refs[:n], refs[n:2 * n], *refs[2 * n:], n_sharded=ns)

    sems = pltpu.SemaphoreType.DMA((n, N_DEV - 1))
    outs = pl.pallas_call(
        body, name=name, in_specs=[_ANY] * n, out_specs=[_ANY] * n,
        out_shape=[jax.ShapeDtypeStruct(a.shape, a.dtype) for a in gs]
        + [jax.ShapeDtypeStruct((N_DEV,) + a.shape, a.dtype) for a in gr],
        scratch_shapes=[sems, sems, pltpu.SemaphoreType.DMA((n,))],
    )(*gs, *gr)
    return outs[:ns], outs[ns:]


def _exchange_copies(in_refs, out_refs, send_sems, recv_sems, local_sems, n_sharded):
    mx, my, mc, me = _mesh_position()
    n = len(in_refs)
    own = [pltpu.make_async_copy(in_refs[a].at[me] if a < n_sharded else in_refs[a], out_refs[a].at[me],
                                 local_sems.at[a]) for a in range(n)]
    remote = []
    for k, (peer, pid) in enumerate(_peers(mx, my, mc)):
        for a in range(n):
            src = in_refs[a].at[pid] if a < n_sharded else in_refs[a]
            remote.append(pltpu.make_async_remote_copy(
                src_ref=src, dst_ref=out_refs[a].at[me], send_sem=send_sems.at[a, k], recv_sem=recv_sems.at[a, k],
                device_id=peer, device_id_type=pl.DeviceIdType.MESH))
    return own, remote


def _exchange_start(in_refs, out_refs, send_sems, recv_sems, local_sems, n_sharded=None):
    ns = len(in_refs) if n_sharded is None else n_sharded
    own, remote = _exchange_copies(in_refs, out_refs, send_sems, recv_sems, local_sems, ns)
    for cp in own + remote:
        cp.start()


def _exchange_finish(in_refs, out_refs, send_sems, recv_sems, local_sems, n_sharded=None):
    ns = len(in_refs) if n_sharded is None else n_sharded
    own, remote = _exchange_copies(in_refs, out_refs, send_sems, recv_sems, local_sems, ns)
    for cp in remote:
        cp.wait_send()
        cp.wait_recv()
    for cp in own:
        cp.wait()


def _reduce_adamw(name, parts, w, m, v):
    shape = w.shape
    cols = shape[-1]
    w2, m2, v2 = (t.reshape(-1, cols) for t in (w, m, v))
    p3 = parts.reshape(N_DEV, -1, cols)
    rows = w2.shape[0]
    tr = _pick(rows, 512, 16) if rows > 1024 else rows
    c1 = 1.0 - ADAM_B1 ** ADAM_STEP
    c2 = 1.0 - ADAM_B2 ** ADAM_STEP

    def body(p_ref, w_ref, m_ref, v_ref, g_ref, d_ref, nm_ref, nv_ref):
        gv = p_ref[0].astype(F32)
        for d in range(1, N_DEV):
            gv = gv + p_ref[d].astype(F32)
        nm = ADAM_B1 * m_ref[...] + (1.0 - ADAM_B1) * gv
        nv = ADAM_B2 * v_ref[...] + (1.0 - ADAM_B2) * (gv * gv)
        g_ref[...] = gv
        d_ref[...] = -ADAM_LR * ((nm / c1) / (jnp.sqrt(nv / c2) + ADAM_EPS) + ADAM_WD * w_ref[...])
        nm_ref[...] = nm
        nv_ref[...] = nv

    spec = pl.BlockSpec((tr, cols), lambda i: (i, 0))
    outs = pl.pallas_call(
        body, name=name, grid=(rows // tr,),
        in_specs=[pl.BlockSpec((N_DEV, tr, cols), lambda i: (0, i, 0))] + [spec] * 3, out_specs=[spec] * 4,
        out_shape=[jax.ShapeDtypeStruct((rows, cols), F32)] * 4, compiler_params=_params(),
    )(p3, w2, m2, v2)
    return tuple(o.reshape(shape) for o in outs)


def kernel(x, ffn1_norm, ffn1_w_gate, ffn1_w_up, ffn1_w_down, mix_norm, w_in, dn_conv, dn_a_log, dn_dt_bias, dn_out_norm, pool_w, pool_scale, w_proj_a, w_proj_b, w_proj_c, w_gate, b_gate, w_out, ffn2_norm, ffn2_w_gate, ffn2_w_up, ffn2_w_down, final_norm, loss_target, m_ffn1_norm, m_ffn1_w_gate, m_ffn1_w_up, m_ffn1_w_down, m_mix_norm, m_w_in, m_dn_conv, m_dn_a_log, m_dn_dt_bias, m_dn_out_norm, m_pool_w, m_pool_scale, m_w_proj_a, m_w_proj_b, m_w_proj_c, m_w_gate, m_b_gate, m_w_out, m_ffn2_norm, m_ffn2_w_gate, m_ffn2_w_up, m_ffn2_w_down, m_final_norm, v_ffn1_norm, v_ffn1_w_gate, v_ffn1_w_up, v_ffn1_w_down, v_mix_norm, v_w_in, v_dn_conv, v_dn_a_log, v_dn_dt_bias, v_dn_out_norm, v_pool_w, v_pool_scale, v_w_proj_a, v_w_proj_b, v_w_proj_c, v_w_gate, v_b_gate, v_w_out, v_ffn2_norm, v_ffn2_w_gate, v_ffn2_w_up, v_ffn2_w_down, v_final_norm):
    args = locals()
    wts = {k: args[k] for k in WEIGHTS}
    ms = {k: args["m_" + k] for k in WEIGHTS}
    vs = {k: args["v_" + k] for k in WEIGHTS}

    depth = w_in.shape[0]
    rep = {k: wts[k] for k in REPLICATED}
    shards = [{k: wts[k][l].astype(BF16) for k in SHARDED} for l in range(depth)]
    loss_local, dx, per_layer, dfinal, exchanged, pending = _local_step(x, loss_target, rep, shards, True)
    loss = lax.psum(loss_local, ("x", "y", "c"))

    gr = [dfinal if k == "final_norm" else jnp.stack([pg[k] for pg in per_layer]).astype(F32).reshape(wts[k].shape)
          for k in REPLICATED]
    exchanged[0], got_r = _exchange_grads("exchange_grads", pending, gr)
    parts = {k: jnp.stack([exchanged[l][j] for l in range(depth)], axis=1) for j, k in enumerate(SHARDED)}
    parts.update(zip(REPLICATED, got_r))

    g_final, deltas, new_m, new_v = {}, {}, {}, {}
    for k in WEIGHTS:
        g_final[k], deltas[k], new_m[k], new_v[k] = _reduce_adamw("adamw_" + k, parts[k], wts[k], ms[k], vs[k])
    return (loss, dx, *[g_final[k] for k in WEIGHTS], *[deltas[k] for k in WEIGHTS], *[new_m[k] for k in WEIGHTS],
            *[new_v[k] for k in WEIGHTS])
```

```python
import functools
import math

import jax
import jax.numpy as jnp
from jax import lax
from jax.experimental import pallas as pl
from jax.experimental.pallas import tpu as pltpu

F32 = jnp.float32
BF16 = jnp.bfloat16

N_DEV = 8
RMS_EPS = 1e-6
L2_EPS = 1e-6
DN_HEADS = 4
DN_DIM = 128
DN_WIDTH = DN_HEADS * DN_DIM
DN_CONV = 5
DN_CHUNK = 64
DN_SUPER = 256
POOL_GROUPS = 4
POOL_DIM = 128
POOL_WIDTH = POOL_GROUPS * POOL_DIM
POOL_MAX_HALF = 8
DA_GROUPS = 3
DA_HEADS = 4
DA_DIM = 64
DA_WIDTH = DA_GROUPS * DA_HEADS * DA_DIM
DA_OUT = DA_HEADS * DA_DIM
DA_DILATIONS = (1, 4, 16)
DA_RADIUS = 64
DA_TQ = 256
FFN_BWD_TF = 512
ROPE_THETA = 10000.0
MASK_VALUE = -1e30
BA_PAD = 128

ADAM_LR = 0.001
ADAM_B1 = 0.9
ADAM_B2 = 0.999
ADAM_EPS = 1e-08
ADAM_WD = 0.01
ADAM_STEP = 10

VMEM_LIMIT_V7X = 56 * 1024 * 1024
LANES = 1024

SHARDED = ("ffn1_w_gate", "ffn1_w_up", "ffn1_w_down", "w_in", "dn_conv", "w_proj_a", "w_proj_b", "w_proj_c",
           "w_gate", "w_out", "ffn2_w_gate", "ffn2_w_up", "ffn2_w_down")
SHARD_AXIS = {"ffn1_w_gate": 2, "ffn1_w_up": 2, "ffn1_w_down": 1, "w_in": 2, "dn_conv": 2, "w_proj_a": 2,
              "w_proj_b": 2, "w_proj_c": 2, "w_gate": 2, "w_out": 1, "ffn2_w_gate": 2, "ffn2_w_up": 2,
              "ffn2_w_down": 1}
REPLICATED = ("ffn1_norm", "mix_norm", "dn_a_log", "dn_dt_bias", "dn_out_norm", "pool_w", "pool_scale", "b_gate",
              "ffn2_norm", "final_norm")
WEIGHTS = ("ffn1_norm", "ffn1_w_gate", "ffn1_w_up", "ffn1_w_down", "mix_norm", "w_in", "dn_conv", "dn_a_log",
           "dn_dt_bias", "dn_out_norm", "pool_w", "pool_scale", "w_proj_a", "w_proj_b", "w_proj_c", "w_gate",
           "b_gate", "w_out", "ffn2_norm", "ffn2_w_gate", "ffn2_w_up", "ffn2_w_down", "final_norm")


def _params(**kw):
    return pltpu.CompilerParams(vmem_limit_bytes=VMEM_LIMIT_V7X, **kw)


def _pick(n, target, align):
    best = None
    t = align
    while t <= min(n, target):
        if n % t == 0:
            best = t
        t += align
    return best if best is not None else n


_DIMS = {"nn": (((1,), (0,)), ((), ())), "nt": (((1,), (1,)), ((), ())), "tn": (((0,), (0,)), ((), ()))}


def _dg(a, b, mode):
    return lax.dot_general(a, b, _DIMS[mode], preferred_element_type=F32)


def _split2(a):
    hi = a.astype(BF16)
    lo = (a - hi.astype(F32)).astype(BF16)
    return hi, lo


def _dotp(a, b, mode, passes):
    if passes == 1:
        return _dg(a.astype(BF16), b.astype(BF16), mode)
    ah, al = _split2(a.astype(F32))
    bh, bl = _split2(b.astype(F32))
    return _dg(ah, bh, mode) + (_dg(ah, bl, mode) + _dg(al, bh, mode))


@functools.partial(jax.custom_vjp, nondiff_argnums=(2, 3))
def _dot(a, b, mode, passes):
    return _dotp(a, b, mode, passes)


def _dot_fwd(a, b, mode, passes):
    return _dotp(a, b, mode, passes), (a, b)


def _dot_bwd(mode, passes, res, ct):
    a, b = res
    if mode == "nn":
        da, db = _dotp(ct, b, "nt", passes), _dotp(a, ct, "tn", passes)
    elif mode == "nt":
        da, db = _dotp(ct, b, "nn", passes), _dotp(ct, a, "tn", passes)
    else:
        da, db = _dotp(b, ct, "nt", passes), _dotp(a, ct, "nn", passes)
    return da.astype(a.dtype), db.astype(b.dtype)


_dot.defvjp(_dot_fwd, _dot_bwd)


def _split3(x):
    x1 = x.astype(BF16)
    r = x - x1.astype(F32)
    x2 = r.astype(BF16)
    x3 = (r - x2.astype(F32)).astype(BF16)
    return x1, x2, x3


def _mdotp(mask, x, mode):
    x1, x2, x3 = _split3(x)
    return _dg(mask, x1, mode) + (_dg(mask, x2, mode) + _dg(mask, x3, mode))


@jax.custom_vjp
def _mdot(mask, x):
    return _mdotp(mask, x, "nn")


def _mdot_fwd(mask, x):
    return _mdotp(mask, x, "nn"), mask


def _mdot_bwd(mask, ct):
    return jnp.zeros_like(mask), _mdotp(mask, ct, "tn")


_mdot.defvjp(_mdot_fwd, _mdot_bwd)


_SOLVE_SQUARINGS = int(math.log2(DN_CHUNK)) - 1


def _unit_solve_fwd(A, R):
    Ab = A.astype(BF16)
    n = A.shape[0]
    eye = (lax.broadcasted_iota(jnp.int32, A.shape, 0) == lax.broadcasted_iota(jnp.int32, A.shape, 1)).astype(F32)
    Tm = eye - A
    P = _dg(Ab, Ab, "nn").astype(BF16)
    for _ in range(_SOLVE_SQUARINGS - 1):
        M = _dg(P, jnp.concatenate([Tm.astype(BF16), P], axis=1), "nn")
        Tm, P = Tm + M[:, :n], M[:, n:].astype(BF16)
    Tb = (Tm + _dg(P, Tm.astype(BF16), "nn")).astype(BF16)
    X = _dg(Tb, R.astype(BF16), "nn")
    return X, (Tb, X)


@jax.custom_vjp
def _unit_solve(A, R):
    return _unit_solve_fwd(A, R)[0]


def _unit_solve_bwd(res, dX):
    Tb, X = res
    Y = _dg(Tb, dX.astype(BF16), "tn")
    return -_dg(Y.astype(BF16), X.astype(BF16), "nt"), Y


_unit_solve.defvjp(_unit_solve_fwd, _unit_solve_bwd)


def _shift_impl(x, o):
    if o == 0:
        return x
    n = x.shape[0]
    y = pltpu.roll(x, (-o) % n, axis=0)
    t = lax.broadcasted_iota(jnp.int32, x.shape, 0) + o
    return jnp.where((t >= 0) & (t < n), y, 0.0)


@functools.partial(jax.custom_vjp, nondiff_argnums=(1,))
def _shift(x, o):
    return _shift_impl(x, o)


def _shift_fwd(x, o):
    return _shift_impl(x, o), None


def _shift_bwd(o, _, ct):
    return (_shift_impl(ct, -o),)


_shift.defvjp(_shift_fwd, _shift_bwd)


def _rot_impl(x):
    w = x.shape[1]
    half = DA_DIM // 2
    lane = lax.broadcasted_iota(jnp.int32, x.shape, 1)
    first = (lane & (DA_DIM - 1)) < half
    return jnp.where(first, -pltpu.roll(x, w - half, axis=1), pltpu.roll(x, half, axis=1))


@jax.custom_vjp
def _rot(x):
    return _rot_impl(x)


def _rot_fwd(x):
    return _rot_impl(x), None


def _rot_bwd(_, ct):
    return (-_rot_impl(ct),)


_rot.defvjp(_rot_fwd, _rot_bwd)


def _sigmoid(x):
    return 1.0 / (1.0 + jnp.exp(-x))


def _silu(x):
    return x * _sigmoid(x)


def _softplus(x):
    return jnp.maximum(x, 0.0) + jnp.log(1.0 + jnp.exp(-jnp.abs(x)))


def _rms(x, gain):
    return x * lax.rsqrt(jnp.mean(x * x, axis=-1, keepdims=True) + RMS_EPS) * gain


class _In:
    def __init__(self, arr, block, imap, kind="t", acc=False, g=None, gdtype=None, split=False):
        self.arr, self.block, self.imap, self.kind, self.acc, self.g, self.gdtype = arr, block, imap, kind, acc, g, gdtype
        self.split = split


class _Out:
    def __init__(self, shape, dtype, block, imap, split=False):
        self.shape, self.dtype, self.block, self.imap, self.split = shape, dtype, block, imap, split


def _sub_index(split, s):
    if not split:
        return Ellipsis
    return (s,) if split is True else split(s)


def _grid_edges(grid):
    first = last = None
    for a, n in enumerate(grid):
        f, l = pl.program_id(a) == 0, pl.program_id(a) == n - 1
        first = f if first is None else jnp.logical_and(first, f)
        last = l if last is None else jnp.logical_and(last, l)
    return first, last


def _comm_plumbing(comm):
    if comm is None:
        return [], [], [], lambda refs: None, lambda refs: None
    kind, arrs = comm
    n = len(arrs)
    if kind == "gather":
        shapes = [jax.ShapeDtypeStruct((N_DEV,) + a.shape, a.dtype) for a in arrs]
        start, finish = _gather_start, _gather_finish
    else:
        shapes = [jax.ShapeDtypeStruct(a.shape, a.dtype) for a in arrs]
        start, finish = _exchange_start, _exchange_finish
    sems = [pltpu.SemaphoreType.DMA((n, N_DEV - 1)), pltpu.SemaphoreType.DMA((n, N_DEV - 1)),
            pltpu.SemaphoreType.DMA((n,))]
    return list(arrs), shapes, sems, start, finish


def _first_step(acc_from, ngrid):
    c = None
    for a in range(acc_from, ngrid):
        t = pl.program_id(a) == 0
        c = t if c is None else jnp.logical_and(c, t)
    return c


def _tile_fwd(name, f, grid, ins, outs, sub=1, comm=None):
    n_in, n_out = len(ins), len(outs)
    ngrid = len(grid)
    c_arrs, c_shapes, c_sems, c_start, c_finish = _comm_plumbing(comm)
    nc = len(c_arrs)

    def body(*refs):
        in_refs, c_in = refs[:n_in], refs[n_in:n_in + nc]
        out_refs, c_out = refs[n_in + nc:n_in + nc + n_out], refs[n_in + nc + n_out:n_in + 2 * nc + n_out]
        sems = refs[n_in + 2 * nc + n_out:]
        pids = tuple(pl.program_id(a) for a in range(ngrid))
        if nc:
            first, last = _grid_edges(grid)
            pl.when(first)(lambda: c_start(c_in, c_out, *sems))
        for s in range(sub):
            vals = [r[_sub_index(i.split, s)] for r, i in zip(in_refs, ins)]
            res = f(pids + ((s,) if sub > 1 else ()), *vals)
            for r, o, v in zip(out_refs, outs, res):
                r[_sub_index(o.split, s)] = v.astype(r.dtype)
        if nc:
            pl.when(last)(lambda: c_finish(c_in, c_out, *sems))

    res = pl.pallas_call(
        body, name=name, grid=grid,
        in_specs=[pl.BlockSpec(i.block, i.imap) for i in ins] + [_ANY] * nc,
        out_specs=[pl.BlockSpec(o.block, o.imap) for o in outs] + [_ANY] * nc,
        out_shape=[jax.ShapeDtypeStruct(o.shape, o.dtype) for o in outs] + c_shapes,
        scratch_shapes=c_sems, compiler_params=_params(),
    )(*[i.arr for i in ins], *c_arrs)
    return (res[:n_out], res[n_out:]) if nc else res


def _tile_bwd(name, f, grid, ins, outs, cts, acc_from=None, addends=None, sub=1, comm=None):
    n_in, n_out = len(ins), len(outs)
    ngrid = len(grid)
    diff = [k for k, i in enumerate(ins) if i.kind == "t"]
    addends = addends or {}
    add_keys = sorted(addends)
    n_add, n_g = len(add_keys), len(diff)
    c_arrs, c_shapes, c_sems, c_start, c_finish = _comm_plumbing(comm)
    nc = len(c_arrs)

    def body(*refs):
        pids = tuple(pl.program_id(a) for a in range(ngrid))
        in_refs = refs[:n_in]
        ct_refs = refs[n_in:n_in + n_out]
        add_refs = refs[n_in + n_out:n_in + n_out + n_add]
        o = n_in + n_out + n_add
        c_in, g_refs, c_out, sems = refs[o:o + nc], refs[o + nc:o + nc + n_g], refs[o + nc + n_g:o + 2 * nc + n_g], \
            refs[o + 2 * nc + n_g:]
        if nc:
            first_step, last_step = _grid_edges(grid)
            pl.when(first_step)(lambda: c_start(c_in, c_out, *sems))
        sums = {}
        for s in range(sub):
            vals = [r[_sub_index(i.split, s)] for r, i in zip(in_refs, ins)]
            dvals = [vals[k].astype(F32) for k in diff]

            def g(*d, vals=vals, s=s):
                full = list(vals)
                for k, dk in zip(diff, d):
                    full[k] = dk
                return tuple(f(pids + ((s,) if sub > 1 else ()), *full))

            res, vjp = jax.vjp(g, *dvals)
            cvals = [c[_sub_index(o_.split, s)].astype(r.dtype) for c, o_, r in zip(ct_refs, outs, res)]
            grads = vjp(tuple(cvals))
            for k, gr in zip(diff, grads):
                idx = _sub_index(ins[k].split, s)
                key = (k, str(idx))
                sums[key] = (idx, gr if key not in sums else sums[key][1] + gr)
        first = _first_step(acc_from, ngrid) if acc_from is not None else None
        for (k, _), (idx, gr) in sums.items():
            gref = g_refs[diff.index(k)]
            if idx is not Ellipsis:
                gref[idx] = gr.astype(gref.dtype)
                continue
            if k in addends:
                gr = gr + add_refs[add_keys.index(k)][...].astype(F32)
            if ins[k].acc and first is not None:
                @pl.when(first)
                def _(gr=gr, gref=gref):
                    gref[...] = gr.astype(gref.dtype)

                @pl.when(jnp.logical_not(first))
                def _(gr=gr, gref=gref):
                    gref[...] += gr.astype(gref.dtype)
            else:
                gref[...] = gr.astype(gref.dtype)
        if nc:
            pl.when(last_step)(lambda: c_finish(c_in, c_out, *sems))

    g_shapes, g_specs = [], []
    for k in diff:
        i = ins[k]
        if i.g is not None:
            shape, imap = i.g
        else:
            shape, imap = i.arr.shape, i.imap
        dt = i.gdtype or (F32 if i.acc else i.arr.dtype)
        g_shapes.append(jax.ShapeDtypeStruct(shape, dt))
        g_specs.append(pl.BlockSpec(i.block, imap))
    add_specs = [pl.BlockSpec(ins[k].block, ins[k].g[1] if ins[k].g is not None else ins[k].imap) for k in add_keys]
    res = pl.pallas_call(
        body, name=name, grid=grid,
        in_specs=[pl.BlockSpec(i.block, i.imap) for i in ins] + [pl.BlockSpec(o.block, o.imap) for o in outs] + add_specs
        + [_ANY] * nc,
        out_specs=g_specs + [_ANY] * nc, out_shape=g_shapes + c_shapes,
        scratch_shapes=c_sems, compiler_params=_params(),
    )(*[i.arr for i in ins], *cts, *[addends[k] for k in add_keys], *c_arrs)
    return (res[:n_g], res[n_g:]) if nc else res


def _mm(name, a, b, mode, out_dtype=F32, add=None, tm=1024, tn=1024, tk=1024, m=None, n=None):
    if mode == "nn":
        (M, K), N = a.shape, b.shape[1]
    elif mode == "nt":
        (M, K), N = a.shape, b.shape[0]
    else:
        (K, M), N = a.shape, b.shape[1]
    M, N = m or M, n or N
    tm, tn, tk = _pick(M, tm, 128), _pick(N, tn, 128), _pick(K, tk, 128)
    nk = K // tk
    a_spec = pl.BlockSpec((tk, tm), lambda i, j, k: (k, i)) if mode == "tn" else pl.BlockSpec((tm, tk), lambda i, j, k: (i, k))
    b_spec = pl.BlockSpec((tn, tk), lambda i, j, k: (j, k)) if mode == "nt" else pl.BlockSpec((tk, tn), lambda i, j, k: (k, j))
    o_spec = pl.BlockSpec((tm, tn), lambda i, j, k: (i, j))

    def body(*refs):
        if add is None:
            a_ref, b_ref, o_ref, acc = refs
            add_ref = None
        else:
            a_ref, b_ref, add_ref, o_ref, acc = refs
        k = pl.program_id(2)

        @pl.when(k == 0)
        def _():
            acc[...] = jnp.zeros_like(acc)

        acc[...] += _dg(a_ref[...].astype(BF16), b_ref[...].astype(BF16), mode)

        @pl.when(k == nk - 1)
        def _():
            r = acc[...]
            if add_ref is not None:
                r = r + add_ref[...].astype(F32)
            o_ref[...] = r.astype(o_ref.dtype)

    ops = (a, b) if add is None else (a, b, add)
    specs = [a_spec, b_spec] + ([] if add is None else [o_spec])
    return pl.pallas_call(
        body, name=name, grid=(M // tm, N // tn, nk), in_specs=specs, out_specs=o_spec,
        out_shape=jax.ShapeDtypeStruct((M, N), out_dtype), scratch_shapes=[pltpu.VMEM((tm, tn), F32)],
        compiler_params=_params(dimension_semantics=("parallel", "parallel", "arbitrary")),
    )(*ops)


def _ffn_fwd(name, x, gain, wg, wu, wd):
    T, D = x.shape
    F = wg.shape[1]
    tm, tf = _pick(T, 1024, 8), _pick(F, 256, 128)
    nf = F // tf

    def body(x_ref, g_ref, wg_ref, wu_ref, wd_ref, o_ref, h_ref, acc):
        j = pl.program_id(1)

        @pl.when(j == 0)
        def _():
            h_ref[...] = _rms(x_ref[...], g_ref[...]).astype(BF16)
            acc[...] = jnp.zeros_like(acc)

        h = h_ref[...]
        a = _dg(h, wg_ref[...], "nn")
        b = _dg(h, wu_ref[...], "nn")
        s = (_silu(a) * b).astype(BF16)
        acc[...] += _dg(s, wd_ref[...], "nn")

        @pl.when(j == nf - 1)
        def _():
            o_ref[...] = x_ref[...] + 0.5 * acc[...]

    return pl.pallas_call(
        body, name=name, grid=(T // tm, nf),
        in_specs=[pl.BlockSpec((tm, D), lambda i, j: (i, 0)), pl.BlockSpec((1, D), lambda i, j: (0, 0)),
                  pl.BlockSpec((D, tf), lambda i, j: (0, j)), pl.BlockSpec((D, tf), lambda i, j: (0, j)),
                  pl.BlockSpec((tf, D), lambda i, j: (j, 0))],
        out_specs=pl.BlockSpec((tm, D), lambda i, j: (i, 0)),
        out_shape=jax.ShapeDtypeStruct((T, D), F32),
        scratch_shapes=[pltpu.VMEM((tm, D), BF16), pltpu.VMEM((tm, D), F32)],
        compiler_params=_params(dimension_semantics=("parallel", "arbitrary")),
    )(x, gain, wg, wu, wd)


def _ffn_bwd(name, x, gain, wg, wu, wd, dy, f_true):
    T, D = x.shape
    F = wg.shape[1]
    tm, tf = _pick(T, 512, 8), _pick(F, FFN_BWD_TF, 128)
    nf = F // tf

    def body(x_ref, g_ref, wg_ref, wu_ref, wd_ref, dy_ref, dx_ref, dg_ref, da_ref, db_ref, s_ref, h_ref, dyh_ref, dh):
        i, j = pl.program_id(0), pl.program_id(1)

        @pl.when(j == 0)
        def _():
            h_ref[...] = _rms(x_ref[...], g_ref[...]).astype(BF16)
            dyh_ref[...] = (0.5 * dy_ref[...]).astype(BF16)
            dh[...] = jnp.zeros_like(dh)

        h = h_ref[...]
        a = _dg(h, wg_ref[...], "nn")
        b = _dg(h, wu_ref[...], "nn")
        ds = _dg(dyh_ref[...], wd_ref[...], "nt")
        sig = _sigmoid(a)
        silu = a * sig
        da = (ds * b * (sig * (1.0 + a * (1.0 - sig)))).astype(BF16)
        db = (ds * silu).astype(BF16)
        da_ref[...] = da
        db_ref[...] = db
        s_ref[...] = (silu * b).astype(BF16)
        dh[...] += _dg(da, wg_ref[...], "nt") + _dg(db, wu_ref[...], "nt")

        @pl.when(j == nf - 1)
        def _():
            _, vjp = jax.vjp(_rms, x_ref[...], g_ref[...])
            dxn, dgn = vjp(dh[...])
            dx_ref[...] = dy_ref[...] + dxn

            @pl.when(i == 0)
            def _():
                dg_ref[...] = dgn

            @pl.when(i != 0)
            def _():
                dg_ref[...] += dgn

    row = lambda i, j: (i, 0)
    col = lambda i, j: (i, j)
    dx, dgain, da, db, s, h, dyh = pl.pallas_call(
        body, name=name, grid=(T // tm, nf),
        in_specs=[pl.BlockSpec((tm, D), row), pl.BlockSpec((1, D), lambda i, j: (0, 0)),
                  pl.BlockSpec((D, tf), lambda i, j: (0, j)), pl.BlockSpec((D, tf), lambda i, j: (0, j)),
                  pl.BlockSpec((tf, D), lambda i, j: (j, 0)), pl.BlockSpec((tm, D), row)],
        out_specs=[pl.BlockSpec((tm, D), row), pl.BlockSpec((1, D), lambda i, j: (0, 0)),
                   pl.BlockSpec((tm, tf), col), pl.BlockSpec((tm, tf), col), pl.BlockSpec((tm, tf), col),
                   pl.BlockSpec((tm, D), row), pl.BlockSpec((tm, D), row)],
        out_shape=[jax.ShapeDtypeStruct((T, D), F32), jax.ShapeDtypeStruct((1, D), F32),
                   jax.ShapeDtypeStruct((T, F), BF16), jax.ShapeDtypeStruct((T, F), BF16),
                   jax.ShapeDtypeStruct((T, F), BF16), jax.ShapeDtypeStruct((T, D), BF16),
                   jax.ShapeDtypeStruct((T, D), BF16)],
        scratch_shapes=[pltpu.VMEM((tm, D), F32)],
        compiler_params=_params(),
    )(x, gain, wg, wu, wd, dy)
    dwg = _mm(name + "_dwg", h, da, "tn", tm=1024, tn=1408, tk=1024, n=f_true)
    dwu = _mm(name + "_dwu", h, db, "tn", tm=1024, tn=1408, tk=1024, n=f_true)
    dwd = _mm(name + "_dwd", s, dyh, "tn", tm=1408, tn=1024, tk=1024, m=f_true)
    return dx, dgain, dwg, dwu, dwd


def _norm_f(pids, x, gain):
    return (_rms(x, gain),)


def _dn_conv_f(pids, x, w):
    j = pids[0]
    tap = lax.broadcasted_iota(jnp.int32, w.shape, 0)
    y = jnp.zeros_like(x)
    for t in range(DN_CONV):
        wt = jnp.sum(jnp.where(tap == t, w, 0.0), axis=0, keepdims=True)
        y = y + _shift(x, t - DN_CONV // 2) * wt
    y = _silu(y)
    n = y * lax.rsqrt(jnp.sum(y * y, axis=-1, keepdims=True) + L2_EPS)
    is_q = (j < DN_HEADS).astype(F32)
    is_qk = (j < 2 * DN_HEADS).astype(F32)
    scale = is_q * (DN_DIM ** -0.5) + (1.0 - is_q)
    return ((is_qk * n + (1.0 - is_qk) * y) * scale,)


def _dn_gate_f(pids, braw, araw, a_log, dt_bias):
    beta = _sigmoid(braw)
    g = -jnp.exp(a_log) * _softplus(araw + dt_bias)
    return beta, g


def _dn_prep_f(pids, q, k, v, brow, grow):
    cs = DN_SUPER
    sign = 1 - 2 * pids[2]
    ii = lax.broadcasted_iota(jnp.int32, (cs, cs), 0)
    jj = lax.broadcasted_iota(jnp.int32, (cs, cs), 1)
    shift = int(math.log2(DN_CHUNK))
    same = (ii >> shift) == (jj >> shift)
    d = (ii - jj) * sign
    incl = same & (d >= 0)
    strict = same & (d > 0)
    eye = ii == jj
    g_col = jnp.sum(jnp.where(eye, jnp.broadcast_to(grow, (cs, cs)), 0.0), axis=1, keepdims=True)
    b_col = jnp.sum(jnp.where(eye, jnp.broadcast_to(brow, (cs, cs)), 0.0), axis=1, keepdims=True)
    g128 = jnp.broadcast_to(g_col, (cs, DN_DIM))
    G = _mdot(incl.astype(BF16), g128)
    Gt = _mdot(same.astype(BF16), g128)
    Gc = jnp.concatenate([G, G], axis=1)
    Grow = jnp.sum(jnp.where(eye, Gc, 0.0), axis=0, keepdims=True)
    decay = jnp.exp(jnp.where(incl, Gc - Grow, MASK_VALUE))
    eG = jnp.exp(G)
    kb = k * b_col
    A = jnp.where(strict, _dot(kb, k, "nt", 1) * decay, 0.0)
    X = _unit_solve(A, jnp.concatenate([v * b_col, kb * eG], axis=1))
    qk = jnp.where(incl, _dot(q, k, "nt", 1) * decay, 0.0)
    return X, qk, q * eG, k * jnp.exp(Gt - G), jnp.exp(Gt)


def _dn_out_f(pids, of, ob, z, gain):
    return (_rms(of + ob, gain) * _silu(z),)


def _pool_f(pids, u, w, scale):
    g = pids[0]
    half = jnp.left_shift(1, g)
    n = u.shape[0]
    pos = lax.broadcasted_iota(jnp.int32, (n, 1), 0)
    tot = jnp.zeros_like(u)
    cnt = jnp.zeros((n, 1), F32)
    for o in range(-POOL_MAX_HALF, POOL_MAX_HALF):
        use = ((o >= -half) & (o < half)).astype(F32)
        tot = tot + use * _shift(u, o)
        cnt = cnt + use * ((pos + o >= 0) & (pos + o < n)).astype(F32)
    pooled = tot / cnt - u
    return (_dot(pooled, w, "nn", 1) * scale,)


def _rope_f(pids, *args):
    cos, sin = args[-2:]
    qs, ks, vs = args[:DA_GROUPS], args[DA_GROUPS:2 * DA_GROUPS], args[2 * DA_GROUPS:3 * DA_GROUPS]
    qr = [(q * cos + _rot(q) * sin) * (DA_DIM ** -0.5) for q in qs]
    kr = [k * cos + _rot(k) * sin for k in ks]
    return (*qr, *kr, *vs)


def _attn_head(q, k, v, qpos0, kpos0):
    s = _dot(q, k, "nt", 1)
    qi = qpos0 + lax.broadcasted_iota(jnp.int32, s.shape, 0)
    kj = kpos0 + lax.broadcasted_iota(jnp.int32, s.shape, 1)
    s = jnp.where(jnp.abs(kj - qi) <= DA_RADIUS, s, MASK_VALUE)
    m = lax.stop_gradient(jnp.max(s, axis=1, keepdims=True))
    p = jnp.exp(s - m)
    l = jnp.sum(p, axis=1, keepdims=True)
    o = _dot(p, v, "nn", 1) / l
    return o, jnp.broadcast_to(m + jnp.log(l), o.shape)


def _merge_f(pids, o0, o1, o2, l0, l1, l2):
    m = jnp.maximum(jnp.maximum(l0, l1), l2)
    e0, e1, e2 = jnp.exp(l0 - m), jnp.exp(l1 - m), jnp.exp(l2 - m)
    return ((e0 * o0 + e1 * o1 + e2 * o2) / (e0 + e1 + e2),)


def _gate_f(pids, g0, g1, g2, ya, yb, yc, b0, b1, b2):
    return (_sigmoid(g0 + b0) * ya + _sigmoid(g1 + b1) * yb + _sigmoid(g2 + b2) * yc,)


def _attn_window(i, L, tq, W):
    k0 = jnp.clip(i * tq - DA_RADIUS, 0, L - W)
    return pl.multiple_of(k0, DA_RADIUS)


def _strided_view(t, B, dil):
    T, HD = t.shape
    return t.reshape(B, T // B // dil, dil * HD)


def _attn_fwd(name, q, k, v, B, dil):
    T, HD = q.shape
    NS, L = B * dil, T // B // dil
    tq = min(DA_TQ, L)
    W = min(L, tq + 2 * DA_RADIUS)

    def body(q_ref, k_ref, v_ref, o_ref, l_ref):
        i = pl.program_id(1)
        k0 = _attn_window(i, L, tq, W)
        for h in range(DA_HEADS):
            hs = slice(h * DA_DIM, (h + 1) * DA_DIM)
            o, lse = _attn_head(q_ref[:, hs], k_ref[pl.ds(k0, W), hs], v_ref[pl.ds(k0, W), hs], i * tq, k0)
            o_ref[:, hs] = o
            l_ref[:, hs] = lse

    qs = pl.BlockSpec((None, tq, HD), lambda s, i: (s // dil, i, s % dil))
    ks = pl.BlockSpec((None, L, HD), lambda s, i: (s // dil, 0, s % dil))
    o, lse = pl.pallas_call(
        body, name=name, grid=(NS, L // tq), in_specs=[qs, ks, ks], out_specs=[qs, qs],
        out_shape=[jax.ShapeDtypeStruct((B, L, dil * HD), F32)] * 2, compiler_params=_params(),
    )(*[_strided_view(t, B, dil) for t in (q, k, v)])
    return o.reshape(T, HD), lse.reshape(T, HD)


def _attn_bwd(name, q, k, v, do, dl, B, dil):
    T, HD = q.shape
    NS, L = B * dil, T // B // dil
    tq = min(DA_TQ, L)
    W = min(L, tq + 2 * DA_RADIUS)

    def body(q_ref, k_ref, v_ref, do_ref, dl_ref, dq_ref, dk_ref, dv_ref):
        i = pl.program_id(1)
        k0 = _attn_window(i, L, tq, W)

        @pl.when(i == 0)
        def _():
            dk_ref[...] = jnp.zeros_like(dk_ref)
            dv_ref[...] = jnp.zeros_like(dv_ref)

        for h in range(DA_HEADS):
            hs = slice(h * DA_DIM, (h + 1) * DA_DIM)
            f = functools.partial(_attn_head, qpos0=i * tq, kpos0=k0)
            _, vjp = jax.vjp(f, q_ref[:, hs].astype(F32), k_ref[pl.ds(k0, W), hs].astype(F32),
                             v_ref[pl.ds(k0, W), hs].astype(F32))
            dq, dk, dv = vjp((do_ref[:, hs], dl_ref[:, hs]))
            dq_ref[:, hs] = dq
            dk_ref[pl.ds(k0, W), hs] += dk
            dv_ref[pl.ds(k0, W), hs] += dv

    qs = pl.BlockSpec((None, tq, HD), lambda s, i: (s // dil, i, s % dil))
    ks = pl.BlockSpec((None, L, HD), lambda s, i: (s // dil, 0, s % dil))
    res = pl.pallas_call(
        body, name=name, grid=(NS, L // tq), in_specs=[qs, ks, ks, qs, qs], out_specs=[qs, ks, ks],
        out_shape=[jax.ShapeDtypeStruct((B, L, dil * HD), F32)] * 3, compiler_params=_params(),
    )(*[_strided_view(t, B, dil) for t in (q, k, v, do, dl)])
    return tuple(t.reshape(T, HD) for t in res)


def _scan_chunk(t, rev, N):
    c = jnp.where(rev, N - 1 - t, t)
    per = DN_SUPER // DN_CHUNK
    return c, pl.multiple_of(c * DN_CHUNK, DN_CHUNK), pl.multiple_of((c % per) * DN_CHUNK, DN_CHUNK), \
        pl.multiple_of((c // per) * DN_SUPER, DN_SUPER)


def _dn_scan_fwd(name, uw, qk, qd, kd, gl, B):
    R, T, _ = uw.shape
    S = T // B
    N = S // DN_CHUNK
    C, DK = DN_CHUNK, DN_DIM

    PAIR = 2

    def body(uw_ref, qk_ref, qd_ref, kd_ref, gl_ref, o_ref, st_ref, vn_ref):
        rev = pl.program_id(1) * PAIR >= DN_HEADS
        vn_ref[...] = jnp.zeros_like(vn_ref)

        def step(t, states):
            c, r0, w0, s0 = _scan_chunk(t, rev, N)
            rows = pl.ds(r0, C)
            new = []
            for p, state in enumerate(states):
                st_ref[p, c] = state
                vnew = uw_ref[p, rows, 0:DK] - _dotp(uw_ref[p, rows, DK:2 * DK], state, "nn", 1)
                vn_ref[p, pl.ds(w0, C), :] = vnew
                o_ref[p, rows, :] = (_dotp(qd_ref[p, rows, :], state, "nn", 1)
                                     + _dotp(qk_ref[p, rows, :], vn_ref[p], "nn", 1))
                new.append(state * gl_ref[p, pl.ds(r0, 1), :] + _dotp(kd_ref[p, rows, :], vnew, "tn", 1))
            return tuple(new)

        lax.fori_loop(0, N, step, tuple(jnp.zeros((DK, DK), F32) for _ in range(PAIR)))

    def seq(w):
        return pl.BlockSpec((PAIR, S, w), lambda b, r: (r, b, 0))

    return pl.pallas_call(
        body, name=name, grid=(B, R // PAIR),
        in_specs=[seq(2 * DK), seq(DN_SUPER), seq(DK), seq(DK), seq(DK)],
        out_specs=[seq(DK), pl.BlockSpec((None, PAIR, N, DK, DK), lambda b, r: (b, r, 0, 0, 0))],
        out_shape=[jax.ShapeDtypeStruct((R, T, DK), F32), jax.ShapeDtypeStruct((B, R, N, DK, DK), F32)],
        scratch_shapes=[pltpu.VMEM((PAIR, DN_SUPER, DK), F32)], compiler_params=_params(),
    )(uw, qk, qd, kd, gl)


def _dn_scan_bwd(name, uw, qk, qd, kd, gl, st, do, B):
    R, T, _ = uw.shape
    S = T // B
    N = S // DN_CHUNK
    C, DK = DN_CHUNK, DN_DIM

    def body(uw_ref, qk_ref, qd_ref, kd_ref, gl_ref, st_ref, do_ref, duw_ref, dqk_ref, dqd_ref, dkd_ref, dgl_ref,
             vn_ref, tmp_ref):
        rev = pl.program_id(1) >= DN_HEADS
        vn_ref[...] = jnp.zeros_like(vn_ref)
        dgl_ref[...] = jnp.zeros_like(dgl_ref)

        def step(t, dstate):
            c, r0, w0, s0 = _scan_chunk(N - 1 - t, rev, N)
            rows = pl.ds(r0, C)
            state = st_ref[c]
            w = uw_ref[rows, DK:2 * DK]
            vnew = uw_ref[rows, 0:DK] - _dotp(w, state, "nn", 1)
            vn_ref[pl.ds(w0, C), :] = vnew
            do_c = do_ref[rows, :]
            tmp_ref[...] = _dotp(qk_ref[rows, :], do_c, "tn", 1)
            dvn = tmp_ref[pl.ds(w0, C), :] + _dotp(kd_ref[rows, :], dstate, "nn", 1)
            dqk_ref[rows, :] = _dotp(do_c, vn_ref[...], "nt", 1)
            dqd_ref[rows, :] = _dotp(do_c, state, "nt", 1)
            dkd_ref[rows, :] = _dotp(vnew, dstate, "nt", 1)
            dgl_ref[pl.ds(r0, 1), :] = jnp.sum(state * dstate, axis=0, keepdims=True)
            duw_ref[rows, 0:DK] = dvn
            duw_ref[rows, DK:2 * DK] = -_dotp(dvn, state, "nt", 1)
            return (_dotp(qd_ref[rows, :], do_c, "tn", 1) + dstate * gl_ref[pl.ds(r0, 1), :]
                    - _dotp(w, dvn, "tn", 1))

        lax.fori_loop(0, N, step, jnp.zeros((DK, DK), F32))

    def seq(w):
        return pl.BlockSpec((None, S, w), lambda b, r: (r, b, 0))

    return pl.pallas_call(
        body, name=name, grid=(B, R),
        in_specs=[seq(2 * DK), seq(DN_SUPER), seq(DK), seq(DK), seq(DK),
                  pl.BlockSpec((None, None, N, DK, DK), lambda b, r: (b, r, 0, 0, 0)),
                  pl.BlockSpec((None, S, DK), lambda b, r: (r % DN_HEADS, b, 0))],
        out_specs=[seq(2 * DK), seq(DN_SUPER), seq(DK), seq(DK), seq(DK)],
        out_shape=[jax.ShapeDtypeStruct((R, T, 2 * DK), F32), jax.ShapeDtypeStruct((R, T, DN_SUPER), F32),
                   jax.ShapeDtypeStruct((R, T, DK), F32), jax.ShapeDtypeStruct((R, T, DK), F32),
                   jax.ShapeDtypeStruct((R, T, DK), F32)],
        scratch_shapes=[pltpu.VMEM((DN_SUPER, DK), F32), pltpu.VMEM((DN_SUPER, DK), F32)],
        compiler_params=_params(),
    )(uw, qk, qd, kd, gl, st, do)


def _loss_fwd_bwd(name, x, gain, target):
    T, D = x.shape
    tm = _pick(T, 512, 8)

    def body(x_ref, g_ref, t_ref, loss_ref, dx_ref, dg_ref):
        i = pl.program_id(0)

        def f(xv, gv):
            e = _rms(xv, gv) - t_ref[...]
            return 0.5 * jnp.sum(jnp.mean(e * e, axis=-1, keepdims=True))

        val, (dx, dg) = jax.value_and_grad(f, argnums=(0, 1))(x_ref[...], g_ref[...])
        dx_ref[...] = dx
        part = jnp.full(loss_ref.shape, val, F32)

        @pl.when(i == 0)
        def _():
            dg_ref[...] = dg
            loss_ref[...] = part

        @pl.when(i != 0)
        def _():
            dg_ref[...] += dg
            loss_ref[...] += part

    return pl.pallas_call(
        body, name=name, grid=(T // tm,),
        in_specs=[pl.BlockSpec((tm, D), lambda i: (i, 0)), pl.BlockSpec((1, D), lambda i: (0, 0)),
                  pl.BlockSpec((tm, D), lambda i: (i, 0))],
        out_specs=[pl.BlockSpec((8, 128), lambda i: (0, 0)), pl.BlockSpec((tm, D), lambda i: (i, 0)),
                   pl.BlockSpec((1, D), lambda i: (0, 0))],
        out_shape=[jax.ShapeDtypeStruct((8, 128), F32), jax.ShapeDtypeStruct((T, D), F32),
                   jax.ShapeDtypeStruct((1, D), F32)],
        compiler_params=_params(),
    )(x, gain, target)


class _Cols:
    def __init__(self, D):
        assert D % 256 == 0
        self.gate = 0
        self.da = 3 * D
        self.qkv = self.da + 3 * DA_WIDTH
        self.z = self.qkv + 3 * DN_WIDTH
        self.pool = self.z + DN_WIDTH
        self.ba = self.pool + POOL_WIDTH
        self.total = self.ba + BA_PAD


def _rope_tables(S):
    half = DA_DIM // 2
    inv_freq = ROPE_THETA ** (-jnp.arange(half, dtype=F32) / half)
    ang = jnp.arange(S, dtype=F32)[:, None] * inv_freq[None, :]
    reps = DA_OUT // DA_DIM
    cos = jnp.tile(jnp.concatenate([jnp.cos(ang), jnp.cos(ang)], axis=1), (1, reps))
    sin = jnp.tile(jnp.concatenate([jnp.sin(ang), jnp.sin(ang)], axis=1), (1, reps))
    return cos, sin


def _to_strided(t, B, dil):
    T, w = t.shape
    L = T // B // dil
    return t.reshape(B, L, dil, w).transpose(0, 2, 1, 3).reshape(B * dil, L, w)


def _from_strided(t, B, dil):
    NS, L, w = t.shape
    return t.reshape(B, dil, L, w).transpose(0, 2, 1, 3).reshape(B * dil * L, w)


def _mixer(l, x1, w, B, host_gather=None):
    T, D = x1.shape
    S = T // B
    c = _Cols(D)
    tm = _pick(S, 512, 8)
    nmS = S // tm
    n = f"l{l}_"

    norm_ins = [_In(x1, (tm, D), lambda i: (i, 0)), _In(w["mix_norm"], (1, D), lambda i: (0, 0), acc=True)]
    norm_outs = [_Out((T, D), BF16, (tm, D), lambda i: (i, 0))]
    (h,) = _tile_fwd(n + "norm", _norm_f, (T // tm,), norm_ins, norm_outs)
    P = _mm(n + "proj", h, w["w_cat"], "nn", tm=512, tn=2688, tk=1024)
    baT = P[:, c.ba:c.ba + 16].T

    cb = c.qkv // DN_DIM
    conv_ins = [_In(P, (S, DN_DIM), lambda j, b: (b, cb + j), g=((T, 3 * DN_WIDTH), lambda j, b: (b, j)), gdtype=BF16),
                _In(w["dn_conv"], (DN_CONV, DN_DIM), lambda j, b: (0, j), acc=True)]
    conv_outs = [_Out((T, 3 * DN_WIDTH), F32, (S, DN_DIM), lambda j, b: (b, j))]
    conv_grid = (3 * DN_HEADS, B)
    (qkvc,) = _tile_fwd(n + "dnconv", _dn_conv_f, conv_grid, conv_ins, conv_outs)

    tg = _pick(T, 2048, 128)
    gate_ins = [_In(baT, (8, tg), lambda i: (0, i)), _In(baT, (8, tg), lambda i: (1, i)),
                _In(w["dn_a_log"], (8, 1), lambda i: (0, 0), acc=True),
                _In(w["dn_dt_bias"], (8, 1), lambda i: (0, 0), acc=True)]
    gate_ins[0].g = ((8, T), lambda i: (0, i))
    gate_ins[1].g = ((8, T), lambda i: (0, i))
    gate_outs = [_Out((8, T), F32, (8, tg), lambda i: (0, i))] * 2
    beta, gdec = _tile_fwd(n + "dngate", _dn_gate_f, (T // tg,), gate_ins, gate_outs)

    NSC = T // DN_SUPER
    beta4 = beta.reshape(2, DN_HEADS, NSC, 1, DN_SUPER)
    gdec4 = gdec.reshape(2, DN_HEADS, NSC, 1, DN_SUPER)
    R = 2 * DN_HEADS

    def qkv_in(off):
        return _In(qkvc, (DN_SUPER, DN_DIM), lambda hh, m: (m, off + hh), acc=True,
                   g=((T, DN_WIDTH), lambda hh, m: (m, hh)))

    def row_in(a):
        return _In(a, (2, None, None, 1, DN_SUPER), lambda hh, m: (0, hh, m, 0, 0), split=True)

    def chain_out(wd):
        return _Out((2, DN_HEADS, T, wd), F32, (2, None, DN_SUPER, wd), lambda hh, m: (0, hh, m, 0), split=True)

    prep_ins = [qkv_in(0), qkv_in(DN_HEADS), qkv_in(2 * DN_HEADS), row_in(beta4), row_in(gdec4)]
    prep_outs = [chain_out(2 * DN_DIM), chain_out(DN_SUPER), chain_out(DN_DIM), chain_out(DN_DIM), chain_out(DN_DIM)]
    prep_grid = (DN_HEADS, NSC)
    prep_res = _tile_fwd(n + "dnprep", _dn_prep_f, prep_grid, prep_ins, prep_outs, sub=2, comm=host_gather)
    gathered_next = None
    if host_gather is not None:
        prep_res, gathered_next = prep_res
    uw, qk, qd, kd, gl = (t.reshape((R,) + t.shape[2:]) for t in prep_res)
    o_dn, states = _dn_scan_fwd(n + "dnscan", uw, qk, qd, kd, gl, B)

    zb = c.z // DN_DIM
    out_ins = [_In(o_dn, (None, S, DN_DIM), lambda b, hh: (hh, b, 0)),
               _In(o_dn, (None, S, DN_DIM), lambda b, hh: (DN_HEADS + hh, b, 0)),
               _In(P, (S, DN_DIM), lambda b, hh: (b, zb + hh), g=((T, DN_WIDTH), lambda b, hh: (b, hh)), gdtype=BF16),
               _In(w["dn_out_norm"], (1, DN_DIM), lambda b, hh: (0, 0), acc=True)]
    out_ins[0].g = ((DN_HEADS, T, DN_DIM), lambda b, hh: (hh, b, 0))
    out_ins[1].g = ((DN_HEADS, T, DN_DIM), lambda b, hh: (hh, b, 0))
    out_outs = [_Out((T, DN_WIDTH), BF16, (S, DN_DIM), lambda b, hh: (b, hh))]
    (ya_in,) = _tile_fwd(n + "dnout", _dn_out_f, (B, DN_HEADS), out_ins, out_outs)

    pb = c.pool // POOL_DIM
    pool_ins = [_In(P, (S, POOL_DIM), lambda gi, b: (b, pb + gi), g=((T, POOL_WIDTH), lambda gi, b: (b, gi)), gdtype=BF16),
                _In(w["pool_w"], (None, POOL_DIM, POOL_DIM), lambda gi, b: (gi, 0, 0), acc=True),
                _In(w["pool_scale"], (None, 1, POOL_DIM), lambda gi, b: (gi, 0, 0), acc=True)]
    pool_outs = [_Out((T, POOL_WIDTH), BF16, (S, POOL_DIM), lambda gi, b: (b, gi))]
    (yb_in,) = _tile_fwd(n + "pool", _pool_f, (POOL_GROUPS, B), pool_ins, pool_outs)

    cos, sin = _rope_tables(S)
    db = c.da // DA_OUT

    def da_in(k):
        return _In(P, (tm, DA_OUT), lambda i: (i, db + k), g=((T, DA_OUT), lambda i: (i, 0)), gdtype=BF16)

    rope_ins = [da_in(k) for k in range(3 * DA_GROUPS)]
    rope_ins += [_In(cos, (tm, DA_OUT), lambda i: (i % nmS, 0), kind="c"),
                 _In(sin, (tm, DA_OUT), lambda i: (i % nmS, 0), kind="c")]
    rope_outs = [_Out((T, DA_OUT), BF16, (tm, DA_OUT), lambda i: (i, 0))] * (3 * DA_GROUPS)
    roped = _tile_fwd(n + "rope", _rope_f, (T // tm,), rope_ins, rope_outs)
    strided = []
    o_g, l_g = [], []
    for gi, dil in enumerate(DA_DILATIONS):
        qs, ks, vs = roped[gi], roped[DA_GROUPS + gi], roped[2 * DA_GROUPS + gi]
        strided.append((qs, ks, vs))
        o, lse = _attn_fwd(n + f"attn{gi}", qs, ks, vs, B, dil)
        o_g.append(o)
        l_g.append(lse)
    mrg_ins = [_In(a, (tm, DA_OUT), lambda i: (i, 0)) for a in o_g + l_g]
    mrg_outs = [_Out((T, DA_OUT), BF16, (tm, DA_OUT), lambda i: (i, 0))]
    (yc_in,) = _tile_fwd(n + "merge", _merge_f, (T // tm,), mrg_ins, mrg_outs)

    ya = _mm(n + "pa", ya_in, w["w_proj_a"], "nn")
    yb = _mm(n + "pb", yb_in, w["w_proj_b"], "nn")
    yc = _mm(n + "pc", yc_in, w["w_proj_c"], "nn")

    def gcol(k):
        return _In(P, (tm, D), lambda i: (i, k), g=((T, D), lambda i: (i, 0)), gdtype=BF16)

    def yin(a):
        return _In(a, (tm, D), lambda i: (i, 0), gdtype=BF16)

    def bin_(k):
        return _In(w["b_gate"][k:k + 1], (1, D), lambda i: (0, 0), acc=True)

    gm_ins = [gcol(0), gcol(1), gcol(2), yin(ya), yin(yb), yin(yc), bin_(0), bin_(1), bin_(2)]
    gm_outs = [_Out((T, D), BF16, (tm, D), lambda i: (i, 0))]
    (merged,) = _tile_fwd(n + "gates", _gate_f, (T // tm,), gm_ins, gm_outs)
    x2 = _mm(n + "out", merged, w["w_out"], "nn", add=x1)

    def backward(dx2, host_exchange=None):
        return _mixer_bwd(dx2, host_exchange, **{k: v for k, v in locals_.items() if k in _MIXER_BWD_NEEDS})

    locals_ = dict(locals())
    return x2, backward, gathered_next


_MIXER_BWD_NEEDS = ("n", "B", "T", "D", "tm", "w", "h", "merged", "gm_ins", "gm_outs", "ya_in", "yb_in", "yc_in",
                    "mrg_ins", "mrg_outs", "strided", "rope_ins", "rope_outs", "pool_ins", "pool_outs", "out_ins",
                    "out_outs", "uw", "qk", "qd", "kd", "gl", "states", "prep_grid", "prep_ins", "prep_outs", "tg",
                    "gate_ins", "gate_outs", "conv_grid", "conv_ins", "conv_outs", "norm_ins", "norm_outs")


def _mixer_bwd(dx2, host_exchange, *, n, B, T, D, tm, w, h, merged, gm_ins, gm_outs, ya_in, yb_in, yc_in, mrg_ins, mrg_outs, strided,
               rope_ins, rope_outs, pool_ins, pool_outs, out_ins, out_outs, uw, qk, qd, kd, gl, states, prep_grid,
               prep_ins, prep_outs, tg, gate_ins, gate_outs, conv_grid, conv_ins, conv_outs, norm_ins, norm_outs):
    g = {}
    dmerged = _mm(n + "d_merged", dx2, w["w_out"], "nt")
    g["w_out"] = _mm(n + "d_wout", merged, dx2, "tn", tm=1024, tn=1024, tk=1024)
    dg0, dg1, dg2, dya, dyb, dyc, db0, db1, db2 = _tile_bwd(
        n + "gates_b", _gate_f, (T // tm,), gm_ins, gm_outs, [dmerged], acc_from=0)
    g["b_gate"] = jnp.concatenate([db0, db1, db2], axis=0)
    dya_in = _mm(n + "d_pa", dya, w["w_proj_a"], "nt")
    dyb_in = _mm(n + "d_pb", dyb, w["w_proj_b"], "nt")
    dyc_in = _mm(n + "d_pc", dyc, w["w_proj_c"], "nt")
    g["w_proj_a"] = _mm(n + "d_wpa", ya_in, dya, "tn", tn=1024, tk=2048)
    g["w_proj_b"] = _mm(n + "d_wpb", yb_in, dyb, "tn", tn=1024, tk=2048)
    g["w_proj_c"] = _mm(n + "d_wpc", yc_in, dyc, "tn", tn=1024, tk=2048)

    dmrg = _tile_bwd(n + "merge_b", _merge_f, (T // tm,), mrg_ins, mrg_outs, [dyc_in])
    dq_parts, dk_parts, dv_parts = [], [], []
    for gi, dil in enumerate(DA_DILATIONS):
        qs, ks, vs = strided[gi]
        dq, dk, dv = _attn_bwd(n + f"attn{gi}_b", qs, ks, vs, dmrg[gi], dmrg[DA_GROUPS + gi], B, dil)
        dq_parts.append(dq)
        dk_parts.append(dk)
        dv_parts.append(dv)
    dP_da = _tile_bwd(n + "rope_b", _rope_f, (T // tm,), rope_ins, rope_outs, dq_parts + dk_parts + dv_parts)

    dPpool, g["pool_w"], g["pool_scale"] = _tile_bwd(
        n + "pool_b", _pool_f, (POOL_GROUPS, B), pool_ins, pool_outs, [dyb_in], acc_from=1)

    dof, dob, dPz, g["dn_out_norm"] = _tile_bwd(
        n + "dnout_b", _dn_out_f, (B, DN_HEADS), out_ins, out_outs, [dya_in], acc_from=0)
    del dob
    duw, dqk, dqd, dkd, dgl = _dn_scan_bwd(n + "dnscan_b", uw, qk, qd, kd, gl, states, dof, B)
    prep_cts = [t.reshape((2, DN_HEADS) + t.shape[1:]) for t in (duw, dqk, dqd, dkd, dgl)]
    prep_res = _tile_bwd(n + "dnprep_b", _dn_prep_f, prep_grid, prep_ins, prep_outs, prep_cts, sub=2,
                         comm=host_exchange)
    exchanged = None
    if host_exchange is not None:
        prep_res, exchanged = prep_res
    dq_, dk_, dv_, dbeta4, dgdec4 = prep_res
    dqkvc = jnp.concatenate([dq_, dk_, dv_], axis=1)
    dbraw, daraw, g["dn_a_log"], g["dn_dt_bias"] = _tile_bwd(
        n + "dngate_b", _dn_gate_f, (T // tg,), gate_ins, gate_outs,
        [dbeta4.reshape(8, T), dgdec4.reshape(8, T)], acc_from=0)
    dPqkv, g["dn_conv"] = _tile_bwd(n + "dnconv_b", _dn_conv_f, conv_grid, conv_ins, conv_outs, [dqkvc], acc_from=1)
    dba = jnp.concatenate([dbraw, daraw], axis=0).T.astype(BF16)
    dba = jnp.pad(dba, ((0, 0), (0, BA_PAD - 16)))
    dP = jnp.concatenate([dg0, dg1, dg2, *dP_da, dPqkv, dPz, dPpool, dba], axis=1)
    dh = _mm(n + "d_h", dP, w["w_cat"], "nt", tm=1024, tn=1024, tk=2688)
    g["w_cat"] = _mm(n + "d_wcat", h, dP, "tn", tm=1024, tn=896, tk=1024)
    dx1, g["mix_norm"] = _tile_bwd(n + "norm_b", _norm_f, (T // tm,), norm_ins, norm_outs, [dh], acc_from=0,
                                   addends={0: dx2})
    return dx1, g, exchanged


def _layer_weights(full, l, D):
    c = _Cols(D)
    w_in = full["w_in"][l]
    o_z, o_ba, o_pool, o_da = 3 * DN_WIDTH, 4 * DN_WIDTH, 4 * DN_WIDTH + 16, 4 * DN_WIDTH + 16 + POOL_WIDTH
    w_cat = jnp.concatenate(
        [full["w_gate"][l], w_in[:, o_da:], w_in[:, :o_z], w_in[:, o_z:o_ba], w_in[:, o_pool:o_da], w_in[:, o_ba:o_pool],
         jnp.zeros((D, BA_PAD - 16), w_in.dtype)], axis=1).astype(BF16)
    assert w_cat.shape[1] == c.total
    w = {k: full[k][l].astype(BF16) for k in ("ffn1_w_gate", "ffn1_w_up", "ffn1_w_down", "ffn2_w_gate", "ffn2_w_up",
                                              "ffn2_w_down", "w_proj_a", "w_proj_b", "w_proj_c", "w_out")}
    w["w_cat"] = w_cat
    f_true = w["ffn1_w_gate"].shape[1]
    pad = -f_true % FFN_BWD_TF
    for k in ("ffn1", "ffn2"):
        w[k + "_bwd"] = (jnp.pad(w[k + "_w_gate"], ((0, 0), (0, pad))), jnp.pad(w[k + "_w_up"], ((0, 0), (0, pad))),
                         jnp.pad(w[k + "_w_down"], ((0, pad), (0, 0))), f_true)
    w["ffn1_norm"] = full["ffn1_norm"][l][None].astype(F32)
    w["ffn2_norm"] = full["ffn2_norm"][l][None].astype(F32)
    w["mix_norm"] = full["mix_norm"][l][None].astype(F32)
    w["dn_conv"] = full["dn_conv"][l].astype(F32)
    w["dn_a_log"] = full["dn_a_log"][l].reshape(2 * DN_HEADS, 1).astype(F32)
    w["dn_dt_bias"] = full["dn_dt_bias"][l].reshape(2 * DN_HEADS, 1).astype(F32)
    w["dn_out_norm"] = full["dn_out_norm"][l][None].astype(F32)
    w["pool_w"] = full["pool_w"][l].astype(F32)
    w["pool_scale"] = full["pool_scale"][l].reshape(POOL_GROUPS, 1, POOL_DIM).astype(F32)
    w["b_gate"] = full["b_gate"][l].reshape(3, D).astype(F32)
    return w


def _layer_grads(g, D):
    c = _Cols(D)
    gc = g.pop("w_cat")
    out = dict(g)
    out["w_gate"] = gc[:, :c.da]
    out["w_in"] = jnp.concatenate([gc[:, c.qkv:c.pool], gc[:, c.ba:c.ba + 16], gc[:, c.pool:c.ba], gc[:, c.da:c.qkv]],
                                  axis=1)
    for k in ("ffn1_norm", "ffn2_norm", "mix_norm", "dn_out_norm"):
        out[k] = g[k][0]
    out["dn_a_log"] = g["dn_a_log"].reshape(2, DN_HEADS)
    out["dn_dt_bias"] = g["dn_dt_bias"].reshape(2, DN_HEADS)
    out["pool_scale"] = g["pool_scale"].reshape(POOL_WIDTH)
    out["b_gate"] = g["b_gate"].reshape(3 * D)
    return out


def _unshard(got):
    full = {}
    for k, t in zip(SHARDED, got):
        ax = SHARD_AXIS[k] - 1
        shp = t.shape[1:]
        full[k] = jnp.moveaxis(t, 0, ax).reshape(shp[:ax] + (N_DEV * shp[ax],) + shp[ax + 1:])
    return full


def _to_owner_blocks(grads):
    out = []
    for k in SHARDED:
        ax = SHARD_AXIS[k] - 1
        shp = grads[k].shape
        t = grads[k].reshape(shp[:ax] + (N_DEV, shp[ax] // N_DEV) + shp[ax + 1:])
        out.append(jnp.moveaxis(t, ax, 0).astype(BF16))
    return out


def _local_step(x, target, rep, shards, distributed):
    B, S, D = x.shape
    T = B * S
    depth = len(shards)
    xs = x.reshape(T, D)
    tape = []
    if distributed:
        sharded_now = _unshard(_all_gather("gather_l0", [shards[0][k] for k in SHARDED]))
    else:
        sharded_now = shards[0]
    for l in range(depth):
        full = {k: [v] * (l + 1) for k, v in sharded_now.items()}
        full.update({k: v for k, v in rep.items() if k != "final_norm"})
        w = _layer_weights(full, l, D)
        host = ("gather", [shards[l + 1][k] for k in SHARDED]) if distributed and l + 1 < depth else None
        x1 = _ffn_fwd(f"l{l}_ffn1", xs, w["ffn1_norm"], w["ffn1_w_gate"], w["ffn1_w_up"], w["ffn1_w_down"])
        x2, mixer_bwd, got = _mixer(l, x1, w, B, host)
        x3 = _ffn_fwd(f"l{l}_ffn2", x2, w["ffn2_norm"], w["ffn2_w_gate"], w["ffn2_w_up"], w["ffn2_w_down"])
        tape.append((w, xs, mixer_bwd, x2))
        xs = x3
        if l + 1 < depth:
            sharded_now = _unshard(got) if distributed else shards[l + 1]
    loss8, dx, dfinal = _loss_fwd_bwd("loss", xs, rep["final_norm"][None].astype(F32), target.reshape(T, D))
    per_layer = [None] * depth
    exchanged = [None] * depth
    pending = None
    for l in reversed(range(depth)):
        w, x0, mixer_bwd, x2 = tape[l]
        wg_p, wu_p, wd_p, f_true = w["ffn2_bwd"]
        dx, dn2, dwg2, dwu2, dwd2 = _ffn_bwd(f"l{l}_ffn2b", x2, w["ffn2_norm"], wg_p, wu_p, wd_p, dx, f_true)
        dx, g, got = mixer_bwd(dx, ("exchange", pending) if pending is not None else None)
        if pending is not None:
            exchanged[l + 1] = got
        wg_p, wu_p, wd_p, f_true = w["ffn1_bwd"]
        dx, dn1, dwg1, dwu1, dwd1 = _ffn_bwd(f"l{l}_ffn1b", x0, w["ffn1_norm"], wg_p, wu_p, wd_p, dx, f_true)
        g.update(ffn1_norm=dn1, ffn1_w_gate=dwg1, ffn1_w_up=dwu1, ffn1_w_down=dwd1,
                 ffn2_norm=dn2, ffn2_w_gate=dwg2, ffn2_w_up=dwu2, ffn2_w_down=dwd2)
        per_layer[l] = _layer_grads(g, D)
        if distributed:
            pending = _to_owner_blocks(per_layer[l])
    return loss8[0, 0], dx.reshape(B, S, D), per_layer, dfinal[0], exchanged, pending


def _mesh_position():
    mx, my, mc = lax.axis_index("x"), lax.axis_index("y"), lax.axis_index("c")
    return mx, my, mc, 4 * mx + 2 * my + mc


def _peers(mx, my, mc):
    out = []
    for k in range(1, N_DEV):
        px, py, pc = mx ^ ((k >> 2) & 1), my ^ ((k >> 1) & 1), mc ^ (k & 1)
        out.append(((px, py, pc), 4 * px + 2 * py + pc))
    return out


_ANY = pl.BlockSpec(memory_space=pl.ANY)


def _all_gather(name, xs):
    n = len(xs)

    def body(*refs):
        _gather_start(refs[:n], refs[n:2 * n], *refs[2 * n:])
        _gather_finish(refs[:n], refs[n:2 * n], *refs[2 * n:])

    sems = pltpu.SemaphoreType.DMA((n, N_DEV - 1))
    return pl.pallas_call(
        body, name=name, in_specs=[_ANY] * n, out_specs=[_ANY] * n,
        out_shape=[jax.ShapeDtypeStruct((N_DEV,) + x.shape, x.dtype) for x in xs],
        scratch_shapes=[sems, sems, pltpu.SemaphoreType.DMA((n,))],
    )(*xs)


class _GatherPlan:
    def __init__(self, x_refs, o_refs, send_sems, recv_sems, local_sems):
        self.x, self.o, self.ss, self.rs, self.ls = x_refs, o_refs, send_sems, recv_sems, local_sems
        self.mx, self.my, self.mc, self.me = _mesh_position()
        self.self_id = (self.mx, self.my, self.mc)
        self.sibling = (self.mx, self.my, 1 - self.mc)
        self.chips = [(1 - self.mx, self.my), (self.mx, 1 - self.my), (1 - self.mx, 1 - self.my)]

    def copy(self, a, k, blk, to, from_input=False):
        dst = self.o[a].at[blk]
        return pltpu.make_async_remote_copy(src_ref=self.x[a] if from_input else dst, dst_ref=dst,
                                            send_sem=self.ss.at[a, k], recv_sem=self.rs.at[a, k],
                                            device_id=to, device_id_type=pl.DeviceIdType.MESH)

    def own(self, a):
        return pltpu.make_async_copy(self.x[a], self.o[a].at[self.me], self.ls.at[a])

    def first_sends(self, a):
        cps = [self.copy(a, 0, self.me, self.sibling, from_input=True)]
        return cps + [self.copy(a, 1 + j, self.me, (*chip, self.mc), from_input=True) for j, chip in enumerate(self.chips)]

    def passed_on(self, a, j):
        cx, cy = self.chips[j]
        return self.copy(a, 4 + j, 4 * cx + 2 * cy + self.mc, self.sibling)


def _gather_start(x_refs, o_refs, send_sems, recv_sems, local_sems):
    p = _GatherPlan(x_refs, o_refs, send_sems, recv_sems, local_sems)
    for a in range(len(x_refs)):
        p.own(a).start()
        for cp in p.first_sends(a):
            cp.start()


def _gather_finish(x_refs, o_refs, send_sems, recv_sems, local_sems):
    p = _GatherPlan(x_refs, o_refs, send_sems, recv_sems, local_sems)
    n = len(x_refs)
    for a in range(n):
        for j, (cx, cy) in enumerate(p.chips):
            p.copy(a, 1 + j, 4 * cx + 2 * cy + p.mc, p.self_id).wait_recv()
            p.passed_on(a, j).start()
    for a in range(n):
        p.copy(a, 0, 4 * p.mx + 2 * p.my + 1 - p.mc, p.self_id).wait_recv()
        for j, (cx, cy) in enumerate(p.chips):
            p.copy(a, 4 + j, 4 * cx + 2 * cy + 1 - p.mc, p.self_id).wait_recv()
    for a in range(n):
        for cp in p.first_sends(a):
            cp.wait_send()
        for j in range(len(p.chips)):
            p.passed_on(a, j).wait_send()
        p.own(a).wait()


def _exchange_grads(name, gs, gr):
    ns, n = len(gs), len(gs) + len(gr)

    def body(*refs):
        _exchange_start(refs[:n], refs[n:2 * n], *refs[2 * n:], n_sharded=ns)
        _exchange_finish(refs[:n], refs[n:2 * n], *refs[2 * n:], n_sharded=ns)

    sems = pltpu.SemaphoreType.DMA((n, N_DEV - 1))
    outs = pl.pallas_call(
        body, name=name, in_specs=[_ANY] * n, out_specs=[_ANY] * n,
        out_shape=[jax.ShapeDtypeStruct(a.shape, a.dtype) for a in gs]
        + [jax.ShapeDtypeStruct((N_DEV,) + a.shape, a.dtype) for a in gr],
        scratch_shapes=[sems, sems, pltpu.SemaphoreType.DMA((n,))],
    )(*gs, *gr)
    return outs[:ns], outs[ns:]


def _exchange_copies(in_refs, out_refs, send_sems, recv_sems, local_sems, n_sharded):
    mx, my, mc, me = _mesh_position()
    n = len(in_refs)
    own = [pltpu.make_async_copy(in_refs[a].at[me] if a < n_sharded else in_refs[a], out_refs[a].at[me],
                                 local_sems.at[a]) for a in range(n)]
    remote = []
    for k, (peer, pid) in enumerate(_peers(mx, my, mc)):
        for a in range(n):
            src = in_refs[a].at[pid] if a < n_sharded else in_refs[a]
            remote.append(pltpu.make_async_remote_copy(
                src_ref=src, dst_ref=out_refs[a].at[me], send_sem=send_sems.at[a, k], recv_sem=recv_sems.at[a, k],
                device_id=peer, device_id_type=pl.DeviceIdType.MESH))
    return own, remote


def _exchange_start(in_refs, out_refs, send_sems, recv_sems, local_sems, n_sharded=None):
    ns = len(in_refs) if n_sharded is None else n_sharded
    own, remote = _exchange_copies(in_refs, out_refs, send_sems, recv_sems, local_sems, ns)
    for cp in own + remote:
        cp.start()


def _exchange_finish(in_refs, out_refs, send_sems, recv_sems, local_sems, n_sharded=None):
    ns = len(in_refs) if n_sharded is None else n_sharded
    own, remote = _exchange_copies(in_refs, out_refs, send_sems, recv_sems, local_sems, ns)
    for cp in remote:
        cp.wait_send()
        cp.wait_recv()
    for cp in own:
        cp.wait()


def _reduce_adamw(name, parts, w, m, v):
    shape = w.shape
    cols = shape[-1]
    w2, m2, v2 = (t.reshape(-1, cols) for t in (w, m, v))
    p3 = parts.reshape(N_DEV, -1, cols)
    rows = w2.shape[0]
    tr = _pick(rows, 512, 16) if rows > 1024 else rows
    c1 = 1.0 - ADAM_B1 ** ADAM_STEP
    c2 = 1.0 - ADAM_B2 ** ADAM_STEP

    def body(p_ref, w_ref, m_ref, v_ref, g_ref, d_ref, nm_ref, nv_ref):
        gv = p_ref[0].astype(F32)
        for d in range(1, N_DEV):
            gv = gv + p_ref[d].astype(F32)
        nm = ADAM_B1 * m_ref[...] + (1.0 - ADAM_B1) * gv
        nv = ADAM_B2 * v_ref[...] + (1.0 - ADAM_B2) * (gv * gv)
        g_ref[...] = gv
        d_ref[...] = -ADAM_LR * ((nm / c1) / (jnp.sqrt(nv / c2) + ADAM_EPS) + ADAM_WD * w_ref[...])
        nm_ref[...] = nm
        nv_ref[...] = nv

    spec = pl.BlockSpec((tr, cols), lambda i: (i, 0))
    outs = pl.pallas_call(
        body, name=name, grid=(rows // tr,),
        in_specs=[pl.BlockSpec((N_DEV, tr, cols), lambda i: (0, i, 0))] + [spec] * 3, out_specs=[spec] * 4,
        out_shape=[jax.ShapeDtypeStruct((rows, cols), F32)] * 4, compiler_params=_params(),
    )(p3, w2, m2, v2)
    return tuple(o.reshape(shape) for o in outs)


def kernel(x, ffn1_norm, ffn1_w_gate, ffn1_w_up, ffn1_w_down, mix_norm, w_in, dn_conv, dn_a_log, dn_dt_bias, dn_out_norm, pool_w, pool_scale, w_proj_a, w_proj_b, w_proj_c, w_gate, b_gate, w_out, ffn2_norm, ffn2_w_gate, ffn2_w_up, ffn2_w_down, final_norm, loss_target, m_ffn1_norm, m_ffn1_w_gate, m_ffn1_w_up, m_ffn1_w_down, m_mix_norm, m_w_in, m_dn_conv, m_dn_a_log, m_dn_dt_bias, m_dn_out_norm, m_pool_w, m_pool_scale, m_w_proj_a, m_w_proj_b, m_w_proj_c, m_w_gate, m_b_gate, m_w_out, m_ffn2_norm, m_ffn2_w_gate, m_ffn2_w_up, m_ffn2_w_down, m_final_norm, v_ffn1_norm, v_ffn1_w_gate, v_ffn1_w_up, v_ffn1_w_down, v_mix_norm, v_w_in, v_dn_conv, v_dn_a_log, v_dn_dt_bias, v_dn_out_norm, v_pool_w, v_pool_scale, v_w_proj_a, v_w_proj_b, v_w_proj_c, v_w_gate, v_b_gate, v_w_out, v_ffn2_norm, v_ffn2_w_gate, v_ffn2_w_up, v_ffn2_w_down, v_final_norm):
    args = locals()
    wts = {k: args[k] for k in WEIGHTS}
    ms = {k: args["m_" + k] for k in WEIGHTS}
    vs = {k: args["v_" + k] for k in WEIGHTS}

    depth = w_in.shape[0]
    rep = {k: wts[k] for k in REPLICATED}
    shards = [{k: wts[k][l].astype(BF16) for k in SHARDED} for l in range(depth)]
    loss_local, dx, per_layer, dfinal, exchanged, pending = _local_step(x, loss_target, rep, shards, True)
    loss = lax.psum(loss_local, ("x", "y", "c"))

    gr = [dfinal if k == "final_norm" else jnp.stack([pg[k] for pg in per_layer]).astype(F32).reshape(wts[k].shape)
          for k in REPLICATED]
    exchanged[0], got_r = _exchange_grads("exchange_grads", pending, gr)
    parts = {k: jnp.stack([exchanged[l][j] for l in range(depth)], axis=1) for j, k in enumerate(SHARDED)}
    parts.update(zip(REPLICATED, got_r))

    g_final, deltas, new_m, new_v = {}, {}, {}, {}
    for k in WEIGHTS:
        g_final[k], deltas[k], new_m[k], new_v[k] = _reduce_adamw("adamw_" + k, parts[k], wts[k], ms[k], vs[k])
    return (loss, dx, *[g_final[k] for k in WEIGHTS], *[deltas[k] for k in WEIGHTS], *[new_m[k] for k in WEIGHTS],
            *[new_v[k] for k in WEIGHTS])
```

```python
import functools
import math

import jax
import jax.numpy as jnp
from jax import lax
from jax.experimental import pallas as pl
from jax.experimental.pallas import tpu as pltpu

F32 = jnp.float32
BF16 = jnp.bfloat16

N_DEV = 8
RMS_EPS = 1e-6
L2_EPS = 1e-6
DN_HEADS = 4
DN_DIM = 128
DN_WIDTH = DN_HEADS * DN_DIM
DN_CONV = 5
DN_CHUNK = 64
DN_SUPER = 256
POOL_GROUPS = 4
POOL_DIM = 128
POOL_WIDTH = POOL_GROUPS * POOL_DIM
POOL_MAX_HALF = 8
DA_GROUPS = 3
DA_HEADS = 4
DA_DIM = 64
DA_WIDTH = DA_GROUPS * DA_HEADS * DA_DIM
DA_OUT = DA_HEADS * DA_DIM
DA_DILATIONS = (1, 4, 16)
DA_RADIUS = 64
DA_TQ = 256
FFN_BWD_TF = 512
ROPE_THETA = 10000.0
MASK_VALUE = -1e30
BA_PAD = 128

ADAM_LR = 0.001
ADAM_B1 = 0.9
ADAM_B2 = 0.999
ADAM_EPS = 1e-08
ADAM_WD = 0.01
ADAM_STEP = 10

VMEM_LIMIT_V7X = 56 * 1024 * 1024
LANES = 1024

SHARDED = ("ffn1_w_gate", "ffn1_w_up", "ffn1_w_down", "w_in", "dn_conv", "w_proj_a", "w_proj_b", "w_proj_c",
           "w_gate", "w_out", "ffn2_w_gate", "ffn2_w_up", "ffn2_w_down")
SHARD_AXIS = {"ffn1_w_gate": 2, "ffn1_w_up": 2, "ffn1_w_down": 1, "w_in": 2, "dn_conv": 2, "w_proj_a": 2,
              "w_proj_b": 2, "w_proj_c": 2, "w_gate": 2, "w_out": 1, "ffn2_w_gate": 2, "ffn2_w_up": 2,
              "ffn2_w_down": 1}
REPLICATED = ("ffn1_norm", "mix_norm", "dn_a_log", "dn_dt_bias", "dn_out_norm", "pool_w", "pool_scale", "b_gate",
              "ffn2_norm", "final_norm")
WEIGHTS = ("ffn1_norm", "ffn1_w_gate", "ffn1_w_up", "ffn1_w_down", "mix_norm", "w_in", "dn_conv", "dn_a_log",
           "dn_dt_bias", "dn_out_norm", "pool_w", "pool_scale", "w_proj_a", "w_proj_b", "w_proj_c", "w_gate",
           "b_gate", "w_out", "ffn2_norm", "ffn2_w_gate", "ffn2_w_up", "ffn2_w_down", "final_norm")


def _params(**kw):
    return pltpu.CompilerParams(vmem_limit_bytes=VMEM_LIMIT_V7X, **kw)


def _pick(n, target, align):
    best = None
    t = align
    while t <= min(n, target):
        if n % t == 0:
            best = t
        t += align
    return best if best is not None else n


_DIMS = {"nn": (((1,), (0,)), ((), ())), "nt": (((1,), (1,)), ((), ())), "tn": (((0,), (0,)), ((), ()))}


def _dg(a, b, mode):
    return lax.dot_general(a, b, _DIMS[mode], preferred_element_type=F32)


def _split2(a):
    hi = a.astype(BF16)
    lo = (a - hi.astype(F32)).astype(BF16)
    return hi, lo


def _dotp(a, b, mode, passes):
    if passes == 1:
        return _dg(a.astype(BF16), b.astype(BF16), mode)
    ah, al = _split2(a.astype(F32))
    bh, bl = _split2(b.astype(F32))
    return _dg(ah, bh, mode) + (_dg(ah, bl, mode) + _dg(al, bh, mode))


@functools.partial(jax.custom_vjp, nondiff_argnums=(2, 3))
def _dot(a, b, mode, passes):
    return _dotp(a, b, mode, passes)


def _dot_fwd(a, b, mode, passes):
    return _dotp(a, b, mode, passes), (a, b)


def _dot_bwd(mode, passes, res, ct):
    a, b = res
    if mode == "nn":
        da, db = _dotp(ct, b, "nt", passes), _dotp(a, ct, "tn", passes)
    elif mode == "nt":
        da, db = _dotp(ct, b, "nn", passes), _dotp(ct, a, "tn", passes)
    else:
        da, db = _dotp(b, ct, "nt", passes), _dotp(a, ct, "nn", passes)
    return da.astype(a.dtype), db.astype(b.dtype)


_dot.defvjp(_dot_fwd, _dot_bwd)


def _split3(x):
    x1 = x.astype(BF16)
    r = x - x1.astype(F32)
    x2 = r.astype(BF16)
    x3 = (r - x2.astype(F32)).astype(BF16)
    return x1, x2, x3


def _mdotp(mask, x, mode):
    x1, x2, x3 = _split3(x)
    return _dg(mask, x1, mode) + (_dg(mask, x2, mode) + _dg(mask, x3, mode))


@jax.custom_vjp
def _mdot(mask, x):
    return _mdotp(mask, x, "nn")


def _mdot_fwd(mask, x):
    return _mdotp(mask, x, "nn"), mask


def _mdot_bwd(mask, ct):
    return jnp.zeros_like(mask), _mdotp(mask, ct, "tn")


_mdot.defvjp(_mdot_fwd, _mdot_bwd)


_SOLVE_SQUARINGS = int(math.log2(DN_CHUNK)) - 1


def _unit_solve_fwd(A, R):
    Ab = A.astype(BF16)
    n = A.shape[0]
    eye = (lax.broadcasted_iota(jnp.int32, A.shape, 0) == lax.broadcasted_iota(jnp.int32, A.shape, 1)).astype(F32)
    Tm = eye - A
    P = _dg(Ab, Ab, "nn").astype(BF16)
    for _ in range(_SOLVE_SQUARINGS - 1):
        M = _dg(P, jnp.concatenate([Tm.astype(BF16), P], axis=1), "nn")
        Tm, P = Tm + M[:, :n], M[:, n:].astype(BF16)
    Tb = (Tm + _dg(P, Tm.astype(BF16), "nn")).astype(BF16)
    X = _dg(Tb, R.astype(BF16), "nn")
    return X, (Tb, X)


@jax.custom_vjp
def _unit_solve(A, R):
    return _unit_solve_fwd(A, R)[0]


def _unit_solve_bwd(res, dX):
    Tb, X = res
    Y = _dg(Tb, dX.astype(BF16), "tn")
    return -_dg(Y.astype(BF16), X.astype(BF16), "nt"), Y


_unit_solve.defvjp(_unit_solve_fwd, _unit_solve_bwd)


def _shift_impl(x, o):
    if o == 0:
        return x
    n = x.shape[0]
    y = pltpu.roll(x, (-o) % n, axis=0)
    t = lax.broadcasted_iota(jnp.int32, x.shape, 0) + o
    return jnp.where((t >= 0) & (t < n), y, 0.0)


@functools.partial(jax.custom_vjp, nondiff_argnums=(1,))
def _shift(x, o):
    return _shift_impl(x, o)


def _shift_fwd(x, o):
    return _shift_impl(x, o), None


def _shift_bwd(o, _, ct):
    return (_shift_impl(ct, -o),)


_shift.defvjp(_shift_fwd, _shift_bwd)


def _rot_impl(x):
    w = x.shape[1]
    half = DA_DIM // 2
    lane = lax.broadcasted_iota(jnp.int32, x.shape, 1)
    first = (lane & (DA_DIM - 1)) < half
    return jnp.where(first, -pltpu.roll(x, w - half, axis=1), pltpu.roll(x, half, axis=1))


@jax.custom_vjp
def _rot(x):
    return _rot_impl(x)


def _rot_fwd(x):
    return _rot_impl(x), None


def _rot_bwd(_, ct):
    return (-_rot_impl(ct),)


_rot.defvjp(_rot_fwd, _rot_bwd)


def _sigmoid(x):
    return 1.0 / (1.0 + jnp.exp(-x))


def _silu(x):
    return x * _sigmoid(x)


def _softplus(x):
    return jnp.maximum(x, 0.0) + jnp.log(1.0 + jnp.exp(-jnp.abs(x)))


def _rms(x, gain):
    return x * lax.rsqrt(jnp.mean(x * x, axis=-1, keepdims=True) + RMS_EPS) * gain


class _In:
    def __init__(self, arr, block, imap, kind="t", acc=False, g=None, gdtype=None, split=False):
        self.arr, self.block, self.imap, self.kind, self.acc, self.g, self.gdtype = arr, block, imap, kind, acc, g, gdtype
        self.split = split


class _Out:
    def __init__(self, shape, dtype, block, imap, split=False):
        self.shape, self.dtype, self.block, self.imap, self.split = shape, dtype, block, imap, split


def _sub_index(split, s):
    if not split:
        return Ellipsis
    return (s,) if split is True else split(s)


def _grid_edges(grid):
    first = last = None
    for a, n in enumerate(grid):
        f, l = pl.program_id(a) == 0, pl.program_id(a) == n - 1
        first = f if first is None else jnp.logical_and(first, f)
        last = l if last is None else jnp.logical_and(last, l)
    return first, last


def _comm_plumbing(comm):
    if comm is None:
        return [], [], [], lambda refs: None, lambda refs: None
    kind, arrs = comm
    n = len(arrs)
    if kind == "gather":
        shapes = [jax.ShapeDtypeStruct((N_DEV,) + a.shape, a.dtype) for a in arrs]
        start, finish = _gather_start, _gather_finish
    else:
        shapes = [jax.ShapeDtypeStruct(a.shape, a.dtype) for a in arrs]
        start, finish = _exchange_start, _exchange_finish
    sems = [pltpu.SemaphoreType.DMA((n, N_DEV - 1)), pltpu.SemaphoreType.DMA((n, N_DEV - 1)),
            pltpu.SemaphoreType.DMA((n,))]
    return list(arrs), shapes, sems, start, finish


def _first_step(acc_from, ngrid):
    c = None
    for a in range(acc_from, ngrid):
        t = pl.program_id(a) == 0
        c = t if c is None else jnp.logical_and(c, t)
    return c


def _tile_fwd(name, f, grid, ins, outs, sub=1, comm=None):
    n_in, n_out = len(ins), len(outs)
    ngrid = len(grid)
    c_arrs, c_shapes, c_sems, c_start, c_finish = _comm_plumbing(comm)
    nc = len(c_arrs)

    def body(*refs):
        in_refs, c_in = refs[:n_in], refs[n_in:n_in + nc]
        out_refs, c_out = refs[n_in + nc:n_in + nc + n_out], refs[n_in + nc + n_out:n_in + 2 * nc + n_out]
        sems = refs[n_in + 2 * nc + n_out:]
        pids = tuple(pl.program_id(a) for a in range(ngrid))
        if nc:
            first, last = _grid_edges(grid)
            pl.when(first)(lambda: c_start(c_in, c_out, *sems))
        for s in range(sub):
            vals = [r[_sub_index(i.split, s)] for r, i in zip(in_refs, ins)]
            vals = [v.astype(F32) if i.kind == "t" else v for v, i in zip(vals, ins)]
            res = f(pids + ((s,) if sub > 1 else ()), *vals)
            for r, o, v in zip(out_refs, outs, res):
                r[_sub_index(o.split, s)] = v.astype(r.dtype)
        if nc:
            pl.when(last)(lambda: c_finish(c_in, c_out, *sems))

    res = pl.pallas_call(
        body, name=name, grid=grid,
        in_specs=[pl.BlockSpec(i.block, i.imap) for i in ins] + [_ANY] * nc,
        out_specs=[pl.BlockSpec(o.block, o.imap) for o in outs] + [_ANY] * nc,
        out_shape=[jax.ShapeDtypeStruct(o.shape, o.dtype) for o in outs] + c_shapes,
        scratch_shapes=c_sems, compiler_params=_params(),
    )(*[i.arr for i in ins], *c_arrs)
    return (res[:n_out], res[n_out:]) if nc else res


def _tile_bwd(name, f, grid, ins, outs, cts, acc_from=None, addends=None, sub=1, comm=None):
    n_in, n_out = len(ins), len(outs)
    ngrid = len(grid)
    diff = [k for k, i in enumerate(ins) if i.kind == "t"]
    addends = addends or {}
    add_keys = sorted(addends)
    n_add, n_g = len(add_keys), len(diff)
    c_arrs, c_shapes, c_sems, c_start, c_finish = _comm_plumbing(comm)
    nc = len(c_arrs)

    def body(*refs):
        pids = tuple(pl.program_id(a) for a in range(ngrid))
        in_refs = refs[:n_in]
        ct_refs = refs[n_in:n_in + n_out]
        add_refs = refs[n_in + n_out:n_in + n_out + n_add]
        o = n_in + n_out + n_add
        c_in, g_refs, c_out, sems = refs[o:o + nc], refs[o + nc:o + nc + n_g], refs[o + nc + n_g:o + 2 * nc + n_g], \
            refs[o + 2 * nc + n_g:]
        if nc:
            first_step, last_step = _grid_edges(grid)
            pl.when(first_step)(lambda: c_start(c_in, c_out, *sems))
        sums = {}
        for s in range(sub):
            vals = [r[_sub_index(i.split, s)] for r, i in zip(in_refs, ins)]
            dvals = [vals[k].astype(F32) for k in diff]

            def g(*d, vals=vals, s=s):
                full = list(vals)
                for k, dk in zip(diff, d):
                    full[k] = dk
                return tuple(f(pids + ((s,) if sub > 1 else ()), *full))

            res, vjp = jax.vjp(g, *dvals)
            cvals = [c[_sub_index(o_.split, s)].astype(r.dtype) for c, o_, r in zip(ct_refs, outs, res)]
            grads = vjp(tuple(cvals))
            for k, gr in zip(diff, grads):
                idx = _sub_index(ins[k].split, s)
                key = (k, str(idx))
                sums[key] = (idx, gr if key not in sums else sums[key][1] + gr)
        first = _first_step(acc_from, ngrid) if acc_from is not None else None
        for (k, _), (idx, gr) in sums.items():
            gref = g_refs[diff.index(k)]
            if idx is not Ellipsis:
                gref[idx] = gr.astype(gref.dtype)
                continue
            if k in addends:
                gr = gr + add_refs[add_keys.index(k)][...].astype(F32)
            if ins[k].acc and first is not None:
                @pl.when(first)
                def _(gr=gr, gref=gref):
                    gref[...] = gr.astype(gref.dtype)

                @pl.when(jnp.logical_not(first))
                def _(gr=gr, gref=gref):
                    gref[...] += gr.astype(gref.dtype)
            else:
                gref[...] = gr.astype(gref.dtype)
        if nc:
            pl.when(last_step)(lambda: c_finish(c_in, c_out, *sems))

    g_shapes, g_specs = [], []
    for k in diff:
        i = ins[k]
        if i.g is not None:
            shape, imap = i.g
        else:
            shape, imap = i.arr.shape, i.imap
        dt = i.gdtype or (F32 if i.acc else i.arr.dtype)
        g_shapes.append(jax.ShapeDtypeStruct(shape, dt))
        g_specs.append(pl.BlockSpec(i.block, imap))
    add_specs = [pl.BlockSpec(ins[k].block, ins[k].g[1] if ins[k].g is not None else ins[k].imap) for k in add_keys]
    res = pl.pallas_call(
        body, name=name, grid=grid,
        in_specs=[pl.BlockSpec(i.block, i.imap) for i in ins] + [pl.BlockSpec(o.block, o.imap) for o in outs] + add_specs
        + [_ANY] * nc,
        out_specs=g_specs + [_ANY] * nc, out_shape=g_shapes + c_shapes,
        scratch_shapes=c_sems, compiler_params=_params(),
    )(*[i.arr for i in ins], *cts, *[addends[k] for k in add_keys], *c_arrs)
    return (res[:n_g], res[n_g:]) if nc else res


def _mm(name, a, b, mode, out_dtype=F32, add=None, tm=1024, tn=1024, tk=1024, m=None, n=None):
    if mode == "nn":
        (M, K), N = a.shape, b.shape[1]
    elif mode == "nt":
        (M, K), N = a.shape, b.shape[0]
    else:
        (K, M), N = a.shape, b.shape[1]
    M, N = m or M, n or N
    tm, tn, tk = _pick(M, tm, 128), _pick(N, tn, 128), _pick(K, tk, 128)
    nk = K // tk
    a_spec = pl.BlockSpec((tk, tm), lambda i, j, k: (k, i)) if mode == "tn" else pl.BlockSpec((tm, tk), lambda i, j, k: (i, k))
    b_spec = pl.BlockSpec((tn, tk), lambda i, j, k: (j, k)) if mode == "nt" else pl.BlockSpec((tk, tn), lambda i, j, k: (k, j))
    o_spec = pl.BlockSpec((tm, tn), lambda i, j, k: (i, j))

    def body(*refs):
        if add is None:
            a_ref, b_ref, o_ref, acc = refs
            add_ref = None
        else:
            a_ref, b_ref, add_ref, o_ref, acc = refs
        k = pl.program_id(2)

        @pl.when(k == 0)
        def _():
            acc[...] = jnp.zeros_like(acc)

        acc[...] += _dg(a_ref[...].astype(BF16), b_ref[...].astype(BF16), mode)

        @pl.when(k == nk - 1)
        def _():
            r = acc[...]
            if add_ref is not None:
                r = r + add_ref[...].astype(F32)
            o_ref[...] = r.astype(o_ref.dtype)

    ops = (a, b) if add is None else (a, b, add)
    specs = [a_spec, b_spec] + ([] if add is None else [o_spec])
    return pl.pallas_call(
        body, name=name, grid=(M // tm, N // tn, nk), in_specs=specs, out_specs=o_spec,
        out_shape=jax.ShapeDtypeStruct((M, N), out_dtype), scratch_shapes=[pltpu.VMEM((tm, tn), F32)],
        compiler_params=_params(dimension_semantics=("parallel", "parallel", "arbitrary")),
    )(*ops)


def _ffn_fwd(name, x, gain, wg, wu, wd):
    T, D = x.shape
    F = wg.shape[1]
    tm, tf = _pick(T, 1024, 8), _pick(F, 256, 128)
    nf = F // tf

    def body(x_ref, g_ref, wg_ref, wu_ref, wd_ref, o_ref, h_ref, acc):
        j = pl.program_id(1)

        @pl.when(j == 0)
        def _():
            h_ref[...] = _rms(x_ref[...], g_ref[...]).astype(BF16)
            acc[...] = jnp.zeros_like(acc)

        h = h_ref[...]
        a = _dg(h, wg_ref[...], "nn")
        b = _dg(h, wu_ref[...], "nn")
        s = (_silu(a) * b).astype(BF16)
        acc[...] += _dg(s, wd_ref[...], "nn")

        @pl.when(j == nf - 1)
        def _():
            o_ref[...] = x_ref[...] + 0.5 * acc[...]

    return pl.pallas_call(
        body, name=name, grid=(T // tm, nf),
        in_specs=[pl.BlockSpec((tm, D), lambda i, j: (i, 0)), pl.BlockSpec((1, D), lambda i, j: (0, 0)),
                  pl.BlockSpec((D, tf), lambda i, j: (0, j)), pl.BlockSpec((D, tf), lambda i, j: (0, j)),
                  pl.BlockSpec((tf, D), lambda i, j: (j, 0))],
        out_specs=pl.BlockSpec((tm, D), lambda i, j: (i, 0)),
        out_shape=jax.ShapeDtypeStruct((T, D), F32),
        scratch_shapes=[pltpu.VMEM((tm, D), BF16), pltpu.VMEM((tm, D), F32)],
        compiler_params=_params(dimension_semantics=("parallel", "arbitrary")),
    )(x, gain, wg, wu, wd)


def _ffn_bwd(name, x, gain, wg, wu, wd, dy, f_true):
    T, D = x.shape
    F = wg.shape[1]
    tm, tf = _pick(T, 512, 8), _pick(F, FFN_BWD_TF, 128)
    nf = F // tf

    def body(x_ref, g_ref, wg_ref, wu_ref, wd_ref, dy_ref, dx_ref, dg_ref, da_ref, db_ref, s_ref, h_ref, dyh_ref, dh):
        i, j = pl.program_id(0), pl.program_id(1)

        @pl.when(j == 0)
        def _():
            h_ref[...] = _rms(x_ref[...], g_ref[...]).astype(BF16)
            dyh_ref[...] = (0.5 * dy_ref[...]).astype(BF16)
            dh[...] = jnp.zeros_like(dh)

        h = h_ref[...]
        a = _dg(h, wg_ref[...], "nn")
        b = _dg(h, wu_ref[...], "nn")
        ds = _dg(dyh_ref[...], wd_ref[...], "nt")
        sig = _sigmoid(a)
        silu = a * sig
        da = (ds * b * (sig * (1.0 + a * (1.0 - sig)))).astype(BF16)
        db = (ds * silu).astype(BF16)
        da_ref[...] = da
        db_ref[...] = db
        s_ref[...] = (silu * b).astype(BF16)
        dh[...] += _dg(da, wg_ref[...], "nt") + _dg(db, wu_ref[...], "nt")

        @pl.when(j == nf - 1)
        def _():
            _, vjp = jax.vjp(_rms, x_ref[...], g_ref[...])
            dxn, dgn = vjp(dh[...])
            dx_ref[...] = dy_ref[...] + dxn

            @pl.when(i == 0)
            def _():
                dg_ref[...] = dgn

            @pl.when(i != 0)
            def _():
                dg_ref[...] += dgn

    row = lambda i, j: (i, 0)
    col = lambda i, j: (i, j)
    dx, dgain, da, db, s, h, dyh = pl.pallas_call(
        body, name=name, grid=(T // tm, nf),
        in_specs=[pl.BlockSpec((tm, D), row), pl.BlockSpec((1, D), lambda i, j: (0, 0)),
                  pl.BlockSpec((D, tf), lambda i, j: (0, j)), pl.BlockSpec((D, tf), lambda i, j: (0, j)),
                  pl.BlockSpec((tf, D), lambda i, j: (j, 0)), pl.BlockSpec((tm, D), row)],
        out_specs=[pl.BlockSpec((tm, D), row), pl.BlockSpec((1, D), lambda i, j: (0, 0)),
                   pl.BlockSpec((tm, tf), col), pl.BlockSpec((tm, tf), col), pl.BlockSpec((tm, tf), col),
                   pl.BlockSpec((tm, D), row), pl.BlockSpec((tm, D), row)],
        out_shape=[jax.ShapeDtypeStruct((T, D), F32), jax.ShapeDtypeStruct((1, D), F32),
                   jax.ShapeDtypeStruct((T, F), BF16), jax.ShapeDtypeStruct((T, F), BF16),
                   jax.ShapeDtypeStruct((T, F), BF16), jax.ShapeDtypeStruct((T, D), BF16),
                   jax.ShapeDtypeStruct((T, D), BF16)],
        scratch_shapes=[pltpu.VMEM((tm, D), F32)],
        compiler_params=_params(),
    )(x, gain, wg, wu, wd, dy)
    dwg = _mm(name + "_dwg", h, da, "tn", tm=1024, tn=1408, tk=1024, n=f_true)
    dwu = _mm(name + "_dwu", h, db, "tn", tm=1024, tn=1408, tk=1024, n=f_true)
    dwd = _mm(name + "_dwd", s, dyh, "tn", tm=1408, tn=1024, tk=1024, m=f_true)
    return dx, dgain, dwg, dwu, dwd


def _norm_f(pids, x, gain):
    return (_rms(x, gain),)


def _dn_conv_f(pids, x, w):
    j = pids[0]
    tap = lax.broadcasted_iota(jnp.int32, w.shape, 0)
    y = jnp.zeros_like(x)
    for t in range(DN_CONV):
        wt = jnp.sum(jnp.where(tap == t, w, 0.0), axis=0, keepdims=True)
        y = y + _shift(x, t - DN_CONV // 2) * wt
    y = _silu(y)
    n = y * lax.rsqrt(jnp.sum(y * y, axis=-1, keepdims=True) + L2_EPS)
    is_q = (j < DN_HEADS).astype(F32)
    is_qk = (j < 2 * DN_HEADS).astype(F32)
    scale = is_q * (DN_DIM ** -0.5) + (1.0 - is_q)
    return ((is_qk * n + (1.0 - is_qk) * y) * scale,)


def _dn_gate_f(pids, braw, araw, a_log, dt_bias):
    beta = _sigmoid(braw)
    g = -jnp.exp(a_log) * _softplus(araw + dt_bias)
    return beta, g


def _dn_prep_f(pids, q, k, v, brow, grow):
    cs = DN_SUPER
    sign = 1 - 2 * pids[2]
    ii = lax.broadcasted_iota(jnp.int32, (cs, cs), 0)
    jj = lax.broadcasted_iota(jnp.int32, (cs, cs), 1)
    shift = int(math.log2(DN_CHUNK))
    same = (ii >> shift) == (jj >> shift)
    d = (ii - jj) * sign
    incl = same & (d >= 0)
    strict = same & (d > 0)
    eye = ii == jj
    g_col = jnp.sum(jnp.where(eye, jnp.broadcast_to(grow, (cs, cs)), 0.0), axis=1, keepdims=True)
    b_col = jnp.sum(jnp.where(eye, jnp.broadcast_to(brow, (cs, cs)), 0.0), axis=1, keepdims=True)
    g128 = jnp.broadcast_to(g_col, (cs, DN_DIM))
    G = _mdot(incl.astype(BF16), g128)
    Gt = _mdot(same.astype(BF16), g128)
    Gc = jnp.concatenate([G, G], axis=1)
    Grow = jnp.sum(jnp.where(eye, Gc, 0.0), axis=0, keepdims=True)
    decay = jnp.exp(jnp.where(incl, Gc - Grow, MASK_VALUE))
    eG = jnp.exp(G)
    kb = k * b_col
    A = jnp.where(strict, _dot(kb, k, "nt", 1) * decay, 0.0)
    X = _unit_solve(A, jnp.concatenate([v * b_col, kb * eG], axis=1))
    qk = jnp.where(incl, _dot(q, k, "nt", 1) * decay, 0.0)
    return X, qk, q * eG, k * jnp.exp(Gt - G), jnp.exp(Gt)


def _dn_out_f(pids, of, ob, z, gain):
    return (_rms(of + ob, gain) * _silu(z),)


def _pool_f(pids, u, w, scale):
    g = pids[0]
    half = jnp.left_shift(1, g)
    n = u.shape[0]
    pos = lax.broadcasted_iota(jnp.int32, (n, 1), 0)
    tot = jnp.zeros_like(u)
    cnt = jnp.zeros((n, 1), F32)
    for o in range(-POOL_MAX_HALF, POOL_MAX_HALF):
        use = ((o >= -half) & (o < half)).astype(F32)
        tot = tot + use * _shift(u, o)
        cnt = cnt + use * ((pos + o >= 0) & (pos + o < n)).astype(F32)
    pooled = tot / cnt - u
    return (_dot(pooled, w, "nn", 1) * scale,)


def _rope_f(pids, *args):
    cos, sin = args[-2:]
    qs, ks, vs = args[:DA_GROUPS], args[DA_GROUPS:2 * DA_GROUPS], args[2 * DA_GROUPS:3 * DA_GROUPS]
    qr = [(q * cos + _rot(q) * sin) * (DA_DIM ** -0.5) for q in qs]
    kr = [k * cos + _rot(k) * sin for k in ks]
    return (*qr, *kr, *vs)


def _attn_head(q, k, v, qpos0, kpos0):
    s = _dot(q, k, "nt", 1)
    qi = qpos0 + lax.broadcasted_iota(jnp.int32, s.shape, 0)
    kj = kpos0 + lax.broadcasted_iota(jnp.int32, s.shape, 1)
    s = jnp.where(jnp.abs(kj - qi) <= DA_RADIUS, s, MASK_VALUE)
    m = lax.stop_gradient(jnp.max(s, axis=1, keepdims=True))
    p = jnp.exp(s - m)
    l = jnp.sum(p, axis=1, keepdims=True)
    o = _dot(p, v, "nn", 1) / l
    return o, jnp.broadcast_to(m + jnp.log(l), o.shape)


def _merge_f(pids, o0, o1, o2, l0, l1, l2):
    m = jnp.maximum(jnp.maximum(l0, l1), l2)
    e0, e1, e2 = jnp.exp(l0 - m), jnp.exp(l1 - m), jnp.exp(l2 - m)
    return ((e0 * o0 + e1 * o1 + e2 * o2) / (e0 + e1 + e2),)


def _gate_f(pids, g0, g1, g2, ya, yb, yc, b0, b1, b2):
    return (_sigmoid(g0 + b0) * ya + _sigmoid(g1 + b1) * yb + _sigmoid(g2 + b2) * yc,)


def _attn_window(i, L, tq, W):
    k0 = jnp.clip(i * tq - DA_RADIUS, 0, L - W)
    return pl.multiple_of(k0, DA_RADIUS)


def _strided_view(t, B, dil):
    T, HD = t.shape
    return t.reshape(B, T // B // dil, dil * HD)


def _attn_fwd(name, q, k, v, B, dil):
    T, HD = q.shape
    NS, L = B * dil, T // B // dil
    tq = min(DA_TQ, L)
    W = min(L, tq + 2 * DA_RADIUS)

    def body(q_ref, k_ref, v_ref, o_ref, l_ref):
        i = pl.program_id(1)
        k0 = _attn_window(i, L, tq, W)
        for h in range(DA_HEADS):
            hs = slice(h * DA_DIM, (h + 1) * DA_DIM)
            o, lse = _attn_head(q_ref[:, hs], k_ref[pl.ds(k0, W), hs], v_ref[pl.ds(k0, W), hs], i * tq, k0)
            o_ref[:, hs] = o
            l_ref[:, hs] = lse

    qs = pl.BlockSpec((None, tq, HD), lambda s, i: (s // dil, i, s % dil))
    ks = pl.BlockSpec((None, L, HD), lambda s, i: (s // dil, 0, s % dil))
    o, lse = pl.pallas_call(
        body, name=name, grid=(NS, L // tq), in_specs=[qs, ks, ks], out_specs=[qs, qs],
        out_shape=[jax.ShapeDtypeStruct((B, L, dil * HD), F32)] * 2, compiler_params=_params(),
    )(*[_strided_view(t, B, dil) for t in (q, k, v)])
    return o.reshape(T, HD), lse.reshape(T, HD)


def _attn_bwd(name, q, k, v, do, dl, B, dil):
    T, HD = q.shape
    NS, L = B * dil, T // B // dil
    tq = min(DA_TQ, L)
    W = min(L, tq + 2 * DA_RADIUS)

    def body(q_ref, k_ref, v_ref, do_ref, dl_ref, dq_ref, dk_ref, dv_ref):
        i = pl.program_id(1)
        k0 = _attn_window(i, L, tq, W)

        @pl.when(i == 0)
        def _():
            dk_ref[...] = jnp.zeros_like(dk_ref)
            dv_ref[...] = jnp.zeros_like(dv_ref)

        for h in range(DA_HEADS):
            hs = slice(h * DA_DIM, (h + 1) * DA_DIM)
            f = functools.partial(_attn_head, qpos0=i * tq, kpos0=k0)
            _, vjp = jax.vjp(f, q_ref[:, hs].astype(F32), k_ref[pl.ds(k0, W), hs].astype(F32),
                             v_ref[pl.ds(k0, W), hs].astype(F32))
            dq, dk, dv = vjp((do_ref[:, hs], dl_ref[:, hs]))
            dq_ref[:, hs] = dq
            dk_ref[pl.ds(k0, W), hs] += dk
            dv_ref[pl.ds(k0, W), hs] += dv

    qs = pl.BlockSpec((None, tq, HD), lambda s, i: (s // dil, i, s % dil))
    ks = pl.BlockSpec((None, L, HD), lambda s, i: (s // dil, 0, s % dil))
    res = pl.pallas_call(
        body, name=name, grid=(NS, L // tq), in_specs=[qs, ks, ks, qs, qs], out_specs=[qs, ks, ks],
        out_shape=[jax.ShapeDtypeStruct((B, L, dil * HD), F32)] * 3, compiler_params=_params(),
    )(*[_strided_view(t, B, dil) for t in (q, k, v, do, dl)])
    return tuple(t.reshape(T, HD) for t in res)


def _scan_chunk(t, rev, N):
    c = jnp.where(rev, N - 1 - t, t)
    per = DN_SUPER // DN_CHUNK
    return c, pl.multiple_of(c * DN_CHUNK, DN_CHUNK), pl.multiple_of((c % per) * DN_CHUNK, DN_CHUNK), \
        pl.multiple_of((c // per) * DN_SUPER, DN_SUPER)


def _dn_scan_fwd(name, uw, qk, qd, kd, gl, B):
    R, T, _ = uw.shape
    S = T // B
    N = S // DN_CHUNK
    C, DK = DN_CHUNK, DN_DIM

    PAIR = 2

    def body(uw_ref, qk_ref, qd_ref, kd_ref, gl_ref, o_ref, st_ref, vn_ref):
        rev = pl.program_id(1) * PAIR >= DN_HEADS
        vn_ref[...] = jnp.zeros_like(vn_ref)

        def step(t, states):
            c, r0, w0, s0 = _scan_chunk(t, rev, N)
            rows = pl.ds(r0, C)
            new = []
            for p, state in enumerate(states):
                st_ref[p, c] = state
                vnew = uw_ref[p, rows, 0:DK] - _dotp(uw_ref[p, rows, DK:2 * DK], state, "nn", 1)
                vn_ref[p, pl.ds(w0, C), :] = vnew
                o_ref[p, rows, :] = (_dotp(qd_ref[p, rows, :], state, "nn", 1)
                                     + _dotp(qk_ref[p, rows, :], vn_ref[p], "nn", 1))
                new.append(state * gl_ref[p, pl.ds(r0, 1), :] + _dotp(kd_ref[p, rows, :], vnew, "tn", 1))
            return tuple(new)

        lax.fori_loop(0, N, step, tuple(jnp.zeros((DK, DK), F32) for _ in range(PAIR)))

    def seq(w):
        return pl.BlockSpec((PAIR, S, w), lambda b, r: (r, b, 0))

    return pl.pallas_call(
        body, name=name, grid=(B, R // PAIR),
        in_specs=[seq(2 * DK), seq(DN_SUPER), seq(DK), seq(DK), seq(DK)],
        out_specs=[seq(DK), pl.BlockSpec((None, PAIR, N, DK, DK), lambda b, r: (b, r, 0, 0, 0))],
        out_shape=[jax.ShapeDtypeStruct((R, T, DK), F32), jax.ShapeDtypeStruct((B, R, N, DK, DK), F32)],
        scratch_shapes=[pltpu.VMEM((PAIR, DN_SUPER, DK), F32)], compiler_params=_params(),
    )(uw, qk, qd, kd, gl)


def _dn_scan_bwd(name, uw, qk, qd, kd, gl, st, do, B):
    R, T, _ = uw.shape
    S = T // B
    N = S // DN_CHUNK
    C, DK = DN_CHUNK, DN_DIM

    def body(uw_ref, qk_ref, qd_ref, kd_ref, gl_ref, st_ref, do_ref, duw_ref, dqk_ref, dqd_ref, dkd_ref, dgl_ref,
             vn_ref, tmp_ref):
        rev = pl.program_id(1) >= DN_HEADS
        vn_ref[...] = jnp.zeros_like(vn_ref)
        dgl_ref[...] = jnp.zeros_like(dgl_ref)

        def step(t, dstate):
            c, r0, w0, s0 = _scan_chunk(N - 1 - t, rev, N)
            rows = pl.ds(r0, C)
            state = st_ref[c]
            w = uw_ref[rows, DK:2 * DK]
            vnew = uw_ref[rows, 0:DK] - _dotp(w, state, "nn", 1)
            vn_ref[pl.ds(w0, C), :] = vnew
            do_c = do_ref[rows, :]
            tmp_ref[...] = _dotp(qk_ref[rows, :], do_c, "tn", 1)
            dvn = tmp_ref[pl.ds(w0, C), :] + _dotp(kd_ref[rows, :], dstate, "nn", 1)
            dqk_ref[rows, :] = _dotp(do_c, vn_ref[...], "nt", 1)
            dqd_ref[rows, :] = _dotp(do_c, state, "nt", 1)
            dkd_ref[rows, :] = _dotp(vnew, dstate, "nt", 1)
            dgl_ref[pl.ds(r0, 1), :] = jnp.sum(state * dstate, axis=0, keepdims=True)
            duw_ref[rows, 0:DK] = dvn
            duw_ref[rows, DK:2 * DK] = -_dotp(dvn, state, "nt", 1)
            return (_dotp(qd_ref[rows, :], do_c, "tn", 1) + dstate * gl_ref[pl.ds(r0, 1), :]
                    - _dotp(w, dvn, "tn", 1))

        lax.fori_loop(0, N, step, jnp.zeros((DK, DK), F32))

    def seq(w):
        return pl.BlockSpec((None, S, w), lambda b, r: (r, b, 0))

    return pl.pallas_call(
        body, name=name, grid=(B, R),
        in_specs=[seq(2 * DK), seq(DN_SUPER), seq(DK), seq(DK), seq(DK),
                  pl.BlockSpec((None, None, N, DK, DK), lambda b, r: (b, r, 0, 0, 0)),
                  pl.BlockSpec((None, S, DK), lambda b, r: (r % DN_HEADS, b, 0))],
        out_specs=[seq(2 * DK), seq(DN_SUPER), seq(DK), seq(DK), seq(DK)],
        out_shape=[jax.ShapeDtypeStruct((R, T, 2 * DK), F32), jax.ShapeDtypeStruct((R, T, DN_SUPER), F32),
                   jax.ShapeDtypeStruct((R, T, DK), F32), jax.ShapeDtypeStruct((R, T, DK), F32),
                   jax.ShapeDtypeStruct((R, T, DK), F32)],
        scratch_shapes=[pltpu.VMEM((DN_SUPER, DK), F32), pltpu.VMEM((DN_SUPER, DK), F32)],
        compiler_params=_params(),
    )(uw, qk, qd, kd, gl, st, do)


def _loss_fwd_bwd(name, x, gain, target):
    T, D = x.shape
    tm = _pick(T, 512, 8)

    def body(x_ref, g_ref, t_ref, loss_ref, dx_ref, dg_ref):
        i = pl.program_id(0)

        def f(xv, gv):
            e = _rms(xv, gv) - t_ref[...]
            return 0.5 * jnp.sum(jnp.mean(e * e, axis=-1, keepdims=True))

        val, (dx, dg) = jax.value_and_grad(f, argnums=(0, 1))(x_ref[...], g_ref[...])
        dx_ref[...] = dx
        part = jnp.full(loss_ref.shape, val, F32)

        @pl.when(i == 0)
        def _():
            dg_ref[...] = dg
            loss_ref[...] = part

        @pl.when(i != 0)
        def _():
            dg_ref[...] += dg
            loss_ref[...] += part

    return pl.pallas_call(
        body, name=name, grid=(T // tm,),
        in_specs=[pl.BlockSpec((tm, D), lambda i: (i, 0)), pl.BlockSpec((1, D), lambda i: (0, 0)),
                  pl.BlockSpec((tm, D), lambda i: (i, 0))],
        out_specs=[pl.BlockSpec((8, 128), lambda i: (0, 0)), pl.BlockSpec((tm, D), lambda i: (i, 0)),
                   pl.BlockSpec((1, D), lambda i: (0, 0))],
        out_shape=[jax.ShapeDtypeStruct((8, 128), F32), jax.ShapeDtypeStruct((T, D), F32),
                   jax.ShapeDtypeStruct((1, D), F32)],
        compiler_params=_params(),
    )(x, gain, target)


class _Cols:
    def __init__(self, D):
        assert D % 256 == 0
        self.gate = 0
        self.da = 3 * D
        self.qkv = self.da + 3 * DA_WIDTH
        self.z = self.qkv + 3 * DN_WIDTH
        self.pool = self.z + DN_WIDTH
        self.ba = self.pool + POOL_WIDTH
        self.total = self.ba + BA_PAD


def _rope_tables(S):
    half = DA_DIM // 2
    inv_freq = ROPE_THETA ** (-jnp.arange(half, dtype=F32) / half)
    ang = jnp.arange(S, dtype=F32)[:, None] * inv_freq[None, :]
    reps = DA_OUT // DA_DIM
    cos = jnp.tile(jnp.concatenate([jnp.cos(ang), jnp.cos(ang)], axis=1), (1, reps))
    sin = jnp.tile(jnp.concatenate([jnp.sin(ang), jnp.sin(ang)], axis=1), (1, reps))
    return cos, sin


def _to_strided(t, B, dil):
    T, w = t.shape
    L = T // B // dil
    return t.reshape(B, L, dil, w).transpose(0, 2, 1, 3).reshape(B * dil, L, w)


def _from_strided(t, B, dil):
    NS, L, w = t.shape
    return t.reshape(B, dil, L, w).transpose(0, 2, 1, 3).reshape(B * dil * L, w)


def _mixer(l, x1, w, B, host_gather=None):
    T, D = x1.shape
    S = T // B
    c = _Cols(D)
    tm = _pick(S, 512, 8)
    nmS = S // tm
    n = f"l{l}_"

    norm_ins = [_In(x1, (tm, D), lambda i: (i, 0)), _In(w["mix_norm"], (1, D), lambda i: (0, 0), acc=True)]
    norm_outs = [_Out((T, D), BF16, (tm, D), lambda i: (i, 0))]
    (h,) = _tile_fwd(n + "norm", _norm_f, (T // tm,), norm_ins, norm_outs)
    P = _mm(n + "proj", h, w["w_cat"], "nn", out_dtype=BF16, tm=512, tn=2688, tk=1024)
    P_ba = _mm(n + "proj_ba", h, w["w_cat"][:, c.ba:], "nn")
    baT = P_ba[:, :16].T

    cb = c.qkv // DN_DIM
    conv_ins = [_In(P, (S, DN_DIM), lambda j, b: (b, cb + j), g=((T, 3 * DN_WIDTH), lambda j, b: (b, j)), gdtype=BF16),
                _In(w["dn_conv"], (DN_CONV, DN_DIM), lambda j, b: (0, j), acc=True)]
    conv_outs = [_Out((T, 3 * DN_WIDTH), F32, (S, DN_DIM), lambda j, b: (b, j))]
    conv_grid = (3 * DN_HEADS, B)
    (qkvc,) = _tile_fwd(n + "dnconv", _dn_conv_f, conv_grid, conv_ins, conv_outs)

    tg = _pick(T, 2048, 128)
    gate_ins = [_In(baT, (8, tg), lambda i: (0, i)), _In(baT, (8, tg), lambda i: (1, i)),
                _In(w["dn_a_log"], (8, 1), lambda i: (0, 0), acc=True),
                _In(w["dn_dt_bias"], (8, 1), lambda i: (0, 0), acc=True)]
    gate_ins[0].g = ((8, T), lambda i: (0, i))
    gate_ins[1].g = ((8, T), lambda i: (0, i))
    gate_outs = [_Out((8, T), F32, (8, tg), lambda i: (0, i))] * 2
    beta, gdec = _tile_fwd(n + "dngate", _dn_gate_f, (T // tg,), gate_ins, gate_outs)

    NSC = T // DN_SUPER
    beta4 = beta.reshape(2, DN_HEADS, NSC, 1, DN_SUPER)
    gdec4 = gdec.reshape(2, DN_HEADS, NSC, 1, DN_SUPER)
    R = 2 * DN_HEADS

    def qkv_in(off):
        return _In(qkvc, (DN_SUPER, DN_DIM), lambda hh, m: (m, off + hh), acc=True,
                   g=((T, DN_WIDTH), lambda hh, m: (m, hh)))

    def row_in(a):
        return _In(a, (2, None, None, 1, DN_SUPER), lambda hh, m: (0, hh, m, 0, 0), split=True)

    def chain_out(wd):
        return _Out((2, DN_HEADS, T, wd), F32, (2, None, DN_SUPER, wd), lambda hh, m: (0, hh, m, 0), split=True)

    prep_ins = [qkv_in(0), qkv_in(DN_HEADS), qkv_in(2 * DN_HEADS), row_in(beta4), row_in(gdec4)]
    prep_outs = [chain_out(2 * DN_DIM), chain_out(DN_SUPER), chain_out(DN_DIM), chain_out(DN_DIM), chain_out(DN_DIM)]
    prep_grid = (DN_HEADS, NSC)
    prep_res = _tile_fwd(n + "dnprep", _dn_prep_f, prep_grid, prep_ins, prep_outs, sub=2, comm=host_gather)
    gathered_next = None
    if host_gather is not None:
        prep_res, gathered_next = prep_res
    uw, qk, qd, kd, gl = (t.reshape((R,) + t.shape[2:]) for t in prep_res)
    o_dn, states = _dn_scan_fwd(n + "dnscan", uw, qk, qd, kd, gl, B)

    zb = c.z // DN_DIM
    out_ins = [_In(o_dn, (None, S, DN_DIM), lambda b, hh: (hh, b, 0)),
               _In(o_dn, (None, S, DN_DIM), lambda b, hh: (DN_HEADS + hh, b, 0)),
               _In(P, (S, DN_DIM), lambda b, hh: (b, zb + hh), g=((T, DN_WIDTH), lambda b, hh: (b, hh)), gdtype=BF16),
               _In(w["dn_out_norm"], (1, DN_DIM), lambda b, hh: (0, 0), acc=True)]
    out_ins[0].g = ((DN_HEADS, T, DN_DIM), lambda b, hh: (hh, b, 0))
    out_ins[1].g = ((DN_HEADS, T, DN_DIM), lambda b, hh: (hh, b, 0))
    out_outs = [_Out((T, DN_WIDTH), BF16, (S, DN_DIM), lambda b, hh: (b, hh))]
    (ya_in,) = _tile_fwd(n + "dnout", _dn_out_f, (B, DN_HEADS), out_ins, out_outs)

    pb = c.pool // POOL_DIM
    pool_ins = [_In(P, (S, POOL_DIM), lambda gi, b: (b, pb + gi), g=((T, POOL_WIDTH), lambda gi, b: (b, gi)), gdtype=BF16),
                _In(w["pool_w"], (None, POOL_DIM, POOL_DIM), lambda gi, b: (gi, 0, 0), acc=True),
                _In(w["pool_scale"], (None, 1, POOL_DIM), lambda gi, b: (gi, 0, 0), acc=True)]
    pool_outs = [_Out((T, POOL_WIDTH), BF16, (S, POOL_DIM), lambda gi, b: (b, gi))]
    (yb_in,) = _tile_fwd(n + "pool", _pool_f, (POOL_GROUPS, B), pool_ins, pool_outs)

    cos, sin = _rope_tables(S)
    db = c.da // DA_OUT

    def da_in(k):
        return _In(P, (tm, DA_OUT), lambda i: (i, db + k), g=((T, DA_OUT), lambda i: (i, 0)), gdtype=BF16)

    rope_ins = [da_in(k) for k in range(3 * DA_GROUPS)]
    rope_ins += [_In(cos, (tm, DA_OUT), lambda i: (i % nmS, 0), kind="c"),
                 _In(sin, (tm, DA_OUT), lambda i: (i % nmS, 0), kind="c")]
    rope_outs = [_Out((T, DA_OUT), BF16, (tm, DA_OUT), lambda i: (i, 0))] * (3 * DA_GROUPS)
    roped = _tile_fwd(n + "rope", _rope_f, (T // tm,), rope_ins, rope_outs)
    strided = []
    o_g, l_g = [], []
    for gi, dil in enumerate(DA_DILATIONS):
        qs, ks, vs = roped[gi], roped[DA_GROUPS + gi], roped[2 * DA_GROUPS + gi]
        strided.append((qs, ks, vs))
        o, lse = _attn_fwd(n + f"attn{gi}", qs, ks, vs, B, dil)
        o_g.append(o)
        l_g.append(lse)
    mrg_ins = [_In(a, (tm, DA_OUT), lambda i: (i, 0)) for a in o_g + l_g]
    mrg_outs = [_Out((T, DA_OUT), BF16, (tm, DA_OUT), lambda i: (i, 0))]
    (yc_in,) = _tile_fwd(n + "merge", _merge_f, (T // tm,), mrg_ins, mrg_outs)

    ya = _mm(n + "pa", ya_in, w["w_proj_a"], "nn", out_dtype=BF16)
    yb = _mm(n + "pb", yb_in, w["w_proj_b"], "nn", out_dtype=BF16)
    yc = _mm(n + "pc", yc_in, w["w_proj_c"], "nn", out_dtype=BF16)

    def gcol(k):
        return _In(P, (tm, D), lambda i: (i, k), g=((T, D), lambda i: (i, 0)), gdtype=BF16)

    def yin(a):
        return _In(a, (tm, D), lambda i: (i, 0), gdtype=BF16)

    def bin_(k):
        return _In(w["b_gate"][k:k + 1], (1, D), lambda i: (0, 0), acc=True)

    gm_ins = [gcol(0), gcol(1), gcol(2), yin(ya), yin(yb), yin(yc), bin_(0), bin_(1), bin_(2)]
    gm_outs = [_Out((T, D), BF16, (tm, D), lambda i: (i, 0))]
    (merged,) = _tile_fwd(n + "gates", _gate_f, (T // tm,), gm_ins, gm_outs)
    x2 = _mm(n + "out", merged, w["w_out"], "nn", add=x1)

    def backward(dx2, host_exchange=None):
        return _mixer_bwd(dx2, host_exchange, **{k: v for k, v in locals_.items() if k in _MIXER_BWD_NEEDS})

    locals_ = dict(locals())
    return x2, backward, gathered_next


_MIXER_BWD_NEEDS = ("n", "B", "T", "D", "tm", "w", "h", "merged", "gm_ins", "gm_outs", "ya_in", "yb_in", "yc_in",
                    "mrg_ins", "mrg_outs", "strided", "rope_ins", "rope_outs", "pool_ins", "pool_outs", "out_ins",
                    "out_outs", "uw", "qk", "qd", "kd", "gl", "states", "prep_grid", "prep_ins", "prep_outs", "tg",
                    "gate_ins", "gate_outs", "conv_grid", "conv_ins", "conv_outs", "norm_ins", "norm_outs")


def _mixer_bwd(dx2, host_exchange, *, n, B, T, D, tm, w, h, merged, gm_ins, gm_outs, ya_in, yb_in, yc_in, mrg_ins, mrg_outs, strided,
               rope_ins, rope_outs, pool_ins, pool_outs, out_ins, out_outs, uw, qk, qd, kd, gl, states, prep_grid,
               prep_ins, prep_outs, tg, gate_ins, gate_outs, conv_grid, conv_ins, conv_outs, norm_ins, norm_outs):
    g = {}
    dmerged = _mm(n + "d_merged", dx2, w["w_out"], "nt", out_dtype=BF16)
    g["w_out"] = _mm(n + "d_wout", merged, dx2, "tn", tm=1024, tn=1024, tk=1024)
    dg0, dg1, dg2, dya, dyb, dyc, db0, db1, db2 = _tile_bwd(
        n + "gates_b", _gate_f, (T // tm,), gm_ins, gm_outs, [dmerged], acc_from=0)
    g["b_gate"] = jnp.concatenate([db0, db1, db2], axis=0)
    dya_in = _mm(n + "d_pa", dya, w["w_proj_a"], "nt")
    dyb_in = _mm(n + "d_pb", dyb, w["w_proj_b"], "nt")
    dyc_in = _mm(n + "d_pc", dyc, w["w_proj_c"], "nt")
    g["w_proj_a"] = _mm(n + "d_wpa", ya_in, dya, "tn", tn=1024, tk=2048)
    g["w_proj_b"] = _mm(n + "d_wpb", yb_in, dyb, "tn", tn=1024, tk=2048)
    g["w_proj_c"] = _mm(n + "d_wpc", yc_in, dyc, "tn", tn=1024, tk=2048)

    dmrg = _tile_bwd(n + "merge_b", _merge_f, (T // tm,), mrg_ins, mrg_outs, [dyc_in])
    dq_parts, dk_parts, dv_parts = [], [], []
    for gi, dil in enumerate(DA_DILATIONS):
        qs, ks, vs = strided[gi]
        dq, dk, dv = _attn_bwd(n + f"attn{gi}_b", qs, ks, vs, dmrg[gi], dmrg[DA_GROUPS + gi], B, dil)
        dq_parts.append(dq)
        dk_parts.append(dk)
        dv_parts.append(dv)
    dP_da = _tile_bwd(n + "rope_b", _rope_f, (T // tm,), rope_ins, rope_outs, dq_parts + dk_parts + dv_parts)

    dPpool, g["pool_w"], g["pool_scale"] = _tile_bwd(
        n + "pool_b", _pool_f, (POOL_GROUPS, B), pool_ins, pool_outs, [dyb_in], acc_from=1)

    dof, dob, dPz, g["dn_out_norm"] = _tile_bwd(
        n + "dnout_b", _dn_out_f, (B, DN_HEADS), out_ins, out_outs, [dya_in], acc_from=0)
    del dob
    duw, dqk, dqd, dkd, dgl = _dn_scan_bwd(n + "dnscan_b", uw, qk, qd, kd, gl, states, dof, B)
    prep_cts = [t.reshape((2, DN_HEADS) + t.shape[1:]) for t in (duw, dqk, dqd, dkd, dgl)]
    prep_res = _tile_bwd(n + "dnprep_b", _dn_prep_f, prep_grid, prep_ins, prep_outs, prep_cts, sub=2,
                         comm=host_exchange)
    exchanged = None
    if host_exchange is not None:
        prep_res, exchanged = prep_res
    dq_, dk_, dv_, dbeta4, dgdec4 = prep_res
    dqkvc = jnp.concatenate([dq_, dk_, dv_], axis=1)
    dbraw, daraw, g["dn_a_log"], g["dn_dt_bias"] = _tile_bwd(
        n + "dngate_b", _dn_gate_f, (T // tg,), gate_ins, gate_outs,
        [dbeta4.reshape(8, T), dgdec4.reshape(8, T)], acc_from=0)
    dPqkv, g["dn_conv"] = _tile_bwd(n + "dnconv_b", _dn_conv_f, conv_grid, conv_ins, conv_outs, [dqkvc], acc_from=1)
    dba = jnp.concatenate([dbraw, daraw], axis=0).T.astype(BF16)
    dba = jnp.pad(dba, ((0, 0), (0, BA_PAD - 16)))
    dP = jnp.concatenate([dg0, dg1, dg2, *dP_da, dPqkv, dPz, dPpool, dba], axis=1)
    dh = _mm(n + "d_h", dP, w["w_cat"], "nt", tm=1024, tn=1024, tk=2688)
    g["w_cat"] = _mm(n + "d_wcat", h, dP, "tn", tm=1024, tn=896, tk=1024)
    dx1, g["mix_norm"] = _tile_bwd(n + "norm_b", _norm_f, (T // tm,), norm_ins, norm_outs, [dh], acc_from=0,
                                   addends={0: dx2})
    return dx1, g, exchanged


def _layer_weights(full, l, D):
    c = _Cols(D)
    w_in = full["w_in"][l]
    o_z, o_ba, o_pool, o_da = 3 * DN_WIDTH, 4 * DN_WIDTH, 4 * DN_WIDTH + 16, 4 * DN_WIDTH + 16 + POOL_WIDTH
    w_cat = jnp.concatenate(
        [full["w_gate"][l], w_in[:, o_da:], w_in[:, :o_z], w_in[:, o_z:o_ba], w_in[:, o_pool:o_da], w_in[:, o_ba:o_pool],
         jnp.zeros((D, BA_PAD - 16), w_in.dtype)], axis=1).astype(BF16)
    assert w_cat.shape[1] == c.total
    w = {k: full[k][l].astype(BF16) for k in ("ffn1_w_gate", "ffn1_w_up", "ffn1_w_down", "ffn2_w_gate", "ffn2_w_up",
                                              "ffn2_w_down", "w_proj_a", "w_proj_b", "w_proj_c", "w_out")}
    w["w_cat"] = w_cat
    f_true = w["ffn1_w_gate"].shape[1]
    pad = -f_true % FFN_BWD_TF
    for k in ("ffn1", "ffn2"):
        w[k + "_bwd"] = (jnp.pad(w[k + "_w_gate"], ((0, 0), (0, pad))), jnp.pad(w[k + "_w_up"], ((0, 0), (0, pad))),
                         jnp.pad(w[k + "_w_down"], ((0, pad), (0, 0))), f_true)
    w["ffn1_norm"] = full["ffn1_norm"][l][None].astype(F32)
    w["ffn2_norm"] = full["ffn2_norm"][l][None].astype(F32)
    w["mix_norm"] = full["mix_norm"][l][None].astype(F32)
    w["dn_conv"] = full["dn_conv"][l].astype(F32)
    w["dn_a_log"] = full["dn_a_log"][l].reshape(2 * DN_HEADS, 1).astype(F32)
    w["dn_dt_bias"] = full["dn_dt_bias"][l].reshape(2 * DN_HEADS, 1).astype(F32)
    w["dn_out_norm"] = full["dn_out_norm"][l][None].astype(F32)
    w["pool_w"] = full["pool_w"][l].astype(F32)
    w["pool_scale"] = full["pool_scale"][l].reshape(POOL_GROUPS, 1, POOL_DIM).astype(F32)
    w["b_gate"] = full["b_gate"][l].reshape(3, D).astype(F32)
    return w


def _layer_grads(g, D):
    c = _Cols(D)
    gc = g.pop("w_cat")
    out = dict(g)
    out["w_gate"] = gc[:, :c.da]
    out["w_in"] = jnp.concatenate([gc[:, c.qkv:c.pool], gc[:, c.ba:c.ba + 16], gc[:, c.pool:c.ba], gc[:, c.da:c.qkv]],
                                  axis=1)
    for k in ("ffn1_norm", "ffn2_norm", "mix_norm", "dn_out_norm"):
        out[k] = g[k][0]
    out["dn_a_log"] = g["dn_a_log"].reshape(2, DN_HEADS)
    out["dn_dt_bias"] = g["dn_dt_bias"].reshape(2, DN_HEADS)
    out["pool_scale"] = g["pool_scale"].reshape(POOL_WIDTH)
    out["b_gate"] = g["b_gate"].reshape(3 * D)
    return out


def _unshard(got):
    full = {}
    for k, t in zip(SHARDED, got):
        ax = SHARD_AXIS[k] - 1
        shp = t.shape[1:]
        full[k] = jnp.moveaxis(t, 0, ax).reshape(shp[:ax] + (N_DEV * shp[ax],) + shp[ax + 1:])
    return full


def _to_owner_blocks(grads):
    out = []
    for k in SHARDED:
        ax = SHARD_AXIS[k] - 1
        shp = grads[k].shape
        t = grads[k].reshape(shp[:ax] + (N_DEV, shp[ax] // N_DEV) + shp[ax + 1:])
        out.append(jnp.moveaxis(t, ax, 0).astype(BF16))
    return out


def _local_step(x, target, rep, shards, distributed):
    B, S, D = x.shape
    T = B * S
    depth = len(shards)
    xs = x.reshape(T, D)
    tape = []
    if distributed:
        sharded_now = _unshard(_all_gather("gather_l0", [shards[0][k] for k in SHARDED]))
    else:
        sharded_now = shards[0]
    for l in range(depth):
        full = {k: [v] * (l + 1) for k, v in sharded_now.items()}
        full.update({k: v for k, v in rep.items() if k != "final_norm"})
        w = _layer_weights(full, l, D)
        host = ("gather", [shards[l + 1][k] for k in SHARDED]) if distributed and l + 1 < depth else None
        x1 = _ffn_fwd(f"l{l}_ffn1", xs, w["ffn1_norm"], w["ffn1_w_gate"], w["ffn1_w_up"], w["ffn1_w_down"])
        x2, mixer_bwd, got = _mixer(l, x1, w, B, host)
        x3 = _ffn_fwd(f"l{l}_ffn2", x2, w["ffn2_norm"], w["ffn2_w_gate"], w["ffn2_w_up"], w["ffn2_w_down"])
        tape.append((w, xs, mixer_bwd, x2))
        xs = x3
        if l + 1 < depth:
            sharded_now = _unshard(got) if distributed else shards[l + 1]
    loss8, dx, dfinal = _loss_fwd_bwd("loss", xs, rep["final_norm"][None].astype(F32), target.reshape(T, D))
    per_layer = [None] * depth
    exchanged = [None] * depth
    pending = None
    for l in reversed(range(depth)):
        w, x0, mixer_bwd, x2 = tape[l]
        wg_p, wu_p, wd_p, f_true = w["ffn2_bwd"]
        dx, dn2, dwg2, dwu2, dwd2 = _ffn_bwd(f"l{l}_ffn2b", x2, w["ffn2_norm"], wg_p, wu_p, wd_p, dx, f_true)
        dx, g, got = mixer_bwd(dx, ("exchange", pending) if pending is not None else None)
        if pending is not None:
            exchanged[l + 1] = got
        wg_p, wu_p, wd_p, f_true = w["ffn1_bwd"]
        dx, dn1, dwg1, dwu1, dwd1 = _ffn_bwd(f"l{l}_ffn1b", x0, w["ffn1_norm"], wg_p, wu_p, wd_p, dx, f_true)
        g.update(ffn1_norm=dn1, ffn1_w_gate=dwg1, ffn1_w_up=dwu1, ffn1_w_down=dwd1,
                 ffn2_norm=dn2, ffn2_w_gate=dwg2, ffn2_w_up=dwu2, ffn2_w_down=dwd2)
        per_layer[l] = _layer_grads(g, D)
        if distributed:
            pending = _to_owner_blocks(per_layer[l])
    return loss8[0, 0], dx.reshape(B, S, D), per_layer, dfinal[0], exchanged, pending


def _mesh_position():
    mx, my, mc = lax.axis_index("x"), lax.axis_index("y"), lax.axis_index("c")
    return mx, my, mc, 4 * mx + 2 * my + mc


def _peers(mx, my, mc):
    out = []
    for k in range(1, N_DEV):
        px, py, pc = mx ^ ((k >> 2) & 1), my ^ ((k >> 1) & 1), mc ^ (k & 1)
        out.append(((px, py, pc), 4 * px + 2 * py + pc))
    return out


_ANY = pl.BlockSpec(memory_space=pl.ANY)


def _all_gather(name, xs):
    n = len(xs)

    def body(*refs):
        _gather_start(refs[:n], refs[n:2 * n], *refs[2 * n:])
        _gather_finish(refs[:n], refs[n:2 * n], *refs[2 * n:])

    sems = pltpu.SemaphoreType.DMA((n, N_DEV - 1))
    return pl.pallas_call(
        body, name=name, in_specs=[_ANY] * n, out_specs=[_ANY] * n,
        out_shape=[jax.ShapeDtypeStruct((N_DEV,) + x.shape, x.dtype) for x in xs],
        scratch_shapes=[sems, sems, pltpu.SemaphoreType.DMA((n,))],
    )(*xs)


class _GatherPlan:
    def __init__(self, x_refs, o_refs, send_sems, recv_sems, local_sems):
        self.x, self.o, self.ss, self.rs, self.ls = x_refs, o_refs, send_sems, recv_sems, local_sems
        self.mx, self.my, self.mc, self.me = _mesh_position()
        self.self_id = (self.mx, self.my, self.mc)
        self.sibling = (self.mx, self.my, 1 - self.mc)
        self.chips = [(1 - self.mx, self.my), (self.mx, 1 - self.my), (1 - self.mx, 1 - self.my)]

    def copy(self, a, k, blk, to, from_input=False):
        dst = self.o[a].at[blk]
        return pltpu.make_async_remote_copy(src_ref=self.x[a] if from_input else dst, dst_ref=dst,
                                            send_sem=self.ss.at[a, k], recv_sem=self.rs.at[a, k],
                                            device_id=to, device_id_type=pl.DeviceIdType.MESH)

    def own(self, a):
        return pltpu.make_async_copy(self.x[a], self.o[a].at[self.me], self.ls.at[a])

    def first_sends(self, a):
        cps = [self.copy(a, 0, self.me, self.sibling, from_input=True)]
        return cps + [self.copy(a, 1 + j, self.me, (*chip, self.mc), from_input=True) for j, chip in enumerate(self.chips)]

    def passed_on(self, a, j):
        cx, cy = self.chips[j]
        return self.copy(a, 4 + j, 4 * cx + 2 * cy + self.mc, self.sibling)


def _gather_start(x_refs, o_refs, send_sems, recv_sems, local_sems):
    p = _GatherPlan(x_refs, o_refs, send_sems, recv_sems, local_sems)
    for a in range(len(x_refs)):
        p.own(a).start()
        for cp in p.first_sends(a):
            cp.start()


def _gather_finish(x_refs, o_refs, send_sems, recv_sems, local_sems):
    p = _GatherPlan(x_refs, o_refs, send_sems, recv_sems, local_sems)
    n = len(x_refs)
    for a in range(n):
        for j, (cx, cy) in enumerate(p.chips):
            p.copy(a, 1 + j, 4 * cx + 2 * cy + p.mc, p.self_id).wait_recv()
            p.passed_on(a, j).start()
    for a in range(n):
        p.copy(a, 0, 4 * p.mx + 2 * p.my + 1 - p.mc, p.self_id).wait_recv()
        for j, (cx, cy) in enumerate(p.chips):
            p.copy(a, 4 + j, 4 * cx + 2 * cy + 1 - p.mc, p.self_id).wait_recv()
    for a in range(n):
        for cp in p.first_sends(a):
            cp.wait_send()
        for j in range(len(p.chips)):
            p.passed_on(a, j).wait_send()
        p.own(a).wait()


def _exchange_grads(name, gs, gr):
    ns, n = len(gs), len(gs) + len(gr)

    def body(*refs):
        _exchange_start(refs[:n], refs[n:2 * n], *refs[2 * n:], n_sharded=ns)
        _exchange_finish(refs[:n], refs[n:2 * n], *refs[2 * n:], n_sharded=ns)

    sems = pltpu.SemaphoreType.DMA((n, N_DEV - 1))
    outs = pl.pallas_call(
        body, name=name, in_specs=[_ANY] * n, out_specs=[_ANY] * n,
        out_shape=[jax.ShapeDtypeStruct(a.shape, a.dtype) for a in gs]
        + [jax.ShapeDtypeStruct((N_DEV,) + a.shape, a.dtype) for a in gr],
        scratch_shapes=[sems, sems, pltpu.SemaphoreType.DMA((n,))],
    )(*gs, *gr)
    return outs[:ns], outs[ns:]


def _exchange_copies(in_refs, out_refs, send_sems, recv_sems, local_sems, n_sharded):
    mx, my, mc, me = _mesh_position()
    n = len(in_refs)
    own = [pltpu.make_async_copy(in_refs[a].at[me] if a < n_sharded else in_refs[a], out_refs[a].at[me],
                                 local_sems.at[a]) for a in range(n)]
    remote = []
    for k, (peer, pid) in enumerate(_peers(mx, my, mc)):
        for a in range(n):
            src = in_refs[a].at[pid] if a < n_sharded else in_refs[a]
            remote.append(pltpu.make_async_remote_copy(
                src_ref=src, dst_ref=out_refs[a].at[me], send_sem=send_sems.at[a, k], recv_sem=recv_sems.at[a, k],
                device_id=peer, device_id_type=pl.DeviceIdType.MESH))
    return own, remote


def _exchange_start(in_refs, out_refs, send_sems, recv_sems, local_sems, n_sharded=None):
    ns = len(in_refs) if n_sharded is None else n_sharded
    own, remote = _exchange_copies(in_refs, out_refs, send_sems, recv_sems, local_sems, ns)
    for cp in own + remote:
        cp.start()


def _exchange_finish(in_refs, out_refs, send_sems, recv_sems, local_sems, n_sharded=None):
    ns = len(in_refs) if n_sharded is None else n_sharded
    own, remote = _exchange_copies(in_refs, out_refs, send_sems, recv_sems, local_sems, ns)
    for cp in remote:
        cp.wait_send()
        cp.wait_recv()
    for cp in own:
        cp.wait()


def _reduce_adamw(name, parts, w, m, v):
    shape = w.shape
    cols = shape[-1]
    w2, m2, v2 = (t.reshape(-1, cols) for t in (w, m, v))
    p3 = parts.reshape(N_DEV, -1, cols)
    rows = w2.shape[0]
    tr = _pick(rows, 512, 16) if rows > 1024 else rows
    c1 = 1.0 - ADAM_B1 ** ADAM_STEP
    c2 = 1.0 - ADAM_B2 ** ADAM_STEP

    def body(p_ref, w_ref, m_ref, v_ref, g_ref, d_ref, nm_ref, nv_ref):
        gv = p_ref[0].astype(F32)
        for d in range(1, N_DEV):
            gv = gv + p_ref[d].astype(F32)
        nm = ADAM_B1 * m_ref[...] + (1.0 - ADAM_B1) * gv
        nv = ADAM_B2 * v_ref[...] + (1.0 - ADAM_B2) * (gv * gv)
        g_ref[...] = gv
        d_ref[...] = -ADAM_LR * ((nm / c1) / (jnp.sqrt(nv / c2) + ADAM_EPS) + ADAM_WD * w_ref[...])
        nm_ref[...] = nm
        nv_ref[...] = nv

    spec = pl.BlockSpec((tr, cols), lambda i: (i, 0))
    outs = pl.pallas_call(
        body, name=name, grid=(rows // tr,),
        in_specs=[pl.BlockSpec((N_DEV, tr, cols), lambda i: (0, i, 0))] + [spec] * 3, out_specs=[spec] * 4,
        out_shape=[jax.ShapeDtypeStruct((rows, cols), F32)] * 4, compiler_params=_params(),
    )(p3, w2, m2, v2)
    return tuple(o.reshape(shape) for o in outs)


def kernel(x, ffn1_norm, ffn1_w_gate, ffn1_w_up, ffn1_w_down, mix_norm, w_in, dn_conv, dn_a_log, dn_dt_bias, dn_out_norm, pool_w, pool_scale, w_proj_a, w_proj_b, w_proj_c, w_gate, b_gate, w_out, ffn2_norm, ffn2_w_gate, ffn2_w_up, ffn2_w_down, final_norm, loss_target, m_ffn1_norm, m_ffn1_w_gate, m_ffn1_w_up, m_ffn1_w_down, m_mix_norm, m_w_in, m_dn_conv, m_dn_a_log, m_dn_dt_bias, m_dn_out_norm, m_pool_w, m_pool_scale, m_w_proj_a, m_w_proj_b, m_w_proj_c, m_w_gate, m_b_gate, m_w_out, m_ffn2_norm, m_ffn2_w_gate, m_ffn2_w_up, m_ffn2_w_down, m_final_norm, v_ffn1_norm, v_ffn1_w_gate, v_ffn1_w_up, v_ffn1_w_down, v_mix_norm, v_w_in, v_dn_conv, v_dn_a_log, v_dn_dt_bias, v_dn_out_norm, v_pool_w, v_pool_scale, v_w_proj_a, v_w_proj_b, v_w_proj_c, v_w_gate, v_b_gate, v_w_out, v_ffn2_norm, v_ffn2_w_gate, v_ffn2_w_up, v_ffn2_w_down, v_final_norm):
    args = locals()
    wts = {k: args[k] for k in WEIGHTS}
    ms = {k: args["m_" + k] for k in WEIGHTS}
    vs = {k: args["v_" + k] for k in WEIGHTS}

    depth = w_in.shape[0]
    rep = {k: wts[k] for k in REPLICATED}
    shards = [{k: wts[k][l].astype(BF16) for k in SHARDED} for l in range(depth)]
    loss_local, dx, per_layer, dfinal, exchanged, pending = _local_step(x, loss_target, rep, shards, True)
    loss = lax.psum(loss_local, ("x", "y", "c"))

    gr = [dfinal if k == "final_norm" else jnp.stack([pg[k] for pg in per_layer]).astype(F32).reshape(wts[k].shape)
          for k in REPLICATED]
    exchanged[0], got_r = _exchange_grads("exchange_grads", pending, gr)
    parts = {k: jnp.stack([exchanged[l][j] for l in range(depth)], axis=1) for j, k in enumerate(SHARDED)}
    parts.update(zip(REPLICATED, got_r))

    g_final, deltas, new_m, new_v = {}, {}, {}, {}
    for k in WEIGHTS:
        g_final[k], deltas[k], new_m[k], new_v[k] = _reduce_adamw("adamw_" + k, parts[k], wts[k], ms[k], vs[k])
    return (loss, dx, *[g_final[k] for k in WEIGHTS], *[deltas[k] for k in WEIGHTS], *[new_m[k] for k in WEIGHTS],
            *[new_v[k] for k in WEIGHTS])
```

```python
import functools
import math

import jax
import jax.numpy as jnp
from jax import lax
from jax.experimental import pallas as pl
from jax.experimental.pallas import tpu as pltpu

F32 = jnp.float32
BF16 = jnp.bfloat16

N_DEV = 8
RMS_EPS = 1e-6
L2_EPS = 1e-6
DN_HEADS = 4
DN_DIM = 128
DN_WIDTH = DN_HEADS * DN_DIM
DN_CONV = 5
DN_CHUNK = 64
DN_SUPER = 256
POOL_GROUPS = 4
POOL_DIM = 128
POOL_WIDTH = POOL_GROUPS * POOL_DIM
POOL_MAX_HALF = 8
DA_GROUPS = 3
DA_HEADS = 4
DA_DIM = 64
DA_WIDTH = DA_GROUPS * DA_HEADS * DA_DIM
DA_OUT = DA_HEADS * DA_DIM
DA_DILATIONS = (1, 4, 16)
DA_RADIUS = 64
DA_TQ = 256
FFN_BWD_TF = 512
ROPE_THETA = 10000.0
MASK_VALUE = -1e30
BA_PAD = 128

ADAM_LR = 0.001
ADAM_B1 = 0.9
ADAM_B2 = 0.999
ADAM_EPS = 1e-08
ADAM_WD = 0.01
ADAM_STEP = 10

VMEM_LIMIT_V7X = 56 * 1024 * 1024
LANES = 1024

SHARDED = ("ffn1_w_gate", "ffn1_w_up", "ffn1_w_down", "w_in", "dn_conv", "w_proj_a", "w_proj_b", "w_proj_c",
           "w_gate", "w_out", "ffn2_w_gate", "ffn2_w_up", "ffn2_w_down")
SHARD_AXIS = {"ffn1_w_gate": 2, "ffn1_w_up": 2, "ffn1_w_down": 1, "w_in": 2, "dn_conv": 2, "w_proj_a": 2,
              "w_proj_b": 2, "w_proj_c": 2, "w_gate": 2, "w_out": 1, "ffn2_w_gate": 2, "ffn2_w_up": 2,
              "ffn2_w_down": 1}
REPLICATED = ("ffn1_norm", "mix_norm", "dn_a_log", "dn_dt_bias", "dn_out_norm", "pool_w", "pool_scale", "b_gate",
              "ffn2_norm", "final_norm")
WEIGHTS = ("ffn1_norm", "ffn1_w_gate", "ffn1_w_up", "ffn1_w_down", "mix_norm", "w_in", "dn_conv", "dn_a_log",
           "dn_dt_bias", "dn_out_norm", "pool_w", "pool_scale", "w_proj_a", "w_proj_b", "w_proj_c", "w_gate",
           "b_gate", "w_out", "ffn2_norm", "ffn2_w_gate", "ffn2_w_up", "ffn2_w_down", "final_norm")


def _params(**kw):
    return pltpu.CompilerParams(vmem_limit_bytes=VMEM_LIMIT_V7X, **kw)


def _pick(n, target, align):
    best = None
    t = align
    while t <= min(n, target):
        if n % t == 0:
            best = t
        t += align
    return best if best is not None else n


_DIMS = {"nn": (((1,), (0,)), ((), ())), "nt": (((1,), (1,)), ((), ())), "tn": (((0,), (0,)), ((), ()))}


def _dg(a, b, mode):
    return lax.dot_general(a, b, _DIMS[mode], preferred_element_type=F32)


def _split2(a):
    hi = a.astype(BF16)
    lo = (a - hi.astype(F32)).astype(BF16)
    return hi, lo


def _dotp(a, b, mode, passes):
    if passes == 1:
        return _dg(a.astype(BF16), b.astype(BF16), mode)
    ah, al = _split2(a.astype(F32))
    bh, bl = _split2(b.astype(F32))
    return _dg(ah, bh, mode) + (_dg(ah, bl, mode) + _dg(al, bh, mode))


@functools.partial(jax.custom_vjp, nondiff_argnums=(2, 3))
def _dot(a, b, mode, passes):
    return _dotp(a, b, mode, passes)


def _dot_fwd(a, b, mode, passes):
    return _dotp(a, b, mode, passes), (a, b)


def _dot_bwd(mode, passes, res, ct):
    a, b = res
    if mode == "nn":
        da, db = _dotp(ct, b, "nt", passes), _dotp(a, ct, "tn", passes)
    elif mode == "nt":
        da, db = _dotp(ct, b, "nn", passes), _dotp(ct, a, "tn", passes)
    else:
        da, db = _dotp(b, ct, "nt", passes), _dotp(a, ct, "nn", passes)
    return da.astype(a.dtype), db.astype(b.dtype)


_dot.defvjp(_dot_fwd, _dot_bwd)


def _split3(x):
    x1 = x.astype(BF16)
    r = x - x1.astype(F32)
    x2 = r.astype(BF16)
    x3 = (r - x2.astype(F32)).astype(BF16)
    return x1, x2, x3


def _mdotp(mask, x, mode):
    x1, x2, x3 = _split3(x)
    return _dg(mask, x1, mode) + (_dg(mask, x2, mode) + _dg(mask, x3, mode))


@jax.custom_vjp
def _mdot(mask, x):
    return _mdotp(mask, x, "nn")


def _mdot_fwd(mask, x):
    return _mdotp(mask, x, "nn"), mask


def _mdot_bwd(mask, ct):
    return jnp.zeros_like(mask), _mdotp(mask, ct, "tn")


_mdot.defvjp(_mdot_fwd, _mdot_bwd)


_SOLVE_SQUARINGS = int(math.log2(DN_CHUNK)) - 1


def _unit_solve_fwd(A, R):
    Ab = A.astype(BF16)
    n = A.shape[0]
    eye = (lax.broadcasted_iota(jnp.int32, A.shape, 0) == lax.broadcasted_iota(jnp.int32, A.shape, 1)).astype(F32)
    Tm = eye - A
    P = _dg(Ab, Ab, "nn").astype(BF16)
    for _ in range(_SOLVE_SQUARINGS - 1):
        M = _dg(P, jnp.concatenate([Tm.astype(BF16), P], axis=1), "nn")
        Tm, P = Tm + M[:, :n], M[:, n:].astype(BF16)
    Tb = (Tm + _dg(P, Tm.astype(BF16), "nn")).astype(BF16)
    X = _dg(Tb, R.astype(BF16), "nn")
    return X, (Tb, X)


@jax.custom_vjp
def _unit_solve(A, R):
    return _unit_solve_fwd(A, R)[0]


def _unit_solve_bwd(res, dX):
    Tb, X = res
    Y = _dg(Tb, dX.astype(BF16), "tn")
    return -_dg(Y.astype(BF16), X.astype(BF16), "nt"), Y


_unit_solve.defvjp(_unit_solve_fwd, _unit_solve_bwd)


def _shift_impl(x, o):
    if o == 0:
        return x
    n = x.shape[0]
    y = pltpu.roll(x, (-o) % n, axis=0)
    t = lax.broadcasted_iota(jnp.int32, x.shape, 0) + o
    return jnp.where((t >= 0) & (t < n), y, 0.0)


@functools.partial(jax.custom_vjp, nondiff_argnums=(1,))
def _shift(x, o):
    return _shift_impl(x, o)


def _shift_fwd(x, o):
    return _shift_impl(x, o), None


def _shift_bwd(o, _, ct):
    return (_shift_impl(ct, -o),)


_shift.defvjp(_shift_fwd, _shift_bwd)


def _rot_impl(x):
    w = x.shape[1]
    half = DA_DIM // 2
    lane = lax.broadcasted_iota(jnp.int32, x.shape, 1)
    first = (lane & (DA_DIM - 1)) < half
    return jnp.where(first, -pltpu.roll(x, w - half, axis=1), pltpu.roll(x, half, axis=1))


@jax.custom_vjp
def _rot(x):
    return _rot_impl(x)


def _rot_fwd(x):
    return _rot_impl(x), None


def _rot_bwd(_, ct):
    return (-_rot_impl(ct),)


_rot.defvjp(_rot_fwd, _rot_bwd)


def _sigmoid(x):
    return 1.0 / (1.0 + jnp.exp(-x))


def _silu(x):
    return x * _sigmoid(x)


def _softplus(x):
    return jnp.maximum(x, 0.0) + jnp.log(1.0 + jnp.exp(-jnp.abs(x)))


def _rms(x, gain):
    return x * lax.rsqrt(jnp.mean(x * x, axis=-1, keepdims=True) + RMS_EPS) * gain


class _In:
    def __init__(self, arr, block, imap, kind="t", acc=False, g=None, gdtype=None, split=False):
        self.arr, self.block, self.imap, self.kind, self.acc, self.g, self.gdtype = arr, block, imap, kind, acc, g, gdtype
        self.split = split


class _Out:
    def __init__(self, shape, dtype, block, imap, split=False):
        self.shape, self.dtype, self.block, self.imap, self.split = shape, dtype, block, imap, split


def _sub_index(split, s):
    if not split:
        return Ellipsis
    return (s,) if split is True else split(s)


def _grid_edges(grid):
    first = last = None
    for a, n in enumerate(grid):
        f, l = pl.program_id(a) == 0, pl.program_id(a) == n - 1
        first = f if first is None else jnp.logical_and(first, f)
        last = l if last is None else jnp.logical_and(last, l)
    return first, last


def _comm_plumbing(comm):
    if comm is None:
        return [], [], [], lambda refs: None, lambda refs: None
    kind, arrs = comm
    n = len(arrs)
    if kind == "gather":
        shapes = [jax.ShapeDtypeStruct((N_DEV,) + a.shape, a.dtype) for a in arrs]
        start, finish = _gather_start, _gather_finish
    else:
        shapes = [jax.ShapeDtypeStruct(a.shape, a.dtype) for a in arrs]
        start, finish = _exchange_start, _exchange_finish
    sems = [pltpu.SemaphoreType.DMA((n, N_DEV - 1)), pltpu.SemaphoreType.DMA((n, N_DEV - 1)),
            pltpu.SemaphoreType.DMA((n,))]
    return list(arrs), shapes, sems, start, finish


def _first_step(acc_from, ngrid):
    c = None
    for a in range(acc_from, ngrid):
        t = pl.program_id(a) == 0
        c = t if c is None else jnp.logical_and(c, t)
    return c


def _tile_fwd(name, f, grid, ins, outs, sub=1, comm=None):
    n_in, n_out = len(ins), len(outs)
    ngrid = len(grid)
    c_arrs, c_shapes, c_sems, c_start, c_finish = _comm_plumbing(comm)
    nc = len(c_arrs)

    def body(*refs):
        in_refs, c_in = refs[:n_in], refs[n_in:n_in + nc]
        out_refs, c_out = refs[n_in + nc:n_in + nc + n_out], refs[n_in + nc + n_out:n_in + 2 * nc + n_out]
        sems = refs[n_in + 2 * nc + n_out:]
        pids = tuple(pl.program_id(a) for a in range(ngrid))
        if nc:
            first, last = _grid_edges(grid)
            pl.when(first)(lambda: c_start(c_in, c_out, *sems))
        for s in range(sub):
            vals = [r[_sub_index(i.split, s)] for r, i in zip(in_refs, ins)]
            vals = [v.astype(F32) if i.kind == "t" else v for v, i in zip(vals, ins)]
            res = f(pids + ((s,) if sub > 1 else ()), *vals)
            for r, o, v in zip(out_refs, outs, res):
                r[_sub_index(o.split, s)] = v.astype(r.dtype)
        if nc:
            pl.when(last)(lambda: c_finish(c_in, c_out, *sems))

    res = pl.pallas_call(
        body, name=name, grid=grid,
        in_specs=[pl.BlockSpec(i.block, i.imap) for i in ins] + [_ANY] * nc,
        out_specs=[pl.BlockSpec(o.block, o.imap) for o in outs] + [_ANY] * nc,
        out_shape=[jax.ShapeDtypeStruct(o.shape, o.dtype) for o in outs] + c_shapes,
        scratch_shapes=c_sems, compiler_params=_params(),
    )(*[i.arr for i in ins], *c_arrs)
    return (res[:n_out], res[n_out:]) if nc else res


def _tile_bwd(name, f, grid, ins, outs, cts, acc_from=None, addends=None, sub=1, comm=None):
    n_in, n_out = len(ins), len(outs)
    ngrid = len(grid)
    diff = [k for k, i in enumerate(ins) if i.kind == "t"]
    addends = addends or {}
    add_keys = sorted(addends)
    n_add, n_g = len(add_keys), len(diff)
    c_arrs, c_shapes, c_sems, c_start, c_finish = _comm_plumbing(comm)
    nc = len(c_arrs)

    def body(*refs):
        pids = tuple(pl.program_id(a) for a in range(ngrid))
        in_refs = refs[:n_in]
        ct_refs = refs[n_in:n_in + n_out]
        add_refs = refs[n_in + n_out:n_in + n_out + n_add]
        o = n_in + n_out + n_add
        c_in, g_refs, c_out, sems = refs[o:o + nc], refs[o + nc:o + nc + n_g], refs[o + nc + n_g:o + 2 * nc + n_g], \
            refs[o + 2 * nc + n_g:]
        if nc:
            first_step, last_step = _grid_edges(grid)
            pl.when(first_step)(lambda: c_start(c_in, c_out, *sems))
        sums = {}
        for s in range(sub):
            vals = [r[_sub_index(i.split, s)] for r, i in zip(in_refs, ins)]
            dvals = [vals[k].astype(F32) for k in diff]

            def g(*d, vals=vals, s=s):
                full = list(vals)
                for k, dk in zip(diff, d):
                    full[k] = dk
                return tuple(f(pids + ((s,) if sub > 1 else ()), *full))

            res, vjp = jax.vjp(g, *dvals)
            cvals = [c[_sub_index(o_.split, s)].astype(r.dtype) for c, o_, r in zip(ct_refs, outs, res)]
            grads = vjp(tuple(cvals))
            for k, gr in zip(diff, grads):
                idx = _sub_index(ins[k].split, s)
                key = (k, str(idx))
                sums[key] = (idx, gr if key not in sums else sums[key][1] + gr)
        first = _first_step(acc_from, ngrid) if acc_from is not None else None
        for (k, _), (idx, gr) in sums.items():
            gref = g_refs[diff.index(k)]
            if idx is not Ellipsis:
                gref[idx] = gr.astype(gref.dtype)
                continue
            if k in addends:
                gr = gr + add_refs[add_keys.index(k)][...].astype(F32)
            if ins[k].acc and first is not None:
                @pl.when(first)
                def _(gr=gr, gref=gref):
                    gref[...] = gr.astype(gref.dtype)

                @pl.when(jnp.logical_not(first))
                def _(gr=gr, gref=gref):
                    gref[...] += gr.astype(gref.dtype)
            else:
                gref[...] = gr.astype(gref.dtype)
        if nc:
            pl.when(last_step)(lambda: c_finish(c_in, c_out, *sems))

    g_shapes, g_specs = [], []
    for k in diff:
        i = ins[k]
        if i.g is not None:
            shape, imap = i.g
        else:
            shape, imap = i.arr.shape, i.imap
        dt = i.gdtype or (F32 if i.acc else i.arr.dtype)
        g_shapes.append(jax.ShapeDtypeStruct(shape, dt))
        g_specs.append(pl.BlockSpec(i.block, imap))
    add_specs = [pl.BlockSpec(ins[k].block, ins[k].g[1] if ins[k].g is not None else ins[k].imap) for k in add_keys]
    res = pl.pallas_call(
        body, name=name, grid=grid,
        in_specs=[pl.BlockSpec(i.block, i.imap) for i in ins] + [pl.BlockSpec(o.block, o.imap) for o in outs] + add_specs
        + [_ANY] * nc,
        out_specs=g_specs + [_ANY] * nc, out_shape=g_shapes + c_shapes,
        scratch_shapes=c_sems, compiler_params=_params(),
    )(*[i.arr for i in ins], *cts, *[addends[k] for k in add_keys], *c_arrs)
    return (res[:n_g], res[n_g:]) if nc else res


def _mm(name, a, b, mode, out_dtype=F32, add=None, tm=1024, tn=1024, tk=1024, m=None, n=None):
    if mode == "nn":
        (M, K), N = a.shape, b.shape[1]
    elif mode == "nt":
        (M, K), N = a.shape, b.shape[0]
    else:
        (K, M), N = a.shape, b.shape[1]
    M, N = m or M, n or N
    tm, tn, tk = _pick(M, tm, 128), _pick(N, tn, 128), _pick(K, tk, 128)
    nk = K // tk
    a_spec = pl.BlockSpec((tk, tm), lambda i, j, k: (k, i)) if mode == "tn" else pl.BlockSpec((tm, tk), lambda i, j, k: (i, k))
    b_spec = pl.BlockSpec((tn, tk), lambda i, j, k: (j, k)) if mode == "nt" else pl.BlockSpec((tk, tn), lambda i, j, k: (k, j))
    o_spec = pl.BlockSpec((tm, tn), lambda i, j, k: (i, j))

    def body(*refs):
        if add is None:
            a_ref, b_ref, o_ref, acc = refs
            add_ref = None
        else:
            a_ref, b_ref, add_ref, o_ref, acc = refs
        k = pl.program_id(2)

        @pl.when(k == 0)
        def _():
            acc[...] = jnp.zeros_like(acc)

        acc[...] += _dg(a_ref[...].astype(BF16), b_ref[...].astype(BF16), mode)

        @pl.when(k == nk - 1)
        def _():
            r = acc[...]
            if add_ref is not None:
                r = r + add_ref[...].astype(F32)
            o_ref[...] = r.astype(o_ref.dtype)

    ops = (a, b) if add is None else (a, b, add)
    specs = [a_spec, b_spec] + ([] if add is None else [o_spec])
    return pl.pallas_call(
        body, name=name, grid=(M // tm, N // tn, nk), in_specs=specs, out_specs=o_spec,
        out_shape=jax.ShapeDtypeStruct((M, N), out_dtype), scratch_shapes=[pltpu.VMEM((tm, tn), F32)],
        compiler_params=_params(dimension_semantics=("parallel", "parallel", "arbitrary")),
    )(*ops)


def _ffn_fwd(name, x, gain, wg, wu, wd):
    T, D = x.shape
    F = wg.shape[1]
    tm, tf = _pick(T, 1024, 8), _pick(F, 256, 128)
    nf = F // tf

    def body(x_ref, g_ref, wg_ref, wu_ref, wd_ref, o_ref, h_ref, acc):
        j = pl.program_id(1)

        @pl.when(j == 0)
        def _():
            h_ref[...] = _rms(x_ref[...], g_ref[...]).astype(BF16)
            acc[...] = jnp.zeros_like(acc)

        h = h_ref[...]
        a = _dg(h, wg_ref[...], "nn")
        b = _dg(h, wu_ref[...], "nn")
        s = (_silu(a) * b).astype(BF16)
        acc[...] += _dg(s, wd_ref[...], "nn")

        @pl.when(j == nf - 1)
        def _():
            o_ref[...] = x_ref[...] + 0.5 * acc[...]

    return pl.pallas_call(
        body, name=name, grid=(T // tm, nf),
        in_specs=[pl.BlockSpec((tm, D), lambda i, j: (i, 0)), pl.BlockSpec((1, D), lambda i, j: (0, 0)),
                  pl.BlockSpec((D, tf), lambda i, j: (0, j)), pl.BlockSpec((D, tf), lambda i, j: (0, j)),
                  pl.BlockSpec((tf, D), lambda i, j: (j, 0))],
        out_specs=pl.BlockSpec((tm, D), lambda i, j: (i, 0)),
        out_shape=jax.ShapeDtypeStruct((T, D), F32),
        scratch_shapes=[pltpu.VMEM((tm, D), BF16), pltpu.VMEM((tm, D), F32)],
        compiler_params=_params(dimension_semantics=("parallel", "arbitrary")),
    )(x, gain, wg, wu, wd)


def _ffn_bwd(name, x, gain, wg, wu, wd, dy, f_true):
    T, D = x.shape
    F = wg.shape[1]
    tm, tf = _pick(T, 512, 8), _pick(F, FFN_BWD_TF, 128)
    nf = F // tf

    def body(x_ref, g_ref, wg_ref, wu_ref, wd_ref, dy_ref, dx_ref, dg_ref, da_ref, db_ref, s_ref, h_ref, dyh_ref, dh):
        i, j = pl.program_id(0), pl.program_id(1)

        @pl.when(j == 0)
        def _():
            h_ref[...] = _rms(x_ref[...], g_ref[...]).astype(BF16)
            dyh_ref[...] = (0.5 * dy_ref[...]).astype(BF16)
            dh[...] = jnp.zeros_like(dh)

        h = h_ref[...]
        a = _dg(h, wg_ref[...], "nn")
        b = _dg(h, wu_ref[...], "nn")
        ds = _dg(dyh_ref[...], wd_ref[...], "nt")
        sig = _sigmoid(a)
        silu = a * sig
        da = (ds * b * (sig * (1.0 + a * (1.0 - sig)))).astype(BF16)
        db = (ds * silu).astype(BF16)
        da_ref[...] = da
        db_ref[...] = db
        s_ref[...] = (silu * b).astype(BF16)
        dh[...] += _dg(da, wg_ref[...], "nt") + _dg(db, wu_ref[...], "nt")

        @pl.when(j == nf - 1)
        def _():
            _, vjp = jax.vjp(_rms, x_ref[...], g_ref[...])
            dxn, dgn = vjp(dh[...])
            dx_ref[...] = dy_ref[...] + dxn

            @pl.when(i == 0)
            def _():
                dg_ref[...] = dgn

            @pl.when(i != 0)
            def _():
                dg_ref[...] += dgn

    row = lambda i, j: (i, 0)
    col = lambda i, j: (i, j)
    dx, dgain, da, db, s, h, dyh = pl.pallas_call(
        body, name=name, grid=(T // tm, nf),
        in_specs=[pl.BlockSpec((tm, D), row), pl.BlockSpec((1, D), lambda i, j: (0, 0)),
                  pl.BlockSpec((D, tf), lambda i, j: (0, j)), pl.BlockSpec((D, tf), lambda i, j: (0, j)),
                  pl.BlockSpec((tf, D), lambda i, j: (j, 0)), pl.BlockSpec((tm, D), row)],
        out_specs=[pl.BlockSpec((tm, D), row), pl.BlockSpec((1, D), lambda i, j: (0, 0)),
                   pl.BlockSpec((tm, tf), col), pl.BlockSpec((tm, tf), col), pl.BlockSpec((tm, tf), col),
                   pl.BlockSpec((tm, D), row), pl.BlockSpec((tm, D), row)],
        out_shape=[jax.ShapeDtypeStruct((T, D), F32), jax.ShapeDtypeStruct((1, D), F32),
                   jax.ShapeDtypeStruct((T, F), BF16), jax.ShapeDtypeStruct((T, F), BF16),
                   jax.ShapeDtypeStruct((T, F), BF16), jax.ShapeDtypeStruct((T, D), BF16),
                   jax.ShapeDtypeStruct((T, D), BF16)],
        scratch_shapes=[pltpu.VMEM((tm, D), F32)],
        compiler_params=_params(),
    )(x, gain, wg, wu, wd, dy)
    dwg = _mm(name + "_dwg", h, da, "tn", tm=1024, tn=1408, tk=1024, n=f_true)
    dwu = _mm(name + "_dwu", h, db, "tn", tm=1024, tn=1408, tk=1024, n=f_true)
    dwd = _mm(name + "_dwd", s, dyh, "tn", tm=1408, tn=1024, tk=1024, m=f_true)
    return dx, dgain, dwg, dwu, dwd


def _norm_f(pids, x, gain):
    return (_rms(x, gain),)


def _dn_conv_f(pids, x, w):
    j = pids[0]
    tap = lax.broadcasted_iota(jnp.int32, w.shape, 0)
    y = jnp.zeros_like(x)
    for t in range(DN_CONV):
        wt = jnp.sum(jnp.where(tap == t, w, 0.0), axis=0, keepdims=True)
        y = y + _shift(x, t - DN_CONV // 2) * wt
    y = _silu(y)
    n = y * lax.rsqrt(jnp.sum(y * y, axis=-1, keepdims=True) + L2_EPS)
    is_q = (j < DN_HEADS).astype(F32)
    is_qk = (j < 2 * DN_HEADS).astype(F32)
    scale = is_q * (DN_DIM ** -0.5) + (1.0 - is_q)
    return ((is_qk * n + (1.0 - is_qk) * y) * scale,)


def _dn_gate_f(pids, braw, araw, a_log, dt_bias):
    beta = _sigmoid(braw)
    g = -jnp.exp(a_log) * _softplus(araw + dt_bias)
    return beta, g


def _dn_prep_f(pids, q, k, v, brow, grow):
    cs = DN_SUPER
    sign = 1 - 2 * pids[2]
    ii = lax.broadcasted_iota(jnp.int32, (cs, cs), 0)
    jj = lax.broadcasted_iota(jnp.int32, (cs, cs), 1)
    shift = int(math.log2(DN_CHUNK))
    same = (ii >> shift) == (jj >> shift)
    d = (ii - jj) * sign
    incl = same & (d >= 0)
    strict = same & (d > 0)
    eye = ii == jj
    g_col = jnp.sum(jnp.where(eye, jnp.broadcast_to(grow, (cs, cs)), 0.0), axis=1, keepdims=True)
    b_col = jnp.sum(jnp.where(eye, jnp.broadcast_to(brow, (cs, cs)), 0.0), axis=1, keepdims=True)
    g128 = jnp.broadcast_to(g_col, (cs, DN_DIM))
    G = _mdot(incl.astype(BF16), g128)
    Gt = _mdot(same.astype(BF16), g128)
    Gc = jnp.concatenate([G, G], axis=1)
    Grow = jnp.sum(jnp.where(eye, Gc, 0.0), axis=0, keepdims=True)
    decay = jnp.exp(jnp.where(incl, Gc - Grow, MASK_VALUE))
    eG = jnp.exp(G)
    kb = k * b_col
    A = jnp.where(strict, _dot(kb, k, "nt", 1) * decay, 0.0)
    X = _unit_solve(A, jnp.concatenate([v * b_col, kb * eG], axis=1))
    qk = jnp.where(incl, _dot(q, k, "nt", 1) * decay, 0.0)
    return X, qk, q * eG, k * jnp.exp(Gt - G), jnp.exp(Gt)


def _dn_out_f(pids, of, ob, z, gain):
    return (_rms(of + ob, gain) * _silu(z),)


def _pool_f(pids, u, w, scale):
    g = pids[0]
    half = jnp.left_shift(1, g)
    n = u.shape[0]
    pos = lax.broadcasted_iota(jnp.int32, (n, 1), 0)
    tot = jnp.zeros_like(u)
    cnt = jnp.zeros((n, 1), F32)
    for o in range(-POOL_MAX_HALF, POOL_MAX_HALF):
        use = ((o >= -half) & (o < half)).astype(F32)
        tot = tot + use * _shift(u, o)
        cnt = cnt + use * ((pos + o >= 0) & (pos + o < n)).astype(F32)
    pooled = tot / cnt - u
    return (_dot(pooled, w, "nn", 1) * scale,)


def _rope_f(pids, *args):
    cos, sin = args[-2:]
    qs, ks, vs = args[:DA_GROUPS], args[DA_GROUPS:2 * DA_GROUPS], args[2 * DA_GROUPS:3 * DA_GROUPS]
    qr = [(q * cos + _rot(q) * sin) * (DA_DIM ** -0.5) for q in qs]
    kr = [k * cos + _rot(k) * sin for k in ks]
    return (*qr, *kr, *vs)


def _attn_head(q, k, v, qpos0, kpos0):
    s = _dot(q, k, "nt", 1)
    qi = qpos0 + lax.broadcasted_iota(jnp.int32, s.shape, 0)
    kj = kpos0 + lax.broadcasted_iota(jnp.int32, s.shape, 1)
    s = jnp.where(jnp.abs(kj - qi) <= DA_RADIUS, s, MASK_VALUE)
    m = lax.stop_gradient(jnp.max(s, axis=1, keepdims=True))
    p = jnp.exp(s - m)
    l = jnp.sum(p, axis=1, keepdims=True)
    o = _dot(p, v, "nn", 1) / l
    return o, jnp.broadcast_to(m + jnp.log(l), o.shape)


def _merge_f(pids, o0, o1, o2, l0, l1, l2):
    m = jnp.maximum(jnp.maximum(l0, l1), l2)
    e0, e1, e2 = jnp.exp(l0 - m), jnp.exp(l1 - m), jnp.exp(l2 - m)
    return ((e0 * o0 + e1 * o1 + e2 * o2) / (e0 + e1 + e2),)


def _gate_f(pids, g0, g1, g2, ya, yb, yc, b0, b1, b2):
    return (_sigmoid(g0 + b0) * ya + _sigmoid(g1 + b1) * yb + _sigmoid(g2 + b2) * yc,)


def _attn_window(i, L, tq, W):
    k0 = jnp.clip(i * tq - DA_RADIUS, 0, L - W)
    return pl.multiple_of(k0, DA_RADIUS)


def _strided_view(t, B, dil):
    T, HD = t.shape
    return t.reshape(B, T // B // dil, dil * HD)


def _attn_fwd(name, q, k, v, B, dil):
    T, HD = q.shape
    NS, L = B * dil, T // B // dil
    tq = min(DA_TQ, L)
    W = min(L, tq + 2 * DA_RADIUS)

    def body(q_ref, k_ref, v_ref, o_ref, l_ref):
        i = pl.program_id(1)
        k0 = _attn_window(i, L, tq, W)
        for h in range(DA_HEADS):
            hs = slice(h * DA_DIM, (h + 1) * DA_DIM)
            o, lse = _attn_head(q_ref[:, hs], k_ref[pl.ds(k0, W), hs], v_ref[pl.ds(k0, W), hs], i * tq, k0)
            o_ref[:, hs] = o
            l_ref[:, hs] = lse

    qs = pl.BlockSpec((None, tq, HD), lambda s, i: (s // dil, i, s % dil))
    ks = pl.BlockSpec((None, L, HD), lambda s, i: (s // dil, 0, s % dil))
    o, lse = pl.pallas_call(
        body, name=name, grid=(NS, L // tq), in_specs=[qs, ks, ks], out_specs=[qs, qs],
        out_shape=[jax.ShapeDtypeStruct((B, L, dil * HD), F32)] * 2, compiler_params=_params(),
    )(*[_strided_view(t, B, dil) for t in (q, k, v)])
    return o.reshape(T, HD), lse.reshape(T, HD)


def _attn_bwd(name, q, k, v, do, dl, B, dil):
    T, HD = q.shape
    NS, L = B * dil, T // B // dil
    tq = min(DA_TQ, L)
    W = min(L, tq + 2 * DA_RADIUS)

    def body(q_ref, k_ref, v_ref, do_ref, dl_ref, dq_ref, dk_ref, dv_ref):
        i = pl.program_id(1)
        k0 = _attn_window(i, L, tq, W)

        @pl.when(i == 0)
        def _():
            dk_ref[...] = jnp.zeros_like(dk_ref)
            dv_ref[...] = jnp.zeros_like(dv_ref)

        for h in range(DA_HEADS):
            hs = slice(h * DA_DIM, (h + 1) * DA_DIM)
            f = functools.partial(_attn_head, qpos0=i * tq, kpos0=k0)
            _, vjp = jax.vjp(f, q_ref[:, hs].astype(F32), k_ref[pl.ds(k0, W), hs].astype(F32),
                             v_ref[pl.ds(k0, W), hs].astype(F32))
            dq, dk, dv = vjp((do_ref[:, hs], dl_ref[:, hs]))
            dq_ref[:, hs] = dq
            dk_ref[pl.ds(k0, W), hs] += dk
            dv_ref[pl.ds(k0, W), hs] += dv

    qs = pl.BlockSpec((None, tq, HD), lambda s, i: (s // dil, i, s % dil))
    ks = pl.BlockSpec((None, L, HD), lambda s, i: (s // dil, 0, s % dil))
    res = pl.pallas_call(
        body, name=name, grid=(NS, L // tq), in_specs=[qs, ks, ks, qs, qs], out_specs=[qs, ks, ks],
        out_shape=[jax.ShapeDtypeStruct((B, L, dil * HD), F32)] * 3, compiler_params=_params(),
    )(*[_strided_view(t, B, dil) for t in (q, k, v, do, dl)])
    return tuple(t.reshape(T, HD) for t in res)


def _scan_chunk(t, rev, N):
    c = jnp.where(rev, N - 1 - t, t)
    per = DN_SUPER // DN_CHUNK
    return c, pl.multiple_of(c * DN_CHUNK, DN_CHUNK), pl.multiple_of((c % per) * DN_CHUNK, DN_CHUNK), \
        pl.multiple_of((c // per) * DN_SUPER, DN_SUPER)


def _dn_scan_fwd(name, uw, qk, qd, kd, gl, B):
    R, T, _ = uw.shape
    S = T // B
    N = S // DN_CHUNK
    C, DK = DN_CHUNK, DN_DIM

    PAIR = 2

    def body(uw_ref, qk_ref, qd_ref, kd_ref, gl_ref, o_ref, st_ref, vn_ref):
        rev = pl.program_id(1) * PAIR >= DN_HEADS
        vn_ref[...] = jnp.zeros_like(vn_ref)

        def step(t, states):
            c, r0, w0, s0 = _scan_chunk(t, rev, N)
            rows = pl.ds(r0, C)
            new = []
            for p, state in enumerate(states):
                st_ref[p, c] = state
                vnew = uw_ref[p, rows, 0:DK] - _dotp(uw_ref[p, rows, DK:2 * DK], state, "nn", 1)
                vn_ref[p, pl.ds(w0, C), :] = vnew
                o_ref[p, rows, :] = (_dotp(qd_ref[p, rows, :], state, "nn", 1)
                                     + _dotp(qk_ref[p, rows, :], vn_ref[p], "nn", 1))
                new.append(state * gl_ref[p, pl.ds(r0, 1), :] + _dotp(kd_ref[p, rows, :], vnew, "tn", 1))
            return tuple(new)

        lax.fori_loop(0, N, step, tuple(jnp.zeros((DK, DK), F32) for _ in range(PAIR)))

    def seq(w):
        return pl.BlockSpec((PAIR, S, w), lambda b, r: (r, b, 0))

    return pl.pallas_call(
        body, name=name, grid=(B, R // PAIR),
        in_specs=[seq(2 * DK), seq(DN_SUPER), seq(DK), seq(DK), seq(DK)],
        out_specs=[seq(DK), pl.BlockSpec((None, PAIR, N, DK, DK), lambda b, r: (b, r, 0, 0, 0))],
        out_shape=[jax.ShapeDtypeStruct((R, T, DK), F32), jax.ShapeDtypeStruct((B, R, N, DK, DK), F32)],
        scratch_shapes=[pltpu.VMEM((PAIR, DN_SUPER, DK), F32)], compiler_params=_params(),
    )(uw, qk, qd, kd, gl)


def _dn_scan_bwd(name, uw, qk, qd, kd, gl, st, do, B):
    R, T, _ = uw.shape
    S = T // B
    N = S // DN_CHUNK
    C, DK = DN_CHUNK, DN_DIM

    def body(uw_ref, qk_ref, qd_ref, kd_ref, gl_ref, st_ref, do_ref, duw_ref, dqk_ref, dqd_ref, dkd_ref, dgl_ref,
             vn_ref, tmp_ref):
        rev = pl.program_id(1) >= DN_HEADS
        vn_ref[...] = jnp.zeros_like(vn_ref)
        dgl_ref[...] = jnp.zeros_like(dgl_ref)

        def step(t, dstate):
            c, r0, w0, s0 = _scan_chunk(N - 1 - t, rev, N)
            rows = pl.ds(r0, C)
            state = st_ref[c]
            w = uw_ref[rows, DK:2 * DK]
            vnew = uw_ref[rows, 0:DK] - _dotp(w, state, "nn", 1)
            vn_ref[pl.ds(w0, C), :] = vnew
            do_c = do_ref[rows, :]
            tmp_ref[...] = _dotp(qk_ref[rows, :], do_c, "tn", 1)
            dvn = tmp_ref[pl.ds(w0, C), :] + _dotp(kd_ref[rows, :], dstate, "nn", 1)
            dqk_ref[rows, :] = _dotp(do_c, vn_ref[...], "nt", 1)
            dqd_ref[rows, :] = _dotp(do_c, state, "nt", 1)
            dkd_ref[rows, :] = _dotp(vnew, dstate, "nt", 1)
            dgl_ref[pl.ds(r0, 1), :] = jnp.sum(state * dstate, axis=0, keepdims=True)
            duw_ref[rows, 0:DK] = dvn
            duw_ref[rows, DK:2 * DK] = -_dotp(dvn, state, "nt", 1)
            return (_dotp(qd_ref[rows, :], do_c, "tn", 1) + dstate * gl_ref[pl.ds(r0, 1), :]
                    - _dotp(w, dvn, "tn", 1))

        lax.fori_loop(0, N, step, jnp.zeros((DK, DK), F32))

    def seq(w):
        return pl.BlockSpec((None, S, w), lambda b, r: (r, b, 0))

    return pl.pallas_call(
        body, name=name, grid=(B, R),
        in_specs=[seq(2 * DK), seq(DN_SUPER), seq(DK), seq(DK), seq(DK),
                  pl.BlockSpec((None, None, N, DK, DK), lambda b, r: (b, r, 0, 0, 0)),
                  pl.BlockSpec((None, S, DK), lambda b, r: (r % DN_HEADS, b, 0))],
        out_specs=[seq(2 * DK), seq(DN_SUPER), seq(DK), seq(DK), seq(DK)],
        out_shape=[jax.ShapeDtypeStruct((R, T, 2 * DK), F32), jax.ShapeDtypeStruct((R, T, DN_SUPER), F32),
                   jax.ShapeDtypeStruct((R, T, DK), F32), jax.ShapeDtypeStruct((R, T, DK), F32),
                   jax.ShapeDtypeStruct((R, T, DK), F32)],
        scratch_shapes=[pltpu.VMEM((DN_SUPER, DK), F32), pltpu.VMEM((DN_SUPER, DK), F32)],
        compiler_params=_params(),
    )(uw, qk, qd, kd, gl, st, do)


def _loss_fwd_bwd(name, x, gain, target):
    T, D = x.shape
    tm = _pick(T, 512, 8)

    def body(x_ref, g_ref, t_ref, loss_ref, dx_ref, dg_ref):
        i = pl.program_id(0)

        def f(xv, gv):
            e = _rms(xv, gv) - t_ref[...]
            return 0.5 * jnp.sum(jnp.mean(e * e, axis=-1, keepdims=True))

        val, (dx, dg) = jax.value_and_grad(f, argnums=(0, 1))(x_ref[...], g_ref[...])
        dx_ref[...] = dx
        part = jnp.full(loss_ref.shape, val, F32)

        @pl.when(i == 0)
        def _():
            dg_ref[...] = dg
            loss_ref[...] = part

        @pl.when(i != 0)
        def _():
            dg_ref[...] += dg
            loss_ref[...] += part

    return pl.pallas_call(
        body, name=name, grid=(T // tm,),
        in_specs=[pl.BlockSpec((tm, D), lambda i: (i, 0)), pl.BlockSpec((1, D), lambda i: (0, 0)),
                  pl.BlockSpec((tm, D), lambda i: (i, 0))],
        out_specs=[pl.BlockSpec((8, 128), lambda i: (0, 0)), pl.BlockSpec((tm, D), lambda i: (i, 0)),
                   pl.BlockSpec((1, D), lambda i: (0, 0))],
        out_shape=[jax.ShapeDtypeStruct((8, 128), F32), jax.ShapeDtypeStruct((T, D), F32),
                   jax.ShapeDtypeStruct((1, D), F32)],
        compiler_params=_params(),
    )(x, gain, target)


class _Cols:
    def __init__(self, D):
        assert D % 256 == 0
        self.gate = 0
        self.da = 3 * D
        self.qkv = self.da + 3 * DA_WIDTH
        self.z = self.qkv + 3 * DN_WIDTH
        self.pool = self.z + DN_WIDTH
        self.ba = self.pool + POOL_WIDTH
        self.total = self.ba + BA_PAD


def _rope_tables(S):
    half = DA_DIM // 2
    inv_freq = ROPE_THETA ** (-jnp.arange(half, dtype=F32) / half)
    ang = jnp.arange(S, dtype=F32)[:, None] * inv_freq[None, :]
    reps = DA_OUT // DA_DIM
    cos = jnp.tile(jnp.concatenate([jnp.cos(ang), jnp.cos(ang)], axis=1), (1, reps))
    sin = jnp.tile(jnp.concatenate([jnp.sin(ang), jnp.sin(ang)], axis=1), (1, reps))
    return cos, sin


def _to_strided(t, B, dil):
    T, w = t.shape
    L = T // B // dil
    return t.reshape(B, L, dil, w).transpose(0, 2, 1, 3).reshape(B * dil, L, w)


def _from_strided(t, B, dil):
    NS, L, w = t.shape
    return t.reshape(B, dil, L, w).transpose(0, 2, 1, 3).reshape(B * dil * L, w)


def _mixer(l, x1, w, B, host_gather=None):
    T, D = x1.shape
    S = T // B
    c = _Cols(D)
    tm = _pick(S, 512, 8)
    nmS = S // tm
    n = f"l{l}_"

    norm_ins = [_In(x1, (tm, D), lambda i: (i, 0)), _In(w["mix_norm"], (1, D), lambda i: (0, 0), acc=True)]
    norm_outs = [_Out((T, D), BF16, (tm, D), lambda i: (i, 0))]
    (h,) = _tile_fwd(n + "norm", _norm_f, (T // tm,), norm_ins, norm_outs)
    P = _mm(n + "proj", h, w["w_cat"], "nn", out_dtype=BF16, tm=1024, tn=2688, tk=1024)
    P_ba = _mm(n + "proj_ba", h, w["w_cat"][:, c.ba:], "nn")
    baT = P_ba[:, :16].T

    cb = c.qkv // DN_DIM
    conv_ins = [_In(P, (S, DN_DIM), lambda j, b: (b, cb + j), g=((T, 3 * DN_WIDTH), lambda j, b: (b, j)), gdtype=BF16),
                _In(w["dn_conv"], (DN_CONV, DN_DIM), lambda j, b: (0, j), acc=True)]
    conv_outs = [_Out((T, 3 * DN_WIDTH), F32, (S, DN_DIM), lambda j, b: (b, j))]
    conv_grid = (3 * DN_HEADS, B)
    (qkvc,) = _tile_fwd(n + "dnconv", _dn_conv_f, conv_grid, conv_ins, conv_outs)

    tg = _pick(T, 2048, 128)
    gate_ins = [_In(baT, (8, tg), lambda i: (0, i)), _In(baT, (8, tg), lambda i: (1, i)),
                _In(w["dn_a_log"], (8, 1), lambda i: (0, 0), acc=True),
                _In(w["dn_dt_bias"], (8, 1), lambda i: (0, 0), acc=True)]
    gate_ins[0].g = ((8, T), lambda i: (0, i))
    gate_ins[1].g = ((8, T), lambda i: (0, i))
    gate_outs = [_Out((8, T), F32, (8, tg), lambda i: (0, i))] * 2
    beta, gdec = _tile_fwd(n + "dngate", _dn_gate_f, (T // tg,), gate_ins, gate_outs)

    NSC = T // DN_SUPER
    beta4 = beta.reshape(2, DN_HEADS, NSC, 1, DN_SUPER)
    gdec4 = gdec.reshape(2, DN_HEADS, NSC, 1, DN_SUPER)
    R = 2 * DN_HEADS

    def qkv_in(off):
        return _In(qkvc, (DN_SUPER, DN_DIM), lambda hh, m: (m, off + hh), acc=True,
                   g=((T, DN_WIDTH), lambda hh, m: (m, hh)))

    def row_in(a):
        return _In(a, (2, None, None, 1, DN_SUPER), lambda hh, m: (0, hh, m, 0, 0), split=True)

    def chain_out(wd):
        return _Out((2, DN_HEADS, T, wd), F32, (2, None, DN_SUPER, wd), lambda hh, m: (0, hh, m, 0), split=True)

    prep_ins = [qkv_in(0), qkv_in(DN_HEADS), qkv_in(2 * DN_HEADS), row_in(beta4), row_in(gdec4)]
    prep_outs = [chain_out(2 * DN_DIM), chain_out(DN_SUPER), chain_out(DN_DIM), chain_out(DN_DIM), chain_out(DN_DIM)]
    prep_grid = (DN_HEADS, NSC)
    prep_res = _tile_fwd(n + "dnprep", _dn_prep_f, prep_grid, prep_ins, prep_outs, sub=2, comm=host_gather)
    gathered_next = None
    if host_gather is not None:
        prep_res, gathered_next = prep_res
    uw, qk, qd, kd, gl = (t.reshape((R,) + t.shape[2:]) for t in prep_res)
    o_dn, states = _dn_scan_fwd(n + "dnscan", uw, qk, qd, kd, gl, B)

    zb = c.z // DN_DIM
    out_ins = [_In(o_dn, (None, S, DN_DIM), lambda b, hh: (hh, b, 0)),
               _In(o_dn, (None, S, DN_DIM), lambda b, hh: (DN_HEADS + hh, b, 0)),
               _In(P, (S, DN_DIM), lambda b, hh: (b, zb + hh), g=((T, DN_WIDTH), lambda b, hh: (b, hh)), gdtype=BF16),
               _In(w["dn_out_norm"], (1, DN_DIM), lambda b, hh: (0, 0), acc=True)]
    out_ins[0].g = ((DN_HEADS, T, DN_DIM), lambda b, hh: (hh, b, 0))
    out_ins[1].g = ((DN_HEADS, T, DN_DIM), lambda b, hh: (hh, b, 0))
    out_outs = [_Out((T, DN_WIDTH), BF16, (S, DN_DIM), lambda b, hh: (b, hh))]
    (ya_in,) = _tile_fwd(n + "dnout", _dn_out_f, (B, DN_HEADS), out_ins, out_outs)

    pb = c.pool // POOL_DIM
    pool_ins = [_In(P, (S, POOL_DIM), lambda gi, b: (b, pb + gi), g=((T, POOL_WIDTH), lambda gi, b: (b, gi)), gdtype=BF16),
                _In(w["pool_w"], (None, POOL_DIM, POOL_DIM), lambda gi, b: (gi, 0, 0), acc=True),
                _In(w["pool_scale"], (None, 1, POOL_DIM), lambda gi, b: (gi, 0, 0), acc=True)]
    pool_outs = [_Out((T, POOL_WIDTH), BF16, (S, POOL_DIM), lambda gi, b: (b, gi))]
    (yb_in,) = _tile_fwd(n + "pool", _pool_f, (POOL_GROUPS, B), pool_ins, pool_outs)

    cos, sin = _rope_tables(S)
    db = c.da // DA_OUT

    def da_in(k):
        return _In(P, (tm, DA_OUT), lambda i: (i, db + k), g=((T, DA_OUT), lambda i: (i, 0)), gdtype=BF16)

    rope_ins = [da_in(k) for k in range(3 * DA_GROUPS)]
    rope_ins += [_In(cos, (tm, DA_OUT), lambda i: (i % nmS, 0), kind="c"),
                 _In(sin, (tm, DA_OUT), lambda i: (i % nmS, 0), kind="c")]
    rope_outs = [_Out((T, DA_OUT), BF16, (tm, DA_OUT), lambda i: (i, 0))] * (3 * DA_GROUPS)
    roped = _tile_fwd(n + "rope", _rope_f, (T // tm,), rope_ins, rope_outs)
    strided = []
    o_g, l_g = [], []
    for gi, dil in enumerate(DA_DILATIONS):
        qs, ks, vs = roped[gi], roped[DA_GROUPS + gi], roped[2 * DA_GROUPS + gi]
        strided.append((qs, ks, vs))
        o, lse = _attn_fwd(n + f"attn{gi}", qs, ks, vs, B, dil)
        o_g.append(o)
        l_g.append(lse)
    mrg_ins = [_In(a, (tm, DA_OUT), lambda i: (i, 0)) for a in o_g + l_g]
    mrg_outs = [_Out((T, DA_OUT), BF16, (tm, DA_OUT), lambda i: (i, 0))]
    (yc_in,) = _tile_fwd(n + "merge", _merge_f, (T // tm,), mrg_ins, mrg_outs)

    ya = _mm(n + "pa", ya_in, w["w_proj_a"], "nn", out_dtype=BF16)
    yb = _mm(n + "pb", yb_in, w["w_proj_b"], "nn", out_dtype=BF16)
    yc = _mm(n + "pc", yc_in, w["w_proj_c"], "nn", out_dtype=BF16)

    def gcol(k):
        return _In(P, (tm, D), lambda i: (i, k), g=((T, D), lambda i: (i, 0)), gdtype=BF16)

    def yin(a):
        return _In(a, (tm, D), lambda i: (i, 0), gdtype=BF16)

    def bin_(k):
        return _In(w["b_gate"][k:k + 1], (1, D), lambda i: (0, 0), acc=True)

    gm_ins = [gcol(0), gcol(1), gcol(2), yin(ya), yin(yb), yin(yc), bin_(0), bin_(1), bin_(2)]
    gm_outs = [_Out((T, D), BF16, (tm, D), lambda i: (i, 0))]
    (merged,) = _tile_fwd(n + "gates", _gate_f, (T // tm,), gm_ins, gm_outs)
    x2 = _mm(n + "out", merged, w["w_out"], "nn", add=x1)

    def backward(dx2, host_exchange=None):
        return _mixer_bwd(dx2, host_exchange, **{k: v for k, v in locals_.items() if k in _MIXER_BWD_NEEDS})

    locals_ = dict(locals())
    return x2, backward, gathered_next


_MIXER_BWD_NEEDS = ("n", "B", "T", "D", "tm", "w", "h", "merged", "gm_ins", "gm_outs", "ya_in", "yb_in", "yc_in",
                    "mrg_ins", "mrg_outs", "strided", "rope_ins", "rope_outs", "pool_ins", "pool_outs", "out_ins",
                    "out_outs", "uw", "qk", "qd", "kd", "gl", "states", "prep_grid", "prep_ins", "prep_outs", "tg",
                    "gate_ins", "gate_outs", "conv_grid", "conv_ins", "conv_outs", "norm_ins", "norm_outs")


def _mixer_bwd(dx2, host_exchange, *, n, B, T, D, tm, w, h, merged, gm_ins, gm_outs, ya_in, yb_in, yc_in, mrg_ins, mrg_outs, strided,
               rope_ins, rope_outs, pool_ins, pool_outs, out_ins, out_outs, uw, qk, qd, kd, gl, states, prep_grid,
               prep_ins, prep_outs, tg, gate_ins, gate_outs, conv_grid, conv_ins, conv_outs, norm_ins, norm_outs):
    g = {}
    dmerged = _mm(n + "d_merged", dx2, w["w_out"], "nt", out_dtype=BF16)
    g["w_out"] = _mm(n + "d_wout", merged, dx2, "tn", tm=1024, tn=1024, tk=1024)
    dg0, dg1, dg2, dya, dyb, dyc, db0, db1, db2 = _tile_bwd(
        n + "gates_b", _gate_f, (T // tm,), gm_ins, gm_outs, [dmerged], acc_from=0)
    g["b_gate"] = jnp.concatenate([db0, db1, db2], axis=0)
    dya_in = _mm(n + "d_pa", dya, w["w_proj_a"], "nt")
    dyb_in = _mm(n + "d_pb", dyb, w["w_proj_b"], "nt")
    dyc_in = _mm(n + "d_pc", dyc, w["w_proj_c"], "nt")
    g["w_proj_a"] = _mm(n + "d_wpa", ya_in, dya, "tn", tn=1024, tk=2048)
    g["w_proj_b"] = _mm(n + "d_wpb", yb_in, dyb, "tn", tn=1024, tk=2048)
    g["w_proj_c"] = _mm(n + "d_wpc", yc_in, dyc, "tn", tn=1024, tk=2048)

    dmrg = _tile_bwd(n + "merge_b", _merge_f, (T // tm,), mrg_ins, mrg_outs, [dyc_in])
    dq_parts, dk_parts, dv_parts = [], [], []
    for gi, dil in enumerate(DA_DILATIONS):
        qs, ks, vs = strided[gi]
        dq, dk, dv = _attn_bwd(n + f"attn{gi}_b", qs, ks, vs, dmrg[gi], dmrg[DA_GROUPS + gi], B, dil)
        dq_parts.append(dq)
        dk_parts.append(dk)
        dv_parts.append(dv)
    dP_da = _tile_bwd(n + "rope_b", _rope_f, (T // tm,), rope_ins, rope_outs, dq_parts + dk_parts + dv_parts)

    dPpool, g["pool_w"], g["pool_scale"] = _tile_bwd(
        n + "pool_b", _pool_f, (POOL_GROUPS, B), pool_ins, pool_outs, [dyb_in], acc_from=1)

    dof, dob, dPz, g["dn_out_norm"] = _tile_bwd(
        n + "dnout_b", _dn_out_f, (B, DN_HEADS), out_ins, out_outs, [dya_in], acc_from=0)
    del dob
    duw, dqk, dqd, dkd, dgl = _dn_scan_bwd(n + "dnscan_b", uw, qk, qd, kd, gl, states, dof, B)
    prep_cts = [t.reshape((2, DN_HEADS) + t.shape[1:]) for t in (duw, dqk, dqd, dkd, dgl)]
    prep_res = _tile_bwd(n + "dnprep_b", _dn_prep_f, prep_grid, prep_ins, prep_outs, prep_cts, sub=2,
                         comm=host_exchange)
    exchanged = None
    if host_exchange is not None:
        prep_res, exchanged = prep_res
    dq_, dk_, dv_, dbeta4, dgdec4 = prep_res
    dqkvc = jnp.concatenate([dq_, dk_, dv_], axis=1)
    dbraw, daraw, g["dn_a_log"], g["dn_dt_bias"] = _tile_bwd(
        n + "dngate_b", _dn_gate_f, (T // tg,), gate_ins, gate_outs,
        [dbeta4.reshape(8, T), dgdec4.reshape(8, T)], acc_from=0)
    dPqkv, g["dn_conv"] = _tile_bwd(n + "dnconv_b", _dn_conv_f, conv_grid, conv_ins, conv_outs, [dqkvc], acc_from=1)
    dba = jnp.concatenate([dbraw, daraw], axis=0).T.astype(BF16)
    dba = jnp.pad(dba, ((0, 0), (0, BA_PAD - 16)))
    dP = jnp.concatenate([dg0, dg1, dg2, *dP_da, dPqkv, dPz, dPpool, dba], axis=1)
    dh = _mm(n + "d_h", dP, w["w_cat"], "nt", tm=1024, tn=1024, tk=2688)
    g["w_cat"] = _mm(n + "d_wcat", h, dP, "tn", tm=1024, tn=2688, tk=1024)
    dx1, g["mix_norm"] = _tile_bwd(n + "norm_b", _norm_f, (T // tm,), norm_ins, norm_outs, [dh], acc_from=0,
                                   addends={0: dx2})
    return dx1, g, exchanged


def _layer_weights(full, l, D):
    c = _Cols(D)
    w_in = full["w_in"][l]
    o_z, o_ba, o_pool, o_da = 3 * DN_WIDTH, 4 * DN_WIDTH, 4 * DN_WIDTH + 16, 4 * DN_WIDTH + 16 + POOL_WIDTH
    w_cat = jnp.concatenate(
        [full["w_gate"][l], w_in[:, o_da:], w_in[:, :o_z], w_in[:, o_z:o_ba], w_in[:, o_pool:o_da], w_in[:, o_ba:o_pool],
         jnp.zeros((D, BA_PAD - 16), w_in.dtype)], axis=1).astype(BF16)
    assert w_cat.shape[1] == c.total
    w = {k: full[k][l].astype(BF16) for k in ("ffn1_w_gate", "ffn1_w_up", "ffn1_w_down", "ffn2_w_gate", "ffn2_w_up",
                                              "ffn2_w_down", "w_proj_a", "w_proj_b", "w_proj_c", "w_out")}
    w["w_cat"] = w_cat
    f_true = w["ffn1_w_gate"].shape[1]
    pad = -f_true % FFN_BWD_TF
    for k in ("ffn1", "ffn2"):
        w[k + "_bwd"] = (jnp.pad(w[k + "_w_gate"], ((0, 0), (0, pad))), jnp.pad(w[k + "_w_up"], ((0, 0), (0, pad))),
                         jnp.pad(w[k + "_w_down"], ((0, pad), (0, 0))), f_true)
    w["ffn1_norm"] = full["ffn1_norm"][l][None].astype(F32)
    w["ffn2_norm"] = full["ffn2_norm"][l][None].astype(F32)
    w["mix_norm"] = full["mix_norm"][l][None].astype(F32)
    w["dn_conv"] = full["dn_conv"][l].astype(F32)
    w["dn_a_log"] = full["dn_a_log"][l].reshape(2 * DN_HEADS, 1).astype(F32)
    w["dn_dt_bias"] = full["dn_dt_bias"][l].reshape(2 * DN_HEADS, 1).astype(F32)
    w["dn_out_norm"] = full["dn_out_norm"][l][None].astype(F32)
    w["pool_w"] = full["pool_w"][l].astype(F32)
    w["pool_scale"] = full["pool_scale"][l].reshape(POOL_GROUPS, 1, POOL_DIM).astype(F32)
    w["b_gate"] = full["b_gate"][l].reshape(3, D).astype(F32)
    return w


def _layer_grads(g, D):
    c = _Cols(D)
    gc = g.pop("w_cat")
    out = dict(g)
    out["w_gate"] = gc[:, :c.da]
    out["w_in"] = jnp.concatenate([gc[:, c.qkv:c.pool], gc[:, c.ba:c.ba + 16], gc[:, c.pool:c.ba], gc[:, c.da:c.qkv]],
                                  axis=1)
    for k in ("ffn1_norm", "ffn2_norm", "mix_norm", "dn_out_norm"):
        out[k] = g[k][0]
    out["dn_a_log"] = g["dn_a_log"].reshape(2, DN_HEADS)
    out["dn_dt_bias"] = g["dn_dt_bias"].reshape(2, DN_HEADS)
    out["pool_scale"] = g["pool_scale"].reshape(POOL_WIDTH)
    out["b_gate"] = g["b_gate"].reshape(3 * D)
    return out


def _unshard(got):
    full = {}
    for k, t in zip(SHARDED, got):
        ax = SHARD_AXIS[k] - 1
        shp = t.shape[1:]
        full[k] = jnp.moveaxis(t, 0, ax).reshape(shp[:ax] + (N_DEV * shp[ax],) + shp[ax + 1:])
    return full


def _to_owner_blocks(grads):
    out = []
    for k in SHARDED:
        ax = SHARD_AXIS[k] - 1
        shp = grads[k].shape
        t = grads[k].reshape(shp[:ax] + (N_DEV, shp[ax] // N_DEV) + shp[ax + 1:])
        out.append(jnp.moveaxis(t, ax, 0).astype(BF16))
    return out


def _local_step(x, target, rep, shards, distributed):
    B, S, D = x.shape
    T = B * S
    depth = len(shards)
    xs = x.reshape(T, D)
    tape = []
    if distributed:
        sharded_now = _unshard(_all_gather("gather_l0", [shards[0][k] for k in SHARDED]))
    else:
        sharded_now = shards[0]
    for l in range(depth):
        full = {k: [v] * (l + 1) for k, v in sharded_now.items()}
        full.update({k: v for k, v in rep.items() if k != "final_norm"})
        w = _layer_weights(full, l, D)
        host = ("gather", [shards[l + 1][k] for k in SHARDED]) if distributed and l + 1 < depth else None
        x1 = _ffn_fwd(f"l{l}_ffn1", xs, w["ffn1_norm"], w["ffn1_w_gate"], w["ffn1_w_up"], w["ffn1_w_down"])
        x2, mixer_bwd, got = _mixer(l, x1, w, B, host)
        x3 = _ffn_fwd(f"l{l}_ffn2", x2, w["ffn2_norm"], w["ffn2_w_gate"], w["ffn2_w_up"], w["ffn2_w_down"])
        tape.append((w, xs, mixer_bwd, x2))
        xs = x3
        if l + 1 < depth:
            sharded_now = _unshard(got) if distributed else shards[l + 1]
    loss8, dx, dfinal = _loss_fwd_bwd("loss", xs, rep["final_norm"][None].astype(F32), target.reshape(T, D))
    per_layer = [None] * depth
    exchanged = [None] * depth
    pending = None
    for l in reversed(range(depth)):
        w, x0, mixer_bwd, x2 = tape[l]
        wg_p, wu_p, wd_p, f_true = w["ffn2_bwd"]
        dx, dn2, dwg2, dwu2, dwd2 = _ffn_bwd(f"l{l}_ffn2b", x2, w["ffn2_norm"], wg_p, wu_p, wd_p, dx, f_true)
        dx, g, got = mixer_bwd(dx, ("exchange", pending) if pending is not None else None)
        if pending is not None:
            exchanged[l + 1] = got
        wg_p, wu_p, wd_p, f_true = w["ffn1_bwd"]
        dx, dn1, dwg1, dwu1, dwd1 = _ffn_bwd(f"l{l}_ffn1b", x0, w["ffn1_norm"], wg_p, wu_p, wd_p, dx, f_true)
        g.update(ffn1_norm=dn1, ffn1_w_gate=dwg1, ffn1_w_up=dwu1, ffn1_w_down=dwd1,
                 ffn2_norm=dn2, ffn2_w_gate=dwg2, ffn2_w_up=dwu2, ffn2_w_down=dwd2)
        per_layer[l] = _layer_grads(g, D)
        if distributed:
            pending = _to_owner_blocks(per_layer[l])
    return loss8[0, 0], dx.reshape(B, S, D), per_layer, dfinal[0], exchanged, pending


def _mesh_position():
    mx, my, mc = lax.axis_index("x"), lax.axis_index("y"), lax.axis_index("c")
    return mx, my, mc, 4 * mx + 2 * my + mc


def _peers(mx, my, mc):
    out = []
    for k in range(1, N_DEV):
        px, py, pc = mx ^ ((k >> 2) & 1), my ^ ((k >> 1) & 1), mc ^ (k & 1)
        out.append(((px, py, pc), 4 * px + 2 * py + pc))
    return out


_ANY = pl.BlockSpec(memory_space=pl.ANY)


def _all_gather(name, xs):
    n = len(xs)

    def body(*refs):
        _gather_start(refs[:n], refs[n:2 * n], *refs[2 * n:])
        _gather_finish(refs[:n], refs[n:2 * n], *refs[2 * n:])

    sems = pltpu.SemaphoreType.DMA((n, N_DEV - 1))
    return pl.pallas_call(
        body, name=name, in_specs=[_ANY] * n, out_specs=[_ANY] * n,
        out_shape=[jax.ShapeDtypeStruct((N_DEV,) + x.shape, x.dtype) for x in xs],
        scratch_shapes=[sems, sems, pltpu.SemaphoreType.DMA((n,))],
    )(*xs)


class _GatherPlan:
    def __init__(self, x_refs, o_refs, send_sems, recv_sems, local_sems):
        self.x, self.o, self.ss, self.rs, self.ls = x_refs, o_refs, send_sems, recv_sems, local_sems
        self.mx, self.my, self.mc, self.me = _mesh_position()
        self.self_id = (self.mx, self.my, self.mc)
        self.sibling = (self.mx, self.my, 1 - self.mc)
        self.chips = [(1 - self.mx, self.my), (self.mx, 1 - self.my), (1 - self.mx, 1 - self.my)]

    def copy(self, a, k, blk, to, from_input=False):
        dst = self.o[a].at[blk]
        return pltpu.make_async_remote_copy(src_ref=self.x[a] if from_input else dst, dst_ref=dst,
                                            send_sem=self.ss.at[a, k], recv_sem=self.rs.at[a, k],
                                            device_id=to, device_id_type=pl.DeviceIdType.MESH)

    def own(self, a):
        return pltpu.make_async_copy(self.x[a], self.o[a].at[self.me], self.ls.at[a])

    def first_sends(self, a):
        cps = [self.copy(a, 0, self.me, self.sibling, from_input=True)]
        return cps + [self.copy(a, 1 + j, self.me, (*chip, self.mc), from_input=True) for j, chip in enumerate(self.chips)]

    def passed_on(self, a, j):
        cx, cy = self.chips[j]
        return self.copy(a, 4 + j, 4 * cx + 2 * cy + self.mc, self.sibling)


def _gather_start(x_refs, o_refs, send_sems, recv_sems, local_sems):
    p = _GatherPlan(x_refs, o_refs, send_sems, recv_sems, local_sems)
    for a in range(len(x_refs)):
        p.own(a).start()
        for cp in p.first_sends(a):
            cp.start()


def _gather_finish(x_refs, o_refs, send_sems, recv_sems, local_sems):
    p = _GatherPlan(x_refs, o_refs, send_sems, recv_sems, local_sems)
    n = len(x_refs)
    for a in range(n):
        for j, (cx, cy) in enumerate(p.chips):
            p.copy(a, 1 + j, 4 * cx + 2 * cy + p.mc, p.self_id).wait_recv()
            p.passed_on(a, j).start()
    for a in range(n):
        p.copy(a, 0, 4 * p.mx + 2 * p.my + 1 - p.mc, p.self_id).wait_recv()
        for j, (cx, cy) in enumerate(p.chips):
            p.copy(a, 4 + j, 4 * cx + 2 * cy + 1 - p.mc, p.self_id).wait_recv()
    for a in range(n):
        for cp in p.first_sends(a):
            cp.wait_send()
        for j in range(len(p.chips)):
            p.passed_on(a, j).wait_send()
        p.own(a).wait()


def _exchange_grads(name, gs, gr):
    ns, n = len(gs), len(gs) + len(gr)

    def body(*refs):
        _exchange_start(refs[:n], refs[n:2 * n], *refs[2 * n:], n_sharded=ns)
        _exchange_finish(refs[:n], refs[n:2 * n], *refs[2 * n:], n_sharded=ns)

    sems = pltpu.SemaphoreType.DMA((n, N_DEV - 1))
    outs = pl.pallas_call(
        body, name=name, in_specs=[_ANY] * n, out_specs=[_ANY] * n,
        out_shape=[jax.ShapeDtypeStruct(a.shape, a.dtype) for a in gs]
        + [jax.ShapeDtypeStruct((N_DEV,) + a.shape, a.dtype) for a in gr],
        scratch_shapes=[sems, sems, pltpu.SemaphoreType.DMA((n,))],
    )(*gs, *gr)
    return outs[:ns], outs[ns:]


def _exchange_copies(in_refs, out_refs, send_sems, recv_sems, local_sems, n_sharded):
    mx, my, mc, me = _mesh_position()
    n = len(in_refs)
    own = [pltpu.make_async_copy(in_refs[a].at[me] if a < n_sharded else in_refs[a], out_refs[a].at[me],
                                 local_sems.at[a]) for a in range(n)]
    remote = []
    for k, (peer, pid) in enumerate(_peers(mx, my, mc)):
        for a in range(n):
            src = in_refs[a].at[pid] if a < n_sharded else in_refs[a]
            remote.append(pltpu.make_async_remote_copy(
                src_ref=src, dst_ref=out_refs[a].at[me], send_sem=send_sems.at[a, k], recv_sem=recv_sems.at[a, k],
                device_id=peer, device_id_type=pl.DeviceIdType.MESH))
    return own, remote


def _exchange_start(in_refs, out_refs, send_sems, recv_sems, local_sems, n_sharded=None):
    ns = len(in_refs) if n_sharded is None else n_sharded
    own, remote = _exchange_copies(in_refs, out_refs, send_sems, recv_sems, local_sems, ns)
    for cp in own + remote:
        cp.start()


def _exchange_finish(in_refs, out_refs, send_sems, recv_sems, local_sems, n_sharded=None):
    ns = len(in_refs) if n_sharded is None else n_sharded
    own, remote = _exchange_copies(in_refs, out_refs, send_sems, recv_sems, local_sems, ns)
    for cp in remote:
        cp.wait_send()
        cp.wait_recv()
    for cp in own:
        cp.wait()


def _reduce_adamw(name, parts, w, m, v):
    shape = w.shape
    cols = shape[-1]
    w2, m2, v2 = (t.reshape(-1, cols) for t in (w, m, v))
    p3 = parts.reshape(N_DEV, -1, cols)
    rows = w2.shape[0]
    tr = _pick(rows, 512, 16) if rows > 1024 else rows
    c1 = 1.0 - ADAM_B1 ** ADAM_STEP
    c2 = 1.0 - ADAM_B2 ** ADAM_STEP

    def body(p_ref, w_ref, m_ref, v_ref, g_ref, d_ref, nm_ref, nv_ref):
        gv = p_ref[0].astype(F32)
        for d in range(1, N_DEV):
            gv = gv + p_ref[d].astype(F32)
        nm = ADAM_B1 * m_ref[...] + (1.0 - ADAM_B1) * gv
        nv = ADAM_B2 * v_ref[...] + (1.0 - ADAM_B2) * (gv * gv)
        g_ref[...] = gv
        d_ref[...] = -ADAM_LR * ((nm / c1) / (jnp.sqrt(nv / c2) + ADAM_EPS) + ADAM_WD * w_ref[...])
        nm_ref[...] = nm
        nv_ref[...] = nv

    spec = pl.BlockSpec((tr, cols), lambda i: (i, 0))
    outs = pl.pallas_call(
        body, name=name, grid=(rows // tr,),
        in_specs=[pl.BlockSpec((N_DEV, tr, cols), lambda i: (0, i, 0))] + [spec] * 3, out_specs=[spec] * 4,
        out_shape=[jax.ShapeDtypeStruct((rows, cols), F32)] * 4, compiler_params=_params(),
    )(p3, w2, m2, v2)
    return tuple(o.reshape(shape) for o in outs)


def kernel(x, ffn1_norm, ffn1_w_gate, ffn1_w_up, ffn1_w_down, mix_norm, w_in, dn_conv, dn_a_log, dn_dt_bias, dn_out_norm, pool_w, pool_scale, w_proj_a, w_proj_b, w_proj_c, w_gate, b_gate, w_out, ffn2_norm, ffn2_w_gate, ffn2_w_up, ffn2_w_down, final_norm, loss_target, m_ffn1_norm, m_ffn1_w_gate, m_ffn1_w_up, m_ffn1_w_down, m_mix_norm, m_w_in, m_dn_conv, m_dn_a_log, m_dn_dt_bias, m_dn_out_norm, m_pool_w, m_pool_scale, m_w_proj_a, m_w_proj_b, m_w_proj_c, m_w_gate, m_b_gate, m_w_out, m_ffn2_norm, m_ffn2_w_gate, m_ffn2_w_up, m_ffn2_w_down, m_final_norm, v_ffn1_norm, v_ffn1_w_gate, v_ffn1_w_up, v_ffn1_w_down, v_mix_norm, v_w_in, v_dn_conv, v_dn_a_log, v_dn_dt_bias, v_dn_out_norm, v_pool_w, v_pool_scale, v_w_proj_a, v_w_proj_b, v_w_proj_c, v_w_gate, v_b_gate, v_w_out, v_ffn2_norm, v_ffn2_w_gate, v_ffn2_w_up, v_ffn2_w_down, v_final_norm):
    args = locals()
    wts = {k: args[k] for k in WEIGHTS}
    ms = {k: args["m_" + k] for k in WEIGHTS}
    vs = {k: args["v_" + k] for k in WEIGHTS}

    depth = w_in.shape[0]
    rep = {k: wts[k] for k in REPLICATED}
    shards = [{k: wts[k][l].astype(BF16) for k in SHARDED} for l in range(depth)]
    loss_local, dx, per_layer, dfinal, exchanged, pending = _local_step(x, loss_target, rep, shards, True)
    loss = lax.psum(loss_local, ("x", "y", "c"))

    gr = [dfinal if k == "final_norm" else jnp.stack([pg[k] for pg in per_layer]).astype(F32).reshape(wts[k].shape)
          for k in REPLICATED]
    exchanged[0], got_r = _exchange_grads("exchange_grads", pending, gr)
    parts = {k: jnp.stack([exchanged[l][j] for l in range(depth)], axis=1) for j, k in enumerate(SHARDED)}
    parts.update(zip(REPLICATED, got_r))

    g_final, deltas, new_m, new_v = {}, {}, {}, {}
    for k in WEIGHTS:
        g_final[k], deltas[k], new_m[k], new_v[k] = _reduce_adamw("adamw_" + k, parts[k], wts[k], ms[k], vs[k])
    return (loss, dx, *[g_final[k] for k in WEIGHTS], *[deltas[k] for k in WEIGHTS], *[new_m[k] for k in WEIGHTS],
            *[new_v[k] for k in WEIGHTS])
```
